```python
import math, functools
import jax, jax.numpy as jnp
from jax import lax
import numpy as np

D_MODEL = 1024
BATCH = 4
SEQ = 4096
DEPTH = 2
DEC_BATCH = 32
DEC_SEQ = 8
PAST_LEN = 8192
PAGE_SIZE = 128

HEAD_DIM = 64
H_NSA = 4
H_HGRN = 4
H_MOBA = 4
H_FOX = 4
MIX_WIDTH = (H_NSA + H_HGRN + H_MOBA + H_FOX) * HEAD_DIM
HGRN_DK = 64
HGRN_DV = HEAD_DIM
HGRN_CHUNK = 64
NSA_CMP_BLOCK = 32
NSA_CMP_STRIDE = 16
NSA_SEL_BLOCK = 64
NSA_TOPN = 16
NSA_WINDOW = 512
NSA_ROWS = 4
MOBA_BLOCK = 256
MOBA_TOPK = 3
FOX_BIAS_INIT = 3.0
ROPE_THETA = 10000.0
Q_BLOCK = 128
GATHER_Q_BLOCK = 32
D_FF = 2816
N_EXPERTS = 8
TOP_K = 2
D_FF_EXPERT = 3584
MOE_BLOCK = 128
RMS_EPS = 1e-6
NEG_INF = -1e30
SEL_FORCE = 1e6
ATTN_SCALE = HEAD_DIM ** -0.5
IN_SIZES = (H_NSA * HEAD_DIM, 2 * HEAD_DIM, 2 * HEAD_DIM, 2 * HEAD_DIM, 3 * H_NSA,
            H_HGRN * HGRN_DK, H_HGRN * HGRN_DK, H_HGRN * HGRN_DV, H_HGRN * HGRN_DV,
            H_MOBA * HEAD_DIM, H_MOBA * HEAD_DIM, H_MOBA * HEAD_DIM,
            H_FOX * HEAD_DIM, H_FOX * HEAD_DIM, H_FOX * HEAD_DIM, H_FOX)
N_IN = sum(IN_SIZES)

kernel_name = 'hybrid_nsa_hgrn2_moba_fox_step'


def rmsnorm(x, g):
    xf = x.astype(jnp.float32)
    y = xf * lax.rsqrt(jnp.mean(xf * xf, axis=-1, keepdims=True) + RMS_EPS)
    return (y * g.astype(jnp.float32)).astype(x.dtype)


def rope(x, pos):
    half = HEAD_DIM // 2
    inv = ROPE_THETA ** (-jnp.arange(half, dtype=jnp.float32) / half)
    ang = pos.astype(jnp.float32)[:, None] * inv[None, :]
    cos = jnp.cos(ang)[:, None, :]
    sin = jnp.sin(ang)[:, None, :]
    xf = x.astype(jnp.float32)
    x1, x2 = xf[..., :half], xf[..., half:]
    return jnp.concatenate([x1 * cos - x2 * sin, x2 * cos + x1 * sin], axis=-1).astype(x.dtype)


def masked_softmax(s, mask):
    s = jnp.where(mask, s.astype(jnp.float32), NEG_INF)
    m = jnp.max(s, axis=-1, keepdims=True)
    p = jnp.where(mask, jnp.exp(s - m), 0.0)
    return p / jnp.maximum(jnp.sum(p, axis=-1, keepdims=True), 1.0)


def sweep(fn, blk, *arrays):
    B, T = arrays[0].shape[:2]
    nb = -(-T // blk)
    Tp = nb * blk
    blocks = []
    for a in arrays:
        a = jnp.pad(a, [(0, 0), (0, Tp - T)] + [(0, 0)] * (a.ndim - 2))
        blocks.append(jnp.moveaxis(a.reshape((B, nb, blk) + a.shape[2:]), 1, 0))
    starts = jnp.arange(nb, dtype=jnp.int32) * blk
    out = lax.map(lambda args: fn(args[0], *args[1]), (starts, tuple(blocks)))
    out = jnp.moveaxis(out, 0, 1).reshape((B, Tp) + out.shape[3:])
    return out[:, :T]


def window_attend(q, rows, buf_len):
    B, T, H, Dh = q.shape
    dt = q.dtype
    W = NSA_WINDOW
    qb = min(Q_BLOCK, T)
    nb = -(-T // qb)
    Tp = nb * qb
    band = W + qb
    rp = jnp.pad(rows, ((0, 0), (W, Tp - T), (0, 0), (0, 0)))
    kidx = buf_len + np.arange(nb)[:, None] * qb + np.arange(band)[None, :]
    kb = rp[:, kidx]
    qp = jnp.pad(q, ((0, 0), (0, Tp - T), (0, 0), (0, 0))).reshape(B, nb, qb, H, Dh)
    s = jnp.einsum('bnqhd,bnkd->bhnqk', qp, kb[..., 0, :], preferred_element_type=jnp.float32) * ATTN_SCALE
    qq = buf_len + np.arange(nb)[:, None] * qb + np.arange(qb)[None, :]
    kk = kidx - W
    mask = (kk[:, None, :] >= 0) & (kk[:, None, :] <= qq[:, :, None]) & (kk[:, None, :] > qq[:, :, None] - W)
    p = masked_softmax(s, mask)
    o = jnp.einsum('bhnqk,bnkd->bnqhd', p.astype(dt), kb[..., 1, :]).reshape(B, Tp, H, Dh)
    return o[:, :T]


def nsa_mixer(q, kv_c, kv_s, kv_w, gate_logits, past_rows, win_buf, qk_gain, cmp_pe, cmp_w, offset):
    B, T, H, Dh = q.shape
    dt = q.dtype
    pos = offset + jnp.arange(T, dtype=jnp.int32)
    qn = rmsnorm(q, qk_gain[0])
    qr = rope(qn, pos)
    k_s = rope(rmsnorm(kv_s[:, :, 0:1], qk_gain[2]), pos)[:, :, 0]
    k_w = rope(rmsnorm(kv_w[:, :, 0:1], qk_gain[3]), pos)[:, :, 0]
    new_rows = jnp.stack([kv_c[:, :, 0], kv_c[:, :, 1], k_s, kv_s[:, :, 1]], axis=2)
    rows = jnp.concatenate([past_rows, new_rows], axis=1)
    S = offset + T
    n_cmp = (S - NSA_CMP_BLOCK) // NSA_CMP_STRIDE + 1
    cidx = np.arange(n_cmp)[:, None] * NSA_CMP_STRIDE + np.arange(NSA_CMP_BLOCK)[None, :]

    def compress(r):
        blocks = rows[:, :, r][:, cidx] + cmp_pe[r]
        return blocks.reshape(B, n_cmp, NSA_CMP_BLOCK * Dh) @ cmp_w[r]

    k_cmp = rmsnorm(compress(0), qk_gain[1])
    v_cmp = compress(1)
    s_c = jnp.einsum('bthd,bnd->bhtn', qn, k_cmp, preferred_element_type=jnp.float32) * ATTN_SCALE
    cmp_end = np.arange(n_cmp) * NSA_CMP_STRIDE + NSA_CMP_BLOCK - 1
    p_c = masked_softmax(s_c, cmp_end[None, :] <= pos[:, None])
    o_cmp = jnp.einsum('bhtn,bnd->bthd', p_c.astype(dt), v_cmp)
    n_sel = -(-S // NSA_SEL_BLOCK)
    ci = np.arange(n_cmp)[:, None] * NSA_CMP_STRIDE
    sj = np.arange(n_sel)[None, :] * NSA_SEL_BLOCK
    cover = ((ci < sj + NSA_SEL_BLOCK) & (ci + NSA_CMP_BLOCK > sj)).astype(np.float32)
    imp = jnp.einsum('bhtn,nj->btj', p_c, jnp.asarray(cover))
    cur = (pos // NSA_SEL_BLOCK)[:, None]
    jj = jnp.arange(n_sel)[None, :]
    forced = (jj == 0) | (jj == cur) | (jj == cur - 1)
    imp = jnp.where(jj > cur, NEG_INF, jnp.where(forced, SEL_FORCE, imp))
    _, sel_idx = lax.top_k(imp, min(NSA_TOPN, n_sel))
    kv_sel = jnp.pad(rows[:, :, 2:4], ((0, 0), (0, n_sel * NSA_SEL_BLOCK - S), (0, 0), (0, 0)))
    kv_sel = kv_sel.reshape(B, n_sel, NSA_SEL_BLOCK, 2, Dh)
    bidx = jnp.arange(B)[:, None, None]

    def sel_block(start, q_blk, idx_blk):
        qb = q_blk.shape[1]
        tp = offset + start + jnp.arange(qb)
        g = kv_sel[bidx, idx_blk]
        kpos = idx_blk[..., None] * NSA_SEL_BLOCK + jnp.arange(NSA_SEL_BLOCK)
        mask = (kpos <= tp[None, :, None, None]).reshape(B, 1, qb, -1)
        g = g.reshape(B, qb, -1, 2, Dh)
        s = jnp.einsum('bqhd,bqkd->bhqk', q_blk, g[..., 0, :], preferred_element_type=jnp.float32) * ATTN_SCALE
        p = masked_softmax(s, mask)
        return jnp.einsum('bhqk,bqkd->bqhd', p.astype(dt), g[..., 1, :])

    o_sel = sweep(sel_block, min(GATHER_Q_BLOCK, T), qr, sel_idx)
    win_rows = jnp.concatenate([win_buf, jnp.stack([k_w, kv_w[:, :, 1]], axis=2)], axis=1)
    o_win = window_attend(qr, win_rows, win_buf.shape[1])
    gates = jax.nn.sigmoid(gate_logits.astype(jnp.float32)).astype(dt)
    o = gates[..., 0:1] * o_cmp + gates[..., 1:2] * o_sel + gates[..., 2:3] * o_win
    new_win = win_rows[:, -min(NSA_WINDOW, win_rows.shape[1]):]
    return o, new_rows, new_win


def gated_recurrence(q, k, v, logf, S0):
    B, T, H, DK = q.shape
    DV = v.shape[-1]
    C = math.gcd(T, HGRN_CHUNK)
    nc = T // C

    def chunks(a):
        return jnp.moveaxis(a.reshape((B, nc, C) + a.shape[2:]), 1, 0).swapaxes(2, 3)

    causal = jnp.tril(jnp.ones((C, C), dtype=bool))[:, :, None]

    def step(S, inp):
        qc, kc, vc, gc = inp
        b = jnp.cumsum(gc, axis=2)
        o_inter = jnp.einsum('bhtk,bhkv->bhtv', qc * jnp.exp(b), S)
        diff = b[:, :, :, None, :] - b[:, :, None, :, :]
        decay = jnp.where(causal, jnp.exp(jnp.where(causal, diff, 0.0)), 0.0)
        A = jnp.einsum('bhtk,bhsk,bhtsk->bhts', qc, kc, decay)
        o = o_inter + jnp.einsum('bhts,bhsv->bhtv', A, vc)
        b_last = b[:, :, -1:, :]
        S_new = jnp.exp(b_last[:, :, 0, :])[..., None] * S + jnp.einsum('bhsk,bhsv->bhkv', kc * jnp.exp(b_last - b), vc)
        return S_new, o

    S, o = lax.scan(step, S0, (chunks(q), chunks(k), chunks(v), chunks(logf)))
    o = jnp.moveaxis(o.swapaxes(2, 3), 0, 1).reshape(B, T, H, DV)
    return o, S


def hgrn2_mixer(q, f, i, g, S0, lb, out_gain):
    dt = q.dtype
    H = q.shape[2]
    lb = lb.reshape(H, HGRN_DK)
    z = f.astype(jnp.float32)
    logf = jnp.log(lb + (1.0 - lb) * jax.nn.sigmoid(z))
    k = (1.0 - lb) * jax.nn.sigmoid(-z)
    qf = jax.nn.silu(q.astype(jnp.float32))
    o, S = gated_recurrence(qf, k, i.astype(jnp.float32), logf, S0)
    o = rmsnorm(o, out_gain) * jax.nn.silu(g.astype(jnp.float32))
    return o.astype(dt), S


def moba_mixer(q, k, v, past_rows, qk_gain, offset):
    B, T, H, Dh = q.shape
    dt = q.dtype
    pos = offset + jnp.arange(T, dtype=jnp.int32)
    qr = rope(rmsnorm(q, qk_gain[0]), pos)
    kr = rope(rmsnorm(k, qk_gain[1]), pos)
    new_rows = jnp.stack([kr, v], axis=2)
    rows = jnp.concatenate([past_rows, new_rows], axis=1)
    S = offset + T
    nblk = -(-S // MOBA_BLOCK)
    kvb = jnp.pad(rows, ((0, 0), (0, nblk * MOBA_BLOCK - S), (0, 0), (0, 0), (0, 0)))
    kvb = jnp.transpose(kvb.reshape(B, nblk, MOBA_BLOCK, 2, H, Dh), (0, 4, 1, 2, 3, 5))
    kmean = jnp.mean(kvb[..., 0, :].astype(jnp.float32), axis=3)
    gate = jnp.einsum('bthd,bhnd->bthn', qr.astype(jnp.float32), kmean)
    own = pos // MOBA_BLOCK
    past_ok = jnp.arange(nblk)[None, :] < own[:, None]
    _, top = lax.top_k(jnp.where(past_ok[None, :, None, :], gate, NEG_INF), min(MOBA_TOPK, nblk))
    valid = top < own[None, :, None, None]
    idx = jnp.concatenate([top, jnp.broadcast_to(own[None, :, None, None], (B, T, H, 1)).astype(top.dtype)], axis=-1)
    ok = jnp.concatenate([valid, jnp.ones((B, T, H, 1), dtype=bool)], axis=-1)
    bidx = jnp.arange(B)[:, None, None, None]
    hidx = jnp.arange(H)[None, None, :, None]

    def blk_fn(start, q_blk, idx_blk, ok_blk):
        qb = q_blk.shape[1]
        tp = offset + start + jnp.arange(qb)
        g = kvb[bidx, hidx, idx_blk]
        kpos = idx_blk[..., None] * MOBA_BLOCK + jnp.arange(MOBA_BLOCK)
        mask = (ok_blk[..., None] & (kpos <= tp[None, :, None, None, None])).reshape(B, qb, H, -1)
        g = g.reshape(B, qb, H, -1, 2, Dh)
        s = jnp.einsum('bqhd,bqhkd->bqhk', q_blk, g[..., 0, :], preferred_element_type=jnp.float32) * ATTN_SCALE
        p = masked_softmax(s, mask)
        return jnp.einsum('bqhk,bqhkd->bqhd', p.astype(dt), g[..., 1, :])

    o = sweep(blk_fn, min(GATHER_Q_BLOCK, T), qr, idx, ok)
    return o, new_rows


def fox_mixer(q, k, v, f_logit, past_kv, past_logf, qk_gain, f_bias, offset):
    B, T, H, Dh = q.shape
    dt = q.dtype
    qn = rmsnorm(q, qk_gain[0])
    kn = rmsnorm(k, qk_gain[1])
    logf_new = jax.nn.log_sigmoid(f_logit.astype(jnp.float32) + f_bias.astype(jnp.float32))
    new_rows = jnp.stack([kn, v], axis=2)
    rows = jnp.concatenate([past_kv, new_rows], axis=1)
    c = jnp.cumsum(jnp.concatenate([past_logf.astype(jnp.float32), logf_new], axis=1), axis=1)
    S = offset + T
    K = rows[:, :, 0]
    V = rows[:, :, 1]
    c_k = jnp.moveaxis(c, 1, 2)[:, :, None, :]
    kpos = jnp.arange(S)

    def blk_fn(start, q_blk, cq_blk):
        qb = q_blk.shape[1]
        tp = offset + start + jnp.arange(qb)
        s = jnp.einsum('bqhd,bkhd->bhqk', q_blk, K, preferred_element_type=jnp.float32) * ATTN_SCALE
        s = s + jnp.moveaxis(cq_blk, 1, 2)[..., None] - c_k
        p = masked_softmax(s, kpos[None, :] <= tp[:, None])
        return jnp.einsum('bhqk,bkhd->bqhd', p.astype(dt), V)

    o = sweep(blk_fn, min(Q_BLOCK, T), qn, c[:, offset:])
    return o, new_rows, logf_new.astype(dt)


def swiglu(h, w1, w3, w2):
    return (jax.nn.silu(h @ w1) * (h @ w3)) @ w2


def moe_ffn(h, router, w1, w3, w2):
    B, T, D = h.shape
    dt = h.dtype
    N = B * T
    xf = h.reshape(N, D)
    logits = (xf @ router).astype(jnp.float32)
    top_v, top_e = lax.top_k(logits, TOP_K)
    gates = jax.nn.softmax(top_v, axis=-1)
    NK = N * TOP_K
    flat_e = top_e.reshape(NK)
    flat_tok = jnp.arange(NK, dtype=jnp.int32) // TOP_K
    order = jnp.argsort(flat_e)
    e_sorted = flat_e[order]
    tok_sorted = flat_tok[order]
    counts = jnp.zeros((N_EXPERTS,), jnp.int32).at[flat_e].add(1)
    padded = (counts + MOE_BLOCK - 1) // MOE_BLOCK * MOE_BLOCK
    pend = jnp.cumsum(padded)
    pstart = pend - padded
    start = jnp.cumsum(counts) - counts
    dest = pstart[e_sorted] + (jnp.arange(NK, dtype=jnp.int32) - start[e_sorted])
    n_blocks = -(-NK // MOE_BLOCK) + N_EXPERTS
    slot_tok = jnp.full((n_blocks * MOE_BLOCK,), N, jnp.int32).at[dest].set(tok_sorted)
    block_exp = jnp.clip(jnp.searchsorted(pend, jnp.arange(n_blocks) * MOE_BLOCK, side='right'), 0, N_EXPERTS - 1)
    xpad = jnp.concatenate([xf, jnp.zeros((1, D), dt)], axis=0)
    xb = xpad[slot_tok].reshape(n_blocks, MOE_BLOCK, D)

    def expert_block(args):
        xblk, e = args
        return swiglu(xblk, w1[e], w3[e], w2[e])

    yb = lax.map(expert_block, (xb, block_exp)).reshape(n_blocks * MOE_BLOCK, D)
    y_assign = yb[dest] * gates.reshape(NK)[order][:, None].astype(dt)
    out = jnp.zeros((N, D), dt).at[tok_sorted].add(y_assign)
    return out.reshape(B, T, D)


def trunk_layer(x, offset, past_nsa, win_buf, hgrn_s0, past_moba, past_fox_kv, past_fox_logf,
                g_mix, g_ffn, w_in, w_out, nsa_gain, nsa_pe, nsa_w, lb, hgrn_gain, moba_gain,
                fox_gain, fox_bias, ffn):
    B, T, _ = x.shape
    h = rmsnorm(x, g_mix)
    z = h @ w_in
    cuts = np.cumsum(np.array(IN_SIZES))[:-1].tolist()
    (nq, nkc, nks, nkw, ngate, hq, hf, hi, hg, mq, mk, mv, fq, fk, fv, ff) = jnp.split(z, cuts, axis=-1)

    def heads(a, n, d):
        return a.reshape(B, T, n, d)

    o_nsa, nsa_rows, nsa_win = nsa_mixer(heads(nq, H_NSA, HEAD_DIM), heads(nkc, 2, HEAD_DIM), heads(nks, 2, HEAD_DIM),
                                         heads(nkw, 2, HEAD_DIM), heads(ngate, H_NSA, 3), past_nsa, win_buf,
                                         nsa_gain, nsa_pe, nsa_w, offset)
    o_hg, hg_state = hgrn2_mixer(heads(hq, H_HGRN, HGRN_DK), heads(hf, H_HGRN, HGRN_DK), heads(hi, H_HGRN, HGRN_DV),
                                 heads(hg, H_HGRN, HGRN_DV), hgrn_s0, lb, hgrn_gain)
    o_mb, mb_rows = moba_mixer(heads(mq, H_MOBA, HEAD_DIM), heads(mk, H_MOBA, HEAD_DIM), heads(mv, H_MOBA, HEAD_DIM),
                               past_moba, moba_gain, offset)
    o_fx, fx_rows, fx_logf = fox_mixer(heads(fq, H_FOX, HEAD_DIM), heads(fk, H_FOX, HEAD_DIM), heads(fv, H_FOX, HEAD_DIM),
                                       ff, past_fox_kv, past_fox_logf, fox_gain, fox_bias, offset)
    o = jnp.concatenate([o_nsa, o_hg, o_mb, o_fx], axis=2).reshape(B, T, MIX_WIDTH)
    x = x + o @ w_out
    x = x + ffn(rmsnorm(x, g_ffn))
    return x, (nsa_rows, nsa_win, hg_state.astype(x.dtype), mb_rows, fx_rows, fx_logf)


def setup_inputs(seed: int = 0) -> dict:
    key = jax.random.key(seed)
    ks = iter(jax.random.split(key, 40))

    def nrm(shape, scale):
        return jax.random.normal(next(ks), shape, jnp.float32) * scale

    n_pages = PAST_LEN // PAGE_SIZE
    n_used = DEC_BATCH * n_pages
    n_phys = n_used + max(1, n_used // 4)
    perm = jax.random.permutation(next(ks), n_phys)
    page_table = perm[:n_used].reshape(DEC_BATCH, n_pages).astype(jnp.int32)
    win_len = min(NSA_WINDOW, PAST_LEN)
    n_dense = (DEPTH + 1) // 2
    n_moe = DEPTH // 2
    return {
        'x_prompt': nrm((BATCH, SEQ, D_MODEL), 1.0),
        'x_sample': nrm((DEC_BATCH, DEC_SEQ, D_MODEL), 1.0),
        'cache_nsa': nrm((DEPTH, n_phys, PAGE_SIZE, NSA_ROWS, HEAD_DIM), 1.0),
        'state_nsa_win': nrm((DEPTH, DEC_BATCH, win_len, 2, HEAD_DIM), 1.0),
        'state_hgrn': nrm((DEPTH, DEC_BATCH, H_HGRN, HGRN_DK, HGRN_DV), 0.5),
        'cache_moba': nrm((DEPTH, n_phys, PAGE_SIZE, 2, H_MOBA, HEAD_DIM), 1.0),
        'cache_fox_kv': nrm((DEPTH, n_phys, PAGE_SIZE, 2, H_FOX, HEAD_DIM), 1.0),
        'cache_fox_logf': jax.nn.log_sigmoid(FOX_BIAS_INIT + nrm((DEPTH, n_phys, PAGE_SIZE, H_FOX), 1.0)),
        'page_table': page_table,
        'g_mix': 1.0 + nrm((DEPTH, D_MODEL), 0.05),
        'g_ffn': 1.0 + nrm((DEPTH, D_MODEL), 0.05),
        'w_in': nrm((DEPTH, D_MODEL, N_IN), D_MODEL ** -0.5),
        'w_out': nrm((DEPTH, MIX_WIDTH, D_MODEL), MIX_WIDTH ** -0.5),
        'nsa_qk_gain': 1.0 + nrm((DEPTH, 4, HEAD_DIM), 0.05),
        'nsa_cmp_pe': nrm((DEPTH, 2, NSA_CMP_BLOCK, HEAD_DIM), 0.1),
        'nsa_cmp_w': nrm((DEPTH, 2, NSA_CMP_BLOCK * HEAD_DIM, HEAD_DIM), (NSA_CMP_BLOCK * HEAD_DIM) ** -0.5),
        'hgrn_lb_logits': nrm((DEPTH, H_HGRN * HGRN_DK), 0.5),
        'hgrn_out_gain': 1.0 + nrm((DEPTH, HGRN_DV), 0.05),
        'moba_qk_gain': 1.0 + nrm((DEPTH, 2, HEAD_DIM), 0.05),
        'fox_qk_gain': 1.0 + nrm((DEPTH, 2, HEAD_DIM), 0.05),
        'fox_f_bias': FOX_BIAS_INIT + nrm((DEPTH, H_FOX), 0.1),
        'ffn_w1': nrm((n_dense, D_MODEL, D_FF), D_MODEL ** -0.5),
        'ffn_w3': nrm((n_dense, D_MODEL, D_FF), D_MODEL ** -0.5),
        'ffn_w2': nrm((n_dense, D_FF, D_MODEL), D_FF ** -0.5),
        'moe_router': nrm((n_moe, D_MODEL, N_EXPERTS), D_MODEL ** -0.5),
        'moe_w1': nrm((n_moe, N_EXPERTS, D_MODEL, D_FF_EXPERT), D_MODEL ** -0.5),
        'moe_w3': nrm((n_moe, N_EXPERTS, D_MODEL, D_FF_EXPERT), D_MODEL ** -0.5),
        'moe_w2': nrm((n_moe, N_EXPERTS, D_FF_EXPERT, D_MODEL), D_FF_EXPERT ** -0.5),
    }


def reference(x_prompt, x_sample, cache_nsa, state_nsa_win, state_hgrn, cache_moba, cache_fox_kv, cache_fox_logf,
              page_table, g_mix, g_ffn, w_in, w_out, nsa_qk_gain, nsa_cmp_pe, nsa_cmp_w, hgrn_lb_logits,
              hgrn_out_gain, moba_qk_gain, fox_qk_gain, fox_f_bias, ffn_w1, ffn_w3, ffn_w2, moe_router,
              moe_w1, moe_w3, moe_w2):
    dt = x_prompt.dtype
    Bp = x_prompt.shape[0]
    Bs = x_sample.shape[0]
    past_len = page_table.shape[1] * PAGE_SIZE
    lb_w = jax.nn.softmax(hgrn_lb_logits.astype(jnp.float32), axis=0)
    lower_bounds = jnp.cumsum(lb_w, axis=0) - lb_w[0:1]

    def gather_pages(pool):
        g = pool[page_table]
        return g.reshape((Bs, past_len) + pool.shape[2:])

    xp, xs = x_prompt, x_sample
    st_p, st_s = [], []
    for l in range(DEPTH):
        i = l // 2
        if l % 2 == 0:
            ffn = functools.partial(swiglu, w1=ffn_w1[i], w3=ffn_w3[i], w2=ffn_w2[i])
        else:
            ffn = functools.partial(moe_ffn, router=moe_router[i], w1=moe_w1[i], w3=moe_w3[i], w2=moe_w2[i])
        lw = (g_mix[l], g_ffn[l], w_in[l], w_out[l], nsa_qk_gain[l], nsa_cmp_pe[l], nsa_cmp_w[l], lower_bounds[l],
              hgrn_out_gain[l], moba_qk_gain[l], fox_qk_gain[l], fox_f_bias[l], ffn)
        xp, sp = trunk_layer(xp, 0,
                             jnp.zeros((Bp, 0, NSA_ROWS, HEAD_DIM), dt), jnp.zeros((Bp, 0, 2, HEAD_DIM), dt),
                             jnp.zeros((Bp, H_HGRN, HGRN_DK, HGRN_DV), jnp.float32),
                             jnp.zeros((Bp, 0, 2, H_MOBA, HEAD_DIM), dt), jnp.zeros((Bp, 0, 2, H_FOX, HEAD_DIM), dt),
                             jnp.zeros((Bp, 0, H_FOX), dt), *lw)
        xs, ss = trunk_layer(xs, past_len,
                             gather_pages(cache_nsa[l]), state_nsa_win[l], state_hgrn[l].astype(jnp.float32),
                             gather_pages(cache_moba[l]), gather_pages(cache_fox_kv[l]),
                             gather_pages(cache_fox_logf[l]), *lw)
        st_p.append(sp)
        st_s.append(ss)

    def stk(states, j):
        return jnp.stack([s[j] for s in states], axis=0)

    return (xp, xs,
            stk(st_p, 0), stk(st_s, 0), stk(st_p, 1), stk(st_s, 1), stk(st_p, 2), stk(st_s, 2),
            stk(st_p, 3), stk(st_s, 3), stk(st_p, 4), stk(st_s, 4), stk(st_p, 5), stk(st_s, 5))
```

```python
import math, functools
import jax, jax.numpy as jnp
from jax import lax
import numpy as np
from jax.experimental import pallas as pl
from jax.experimental.pallas import tpu as pltpu

D_MODEL = 1024
DEPTH = 2
PAGE_SIZE = 128
HEAD_DIM = 64
H_NSA = 4
H_HGRN = 4
H_MOBA = 4
H_FOX = 4
MIX_WIDTH = (H_NSA + H_HGRN + H_MOBA + H_FOX) * HEAD_DIM
HGRN_DK = 64
HGRN_DV = HEAD_DIM
HGRN_CHUNK = 64
NSA_CMP_BLOCK = 32
NSA_CMP_STRIDE = 16
NSA_SEL_BLOCK = 64
NSA_TOPN = 16
NSA_WINDOW = 512
NSA_ROWS = 4
MOBA_BLOCK = 256
MOBA_TOPK = 3
ROPE_THETA = 10000.0
Q_BLOCK = 128
GATHER_Q_BLOCK = 32
N_EXPERTS = 8
TOP_K = 2
MOE_BLOCK = 128
RMS_EPS = 1e-6
NEG_INF = -1e30
SEL_FORCE = 1e6
ATTN_SCALE = HEAD_DIM ** -0.5
IN_SIZES = (H_NSA * HEAD_DIM, 2 * HEAD_DIM, 2 * HEAD_DIM, 2 * HEAD_DIM, 3 * H_NSA,
            H_HGRN * HGRN_DK, H_HGRN * HGRN_DK, H_HGRN * HGRN_DV, H_HGRN * HGRN_DV,
            H_MOBA * HEAD_DIM, H_MOBA * HEAD_DIM, H_MOBA * HEAD_DIM,
            H_FOX * HEAD_DIM, H_FOX * HEAD_DIM, H_FOX * HEAD_DIM, H_FOX)
N_IN = sum(IN_SIZES)

VMEM_LIMIT = 48 * 1024 * 1024


def _round_up(x, m):
    return -(-x // m) * m


def _mm_kernel(a_ref, b_ref, o_ref):
    k = pl.program_id(2)
    acc = jnp.dot(a_ref[...].astype(jnp.bfloat16), b_ref[...].astype(jnp.bfloat16),
                  preferred_element_type=jnp.float32)

    @pl.when(k == 0)
    def _():
        o_ref[...] = acc

    @pl.when(k != 0)
    def _():
        o_ref[...] += acc


def matmul(a, b, tm=512, tn=512, tk=1024):
    M, K = a.shape
    _, N = b.shape
    tm = min(tm, _round_up(M, 8))
    Mp, Np = _round_up(M, tm), _round_up(N, tn)
    if K % tk:
        tk = K
    if Mp != M:
        a = jnp.pad(a, ((0, Mp - M), (0, 0)))
    if Np != N:
        b = jnp.pad(b, ((0, 0), (0, Np - N)))
    out = pl.pallas_call(
        _mm_kernel,
        grid=(Mp // tm, Np // tn, K // tk),
        in_specs=[pl.BlockSpec((tm, tk), lambda i, j, k: (i, k)),
                  pl.BlockSpec((tk, tn), lambda i, j, k: (k, j))],
        out_specs=pl.BlockSpec((tm, tn), lambda i, j, k: (i, j)),
        out_shape=jax.ShapeDtypeStruct((Mp, Np), jnp.float32),
        compiler_params=pltpu.CompilerParams(
            dimension_semantics=("parallel", "parallel", "arbitrary"),
            vmem_limit_bytes=VMEM_LIMIT),
        name="dense_matmul",
    )(a, b)
    return out[:M, :N]


def mm3(x, w, **kw):
    B, T, D = x.shape
    return matmul(x.reshape(B * T, D), w, **kw).reshape(B, T, w.shape[1])


def rmsnorm(x, g):
    xf = x.astype(jnp.float32)
    y = xf * lax.rsqrt(jnp.mean(xf * xf, axis=-1, keepdims=True) + RMS_EPS)
    return (y * g.astype(jnp.float32)).astype(x.dtype)


def rope(x, pos):
    half = HEAD_DIM // 2
    inv = ROPE_THETA ** (-jnp.arange(half, dtype=jnp.float32) / half)
    ang = pos.astype(jnp.float32)[:, None] * inv[None, :]
    cos = jnp.cos(ang)[:, None, :]
    sin = jnp.sin(ang)[:, None, :]
    xf = x.astype(jnp.float32)
    x1, x2 = xf[..., :half], xf[..., half:]
    return jnp.concatenate([x1 * cos - x2 * sin, x2 * cos + x1 * sin], axis=-1).astype(x.dtype)


def masked_softmax(s, mask):
    s = jnp.where(mask, s.astype(jnp.float32), NEG_INF)
    m = jnp.max(s, axis=-1, keepdims=True)
    p = jnp.where(mask, jnp.exp(s - m), 0.0)
    return p / jnp.maximum(jnp.sum(p, axis=-1, keepdims=True), 1.0)


def sweep(fn, blk, *arrays):
    B, T = arrays[0].shape[:2]
    nb = -(-T // blk)
    Tp = nb * blk
    blocks = []
    for a in arrays:
        a = jnp.pad(a, [(0, 0), (0, Tp - T)] + [(0, 0)] * (a.ndim - 2))
        blocks.append(jnp.moveaxis(a.reshape((B, nb, blk) + a.shape[2:]), 1, 0))
    starts = jnp.arange(nb, dtype=jnp.int32) * blk
    out = lax.map(lambda args: fn(args[0], *args[1]), (starts, tuple(blocks)))
    out = jnp.moveaxis(out, 0, 1).reshape((B, Tp) + out.shape[3:])
    return out[:, :T]


def window_attend(q, rows, buf_len):
    B, T, H, Dh = q.shape
    dt = q.dtype
    W = NSA_WINDOW
    qb = min(Q_BLOCK, T)
    nb = -(-T // qb)
    Tp = nb * qb
    band = W + qb
    rp = jnp.pad(rows, ((0, 0), (W, Tp - T), (0, 0), (0, 0)))
    kidx = buf_len + np.arange(nb)[:, None] * qb + np.arange(band)[None, :]
    kb = rp[:, kidx]
    qp = jnp.pad(q, ((0, 0), (0, Tp - T), (0, 0), (0, 0))).reshape(B, nb, qb, H, Dh)
    s = jnp.einsum('bnqhd,bnkd->bhnqk', qp, kb[..., 0, :], preferred_element_type=jnp.float32) * ATTN_SCALE
    qq = buf_len + np.arange(nb)[:, None] * qb + np.arange(qb)[None, :]
    kk = kidx - W
    mask = (kk[:, None, :] >= 0) & (kk[:, None, :] <= qq[:, :, None]) & (kk[:, None, :] > qq[:, :, None] - W)
    p = masked_softmax(s, mask)
    o = jnp.einsum('bhnqk,bnkd->bnqhd', p.astype(dt), kb[..., 1, :]).reshape(B, Tp, H, Dh)
    return o[:, :T]


def nsa_mixer(q, kv_c, kv_s, kv_w, gate_logits, past_rows, win_buf, qk_gain, cmp_pe, cmp_w, offset):
    B, T, H, Dh = q.shape
    dt = q.dtype
    pos = offset + jnp.arange(T, dtype=jnp.int32)
    qn = rmsnorm(q, qk_gain[0])
    qr = rope(qn, pos)
    k_s = rope(rmsnorm(kv_s[:, :, 0:1], qk_gain[2]), pos)[:, :, 0]
    k_w = rope(rmsnorm(kv_w[:, :, 0:1], qk_gain[3]), pos)[:, :, 0]
    new_rows = jnp.stack([kv_c[:, :, 0], kv_c[:, :, 1], k_s, kv_s[:, :, 1]], axis=2)
    rows = jnp.concatenate([past_rows, new_rows], axis=1)
    S = offset + T
    n_cmp = (S - NSA_CMP_BLOCK) // NSA_CMP_STRIDE + 1
    cidx = np.arange(n_cmp)[:, None] * NSA_CMP_STRIDE + np.arange(NSA_CMP_BLOCK)[None, :]

    def compress(r):
        blocks = rows[:, :, r][:, cidx] + cmp_pe[r]
        return blocks.reshape(B, n_cmp, NSA_CMP_BLOCK * Dh) @ cmp_w[r]

    k_cmp = rmsnorm(compress(0), qk_gain[1])
    v_cmp = compress(1)
    s_c = jnp.einsum('bthd,bnd->bhtn', qn, k_cmp, preferred_element_type=jnp.float32) * ATTN_SCALE
    cmp_end = np.arange(n_cmp) * NSA_CMP_STRIDE + NSA_CMP_BLOCK - 1
    p_c = masked_softmax(s_c, cmp_end[None, :] <= pos[:, None])
    o_cmp = jnp.einsum('bhtn,bnd->bthd', p_c.astype(dt), v_cmp)
    n_sel = -(-S // NSA_SEL_BLOCK)
    ci = np.arange(n_cmp)[:, None] * NSA_CMP_STRIDE
    sj = np.arange(n_sel)[None, :] * NSA_SEL_BLOCK
    cover = ((ci < sj + NSA_SEL_BLOCK) & (ci + NSA_CMP_BLOCK > sj)).astype(np.float32)
    imp = jnp.einsum('bhtn,nj->btj', p_c, jnp.asarray(cover))
    cur = (pos // NSA_SEL_BLOCK)[:, None]
    jj = jnp.arange(n_sel)[None, :]
    forced = (jj == 0) | (jj == cur) | (jj == cur - 1)
    imp = jnp.where(jj > cur, NEG_INF, jnp.where(forced, SEL_FORCE, imp))
    _, sel_idx = lax.top_k(imp, min(NSA_TOPN, n_sel))
    kv_sel = jnp.pad(rows[:, :, 2:4], ((0, 0), (0, n_sel * NSA_SEL_BLOCK - S), (0, 0), (0, 0)))
    kv_sel = kv_sel.reshape(B, n_sel, NSA_SEL_BLOCK, 2, Dh)
    bidx = jnp.arange(B)[:, None, None]

    def sel_block(start, q_blk, idx_blk):
        qb = q_blk.shape[1]
        tp = offset + start + jnp.arange(qb)
        g = kv_sel[bidx, idx_blk]
        kpos = idx_blk[..., None] * NSA_SEL_BLOCK + jnp.arange(NSA_SEL_BLOCK)
        mask = (kpos <= tp[None, :, None, None]).reshape(B, 1, qb, -1)
        g = g.reshape(B, qb, -1, 2, Dh)
        s = jnp.einsum('bqhd,bqkd->bhqk', q_blk, g[..., 0, :], preferred_element_type=jnp.float32) * ATTN_SCALE
        p = masked_softmax(s, mask)
        return jnp.einsum('bhqk,bqkd->bqhd', p.astype(dt), g[..., 1, :])

    o_sel = sweep(sel_block, min(GATHER_Q_BLOCK, T), qr, sel_idx)
    win_rows = jnp.concatenate([win_buf, jnp.stack([k_w, kv_w[:, :, 1]], axis=2)], axis=1)
    o_win = window_attend(qr, win_rows, win_buf.shape[1])
    gates = jax.nn.sigmoid(gate_logits.astype(jnp.float32)).astype(dt)
    o = gates[..., 0:1] * o_cmp + gates[..., 1:2] * o_sel + gates[..., 2:3] * o_win
    new_win = win_rows[:, -min(NSA_WINDOW, win_rows.shape[1]):]
    return o, new_rows, new_win


def gated_recurrence(q, k, v, logf, S0):
    B, T, H, DK = q.shape
    DV = v.shape[-1]
    C = math.gcd(T, HGRN_CHUNK)
    nc = T // C

    def chunks(a):
        return jnp.moveaxis(a.reshape((B, nc, C) + a.shape[2:]), 1, 0).swapaxes(2, 3)

    causal = jnp.tril(jnp.ones((C, C), dtype=bool))[:, :, None]

    def step(S, inp):
        qc, kc, vc, gc = inp
        b = jnp.cumsum(gc, axis=2)
        o_inter = jnp.einsum('bhtk,bhkv->bhtv', qc * jnp.exp(b), S)
        diff = b[:, :, :, None, :] - b[:, :, None, :, :]
        decay = jnp.where(causal, jnp.exp(jnp.where(causal, diff, 0.0)), 0.0)
        A = jnp.einsum('bhtk,bhsk,bhtsk->bhts', qc, kc, decay)
        o = o_inter + jnp.einsum('bhts,bhsv->bhtv', A, vc)
        b_last = b[:, :, -1:, :]
        S_new = jnp.exp(b_last[:, :, 0, :])[..., None] * S + jnp.einsum('bhsk,bhsv->bhkv', kc * jnp.exp(b_last - b), vc)
        return S_new, o

    S, o = lax.scan(step, S0, (chunks(q), chunks(k), chunks(v), chunks(logf)))
    o = jnp.moveaxis(o.swapaxes(2, 3), 0, 1).reshape(B, T, H, DV)
    return o, S


def hgrn2_mixer(q, f, i, g, S0, lb, out_gain):
    dt = q.dtype
    H = q.shape[2]
    lb = lb.reshape(H, HGRN_DK)
    z = f.astype(jnp.float32)
    logf = jnp.log(lb + (1.0 - lb) * jax.nn.sigmoid(z))
    k = (1.0 - lb) * jax.nn.sigmoid(-z)
    qf = jax.nn.silu(q.astype(jnp.float32))
    o, S = gated_recurrence(qf, k, i.astype(jnp.float32), logf, S0)
    o = rmsnorm(o, out_gain) * jax.nn.silu(g.astype(jnp.float32))
    return o.astype(dt), S


def moba_mixer(q, k, v, past_rows, qk_gain, offset):
    B, T, H, Dh = q.shape
    dt = q.dtype
    pos = offset + jnp.arange(T, dtype=jnp.int32)
    qr = rope(rmsnorm(q, qk_gain[0]), pos)
    kr = rope(rmsnorm(k, qk_gain[1]), pos)
    new_rows = jnp.stack([kr, v], axis=2)
    rows = jnp.concatenate([past_rows, new_rows], axis=1)
    S = offset + T
    nblk = -(-S // MOBA_BLOCK)
    kvb = jnp.pad(rows, ((0, 0), (0, nblk * MOBA_BLOCK - S), (0, 0), (0, 0), (0, 0)))
    kvb = jnp.transpose(kvb.reshape(B, nblk, MOBA_BLOCK, 2, H, Dh), (0, 4, 1, 2, 3, 5))
    kmean = jnp.mean(kvb[..., 0, :].astype(jnp.float32), axis=3)
    gate = jnp.einsum('bthd,bhnd->bthn', qr.astype(jnp.float32), kmean)
    own = pos // MOBA_BLOCK
    past_ok = jnp.arange(nblk)[None, :] < own[:, None]
    _, top = lax.top_k(jnp.where(past_ok[None, :, None, :], gate, NEG_INF), min(MOBA_TOPK, nblk))
    valid = top < own[None, :, None, None]
    idx = jnp.concatenate([top, jnp.broadcast_to(own[None, :, None, None], (B, T, H, 1)).astype(top.dtype)], axis=-1)
    ok = jnp.concatenate([valid, jnp.ones((B, T, H, 1), dtype=bool)], axis=-1)
    bidx = jnp.arange(B)[:, None, None, None]
    hidx = jnp.arange(H)[None, None, :, None]

    def blk_fn(start, q_blk, idx_blk, ok_blk):
        qb = q_blk.shape[1]
        tp = offset + start + jnp.arange(qb)
        g = kvb[bidx, hidx, idx_blk]
        kpos = idx_blk[..., None] * MOBA_BLOCK + jnp.arange(MOBA_BLOCK)
        mask = (ok_blk[..., None] & (kpos <= tp[None, :, None, None, None])).reshape(B, qb, H, -1)
        g = g.reshape(B, qb, H, -1, 2, Dh)
        s = jnp.einsum('bqhd,bqhkd->bqhk', q_blk, g[..., 0, :], preferred_element_type=jnp.float32) * ATTN_SCALE
        p = masked_softmax(s, mask)
        return jnp.einsum('bqhk,bqhkd->bqhd', p.astype(dt), g[..., 1, :])

    o = sweep(blk_fn, min(GATHER_Q_BLOCK, T), qr, idx, ok)
    return o, new_rows


def fox_mixer(q, k, v, f_logit, past_kv, past_logf, qk_gain, f_bias, offset):
    B, T, H, Dh = q.shape
    dt = q.dtype
    qn = rmsnorm(q, qk_gain[0])
    kn = rmsnorm(k, qk_gain[1])
    logf_new = jax.nn.log_sigmoid(f_logit.astype(jnp.float32) + f_bias.astype(jnp.float32))
    new_rows = jnp.stack([kn, v], axis=2)
    rows = jnp.concatenate([past_kv, new_rows], axis=1)
    c = jnp.cumsum(jnp.concatenate([past_logf.astype(jnp.float32), logf_new], axis=1), axis=1)
    S = offset + T
    K = rows[:, :, 0]
    V = rows[:, :, 1]
    c_k = jnp.moveaxis(c, 1, 2)[:, :, None, :]
    kpos = jnp.arange(S)

    def blk_fn(start, q_blk, cq_blk):
        qb = q_blk.shape[1]
        tp = offset + start + jnp.arange(qb)
        s = jnp.einsum('bqhd,bkhd->bhqk', q_blk, K, preferred_element_type=jnp.float32) * ATTN_SCALE
        s = s + jnp.moveaxis(cq_blk, 1, 2)[..., None] - c_k
        p = masked_softmax(s, kpos[None, :] <= tp[:, None])
        return jnp.einsum('bhqk,bkhd->bqhd', p.astype(dt), V)

    o = sweep(blk_fn, min(Q_BLOCK, T), qn, c[:, offset:])
    return o, new_rows, logf_new.astype(dt)


def swiglu_dense(h, w1, w3, w2):
    B, T, D = h.shape
    x = h.reshape(B * T, D)
    u = jax.nn.silu(matmul(x, w1)) * matmul(x, w3)
    return matmul(u, w2, tk=1408).reshape(B, T, D)


def swiglu(h, w1, w3, w2):
    return (jax.nn.silu(h @ w1) * (h @ w3)) @ w2


def moe_ffn(h, router, w1, w3, w2):
    B, T, D = h.shape
    dt = h.dtype
    N = B * T
    xf = h.reshape(N, D)
    logits = (xf @ router).astype(jnp.float32)
    top_v, top_e = lax.top_k(logits, TOP_K)
    gates = jax.nn.softmax(top_v, axis=-1)
    NK = N * TOP_K
    flat_e = top_e.reshape(NK)
    flat_tok = jnp.arange(NK, dtype=jnp.int32) // TOP_K
    order = jnp.argsort(flat_e)
    e_sorted = flat_e[order]
    tok_sorted = flat_tok[order]
    counts = jnp.zeros((N_EXPERTS,), jnp.int32).at[flat_e].add(1)
    padded = (counts + MOE_BLOCK - 1) // MOE_BLOCK * MOE_BLOCK
    pend = jnp.cumsum(padded)
    pstart = pend - padded
    start = jnp.cumsum(counts) - counts
    dest = pstart[e_sorted] + (jnp.arange(NK, dtype=jnp.int32) - start[e_sorted])
    n_blocks = -(-NK // MOE_BLOCK) + N_EXPERTS
    slot_tok = jnp.full((n_blocks * MOE_BLOCK,), N, jnp.int32).at[dest].set(tok_sorted)
    block_exp = jnp.clip(jnp.searchsorted(pend, jnp.arange(n_blocks) * MOE_BLOCK, side='right'), 0, N_EXPERTS - 1)
    xpad = jnp.concatenate([xf, jnp.zeros((1, D), dt)], axis=0)
    xb = xpad[slot_tok].reshape(n_blocks, MOE_BLOCK, D)

    def expert_block(args):
        xblk, e = args
        return swiglu(xblk, w1[e], w3[e], w2[e])

    yb = lax.map(expert_block, (xb, block_exp)).reshape(n_blocks * MOE_BLOCK, D)
    y_assign = yb[dest] * gates.reshape(NK)[order][:, None].astype(dt)
    out = jnp.zeros((N, D), dt).at[tok_sorted].add(y_assign)
    return out.reshape(B, T, D)


def trunk_layer(x, offset, past_nsa, win_buf, hgrn_s0, past_moba, past_fox_kv, past_fox_logf,
                g_mix, g_ffn, w_in, w_out, nsa_gain, nsa_pe, nsa_w, lb, hgrn_gain, moba_gain,
                fox_gain, fox_bias, ffn):
    B, T, _ = x.shape
    h = rmsnorm(x, g_mix)
    z = mm3(h, w_in)
    cuts = np.cumsum(np.array(IN_SIZES))[:-1].tolist()
    (nq, nkc, nks, nkw, ngate, hq, hf, hi, hg, mq, mk, mv, fq, fk, fv, ff) = jnp.split(z, cuts, axis=-1)

    def heads(a, n, d):
        return a.reshape(B, T, n, d)

    o_nsa, nsa_rows, nsa_win = nsa_mixer(heads(nq, H_NSA, HEAD_DIM), heads(nkc, 2, HEAD_DIM), heads(nks, 2, HEAD_DIM),
                                         heads(nkw, 2, HEAD_DIM), heads(ngate, H_NSA, 3), past_nsa, win_buf,
                                         nsa_gain, nsa_pe, nsa_w, offset)
    o_hg, hg_state = hgrn2_mixer(heads(hq, H_HGRN, HGRN_DK), heads(hf, H_HGRN, HGRN_DK), heads(hi, H_HGRN, HGRN_DV),
                                 heads(hg, H_HGRN, HGRN_DV), hgrn_s0, lb, hgrn_gain)
    o_mb, mb_rows = moba_mixer(heads(mq, H_MOBA, HEAD_DIM), heads(mk, H_MOBA, HEAD_DIM), heads(mv, H_MOBA, HEAD_DIM),
                               past_moba, moba_gain, offset)
    o_fx, fx_rows, fx_logf = fox_mixer(heads(fq, H_FOX, HEAD_DIM), heads(fk, H_FOX, HEAD_DIM), heads(fv, H_FOX, HEAD_DIM),
                                       ff, past_fox_kv, past_fox_logf, fox_gain, fox_bias, offset)
    o = jnp.concatenate([o_nsa, o_hg, o_mb, o_fx], axis=2).reshape(B, T, MIX_WIDTH)
    x = x + mm3(o, w_out)
    x = x + ffn(rmsnorm(x, g_ffn))
    return x, (nsa_rows, nsa_win, hg_state.astype(x.dtype), mb_rows, fx_rows, fx_logf)


def kernel(x_prompt, x_sample, cache_nsa, state_nsa_win, state_hgrn, cache_moba, cache_fox_kv, cache_fox_logf,
           page_table, g_mix, g_ffn, w_in, w_out, nsa_qk_gain, nsa_cmp_pe, nsa_cmp_w, hgrn_lb_logits,
           hgrn_out_gain, moba_qk_gain, fox_qk_gain, fox_f_bias, ffn_w1, ffn_w3, ffn_w2, moe_router,
           moe_w1, moe_w3, moe_w2):
    dt = x_prompt.dtype
    Bp = x_prompt.shape[0]
    Bs = x_sample.shape[0]
    past_len = page_table.shape[1] * PAGE_SIZE
    lb_w = jax.nn.softmax(hgrn_lb_logits.astype(jnp.float32), axis=0)
    lower_bounds = jnp.cumsum(lb_w, axis=0) - lb_w[0:1]

    def gather_pages(pool):
        g = pool[page_table]
        return g.reshape((Bs, past_len) + pool.shape[2:])

    xp, xs = x_prompt, x_sample
    st_p, st_s = [], []
    for l in range(DEPTH):
        i = l // 2
        if l % 2 == 0:
            ffn = functools.partial(swiglu_dense, w1=ffn_w1[i], w3=ffn_w3[i], w2=ffn_w2[i])
        else:
            ffn = functools.partial(moe_ffn, router=moe_router[i], w1=moe_w1[i], w3=moe_w3[i], w2=moe_w2[i])
        lw = (g_mix[l], g_ffn[l], w_in[l], w_out[l], nsa_qk_gain[l], nsa_cmp_pe[l], nsa_cmp_w[l], lower_bounds[l],
              hgrn_out_gain[l], moba_qk_gain[l], fox_qk_gain[l], fox_f_bias[l], ffn)
        xp, sp = trunk_layer(xp, 0,
                             jnp.zeros((Bp, 0, NSA_ROWS, HEAD_DIM), dt), jnp.zeros((Bp, 0, 2, HEAD_DIM), dt),
                             jnp.zeros((Bp, H_HGRN, HGRN_DK, HGRN_DV), jnp.float32),
                             jnp.zeros((Bp, 0, 2, H_MOBA, HEAD_DIM), dt), jnp.zeros((Bp, 0, 2, H_FOX, HEAD_DIM), dt),
                             jnp.zeros((Bp, 0, H_FOX), dt), *lw)
        xs, ss = trunk_layer(xs, past_len,
                             gather_pages(cache_nsa[l]), state_nsa_win[l], state_hgrn[l].astype(jnp.float32),
                             gather_pages(cache_moba[l]), gather_pages(cache_fox_kv[l]),
                             gather_pages(cache_fox_logf[l]), *lw)
        st_p.append(sp)
        st_s.append(ss)

    def stk(states, j):
        return jnp.stack([s[j] for s in states], axis=0)

    return (xp, xs,
            stk(st_p, 0), stk(st_s, 0), stk(st_p, 1), stk(st_s, 1), stk(st_p, 2), stk(st_s, 2),
            stk(st_p, 3), stk(st_s, 3), stk(st_p, 4), stk(st_s, 4), stk(st_p, 5), stk(st_s, 5))
```

```python
import math, functools
import jax, jax.numpy as jnp
from jax import lax
import numpy as np
from jax.experimental import pallas as pl
from jax.experimental.pallas import tpu as pltpu

D_MODEL = 1024
DEPTH = 2
PAGE_SIZE = 128
HEAD_DIM = 64
H_NSA = 4
H_HGRN = 4
H_MOBA = 4
H_FOX = 4
NH = 4
HW = NH * HEAD_DIM
MIX_WIDTH = (H_NSA + H_HGRN + H_MOBA + H_FOX) * HEAD_DIM
HGRN_DK = 64
HGRN_DV = HEAD_DIM
HGRN_CHUNK = 64
NSA_CMP_BLOCK = 32
NSA_CMP_STRIDE = 16
NSA_SEL_BLOCK = 64
NSA_TOPN = 16
NSA_WINDOW = 512
NSA_ROWS = 4
MOBA_BLOCK = 256
MOBA_TOPK = 3
ROPE_THETA = 10000.0
Q_BLOCK = 128
GATHER_Q_BLOCK = 32
N_EXPERTS = 8
TOP_K = 2
MOE_BLOCK = 128
RMS_EPS = 1e-6
NEG_INF = -1e30
SEL_FORCE = 1e6
ATTN_SCALE = HEAD_DIM ** -0.5
IN_SIZES = (H_NSA * HEAD_DIM, 2 * HEAD_DIM, 2 * HEAD_DIM, 2 * HEAD_DIM, 3 * H_NSA,
            H_HGRN * HGRN_DK, H_HGRN * HGRN_DK, H_HGRN * HGRN_DV, H_HGRN * HGRN_DV,
            H_MOBA * HEAD_DIM, H_MOBA * HEAD_DIM, H_MOBA * HEAD_DIM,
            H_FOX * HEAD_DIM, H_FOX * HEAD_DIM, H_FOX * HEAD_DIM, H_FOX)
N_IN = sum(IN_SIZES)
IN_OFFS = tuple(int(v) for v in np.cumsum((0,) + IN_SIZES))

N_SEC = 13
N_INP = N_SEC * HW
SEC_NQ, SEC_S1, SEC_S2, SEC_HQ, SEC_HF, SEC_HI, SEC_HG = 0, 1, 2, 3, 4, 5, 6
SEC_MQ, SEC_MK, SEC_MV, SEC_FQ, SEC_FK, SEC_FV = 7, 8, 9, 10, 11, 12
MISC_LANE = 128

LANES = 128
VMEM_LIMIT = 48 * 1024 * 1024
PREP_TQ = 256
NSA_TQ = 128
NSA_TK = 256
BF = jnp.bfloat16
F32 = jnp.float32


def _round_up(x, m):
    return -(-x // m) * m


def _cparams(sem):
    return pltpu.CompilerParams(dimension_semantics=sem, vmem_limit_bytes=VMEM_LIMIT)


def _dot(a, b):
    return jnp.dot(a, b, preferred_element_type=F32)


def _dot_nt(a, b):
    return lax.dot_general(a, b, (((1,), (1,)), ((), ())), preferred_element_type=F32)


def _mm_kernel(a_ref, b_ref, o_ref):
    k = pl.program_id(2)
    acc = _dot(a_ref[...].astype(BF), b_ref[...].astype(BF))

    @pl.when(k == 0)
    def _():
        o_ref[...] = acc

    @pl.when(k != 0)
    def _():
        o_ref[...] += acc


def matmul(a, b, tm=512, tn=512, tk=1024):
    M, K = a.shape
    _, N = b.shape
    tm = min(tm, _round_up(M, 8))
    Mp, Np = _round_up(M, tm), _round_up(N, tn)
    if K % tk:
        tk = K
    if Mp != M:
        a = jnp.pad(a, ((0, Mp - M), (0, 0)))
    if Np != N:
        b = jnp.pad(b, ((0, 0), (0, Np - N)))
    out = pl.pallas_call(
        _mm_kernel,
        grid=(Mp // tm, Np // tn, K // tk),
        in_specs=[pl.BlockSpec((tm, tk), lambda i, j, k: (i, k)),
                  pl.BlockSpec((tk, tn), lambda i, j, k: (k, j))],
        out_specs=pl.BlockSpec((tm, tn), lambda i, j, k: (i, j)),
        out_shape=jax.ShapeDtypeStruct((Mp, Np), F32),
        compiler_params=_cparams(("parallel", "parallel", "arbitrary")),
        name="dense_matmul",
    )(a, b)
    return out[:M, :N]


def _inproj_kernel(x_ref, g_ref, w_ref, o_ref):
    x = x_ref[...]
    y = x * lax.rsqrt(jnp.mean(x * x, axis=-1, keepdims=True) + RMS_EPS) * g_ref[...]
    o_ref[...] = _dot(y.astype(BF), w_ref[...])


def in_projection(x, g, w_bf, tm=256):
    N, D = x.shape
    return pl.pallas_call(
        _inproj_kernel,
        grid=(N // tm,),
        in_specs=[pl.BlockSpec((tm, D), lambda i: (i, 0)),
                  pl.BlockSpec((1, D), lambda i: (0, 0)),
                  pl.BlockSpec((D, N_INP), lambda i: (0, 0))],
        out_specs=pl.BlockSpec((tm, N_INP), lambda i: (i, 0)),
        out_shape=jax.ShapeDtypeStruct((N, N_INP), F32),
        compiler_params=_cparams(("parallel",)),
        name="rmsnorm_in_projection",
    )(x, g, w_bf)


def relayout_w_in(w):
    def cols(i):
        return w[:, IN_OFFS[i]:IN_OFFS[i + 1]]
    pad = jnp.zeros((w.shape[0], HW - 2 * HEAD_DIM - IN_SIZES[4] - IN_SIZES[15]), w.dtype)
    parts = [cols(0), cols(1), cols(2), cols(3), cols(4), cols(15), pad] + [cols(i) for i in range(5, 15)]
    return jnp.concatenate(parts, axis=1).astype(BF)


def _head_meansq(x, gmat):
    sq = x * x
    hi = sq.astype(BF)
    lo = (sq - hi.astype(F32)).astype(BF)
    return _dot(hi, gmat) + _dot(lo, gmat)


def _head_rmsnorm(x, gain, gmat):
    return x * lax.rsqrt(_head_meansq(x, gmat) + RMS_EPS) * gain


def _rope(x, cos, sin_signed, lo_half):
    w = x.shape[1]
    swapped = jnp.where(lo_half, pltpu.roll(x, w - HEAD_DIM // 2, 1), pltpu.roll(x, HEAD_DIM // 2, 1))
    return x * cos + swapped * sin_signed


def _store_heads(ref, x):
    for h in range(NH):
        ref[0, h] = x[:, h * HEAD_DIM:(h + 1) * HEAD_DIM].astype(ref.dtype)


def _prep_kernel(nq_ref, s1_ref, s2_ref, mq_ref, mk_ref, mv_ref, fq_ref, fk_ref, fv_ref,
                 cos_ref, sin_ref, gains_ref, gmat_ref,
                 nsa_rows_ref, nsa_kc_ref, nsa_win_ref, moba_rows_ref, fox_rows_ref,
                 nqn_ref, nqr_ref, nks_ref, nvs_ref, nkw_ref, nvw_ref,
                 mqr_ref, mkr_ref, mvv_ref, kmean_ref, fqn_ref, fkn_ref, fvv_ref):
    cos = cos_ref[...]
    sin = sin_ref[...]
    gmat = gmat_ref[...]
    t = cos.shape[0]
    lane = lax.broadcasted_iota(jnp.int32, (t, HW), 1)
    lo_half = (lane % HEAD_DIM) < (HEAD_DIM // 2)
    gains = gains_ref[...]

    qn = _head_rmsnorm(nq_ref[...], gains[0:1], gmat)
    qr = _rope(qn, cos, sin, lo_half)
    _store_heads(nqn_ref, qn * ATTN_SCALE)
    _store_heads(nqr_ref, qr * ATTN_SCALE)

    s1 = s1_ref[...]
    s1r = _rope(_head_rmsnorm(s1, gains[1:2], gmat), cos, sin, lo_half)
    third = (lane >= 2 * HEAD_DIM) & (lane < 3 * HEAD_DIM)
    rows = jnp.where(third, s1r, s1)
    nsa_rows_ref[...] = rows
    nsa_kc_ref[...] = rows[:, :2 * HEAD_DIM]
    nks_ref[0] = rows[:, 2 * HEAD_DIM:3 * HEAD_DIM].astype(BF)
    nvs_ref[0] = rows[:, 3 * HEAD_DIM:].astype(BF)

    s2 = s2_ref[...]
    s2r = _rope(_head_rmsnorm(s2, gains[2:3], gmat), cos, sin, lo_half)
    wrows = jnp.where(lane < HEAD_DIM, s2r, s2)
    nsa_win_ref[...] = wrows[:, :2 * HEAD_DIM]
    nkw_ref[0] = wrows[:, :HEAD_DIM].astype(BF)
    nvw_ref[0] = wrows[:, HEAD_DIM:2 * HEAD_DIM].astype(BF)

    mq = _rope(_head_rmsnorm(mq_ref[...], gains[3:4], gmat), cos, sin, lo_half)
    mk = _rope(_head_rmsnorm(mk_ref[...], gains[4:5], gmat), cos, sin, lo_half)
    mv = mv_ref[...]
    _store_heads(mqr_ref, mq * ATTN_SCALE)
    _store_heads(mkr_ref, mk)
    _store_heads(mvv_ref, mv)
    moba_rows_ref[:, :HW] = mk
    moba_rows_ref[:, HW:] = mv
    kmean_ref[0, 0] = jnp.mean(mk, axis=0, keepdims=True)

    fq = _head_rmsnorm(fq_ref[...], gains[5:6], gmat)
    fk = _head_rmsnorm(fk_ref[...], gains[6:7], gmat)
    fv = fv_ref[...]
    _store_heads(fqn_ref, fq * ATTN_SCALE)
    _store_heads(fkn_ref, fk)
    _store_heads(fvv_ref, fv)
    fox_rows_ref[:, :HW] = fk
    fox_rows_ref[:, HW:] = fv


def rope_tables(pos):
    half = HEAD_DIM // 2
    inv = ROPE_THETA ** (-jnp.arange(half, dtype=F32) / half)
    ang = pos.astype(F32)[:, None] * inv[None, :]
    cos = jnp.cos(ang)
    sin = jnp.sin(ang)
    cos_h = jnp.concatenate([cos, cos], axis=1)
    sin_h = jnp.concatenate([-sin, sin], axis=1)
    return jnp.tile(cos_h, (1, NH)), jnp.tile(sin_h, (1, NH))


def head_gains(nsa_gain, moba_gain, fox_gain):
    one = jnp.ones((HEAD_DIM,), F32)
    t4 = lambda g: jnp.tile(g, NH)
    rows = [t4(nsa_gain[0]),
            jnp.concatenate([one, one, nsa_gain[2], one]),
            jnp.concatenate([nsa_gain[3], one, one, one]),
            t4(moba_gain[0]), t4(moba_gain[1]), t4(fox_gain[0]), t4(fox_gain[1]), t4(one)]
    return jnp.stack(rows, axis=0)


def group_mean_matrix():
    idx = np.arange(HW) // HEAD_DIM
    return jnp.asarray((idx[:, None] == idx[None, :]).astype(np.float32) / HEAD_DIM, dtype=BF)


def prep_prompt(z, row0, B, T, cos, sin, gains, gmat):
    tq = PREP_TQ
    nq = T // tq
    rb0 = row0 // tq

    def sec(c):
        return pl.BlockSpec((tq, HW), lambda b, i, c=c: (rb0 + b * nq + i, c))

    flat = lambda w: pl.BlockSpec((tq, w), lambda b, i: (b * nq + i, 0))
    headmaj = pl.BlockSpec((1, NH, tq, HEAD_DIM), lambda b, i: (b, 0, i, 0))
    single = pl.BlockSpec((1, tq, HEAD_DIM), lambda b, i: (b, i, 0))
    N = B * T
    sd = jax.ShapeDtypeStruct
    hm_shape = sd((B, NH, T, HEAD_DIM), BF)
    sg_shape = sd((B, T, HEAD_DIM), BF)
    outs = pl.pallas_call(
        _prep_kernel,
        grid=(B, nq),
        in_specs=[sec(SEC_NQ), sec(SEC_S1), sec(SEC_S2), sec(SEC_MQ), sec(SEC_MK), sec(SEC_MV),
                  sec(SEC_FQ), sec(SEC_FK), sec(SEC_FV),
                  pl.BlockSpec((tq, HW), lambda b, i: (i, 0)),
                  pl.BlockSpec((tq, HW), lambda b, i: (i, 0)),
                  pl.BlockSpec((8, HW), lambda b, i: (0, 0)),
                  pl.BlockSpec((HW, HW), lambda b, i: (0, 0))],
        out_specs=[flat(HW), flat(2 * HEAD_DIM), flat(2 * HEAD_DIM), flat(2 * HW), flat(2 * HW),
                   headmaj, headmaj, single, single, single, single,
                   headmaj, headmaj, headmaj,
                   pl.BlockSpec((1, 1, 1, HW), lambda b, i: (b, i, 0, 0)),
                   headmaj, headmaj, headmaj],
        out_shape=[sd((N, HW), F32), sd((N, 2 * HEAD_DIM), F32), sd((N, 2 * HEAD_DIM), F32),
                   sd((N, 2 * HW), F32), sd((N, 2 * HW), F32),
                   hm_shape, hm_shape, sg_shape, sg_shape, sg_shape, sg_shape,
                   hm_shape, hm_shape, hm_shape,
                   sd((B, nq, 1, HW), F32),
                   hm_shape, hm_shape, hm_shape],
        compiler_params=_cparams(("parallel", "parallel")),
        name="mixer_prep",
    )(z, z, z, z, z, z, z, z, z, cos, sin, gains, gmat)
    keys = ("nsa_rows", "nsa_kc", "nsa_win", "moba_rows", "fox_rows",
            "nqn", "nqr", "nks", "nvs", "nkw", "nvw", "mqr", "mkr", "mvv", "kmean", "fqn", "fkn", "fvv")
    return dict(zip(keys, outs))


def _nsa_compress_kernel(x_ref, pe_ref, w_ref, gain_ref, k_ref, v_ref):
    x = x_ref[0]
    a = _dot((x + pe_ref[0:1]).astype(BF), w_ref[0])
    b = _dot((x + pe_ref[1:2]).astype(BF), w_ref[1])
    nch = x.shape[0]
    y = a + pltpu.roll(b, nch - 1, 0)
    lane = lax.broadcasted_iota(jnp.int32, y.shape, 1)
    ms = jnp.sum(jnp.where(lane < HEAD_DIM, y * y, 0.0), axis=1, keepdims=True) / HEAD_DIM
    kn = y * lax.rsqrt(ms + RMS_EPS) * gain_ref[...]
    k_ref[0] = kn[:, :HEAD_DIM].astype(BF)
    v_ref[0] = y[:, HEAD_DIM:].astype(BF)


def nsa_compress_weights(cmp_pe, cmp_w, gain1):
    S = NSA_CMP_STRIDE
    pe = cmp_pe.reshape(2, 2, S, HEAD_DIM)
    pe_flat = jnp.transpose(pe, (1, 2, 0, 3)).reshape(2, S * 2 * HEAD_DIM)
    w = cmp_w.reshape(2, 2, S, HEAD_DIM, HEAD_DIM)
    wz = jnp.zeros((2, S, 2, HEAD_DIM, 2, HEAD_DIM), F32)
    wz = wz.at[:, :, 0, :, 0, :].set(w[0]).at[:, :, 1, :, 1, :].set(w[1])
    w_flat = wz.reshape(2, S * 2 * HEAD_DIM, 2 * HEAD_DIM).astype(BF)
    gain = jnp.concatenate([gain1, jnp.ones((HEAD_DIM,), F32)])[None, :]
    return pe_flat, w_flat, gain


def nsa_compress(kc, B, T, pe_flat, w_flat, gain):
    nch = T // NSA_CMP_STRIDE
    cw = NSA_CMP_STRIDE * 2 * HEAD_DIM
    x = kc.reshape(B, nch, cw)
    out_spec = pl.BlockSpec((1, nch, HEAD_DIM), lambda b: (b, 0, 0))
    return pl.pallas_call(
        _nsa_compress_kernel,
        grid=(B,),
        in_specs=[pl.BlockSpec((1, nch, cw), lambda b: (b, 0, 0)),
                  pl.BlockSpec((2, cw), lambda b: (0, 0)),
                  pl.BlockSpec((2, cw, 2 * HEAD_DIM), lambda b: (0, 0, 0)),
                  pl.BlockSpec((1, 2 * HEAD_DIM), lambda b: (0, 0))],
        out_specs=[out_spec, out_spec],
        out_shape=[jax.ShapeDtypeStruct((B, nch, HEAD_DIM), BF)] * 2,
        compiler_params=_cparams(("parallel",)),
        name="nsa_compress",
    )(x, pe_flat, w_flat, gain)


def _softmax_update(s, mask, m_i, l_i, acc, v, lead):
    s = jnp.where(mask, s, NEG_INF)
    m_new = jnp.maximum(m_i, jnp.max(s, axis=-1, keepdims=True))
    p = jnp.where(mask, jnp.exp(s - m_new), 0.0)
    alpha = jnp.exp(m_i - m_new)
    l_new = alpha * l_i + jnp.sum(p, axis=-1, keepdims=True)
    pv = _dot(p.astype(BF).reshape(lead, p.shape[-1]), v).reshape(acc.shape)
    return m_new, l_new, alpha * acc + pv


def _nsa_attn_kernel(qn_ref, qr_ref, kc_ref, vc_ref, ks_ref, vs_ref, kw_ref, vw_ref, gl_ref,
                     cover_ref, expand_ref, o_ref, *, T):
    tq, tk = NSA_TQ, NSA_TK
    M = NH * tq
    i = pl.program_id(1)
    p0 = i * tq
    qn = qn_ref[0].reshape(M, HEAD_DIM)
    qr = qr_ref[0].reshape(M, HEAD_DIM)
    pos = p0 + lax.broadcasted_iota(jnp.int32, (tq, 1), 0)

    nch = kc_ref.shape[1]
    s_c = _dot_nt(qn, kc_ref[0]).reshape(NH, tq, nch)
    n_idx = lax.broadcasted_iota(jnp.int32, (tq, nch), 1)
    mask_c = (n_idx * NSA_CMP_STRIDE + (NSA_CMP_BLOCK - 1) <= pos)[None]
    s_c = jnp.where(mask_c, s_c, NEG_INF)
    m_c = jnp.max(s_c, axis=-1, keepdims=True)
    p_c = jnp.where(mask_c, jnp.exp(s_c - m_c), 0.0)
    p_c = p_c / jnp.maximum(jnp.sum(p_c, axis=-1, keepdims=True), 1.0)
    p_cb = p_c.astype(BF).reshape(M, nch)
    o_cmp = _dot(p_cb, vc_ref[0]).reshape(NH, tq, HEAD_DIM)
    imp = jnp.sum(_dot(p_cb, cover_ref[...]).reshape(NH, tq, LANES), axis=0)

    j = lax.broadcasted_iota(jnp.int32, (tq, LANES), 1)
    cur = pos // NSA_SEL_BLOCK
    forced = (j == 0) | (j == cur) | (j == cur - 1)
    imp = jnp.where(j > cur, NEG_INF, jnp.where(forced, SEL_FORCE, imp))
    n_sel = T // NSA_SEL_BLOCK
    rank = jnp.zeros((tq, LANES), F32)
    for jp in range(n_sel):
        col = imp[:, jp:jp + 1]
        beats = (col > imp) | ((col == imp) & (j > jp))
        rank = rank + jnp.where(beats, 1.0, 0.0)
    sel = jnp.where((rank < NSA_TOPN) & (j <= cur), 1.0, 0.0).astype(BF)

    def sel_body(kj, carry):
        m_i, l_i, acc = carry
        off = pl.multiple_of(kj * tk, tk)
        k = ks_ref[0, pl.ds(off, tk), :]
        v = vs_ref[0, pl.ds(off, tk), :]
        s = _dot_nt(qr, k).reshape(NH, tq, tk)
        e = _dot(sel, expand_ref[kj])
        kpos = off + lax.broadcasted_iota(jnp.int32, (tq, tk), 1)
        mask = ((e > 0.5) & (kpos <= pos))[None]
        return _softmax_update(s, mask, m_i, l_i, acc, v, M)

    n_kt = (p0 + tq - 1) // tk + 1
    init = (jnp.full((NH, tq, 1), NEG_INF, F32), jnp.zeros((NH, tq, 1), F32),
            jnp.zeros((NH, tq, HEAD_DIM), F32))
    _, l_s, acc_s = lax.fori_loop(0, n_kt, sel_body, init)
    o_sel = acc_s / jnp.maximum(l_s, 1.0)

    band = NSA_WINDOW + tq
    start = pl.multiple_of(jnp.maximum(p0 - NSA_WINDOW, 0), tq)
    kw = kw_ref[0, pl.ds(start, band), :]
    vw = vw_ref[0, pl.ds(start, band), :]
    s_w = _dot_nt(qr, kw).reshape(NH, tq, band)
    kpos = start + lax.broadcasted_iota(jnp.int32, (tq, band), 1)
    mask_w = ((kpos <= pos) & (kpos > pos - NSA_WINDOW))[None]
    s_w = jnp.where(mask_w, s_w, NEG_INF)
    m_w = jnp.max(s_w, axis=-1, keepdims=True)
    p_w = jnp.where(mask_w, jnp.exp(s_w - m_w), 0.0)
    l_w = jnp.sum(p_w, axis=-1, keepdims=True)
    o_win = _dot(p_w.astype(BF).reshape(M, band), vw).reshape(NH, tq, HEAD_DIM) / jnp.maximum(l_w, 1.0)

    sig = jax.nn.sigmoid(gl_ref[...])
    outs = []
    for h in range(NH):
        c = MISC_LANE + 3 * h
        outs.append(sig[:, c:c + 1] * o_cmp[h] + sig[:, c + 1:c + 2] * o_sel[h] + sig[:, c + 2:c + 3] * o_win[h])
    o_ref[...] = jnp.concatenate(outs, axis=1).astype(o_ref.dtype)


def nsa_constants(T):
    nch = T // NSA_CMP_STRIDE
    n_cmp = (T - NSA_CMP_BLOCK) // NSA_CMP_STRIDE + 1
    ci = np.arange(nch)[:, None] * NSA_CMP_STRIDE
    sj = np.arange(LANES)[None, :] * NSA_SEL_BLOCK
    cover = (ci < sj + NSA_SEL_BLOCK) & (ci + NSA_CMP_BLOCK > sj) & (np.arange(nch)[:, None] < n_cmp)
    nkt = T // NSA_TK
    blk = (np.arange(nkt)[:, None, None] * NSA_TK + np.arange(NSA_TK)[None, None, :]) // NSA_SEL_BLOCK
    expand = blk == np.arange(LANES)[None, :, None]
    return jnp.asarray(cover.astype(np.float32), dtype=BF), jnp.asarray(expand.astype(np.float32), dtype=BF)


def nsa_attention_prompt(pp, kc, vc, z, row0, B, T, cover, expand):
    tq = NSA_TQ
    nq = T // tq
    rb0 = row0 // tq
    nch = T // NSA_CMP_STRIDE
    assert T >= NSA_WINDOW + tq and T % NSA_TK == 0
    headmaj = pl.BlockSpec((1, NH, tq, HEAD_DIM), lambda b, i: (b, 0, i, 0))
    full1 = pl.BlockSpec((1, T, HEAD_DIM), lambda b, i: (b, 0, 0))
    cmp1 = pl.BlockSpec((1, nch, HEAD_DIM), lambda b, i: (b, 0, 0))
    return pl.pallas_call(
        functools.partial(_nsa_attn_kernel, T=T),
        grid=(B, nq),
        in_specs=[headmaj, headmaj, cmp1, cmp1, full1, full1, full1, full1,
                  pl.BlockSpec((tq, HW), lambda b, i: (rb0 + b * nq + i, SEC_S2)),
                  pl.BlockSpec((nch, LANES), lambda b, i: (0, 0)),
                  pl.BlockSpec((T // NSA_TK, LANES, NSA_TK), lambda b, i: (0, 0, 0))],
        out_specs=pl.BlockSpec((tq, HW), lambda b, i: (b * nq + i, 0)),
        out_shape=jax.ShapeDtypeStruct((B * T, HW), BF),
        compiler_params=_cparams(("parallel", "parallel")),
        name="nsa_attention",
    )(pp["nqn"], pp["nqr"], kc, vc, pp["nks"], pp["nvs"], pp["nkw"], pp["nvw"], z, cover, expand)


def _moba_attn_kernel(q_ref, k_ref, v_ref, km_ref, o_ref):
    tq = tk = MOBA_BLOCK
    qi = pl.program_id(1)
    lane = lax.broadcasted_iota(jnp.int32, (tq, LANES), 1)
    row = lax.broadcasted_iota(jnp.int32, (tq, tk), 0)
    colk = lax.broadcasted_iota(jnp.int32, (tq, tk), 1)
    causal = colk <= row
    outs = []
    for h in range(NH):
        q = q_ref[0, h]
        g = jnp.where(lane < qi, _dot_nt(q, km_ref[0, h]), NEG_INF)
        sel = jnp.zeros((tq, LANES), F32)
        for _ in range(MOBA_TOPK):
            m = jnp.max(g, axis=1, keepdims=True)
            idx = jnp.min(jnp.where(g == m, lane, LANES), axis=1, keepdims=True)
            hit = lane == idx
            sel = jnp.where(hit & (lane < qi), 1.0, sel)
            g = jnp.where(hit, NEG_INF, g)

        def body(kj, carry, q=q, sel=sel, h=h):
            m_i, l_i, acc = carry
            off = pl.multiple_of(kj * tk, tk)
            k = k_ref[0, h, pl.ds(off, tk), :]
            v = v_ref[0, h, pl.ds(off, tk), :]
            s = _dot_nt(q, k)
            picked = jnp.max(jnp.where(lane == kj, sel, 0.0), axis=1, keepdims=True) > 0.5
            mask = jnp.broadcast_to(picked, (tq, tk))
            return _softmax_update(s, mask, m_i, l_i, acc, v, tq)

        init = (jnp.full((tq, 1), NEG_INF, F32), jnp.zeros((tq, 1), F32), jnp.zeros((tq, HEAD_DIM), F32))
        m_i, l_i, acc = lax.fori_loop(0, qi, body, init)
        off = pl.multiple_of(qi * tk, tk)
        k = k_ref[0, h, pl.ds(off, tk), :]
        v = v_ref[0, h, pl.ds(off, tk), :]
        _, l_i, acc = _softmax_update(_dot_nt(q, k), causal, m_i, l_i, acc, v, tq)
        outs.append(acc / jnp.maximum(l_i, 1.0))
    o_ref[...] = jnp.concatenate(outs, axis=1).astype(o_ref.dtype)


def moba_attention_prompt(pp, B, T):
    tq = MOBA_BLOCK
    nq = T // tq
    km = pp["kmean"].reshape(B, nq, NH, HEAD_DIM).transpose(0, 2, 1, 3)
    km = jnp.pad(km, ((0, 0), (0, 0), (0, LANES - nq), (0, 0))).astype(BF)
    headq = pl.BlockSpec((1, NH, tq, HEAD_DIM), lambda b, i: (b, 0, i, 0))
    headfull = pl.BlockSpec((1, NH, T, HEAD_DIM), lambda b, i: (b, 0, 0, 0))
    return pl.pallas_call(
        _moba_attn_kernel,
        grid=(B, nq),
        in_specs=[headq, headfull, headfull,
                  pl.BlockSpec((1, NH, LANES, HEAD_DIM), lambda b, i: (b, 0, 0, 0))],
        out_specs=pl.BlockSpec((tq, HW), lambda b, i: (b * nq + i, 0)),
        out_shape=jax.ShapeDtypeStruct((B * T, HW), BF),
        compiler_params=_cparams(("parallel", "parallel")),
        name="moba_attention",
    )(pp["mqr"], pp["mkr"], pp["mvv"], km)


def _fox_attn_kernel(q_ref, k_ref, v_ref, cq_ref, ck_ref, o_ref):
    tq = tk = MOBA_BLOCK
    qi = pl.program_id(1)
    row = lax.broadcasted_iota(jnp.int32, (tq, tk), 0)
    colk = lax.broadcasted_iota(jnp.int32, (tq, tk), 1)
    causal = colk <= row
    everything = row >= 0
    outs = []
    for h in range(NH):
        q = q_ref[0, h]
        cq = cq_ref[0, h]

        def tile(kj, carry, mask, q=q, cq=cq, h=h):
            m_i, l_i, acc = carry
            off = pl.multiple_of(kj * tk, tk)
            k = k_ref[0, h, pl.ds(off, tk), :]
            v = v_ref[0, h, pl.ds(off, tk), :]
            s = _dot_nt(q, k) + cq - ck_ref[0, h, kj]
            return _softmax_update(s, mask, m_i, l_i, acc, v, tq)

        init = (jnp.full((tq, 1), NEG_INF, F32), jnp.zeros((tq, 1), F32), jnp.zeros((tq, HEAD_DIM), F32))
        carry = lax.fori_loop(0, qi, functools.partial(tile, mask=everything), init)
        _, l_i, acc = tile(qi, carry, causal)
        outs.append(acc / jnp.maximum(l_i, 1.0))
    o_ref[...] = jnp.concatenate(outs, axis=1).astype(o_ref.dtype)


def fox_attention_prompt(pp, c, B, T):
    tq = MOBA_BLOCK
    nq = T // tq
    ch = jnp.transpose(c, (0, 2, 1))
    cq = ch[..., None]
    ck = ch.reshape(B, NH, nq, 1, tq)
    headq = pl.BlockSpec((1, NH, tq, HEAD_DIM), lambda b, i: (b, 0, i, 0))
    headfull = pl.BlockSpec((1, NH, T, HEAD_DIM), lambda b, i: (b, 0, 0, 0))
    return pl.pallas_call(
        _fox_attn_kernel,
        grid=(B, nq),
        in_specs=[headq, headfull, headfull,
                  pl.BlockSpec((1, NH, tq, 1), lambda b, i: (b, 0, i, 0)),
                  pl.BlockSpec((1, NH, nq, 1, tq), lambda b, i: (b, 0, 0, 0, 0))],
        out_specs=pl.BlockSpec((tq, HW), lambda b, i: (b * nq + i, 0)),
        out_shape=jax.ShapeDtypeStruct((B * T, HW), BF),
        compiler_params=_cparams(("parallel", "parallel")),
        name="fox_attention",
    )(pp["fqn"], pp["fkn"], pp["fvv"], cq, ck)


def rmsnorm(x, g):
    xf = x.astype(jnp.float32)
    y = xf * lax.rsqrt(jnp.mean(xf * xf, axis=-1, keepdims=True) + RMS_EPS)
    return (y * g.astype(jnp.float32)).astype(x.dtype)


def rope(x, pos):
    half = HEAD_DIM // 2
    inv = ROPE_THETA ** (-jnp.arange(half, dtype=jnp.float32) / half)
    ang = pos.astype(jnp.float32)[:, None] * inv[None, :]
    cos = jnp.cos(ang)[:, None, :]
    sin = jnp.sin(ang)[:, None, :]
    xf = x.astype(jnp.float32)
    x1, x2 = xf[..., :half], xf[..., half:]
    return jnp.concatenate([x1 * cos - x2 * sin, x2 * cos + x1 * sin], axis=-1).astype(x.dtype)


def masked_softmax(s, mask):
    s = jnp.where(mask, s.astype(jnp.float32), NEG_INF)
    m = jnp.max(s, axis=-1, keepdims=True)
    p = jnp.where(mask, jnp.exp(s - m), 0.0)
    return p / jnp.maximum(jnp.sum(p, axis=-1, keepdims=True), 1.0)


def sweep(fn, blk, *arrays):
    B, T = arrays[0].shape[:2]
    nb = -(-T // blk)
    Tp = nb * blk
    blocks = []
    for a in arrays:
        a = jnp.pad(a, [(0, 0), (0, Tp - T)] + [(0, 0)] * (a.ndim - 2))
        blocks.append(jnp.moveaxis(a.reshape((B, nb, blk) + a.shape[2:]), 1, 0))
    starts = jnp.arange(nb, dtype=jnp.int32) * blk
    out = lax.map(lambda args: fn(args[0], *args[1]), (starts, tuple(blocks)))
    out = jnp.moveaxis(out, 0, 1).reshape((B, Tp) + out.shape[3:])
    return out[:, :T]


def window_attend(q, rows, buf_len):
    B, T, H, Dh = q.shape
    dt = q.dtype
    W = NSA_WINDOW
    qb = min(Q_BLOCK, T)
    nb = -(-T // qb)
    Tp = nb * qb
    band = W + qb
    rp = jnp.pad(rows, ((0, 0), (W, Tp - T), (0, 0), (0, 0)))
    kidx = buf_len + np.arange(nb)[:, None] * qb + np.arange(band)[None, :]
    kb = rp[:, kidx]
    qp = jnp.pad(q, ((0, 0), (0, Tp - T), (0, 0), (0, 0))).reshape(B, nb, qb, H, Dh)
    s = jnp.einsum('bnqhd,bnkd->bhnqk', qp, kb[..., 0, :], preferred_element_type=jnp.float32) * ATTN_SCALE
    qq = buf_len + np.arange(nb)[:, None] * qb + np.arange(qb)[None, :]
    kk = kidx - W
    mask = (kk[:, None, :] >= 0) & (kk[:, None, :] <= qq[:, :, None]) & (kk[:, None, :] > qq[:, :, None] - W)
    p = masked_softmax(s, mask)
    o = jnp.einsum('bhnqk,bnkd->bnqhd', p.astype(dt), kb[..., 1, :]).reshape(B, Tp, H, Dh)
    return o[:, :T]


def nsa_mixer(q, kv_c, kv_s, kv_w, gate_logits, past_rows, win_buf, qk_gain, cmp_pe, cmp_w, offset):
    B, T, H, Dh = q.shape
    dt = q.dtype
    pos = offset + jnp.arange(T, dtype=jnp.int32)
    qn = rmsnorm(q, qk_gain[0])
    qr = rope(qn, pos)
    k_s = rope(rmsnorm(kv_s[:, :, 0:1], qk_gain[2]), pos)[:, :, 0]
    k_w = rope(rmsnorm(kv_w[:, :, 0:1], qk_gain[3]), pos)[:, :, 0]
    new_rows = jnp.stack([kv_c[:, :, 0], kv_c[:, :, 1], k_s, kv_s[:, :, 1]], axis=2)
    rows = jnp.concatenate([past_rows, new_rows], axis=1)
    S = offset + T
    n_cmp = (S - NSA_CMP_BLOCK) // NSA_CMP_STRIDE + 1
    cidx = np.arange(n_cmp)[:, None] * NSA_CMP_STRIDE + np.arange(NSA_CMP_BLOCK)[None, :]

    def compress(r):
        blocks = rows[:, :, r][:, cidx] + cmp_pe[r]
        return blocks.reshape(B, n_cmp, NSA_CMP_BLOCK * Dh) @ cmp_w[r]

    k_cmp = rmsnorm(compress(0), qk_gain[1])
    v_cmp = compress(1)
    s_c = jnp.einsum('bthd,bnd->bhtn', qn, k_cmp, preferred_element_type=jnp.float32) * ATTN_SCALE
    cmp_end = np.arange(n_cmp) * NSA_CMP_STRIDE + NSA_CMP_BLOCK - 1
    p_c = masked_softmax(s_c, cmp_end[None, :] <= pos[:, None])
    o_cmp = jnp.einsum('bhtn,bnd->bthd', p_c.astype(dt), v_cmp)
    n_sel = -(-S // NSA_SEL_BLOCK)
    ci = np.arange(n_cmp)[:, None] * NSA_CMP_STRIDE
    sj = np.arange(n_sel)[None, :] * NSA_SEL_BLOCK
    cover = ((ci < sj + NSA_SEL_BLOCK) & (ci + NSA_CMP_BLOCK > sj)).astype(np.float32)
    imp = jnp.einsum('bhtn,nj->btj', p_c, jnp.asarray(cover))
    cur = (pos // NSA_SEL_BLOCK)[:, None]
    jj = jnp.arange(n_sel)[None, :]
    forced = (jj == 0) | (jj == cur) | (jj == cur - 1)
    imp = jnp.where(jj > cur, NEG_INF, jnp.where(forced, SEL_FORCE, imp))
    _, sel_idx = lax.top_k(imp, min(NSA_TOPN, n_sel))
    kv_sel = jnp.pad(rows[:, :, 2:4], ((0, 0), (0, n_sel * NSA_SEL_BLOCK - S), (0, 0), (0, 0)))
    kv_sel = kv_sel.reshape(B, n_sel, NSA_SEL_BLOCK, 2, Dh)
    bidx = jnp.arange(B)[:, None, None]

    def sel_block(start, q_blk, idx_blk):
        qb = q_blk.shape[1]
        tp = offset + start + jnp.arange(qb)
        g = kv_sel[bidx, idx_blk]
        kpos = idx_blk[..., None] * NSA_SEL_BLOCK + jnp.arange(NSA_SEL_BLOCK)
        mask = (kpos <= tp[None, :, None, None]).reshape(B, 1, qb, -1)
        g = g.reshape(B, qb, -1, 2, Dh)
        s = jnp.einsum('bqhd,bqkd->bhqk', q_blk, g[..., 0, :], preferred_element_type=jnp.float32) * ATTN_SCALE
        p = masked_softmax(s, mask)
        return jnp.einsum('bhqk,bqkd->bqhd', p.astype(dt), g[..., 1, :])

    o_sel = sweep(sel_block, min(GATHER_Q_BLOCK, T), qr, sel_idx)
    win_rows = jnp.concatenate([win_buf, jnp.stack([k_w, kv_w[:, :, 1]], axis=2)], axis=1)
    o_win = window_attend(qr, win_rows, win_buf.shape[1])
    gates = jax.nn.sigmoid(gate_logits.astype(jnp.float32)).astype(dt)
    o = gates[..., 0:1] * o_cmp + gates[..., 1:2] * o_sel + gates[..., 2:3] * o_win
    new_win = win_rows[:, -min(NSA_WINDOW, win_rows.shape[1]):]
    return o, new_rows, new_win


def gated_recurrence(q, k, v, logf, S0):
    B, T, H, DK = q.shape
    DV = v.shape[-1]
    C = math.gcd(T, HGRN_CHUNK)
    nc = T // C

    def chunks(a):
        return jnp.moveaxis(a.reshape((B, nc, C) + a.shape[2:]), 1, 0).swapaxes(2, 3)

    causal = jnp.tril(jnp.ones((C, C), dtype=bool))[:, :, None]

    def step(S, inp):
        qc, kc, vc, gc = inp
        b = jnp.cumsum(gc, axis=2)
        o_inter = jnp.einsum('bhtk,bhkv->bhtv', qc * jnp.exp(b), S)
        diff = b[:, :, :, None, :] - b[:, :, None, :, :]
        decay = jnp.where(causal, jnp.exp(jnp.where(causal, diff, 0.0)), 0.0)
        A = jnp.einsum('bhtk,bhsk,bhtsk->bhts', qc, kc, decay)
        o = o_inter + jnp.einsum('bhts,bhsv->bhtv', A, vc)
        b_last = b[:, :, -1:, :]
        S_new = jnp.exp(b_last[:, :, 0, :])[..., None] * S + jnp.einsum('bhsk,bhsv->bhkv', kc * jnp.exp(b_last - b), vc)
        return S_new, o

    S, o = lax.scan(step, S0, (chunks(q), chunks(k), chunks(v), chunks(logf)))
    o = jnp.moveaxis(o.swapaxes(2, 3), 0, 1).reshape(B, T, H, DV)
    return o, S


def hgrn2_mixer(q, f, i, g, S0, lb, out_gain):
    dt = q.dtype
    H = q.shape[2]
    lb = lb.reshape(H, HGRN_DK)
    z = f.astype(jnp.float32)
    logf = jnp.log(lb + (1.0 - lb) * jax.nn.sigmoid(z))
    k = (1.0 - lb) * jax.nn.sigmoid(-z)
    qf = jax.nn.silu(q.astype(jnp.float32))
    o, S = gated_recurrence(qf, k, i.astype(jnp.float32), logf, S0)
    o = rmsnorm(o, out_gain) * jax.nn.silu(g.astype(jnp.float32))
    return o.astype(dt), S


def moba_mixer(q, k, v, past_rows, qk_gain, offset):
    B, T, H, Dh = q.shape
    dt = q.dtype
    pos = offset + jnp.arange(T, dtype=jnp.int32)
    qr = rope(rmsnorm(q, qk_gain[0]), pos)
    kr = rope(rmsnorm(k, qk_gain[1]), pos)
    new_rows = jnp.stack([kr, v], axis=2)
    rows = jnp.concatenate([past_rows, new_rows], axis=1)
    S = offset + T
    nblk = -(-S // MOBA_BLOCK)
    kvb = jnp.pad(rows, ((0, 0), (0, nblk * MOBA_BLOCK - S), (0, 0), (0, 0), (0, 0)))
    kvb = jnp.transpose(kvb.reshape(B, nblk, MOBA_BLOCK, 2, H, Dh), (0, 4, 1, 2, 3, 5))
    kmean = jnp.mean(kvb[..., 0, :].astype(jnp.float32), axis=3)
    gate = jnp.einsum('bthd,bhnd->bthn', qr.astype(jnp.float32), kmean)
    own = pos // MOBA_BLOCK
    past_ok = jnp.arange(nblk)[None, :] < own[:, None]
    _, top = lax.top_k(jnp.where(past_ok[None, :, None, :], gate, NEG_INF), min(MOBA_TOPK, nblk))
    valid = top < own[None, :, None, None]
    idx = jnp.concatenate([top, jnp.broadcast_to(own[None, :, None, None], (B, T, H, 1)).astype(top.dtype)], axis=-1)
    ok = jnp.concatenate([valid, jnp.ones((B, T, H, 1), dtype=bool)], axis=-1)
    bidx = jnp.arange(B)[:, None, None, None]
    hidx = jnp.arange(H)[None, None, :, None]

    def blk_fn(start, q_blk, idx_blk, ok_blk):
        qb = q_blk.shape[1]
        tp = offset + start + jnp.arange(qb)
        g = kvb[bidx, hidx, idx_blk]
        kpos = idx_blk[..., None] * MOBA_BLOCK + jnp.arange(MOBA_BLOCK)
        mask = (ok_blk[..., None] & (kpos <= tp[None, :, None, None, None])).reshape(B, qb, H, -1)
        g = g.reshape(B, qb, H, -1, 2, Dh)
        s = jnp.einsum('bqhd,bqhkd->bqhk', q_blk, g[..., 0, :], preferred_element_type=jnp.float32) * ATTN_SCALE
        p = masked_softmax(s, mask)
        return jnp.einsum('bqhk,bqhkd->bqhd', p.astype(dt), g[..., 1, :])

    o = sweep(blk_fn, min(GATHER_Q_BLOCK, T), qr, idx, ok)
    return o, new_rows


def fox_mixer(q, k, v, f_logit, past_kv, past_logf, qk_gain, f_bias, offset):
    B, T, H, Dh = q.shape
    dt = q.dtype
    qn = rmsnorm(q, qk_gain[0])
    kn = rmsnorm(k, qk_gain[1])
    logf_new = jax.nn.log_sigmoid(f_logit.astype(jnp.float32) + f_bias.astype(jnp.float32))
    new_rows = jnp.stack([kn, v], axis=2)
    rows = jnp.concatenate([past_kv, new_rows], axis=1)
    c = jnp.cumsum(jnp.concatenate([past_logf.astype(jnp.float32), logf_new], axis=1), axis=1)
    S = offset + T
    K = rows[:, :, 0]
    V = rows[:, :, 1]
    c_k = jnp.moveaxis(c, 1, 2)[:, :, None, :]
    kpos = jnp.arange(S)

    def blk_fn(start, q_blk, cq_blk):
        qb = q_blk.shape[1]
        tp = offset + start + jnp.arange(qb)
        s = jnp.einsum('bqhd,bkhd->bhqk', q_blk, K, preferred_element_type=jnp.float32) * ATTN_SCALE
        s = s + jnp.moveaxis(cq_blk, 1, 2)[..., None] - c_k
        p = masked_softmax(s, kpos[None, :] <= tp[:, None])
        return jnp.einsum('bhqk,bkhd->bqhd', p.astype(dt), V)

    o = sweep(blk_fn, min(Q_BLOCK, T), qn, c[:, offset:])
    return o, new_rows, logf_new.astype(dt)


def swiglu_dense(x, w1, w3, w2):
    u = jax.nn.silu(matmul(x, w1)) * matmul(x, w3)
    return matmul(u, w2, tk=1408)


def swiglu(h, w1, w3, w2):
    return (jax.nn.silu(h @ w1) * (h @ w3)) @ w2


def moe_ffn(xf, router, w1, w3, w2):
    N, D = xf.shape
    dt = xf.dtype
    logits = (xf @ router).astype(jnp.float32)
    top_v, top_e = lax.top_k(logits, TOP_K)
    gates = jax.nn.softmax(top_v, axis=-1)
    NK = N * TOP_K
    flat_e = top_e.reshape(NK)
    flat_tok = jnp.arange(NK, dtype=jnp.int32) // TOP_K
    order = jnp.argsort(flat_e)
    e_sorted = flat_e[order]
    tok_sorted = flat_tok[order]
    counts = jnp.zeros((N_EXPERTS,), jnp.int32).at[flat_e].add(1)
    padded = (counts + MOE_BLOCK - 1) // MOE_BLOCK * MOE_BLOCK
    pend = jnp.cumsum(padded)
    pstart = pend - padded
    start = jnp.cumsum(counts) - counts
    dest = pstart[e_sorted] + (jnp.arange(NK, dtype=jnp.int32) - start[e_sorted])
    n_blocks = -(-NK // MOE_BLOCK) + N_EXPERTS
    slot_tok = jnp.full((n_blocks * MOE_BLOCK,), N, jnp.int32).at[dest].set(tok_sorted)
    block_exp = jnp.clip(jnp.searchsorted(pend, jnp.arange(n_blocks) * MOE_BLOCK, side='right'), 0, N_EXPERTS - 1)
    xpad = jnp.concatenate([xf, jnp.zeros((1, D), dt)], axis=0)
    xb = xpad[slot_tok].reshape(n_blocks, MOE_BLOCK, D)

    def expert_block(args):
        xblk, e = args
        return swiglu(xblk, w1[e], w3[e], w2[e])

    yb = lax.map(expert_block, (xb, block_exp)).reshape(n_blocks * MOE_BLOCK, D)
    y_assign = yb[dest] * gates.reshape(NK)[order][:, None].astype(dt)
    return jnp.zeros((N, D), dt).at[tok_sorted].add(y_assign)


def z_sections(z):
    s = lambda c, a=0, b=HW: z[..., c * HW + a:c * HW + b]
    d = HEAD_DIM
    return dict(nq=s(SEC_NQ), nkc=s(SEC_S1, 0, 2 * d), nks=s(SEC_S1, 2 * d, 4 * d), nkw=s(SEC_S2, 0, 2 * d),
                ngate=s(SEC_S2, MISC_LANE, MISC_LANE + 12), ff=s(SEC_S2, MISC_LANE + 12, MISC_LANE + 16),
                hq=s(SEC_HQ), hf=s(SEC_HF), hi=s(SEC_HI), hg=s(SEC_HG), mq=s(SEC_MQ), mk=s(SEC_MK), mv=s(SEC_MV),
                fq=s(SEC_FQ), fk=s(SEC_FK), fv=s(SEC_FV))


def kernel(x_prompt, x_sample, cache_nsa, state_nsa_win, state_hgrn, cache_moba, cache_fox_kv, cache_fox_logf,
           page_table, g_mix, g_ffn, w_in, w_out, nsa_qk_gain, nsa_cmp_pe, nsa_cmp_w, hgrn_lb_logits,
           hgrn_out_gain, moba_qk_gain, fox_qk_gain, fox_f_bias, ffn_w1, ffn_w3, ffn_w2, moe_router,
           moe_w1, moe_w3, moe_w2):
    dt = x_prompt.dtype
    Bp, Tp, D = x_prompt.shape
    Bs, Ts, _ = x_sample.shape
    Np, Ns = Bp * Tp, Bs * Ts
    past_len = page_table.shape[1] * PAGE_SIZE
    lb_w = jax.nn.softmax(hgrn_lb_logits.astype(jnp.float32), axis=0)
    lower_bounds = jnp.cumsum(lb_w, axis=0) - lb_w[0:1]

    def gather_pages(pool):
        g = pool[page_table]
        return g.reshape((Bs, past_len) + pool.shape[2:])

    cos_p, sin_p = rope_tables(jnp.arange(Tp, dtype=jnp.int32))
    gmat = group_mean_matrix()
    cover, expand = nsa_constants(Tp)

    x = jnp.concatenate([x_prompt.reshape(Np, D), x_sample.reshape(Ns, D)], axis=0)
    st_p, st_s = [], []
    for l in range(DEPTH):
        i = l // 2
        z = in_projection(x, g_mix[l][None, :], relayout_w_in(w_in[l]))

        pp = prep_prompt(z, 0, Bp, Tp, cos_p, sin_p,
                         head_gains(nsa_qk_gain[l], moba_qk_gain[l], fox_qk_gain[l]), gmat)
        pe_flat, w_flat, cgain = nsa_compress_weights(nsa_cmp_pe[l], nsa_cmp_w[l], nsa_qk_gain[l][1])
        kc, vc = nsa_compress(pp["nsa_kc"], Bp, Tp, pe_flat, w_flat, cgain)
        o_nsa_p = nsa_attention_prompt(pp, kc, vc, z, 0, Bp, Tp, cover, expand)
        o_mb_p = moba_attention_prompt(pp, Bp, Tp)
        zp = z_sections(z[:Np].reshape(Bp, Tp, N_INP))
        logf_p = jax.nn.log_sigmoid(zp["ff"] + fox_f_bias[l].astype(F32))
        o_fx_p = fox_attention_prompt(pp, jnp.cumsum(logf_p, axis=1), Bp, Tp)
        hd = lambda a, n, d, B, T: a.reshape(B, T, n, d)
        o_hg_p, hg_state_p = hgrn2_mixer(hd(zp["hq"], NH, HGRN_DK, Bp, Tp), hd(zp["hf"], NH, HGRN_DK, Bp, Tp),
                                         hd(zp["hi"], NH, HGRN_DV, Bp, Tp), hd(zp["hg"], NH, HGRN_DV, Bp, Tp),
                                         jnp.zeros((Bp, NH, HGRN_DK, HGRN_DV), F32), lower_bounds[l],
                                         hgrn_out_gain[l])
        o_p = jnp.concatenate([o_nsa_p.astype(F32), o_hg_p.reshape(Np, HW), o_mb_p.astype(F32),
                               o_fx_p.astype(F32)], axis=1)
        nsa_win_p = pp["nsa_win"].reshape(Bp, Tp, 2, HEAD_DIM)[:, -min(NSA_WINDOW, Tp):]
        st_p.append((pp["nsa_rows"].reshape(Bp, Tp, NSA_ROWS, HEAD_DIM), nsa_win_p, hg_state_p.astype(dt),
                     pp["moba_rows"].reshape(Bp, Tp, 2, NH, HEAD_DIM), pp["fox_rows"].reshape(Bp, Tp, 2, NH, HEAD_DIM),
                     logf_p.astype(dt)))

        zs = z_sections(z[Np:].reshape(Bs, Ts, N_INP))
        o_nsa, nsa_rows, nsa_win = nsa_mixer(hd(zs["nq"], NH, HEAD_DIM, Bs, Ts), hd(zs["nkc"], 2, HEAD_DIM, Bs, Ts),
                                             hd(zs["nks"], 2, HEAD_DIM, Bs, Ts), hd(zs["nkw"], 2, HEAD_DIM, Bs, Ts),
                                             hd(zs["ngate"], NH, 3, Bs, Ts), gather_pages(cache_nsa[l]),
                                             state_nsa_win[l], nsa_qk_gain[l], nsa_cmp_pe[l], nsa_cmp_w[l], past_len)
        o_hg, hg_state = hgrn2_mixer(hd(zs["hq"], NH, HGRN_DK, Bs, Ts), hd(zs["hf"], NH, HGRN_DK, Bs, Ts),
                                     hd(zs["hi"], NH, HGRN_DV, Bs, Ts), hd(zs["hg"], NH, HGRN_DV, Bs, Ts),
                                     state_hgrn[l].astype(F32), lower_bounds[l], hgrn_out_gain[l])
        o_mb, mb_rows = moba_mixer(hd(zs["mq"], NH, HEAD_DIM, Bs, Ts), hd(zs["mk"], NH, HEAD_DIM, Bs, Ts),
                                   hd(zs["mv"], NH, HEAD_DIM, Bs, Ts), gather_pages(cache_moba[l]),
                                   moba_qk_gain[l], past_len)
        o_fx, fx_rows, fx_logf = fox_mixer(hd(zs["fq"], NH, HEAD_DIM, Bs, Ts), hd(zs["fk"], NH, HEAD_DIM, Bs, Ts),
                                           hd(zs["fv"], NH, HEAD_DIM, Bs, Ts), zs["ff"], gather_pages(cache_fox_kv[l]),
                                           gather_pages(cache_fox_logf[l]), fox_qk_gain[l], fox_f_bias[l], past_len)
        o_s = jnp.concatenate([o_nsa, o_hg, o_mb, o_fx], axis=2).reshape(Ns, MIX_WIDTH)
        st_s.append((nsa_rows, nsa_win, hg_state.astype(dt), mb_rows, fx_rows, fx_logf))

        o = jnp.concatenate([o_p, o_s], axis=0)
        x = x + matmul(o, w_out[l])
        hn = rmsnorm(x, g_ffn[l])
        if l % 2 == 0:
            x = x + swiglu_dense(hn, ffn_w1[i], ffn_w3[i], ffn_w2[i])
        else:
            x = x + moe_ffn(hn, moe_router[i], moe_w1[i], moe_w3[i], moe_w2[i])

    def stk(states, j):
        return jnp.stack([s[j] for s in states], axis=0)

    return (x[:Np].reshape(Bp, Tp, D), x[Np:].reshape(Bs, Ts, D),
            stk(st_p, 0), stk(st_s, 0), stk(st_p, 1), stk(st_s, 1), stk(st_p, 2), stk(st_s, 2),
            stk(st_p, 3), stk(st_s, 3), stk(st_p, 4), stk(st_s, 4), stk(st_p, 5), stk(st_s, 5))
```

```python
import math, functools
import jax, jax.numpy as jnp
from jax import lax
import numpy as np
from jax.experimental import pallas as pl
from jax.experimental.pallas import tpu as pltpu

D_MODEL = 1024
DEPTH = 2
PAGE_SIZE = 128
HEAD_DIM = 64
H_NSA = 4
H_HGRN = 4
H_MOBA = 4
H_FOX = 4
NH = 4
HW = NH * HEAD_DIM
MIX_WIDTH = (H_NSA + H_HGRN + H_MOBA + H_FOX) * HEAD_DIM
HGRN_DK = 64
HGRN_DV = HEAD_DIM
HGRN_CHUNK = 64
NSA_CMP_BLOCK = 32
NSA_CMP_STRIDE = 16
NSA_SEL_BLOCK = 64
NSA_TOPN = 16
NSA_WINDOW = 512
NSA_ROWS = 4
MOBA_BLOCK = 256
MOBA_TOPK = 3
ROPE_THETA = 10000.0
Q_BLOCK = 128
GATHER_Q_BLOCK = 32
N_EXPERTS = 8
TOP_K = 2
MOE_BLOCK = 128
RMS_EPS = 1e-6
NEG_INF = -1e30
SEL_FORCE = 1e6
ATTN_SCALE = HEAD_DIM ** -0.5
IN_SIZES = (H_NSA * HEAD_DIM, 2 * HEAD_DIM, 2 * HEAD_DIM, 2 * HEAD_DIM, 3 * H_NSA,
            H_HGRN * HGRN_DK, H_HGRN * HGRN_DK, H_HGRN * HGRN_DV, H_HGRN * HGRN_DV,
            H_MOBA * HEAD_DIM, H_MOBA * HEAD_DIM, H_MOBA * HEAD_DIM,
            H_FOX * HEAD_DIM, H_FOX * HEAD_DIM, H_FOX * HEAD_DIM, H_FOX)
N_IN = sum(IN_SIZES)
IN_OFFS = tuple(int(v) for v in np.cumsum((0,) + IN_SIZES))

N_SEC = 13
N_INP = N_SEC * HW
SEC_NQ, SEC_S1, SEC_S2, SEC_HQ, SEC_HF, SEC_HI, SEC_HG = 0, 1, 2, 3, 4, 5, 6
SEC_MQ, SEC_MK, SEC_MV, SEC_FQ, SEC_FK, SEC_FV = 7, 8, 9, 10, 11, 12
MISC_LANE = 128

LANES = 128
VMEM_LIMIT = 48 * 1024 * 1024
PREP_TQ = 256
NSA_TQ = 128
NSA_TK = 256
MOE_TM = 256
MOE_TF = 896
DENSE_TF = 1408
FFN_TN = 512
BF = jnp.bfloat16
F32 = jnp.float32


def _round_up(x, m):
    return -(-x // m) * m


def _cparams(sem):
    return pltpu.CompilerParams(dimension_semantics=sem, vmem_limit_bytes=VMEM_LIMIT)


def _dot(a, b):
    return jnp.dot(a, b, preferred_element_type=F32)


def _dot_nt(a, b):
    return lax.dot_general(a, b, (((1,), (1,)), ((), ())), preferred_element_type=F32)


def _mm_kernel(a_ref, b_ref, o_ref):
    k = pl.program_id(2)
    acc = _dot(a_ref[...].astype(BF), b_ref[...].astype(BF))

    @pl.when(k == 0)
    def _():
        o_ref[...] = acc

    @pl.when(k != 0)
    def _():
        o_ref[...] += acc


def matmul(a, b, tm=512, tn=512, tk=1024):
    M, K = a.shape
    _, N = b.shape
    tm = min(tm, _round_up(M, 8))
    Mp, Np = _round_up(M, tm), _round_up(N, tn)
    if K % tk:
        tk = K
    if Mp != M:
        a = jnp.pad(a, ((0, Mp - M), (0, 0)))
    if Np != N:
        b = jnp.pad(b, ((0, 0), (0, Np - N)))
    out = pl.pallas_call(
        _mm_kernel,
        grid=(Mp // tm, Np // tn, K // tk),
        in_specs=[pl.BlockSpec((tm, tk), lambda i, j, k: (i, k)),
                  pl.BlockSpec((tk, tn), lambda i, j, k: (k, j))],
        out_specs=pl.BlockSpec((tm, tn), lambda i, j, k: (i, j)),
        out_shape=jax.ShapeDtypeStruct((Mp, Np), F32),
        compiler_params=_cparams(("parallel", "parallel", "arbitrary")),
        name="dense_matmul",
    )(a, b)
    return out[:M, :N]


def _inproj_kernel(x_ref, g_ref, w_ref, o_ref):
    x = x_ref[...]
    y = x * lax.rsqrt(jnp.mean(x * x, axis=-1, keepdims=True) + RMS_EPS) * g_ref[...]
    o_ref[...] = _dot(y.astype(BF), w_ref[...])


def in_projection(x, g, w_bf, tm=256):
    N, D = x.shape
    return pl.pallas_call(
        _inproj_kernel,
        grid=(N // tm,),
        in_specs=[pl.BlockSpec((tm, D), lambda i: (i, 0)),
                  pl.BlockSpec((1, D), lambda i: (0, 0)),
                  pl.BlockSpec((D, N_INP), lambda i: (0, 0))],
        out_specs=pl.BlockSpec((tm, N_INP), lambda i: (i, 0)),
        out_shape=jax.ShapeDtypeStruct((N, N_INP), F32),
        compiler_params=_cparams(("parallel",)),
        name="rmsnorm_in_projection",
    )(x, g, w_bf)


def relayout_w_in(w):
    def cols(i):
        return w[:, IN_OFFS[i]:IN_OFFS[i + 1]]
    pad = jnp.zeros((w.shape[0], HW - 2 * HEAD_DIM - IN_SIZES[4] - IN_SIZES[15]), w.dtype)
    parts = [cols(0), cols(1), cols(2), cols(3), cols(4), cols(15), pad] + [cols(i) for i in range(5, 15)]
    return jnp.concatenate(parts, axis=1).astype(BF)


def _head_meansq(x, gmat):
    sq = x * x
    hi = sq.astype(BF)
    lo = (sq - hi.astype(F32)).astype(BF)
    return _dot(hi, gmat) + _dot(lo, gmat)


def _head_rmsnorm(x, gain, gmat):
    return x * lax.rsqrt(_head_meansq(x, gmat) + RMS_EPS) * gain


def _rope(x, cos, sin_signed, lo_half):
    w = x.shape[1]
    swapped = jnp.where(lo_half, pltpu.roll(x, w - HEAD_DIM // 2, 1), pltpu.roll(x, HEAD_DIM // 2, 1))
    return x * cos + swapped * sin_signed


def _store_heads(ref, x):
    for h in range(NH):
        ref[0, h] = x[:, h * HEAD_DIM:(h + 1) * HEAD_DIM].astype(ref.dtype)


def _prep_kernel(nq_ref, s1_ref, s2_ref, mq_ref, mk_ref, mv_ref, fq_ref, fk_ref, fv_ref,
                 cos_ref, sin_ref, gains_ref, gmat_ref,
                 nsa_rows_ref, nsa_kc_ref, nsa_win_ref, moba_rows_ref, fox_rows_ref,
                 nqn_ref, nqr_ref, nks_ref, nvs_ref, nkw_ref, nvw_ref,
                 mqr_ref, mkr_ref, mvv_ref, kmean_ref, fqn_ref, fkn_ref, fvv_ref):
    cos = cos_ref[...]
    sin = sin_ref[...]
    gmat = gmat_ref[...]
    t = cos.shape[0]
    lane = lax.broadcasted_iota(jnp.int32, (t, HW), 1)
    lo_half = (lane % HEAD_DIM) < (HEAD_DIM // 2)
    gains = gains_ref[...]

    qn = _head_rmsnorm(nq_ref[...], gains[0:1], gmat)
    qr = _rope(qn, cos, sin, lo_half)
    _store_heads(nqn_ref, qn * ATTN_SCALE)
    _store_heads(nqr_ref, qr * ATTN_SCALE)

    s1 = s1_ref[...]
    s1r = _rope(_head_rmsnorm(s1, gains[1:2], gmat), cos, sin, lo_half)
    third = (lane >= 2 * HEAD_DIM) & (lane < 3 * HEAD_DIM)
    rows = jnp.where(third, s1r, s1)
    nsa_rows_ref[...] = rows
    nsa_kc_ref[...] = rows[:, :2 * HEAD_DIM]
    nks_ref[0] = rows[:, 2 * HEAD_DIM:3 * HEAD_DIM].astype(nks_ref.dtype)
    nvs_ref[0] = rows[:, 3 * HEAD_DIM:].astype(nvs_ref.dtype)

    s2 = s2_ref[...]
    s2r = _rope(_head_rmsnorm(s2, gains[2:3], gmat), cos, sin, lo_half)
    wrows = jnp.where(lane < HEAD_DIM, s2r, s2)
    nsa_win_ref[...] = wrows[:, :2 * HEAD_DIM]
    nkw_ref[0] = wrows[:, :HEAD_DIM].astype(nkw_ref.dtype)
    nvw_ref[0] = wrows[:, HEAD_DIM:2 * HEAD_DIM].astype(nvw_ref.dtype)

    mq = _rope(_head_rmsnorm(mq_ref[...], gains[3:4], gmat), cos, sin, lo_half)
    mk = _rope(_head_rmsnorm(mk_ref[...], gains[4:5], gmat), cos, sin, lo_half)
    mv = mv_ref[...]
    _store_heads(mqr_ref, mq * ATTN_SCALE)
    _store_heads(mkr_ref, mk)
    _store_heads(mvv_ref, mv)
    moba_rows_ref[:, :HW] = mk
    moba_rows_ref[:, HW:] = mv
    kmean_ref[0, 0] = jnp.mean(mk, axis=0, keepdims=True)

    fq = _head_rmsnorm(fq_ref[...], gains[5:6], gmat)
    fk = _head_rmsnorm(fk_ref[...], gains[6:7], gmat)
    fv = fv_ref[...]
    _store_heads(fqn_ref, fq * ATTN_SCALE)
    _store_heads(fkn_ref, fk)
    _store_heads(fvv_ref, fv)
    fox_rows_ref[:, :HW] = fk
    fox_rows_ref[:, HW:] = fv


def rope_tables(pos):
    half = HEAD_DIM // 2
    inv = ROPE_THETA ** (-jnp.arange(half, dtype=F32) / half)
    ang = pos.astype(F32)[:, None] * inv[None, :]
    cos = jnp.cos(ang)
    sin = jnp.sin(ang)
    cos_h = jnp.concatenate([cos, cos], axis=1)
    sin_h = jnp.concatenate([-sin, sin], axis=1)
    return jnp.tile(cos_h, (1, NH)), jnp.tile(sin_h, (1, NH))


def head_gains(nsa_gain, moba_gain, fox_gain):
    one = jnp.ones((HEAD_DIM,), F32)
    t4 = lambda g: jnp.tile(g, NH)
    rows = [t4(nsa_gain[0]),
            jnp.concatenate([one, one, nsa_gain[2], one]),
            jnp.concatenate([nsa_gain[3], one, one, one]),
            t4(moba_gain[0]), t4(moba_gain[1]), t4(fox_gain[0]), t4(fox_gain[1]), t4(one)]
    return jnp.stack(rows, axis=0)


def group_mean_matrix():
    idx = np.arange(HW) // HEAD_DIM
    return jnp.asarray((idx[:, None] == idx[None, :]).astype(np.float32) / HEAD_DIM, dtype=BF)


def prep_prompt(z, row0, B, T, cos, sin, gains, gmat, qdt=None):
    qdt = BF if qdt is None else qdt
    tq = PREP_TQ
    nq = T // tq
    rb0 = row0 // tq

    def sec(c):
        return pl.BlockSpec((tq, HW), lambda b, i, c=c: (rb0 + b * nq + i, c))

    flat = lambda w: pl.BlockSpec((tq, w), lambda b, i: (b * nq + i, 0))
    headmaj = pl.BlockSpec((1, NH, tq, HEAD_DIM), lambda b, i: (b, 0, i, 0))
    single = pl.BlockSpec((1, tq, HEAD_DIM), lambda b, i: (b, i, 0))
    N = B * T
    sd = jax.ShapeDtypeStruct
    hm_shape = sd((B, NH, T, HEAD_DIM), qdt)
    sg_shape = sd((B, T, HEAD_DIM), qdt)
    outs = pl.pallas_call(
        _prep_kernel,
        grid=(B, nq),
        in_specs=[sec(SEC_NQ), sec(SEC_S1), sec(SEC_S2), sec(SEC_MQ), sec(SEC_MK), sec(SEC_MV),
                  sec(SEC_FQ), sec(SEC_FK), sec(SEC_FV),
                  pl.BlockSpec((tq, HW), lambda b, i: (i, 0)),
                  pl.BlockSpec((tq, HW), lambda b, i: (i, 0)),
                  pl.BlockSpec((8, HW), lambda b, i: (0, 0)),
                  pl.BlockSpec((HW, HW), lambda b, i: (0, 0))],
        out_specs=[flat(HW), flat(2 * HEAD_DIM), flat(2 * HEAD_DIM), flat(2 * HW), flat(2 * HW),
                   headmaj, headmaj, single, single, single, single,
                   headmaj, headmaj, headmaj,
                   pl.BlockSpec((1, 1, 1, HW), lambda b, i: (b, i, 0, 0)),
                   headmaj, headmaj, headmaj],
        out_shape=[sd((N, HW), F32), sd((N, 2 * HEAD_DIM), F32), sd((N, 2 * HEAD_DIM), F32),
                   sd((N, 2 * HW), F32), sd((N, 2 * HW), F32),
                   hm_shape, hm_shape, sg_shape, sg_shape, sg_shape, sg_shape,
                   hm_shape, hm_shape, hm_shape,
                   sd((B, nq, 1, HW), F32),
                   hm_shape, hm_shape, hm_shape],
        compiler_params=_cparams(("parallel", "parallel")),
        name="mixer_prep",
    )(z, z, z, z, z, z, z, z, z, cos, sin, gains, gmat)
    keys = ("nsa_rows", "nsa_kc", "nsa_win", "moba_rows", "fox_rows",
            "nqn", "nqr", "nks", "nvs", "nkw", "nvw", "mqr", "mkr", "mvv", "kmean", "fqn", "fkn", "fvv")
    return dict(zip(keys, outs))


def _nsa_compress_kernel(x_ref, pe_ref, w_ref, gain_ref, k_ref, v_ref):
    x = x_ref[0]
    a = _dot((x + pe_ref[0:1]).astype(BF), w_ref[0])
    b = _dot((x + pe_ref[1:2]).astype(BF), w_ref[1])
    nch = x.shape[0]
    y = a + pltpu.roll(b, nch - 1, 0)
    lane = lax.broadcasted_iota(jnp.int32, y.shape, 1)
    ms = jnp.sum(jnp.where(lane < HEAD_DIM, y * y, 0.0), axis=1, keepdims=True) / HEAD_DIM
    kn = y * lax.rsqrt(ms + RMS_EPS) * gain_ref[...]
    k_ref[0] = kn[:, :HEAD_DIM].astype(BF)
    v_ref[0] = y[:, HEAD_DIM:].astype(BF)


def nsa_compress_weights(cmp_pe, cmp_w, gain1):
    S = NSA_CMP_STRIDE
    pe = cmp_pe.reshape(2, 2, S, HEAD_DIM)
    pe_flat = jnp.transpose(pe, (1, 2, 0, 3)).reshape(2, S * 2 * HEAD_DIM)
    w = cmp_w.reshape(2, 2, S, HEAD_DIM, HEAD_DIM)
    wz = jnp.zeros((2, S, 2, HEAD_DIM, 2, HEAD_DIM), F32)
    wz = wz.at[:, :, 0, :, 0, :].set(w[0]).at[:, :, 1, :, 1, :].set(w[1])
    w_flat = wz.reshape(2, S * 2 * HEAD_DIM, 2 * HEAD_DIM).astype(BF)
    gain = jnp.concatenate([gain1, jnp.ones((HEAD_DIM,), F32)])[None, :]
    return pe_flat, w_flat, gain


def nsa_compress(kc, B, T, pe_flat, w_flat, gain):
    nch = T // NSA_CMP_STRIDE
    cw = NSA_CMP_STRIDE * 2 * HEAD_DIM
    x = kc.reshape(B, nch, cw)
    out_spec = pl.BlockSpec((1, nch, HEAD_DIM), lambda b: (b, 0, 0))
    return pl.pallas_call(
        _nsa_compress_kernel,
        grid=(B,),
        in_specs=[pl.BlockSpec((1, nch, cw), lambda b: (b, 0, 0)),
                  pl.BlockSpec((2, cw), lambda b: (0, 0)),
                  pl.BlockSpec((2, cw, 2 * HEAD_DIM), lambda b: (0, 0, 0)),
                  pl.BlockSpec((1, 2 * HEAD_DIM), lambda b: (0, 0))],
        out_specs=[out_spec, out_spec],
        out_shape=[jax.ShapeDtypeStruct((B, nch, HEAD_DIM), BF)] * 2,
        compiler_params=_cparams(("parallel",)),
        name="nsa_compress",
    )(x, pe_flat, w_flat, gain)


def _softmax_update(s, mask, m_i, l_i, acc, v, lead):
    s = jnp.where(mask, s, NEG_INF)
    m_new = jnp.maximum(m_i, jnp.max(s, axis=-1, keepdims=True))
    p = jnp.where(mask, jnp.exp(s - m_new), 0.0)
    alpha = jnp.exp(m_i - m_new)
    l_new = alpha * l_i + jnp.sum(p, axis=-1, keepdims=True)
    pv = _dot(p.astype(BF).reshape(lead, p.shape[-1]), v).reshape(acc.shape)
    return m_new, l_new, alpha * acc + pv


def _nsa_attn_kernel(qn_ref, qr_ref, kc_ref, vc_ref, ks_ref, vs_ref, kw_ref, vw_ref, gl_ref,
                     cover_ref, expand_ref, o_ref, *, T):
    tq, tk = NSA_TQ, NSA_TK
    M = NH * tq
    i = pl.program_id(1)
    p0 = i * tq
    qn = qn_ref[0].reshape(M, HEAD_DIM)
    qr = qr_ref[0].reshape(M, HEAD_DIM)
    pos = p0 + lax.broadcasted_iota(jnp.int32, (tq, 1), 0)

    nch = kc_ref.shape[1]
    s_c = _dot_nt(qn, kc_ref[0]).reshape(NH, tq, nch)
    n_idx = lax.broadcasted_iota(jnp.int32, (tq, nch), 1)
    mask_c = (n_idx * NSA_CMP_STRIDE + (NSA_CMP_BLOCK - 1) <= pos)[None]
    s_c = jnp.where(mask_c, s_c, NEG_INF)
    m_c = jnp.max(s_c, axis=-1, keepdims=True)
    p_c = jnp.where(mask_c, jnp.exp(s_c - m_c), 0.0)
    p_c = p_c / jnp.maximum(jnp.sum(p_c, axis=-1, keepdims=True), 1.0)
    p_cb = p_c.astype(BF).reshape(M, nch)
    o_cmp = _dot(p_cb, vc_ref[0]).reshape(NH, tq, HEAD_DIM)
    imp = jnp.sum(_dot(p_cb, cover_ref[...]).reshape(NH, tq, LANES), axis=0)

    j = lax.broadcasted_iota(jnp.int32, (tq, LANES), 1)
    cur = pos // NSA_SEL_BLOCK
    forced = (j == 0) | (j == cur) | (j == cur - 1)
    imp = jnp.where(j > cur, NEG_INF, jnp.where(forced, SEL_FORCE, imp))
    n_sel = T // NSA_SEL_BLOCK
    rank = jnp.zeros((tq, LANES), F32)
    for jp in range(n_sel):
        col = imp[:, jp:jp + 1]
        beats = (col > imp) | ((col == imp) & (j > jp))
        rank = rank + jnp.where(beats, 1.0, 0.0)
    sel = jnp.where((rank < NSA_TOPN) & (j <= cur), 1.0, 0.0).astype(BF)

    def sel_body(kj, carry):
        m_i, l_i, acc = carry
        off = pl.multiple_of(kj * tk, tk)
        k = ks_ref[0, pl.ds(off, tk), :]
        v = vs_ref[0, pl.ds(off, tk), :]
        s = _dot_nt(qr, k).reshape(NH, tq, tk)
        e = _dot(sel, expand_ref[kj])
        kpos = off + lax.broadcasted_iota(jnp.int32, (tq, tk), 1)
        mask = ((e > 0.5) & (kpos <= pos))[None]
        return _softmax_update(s, mask, m_i, l_i, acc, v, M)

    n_kt = (p0 + tq - 1) // tk + 1
    init = (jnp.full((NH, tq, 1), NEG_INF, F32), jnp.zeros((NH, tq, 1), F32),
            jnp.zeros((NH, tq, HEAD_DIM), F32))
    _, l_s, acc_s = lax.fori_loop(0, n_kt, sel_body, init)
    o_sel = acc_s / jnp.maximum(l_s, 1.0)

    band = NSA_WINDOW + tq
    start = pl.multiple_of(jnp.maximum(p0 - NSA_WINDOW, 0), tq)
    kw = kw_ref[0, pl.ds(start, band), :]
    vw = vw_ref[0, pl.ds(start, band), :]
    s_w = _dot_nt(qr, kw).reshape(NH, tq, band)
    kpos = start + lax.broadcasted_iota(jnp.int32, (tq, band), 1)
    mask_w = ((kpos <= pos) & (kpos > pos - NSA_WINDOW))[None]
    s_w = jnp.where(mask_w, s_w, NEG_INF)
    m_w = jnp.max(s_w, axis=-1, keepdims=True)
    p_w = jnp.where(mask_w, jnp.exp(s_w - m_w), 0.0)
    l_w = jnp.sum(p_w, axis=-1, keepdims=True)
    o_win = _dot(p_w.astype(BF).reshape(M, band), vw).reshape(NH, tq, HEAD_DIM) / jnp.maximum(l_w, 1.0)

    sig = jax.nn.sigmoid(gl_ref[...])
    outs = []
    for h in range(NH):
        c = MISC_LANE + 3 * h
        outs.append(sig[:, c:c + 1] * o_cmp[h] + sig[:, c + 1:c + 2] * o_sel[h] + sig[:, c + 2:c + 3] * o_win[h])
    o_ref[...] = jnp.concatenate(outs, axis=1).astype(o_ref.dtype)


def nsa_constants(T):
    nch = T // NSA_CMP_STRIDE
    n_cmp = (T - NSA_CMP_BLOCK) // NSA_CMP_STRIDE + 1
    ci = np.arange(nch)[:, None] * NSA_CMP_STRIDE
    sj = np.arange(LANES)[None, :] * NSA_SEL_BLOCK
    cover = (ci < sj + NSA_SEL_BLOCK) & (ci + NSA_CMP_BLOCK > sj) & (np.arange(nch)[:, None] < n_cmp)
    nkt = T // NSA_TK
    blk = (np.arange(nkt)[:, None, None] * NSA_TK + np.arange(NSA_TK)[None, None, :]) // NSA_SEL_BLOCK
    expand = blk == np.arange(LANES)[None, :, None]
    return jnp.asarray(cover.astype(np.float32), dtype=BF), jnp.asarray(expand.astype(np.float32), dtype=BF)


def nsa_attention_prompt(pp, kc, vc, z, row0, B, T, cover, expand):
    tq = NSA_TQ
    nq = T // tq
    rb0 = row0 // tq
    nch = T // NSA_CMP_STRIDE
    assert T >= NSA_WINDOW + tq and T % NSA_TK == 0
    headmaj = pl.BlockSpec((1, NH, tq, HEAD_DIM), lambda b, i: (b, 0, i, 0))
    full1 = pl.BlockSpec((1, T, HEAD_DIM), lambda b, i: (b, 0, 0))
    cmp1 = pl.BlockSpec((1, nch, HEAD_DIM), lambda b, i: (b, 0, 0))
    return pl.pallas_call(
        functools.partial(_nsa_attn_kernel, T=T),
        grid=(B, nq),
        in_specs=[headmaj, headmaj, cmp1, cmp1, full1, full1, full1, full1,
                  pl.BlockSpec((tq, HW), lambda b, i: (rb0 + b * nq + i, SEC_S2)),
                  pl.BlockSpec((nch, LANES), lambda b, i: (0, 0)),
                  pl.BlockSpec((T // NSA_TK, LANES, NSA_TK), lambda b, i: (0, 0, 0))],
        out_specs=pl.BlockSpec((tq, HW), lambda b, i: (b * nq + i, 0)),
        out_shape=jax.ShapeDtypeStruct((B * T, HW), BF),
        compiler_params=_cparams(("parallel", "parallel")),
        name="nsa_attention",
    )(pp["nqn"], pp["nqr"], kc, vc, pp["nks"], pp["nvs"], pp["nkw"], pp["nvw"], z, cover, expand)


def _moba_attn_kernel(q_ref, k_ref, v_ref, km_ref, o_ref):
    tq = tk = MOBA_BLOCK
    qi = pl.program_id(1)
    lane = lax.broadcasted_iota(jnp.int32, (tq, LANES), 1)
    row = lax.broadcasted_iota(jnp.int32, (tq, tk), 0)
    colk = lax.broadcasted_iota(jnp.int32, (tq, tk), 1)
    causal = colk <= row
    outs = []
    for h in range(NH):
        q = q_ref[0, h]
        g = jnp.where(lane < qi, _dot_nt(q, km_ref[0, h]), NEG_INF)
        sel = jnp.zeros((tq, LANES), F32)
        for _ in range(MOBA_TOPK):
            m = jnp.max(g, axis=1, keepdims=True)
            idx = jnp.min(jnp.where(g == m, lane, LANES), axis=1, keepdims=True)
            hit = lane == idx
            sel = jnp.where(hit & (lane < qi), 1.0, sel)
            g = jnp.where(hit, NEG_INF, g)

        def body(kj, carry, q=q, sel=sel, h=h):
            m_i, l_i, acc = carry
            off = pl.multiple_of(kj * tk, tk)
            k = k_ref[0, h, pl.ds(off, tk), :]
            v = v_ref[0, h, pl.ds(off, tk), :]
            s = _dot_nt(q, k)
            picked = jnp.max(jnp.where(lane == kj, sel, 0.0), axis=1, keepdims=True) > 0.5
            mask = jnp.broadcast_to(picked, (tq, tk))
            return _softmax_update(s, mask, m_i, l_i, acc, v, tq)

        init = (jnp.full((tq, 1), NEG_INF, F32), jnp.zeros((tq, 1), F32), jnp.zeros((tq, HEAD_DIM), F32))
        m_i, l_i, acc = lax.fori_loop(0, qi, body, init)
        off = pl.multiple_of(qi * tk, tk)
        k = k_ref[0, h, pl.ds(off, tk), :]
        v = v_ref[0, h, pl.ds(off, tk), :]
        _, l_i, acc = _softmax_update(_dot_nt(q, k), causal, m_i, l_i, acc, v, tq)
        outs.append(acc / jnp.maximum(l_i, 1.0))
    o_ref[...] = jnp.concatenate(outs, axis=1).astype(o_ref.dtype)


def moba_attention_prompt(pp, B, T):
    tq = MOBA_BLOCK
    nq = T // tq
    km = pp["kmean"].reshape(B, nq, NH, HEAD_DIM).transpose(0, 2, 1, 3)
    km = jnp.pad(km, ((0, 0), (0, 0), (0, LANES - nq), (0, 0))).astype(BF)
    headq = pl.BlockSpec((1, NH, tq, HEAD_DIM), lambda b, i: (b, 0, i, 0))
    headfull = pl.BlockSpec((1, NH, T, HEAD_DIM), lambda b, i: (b, 0, 0, 0))
    return pl.pallas_call(
        _moba_attn_kernel,
        grid=(B, nq),
        in_specs=[headq, headfull, headfull,
                  pl.BlockSpec((1, NH, LANES, HEAD_DIM), lambda b, i: (b, 0, 0, 0))],
        out_specs=pl.BlockSpec((tq, HW), lambda b, i: (b * nq + i, 0)),
        out_shape=jax.ShapeDtypeStruct((B * T, HW), BF),
        compiler_params=_cparams(("parallel", "parallel")),
        name="moba_attention",
    )(pp["mqr"], pp["mkr"], pp["mvv"], km)


def _fox_attn_kernel(q_ref, k_ref, v_ref, cq_ref, ck_ref, o_ref):
    tq = tk = MOBA_BLOCK
    qi = pl.program_id(1)
    row = lax.broadcasted_iota(jnp.int32, (tq, tk), 0)
    colk = lax.broadcasted_iota(jnp.int32, (tq, tk), 1)
    causal = colk <= row
    everything = row >= 0
    outs = []
    for h in range(NH):
        q = q_ref[0, h]
        cq = cq_ref[0, h]

        def tile(kj, carry, mask, q=q, cq=cq, h=h):
            m_i, l_i, acc = carry
            off = pl.multiple_of(kj * tk, tk)
            k = k_ref[0, h, pl.ds(off, tk), :]
            v = v_ref[0, h, pl.ds(off, tk), :]
            s = _dot_nt(q, k) + cq - ck_ref[0, h, kj]
            return _softmax_update(s, mask, m_i, l_i, acc, v, tq)

        init = (jnp.full((tq, 1), NEG_INF, F32), jnp.zeros((tq, 1), F32), jnp.zeros((tq, HEAD_DIM), F32))
        carry = lax.fori_loop(0, qi, functools.partial(tile, mask=everything), init)
        _, l_i, acc = tile(qi, carry, causal)
        outs.append(acc / jnp.maximum(l_i, 1.0))
    o_ref[...] = jnp.concatenate(outs, axis=1).astype(o_ref.dtype)


def fox_attention_prompt(pp, c, B, T):
    tq = MOBA_BLOCK
    nq = T // tq
    ch = jnp.transpose(c, (0, 2, 1))
    cq = ch[..., None]
    ck = ch.reshape(B, NH, nq, 1, tq)
    headq = pl.BlockSpec((1, NH, tq, HEAD_DIM), lambda b, i: (b, 0, i, 0))
    headfull = pl.BlockSpec((1, NH, T, HEAD_DIM), lambda b, i: (b, 0, 0, 0))
    return pl.pallas_call(
        _fox_attn_kernel,
        grid=(B, nq),
        in_specs=[headq, headfull, headfull,
                  pl.BlockSpec((1, NH, tq, 1), lambda b, i: (b, 0, i, 0)),
                  pl.BlockSpec((1, NH, nq, 1, tq), lambda b, i: (b, 0, 0, 0, 0))],
        out_specs=pl.BlockSpec((tq, HW), lambda b, i: (b * nq + i, 0)),
        out_shape=jax.ShapeDtypeStruct((B * T, HW), BF),
        compiler_params=_cparams(("parallel", "parallel")),
        name="fox_attention",
    )(pp["fqn"], pp["fkn"], pp["fvv"], cq, ck)


def _hgrn_kernel(q_ref, f_ref, i_ref, g_ref, s0_ref, lb_ref, gain_ref, bd_ref, gmat_ref,
                 o_ref, sout_ref, st_scr, b_scr, k_scr, v_scr, *, C):
    c = pl.program_id(1)

    @pl.when(c == 0)
    def _():
        st_scr[...] = jnp.zeros_like(st_scr)
        for h in range(NH):
            st_scr[h * HEAD_DIM:(h + 1) * HEAD_DIM, h * HEAD_DIM:(h + 1) * HEAD_DIM] = s0_ref[0, h]

    lb = lb_ref[...]
    z = f_ref[...]
    logf = jnp.log(lb + (1.0 - lb) * jax.nn.sigmoid(z))
    kk = (1.0 - lb) * jax.nn.sigmoid(-z)
    q = q_ref[...]
    qf = q * jax.nn.sigmoid(q)
    v = i_ref[...]
    row = lax.broadcasted_iota(jnp.int32, (C, HW), 0)
    b = logf
    sh = 1
    while sh < C:
        b = b + jnp.where(row >= sh, pltpu.roll(b, sh, 0), 0.0)
        sh *= 2
    b_scr[...] = b
    k_scr[...] = kk
    v_scr[...] = v
    bd = bd_ref[...]

    def body(s, o):
        bs = b_scr[pl.ds(s, 1), :]
        causal = row >= s
        e = jnp.exp(jnp.where(causal, b - bs, 0.0))
        fz = jnp.where(causal, qf * k_scr[pl.ds(s, 1), :] * e, 0.0)
        hi = fz.astype(BF)
        lo = (fz - hi.astype(F32)).astype(BF)
        return o + (_dot(hi, bd) + _dot(lo, bd)) * v_scr[pl.ds(s, 1), :]

    o_intra = lax.fori_loop(0, C, body, jnp.zeros((C, HW), F32))
    st = st_scr[...]
    o = _dot_nt((qf * jnp.exp(b)).astype(BF), st.astype(BF)) + o_intra
    b_last = b[C - 1:C, :]
    kt = kk * jnp.exp(b_last - b)
    upd = lax.dot_general(v.astype(BF), kt.astype(BF), (((0,), (0,)), ((), ())), preferred_element_type=F32)
    r2 = lax.broadcasted_iota(jnp.int32, (HW, HW), 0) // HEAD_DIM
    c2 = lax.broadcasted_iota(jnp.int32, (HW, HW), 1) // HEAD_DIM
    st_new = st * jnp.exp(b_last) + jnp.where(r2 == c2, upd, 0.0)
    st_scr[...] = st_new
    g = g_ref[...]
    o_ref[...] = _head_rmsnorm(o, gain_ref[...], gmat_ref[...]) * (g * jax.nn.sigmoid(g))

    @pl.when(c == pl.num_programs(1) - 1)
    def _():
        for h in range(NH):
            sout_ref[0, h] = st_new[h * HEAD_DIM:(h + 1) * HEAD_DIM, h * HEAD_DIM:(h + 1) * HEAD_DIM]


def hgrn_mixer(z, row0, B, T, s0, lb, out_gain, gmat):
    C = math.gcd(T, HGRN_CHUNK)
    nc = T // C
    rb0 = row0 // C
    idx = np.arange(HW) // HEAD_DIM
    bd = jnp.asarray((idx[:, None] == idx[None, :]).astype(np.float32), dtype=BF)

    def sec(cidx):
        return pl.BlockSpec((C, HW), lambda b, c, cidx=cidx: (rb0 + b * nc + c, cidx))

    state = pl.BlockSpec((1, NH, HGRN_DV, HGRN_DK), lambda b, c: (b, 0, 0, 0))
    vec = pl.BlockSpec((1, HW), lambda b, c: (0, 0))
    mat = pl.BlockSpec((HW, HW), lambda b, c: (0, 0))
    o, st = pl.pallas_call(
        functools.partial(_hgrn_kernel, C=C),
        grid=(B, nc),
        in_specs=[sec(SEC_HQ), sec(SEC_HF), sec(SEC_HI), sec(SEC_HG), state, vec, vec, mat, mat],
        out_specs=[pl.BlockSpec((C, HW), lambda b, c: (b * nc + c, 0)), state],
        out_shape=[jax.ShapeDtypeStruct((B * T, HW), F32),
                   jax.ShapeDtypeStruct((B, NH, HGRN_DV, HGRN_DK), F32)],
        scratch_shapes=[pltpu.VMEM((HW, HW), F32), pltpu.VMEM((C, HW), F32), pltpu.VMEM((C, HW), F32),
                        pltpu.VMEM((C, HW), F32)],
        compiler_params=_cparams(("parallel", "arbitrary")),
        name="hgrn_recurrence",
    )(z, z, z, z, jnp.swapaxes(s0, 2, 3), lb[None, :], jnp.tile(out_gain, NH)[None, :], bd, gmat)
    return o, jnp.swapaxes(st, 2, 3)


DEC_PG = 8


def _page_specs(tail, PG):
    return [pl.BlockSpec((1,) + tail, lambda b, s, pt, j=j: (pt[b, s * PG + j], 0, 0)) for j in range(PG)]


def _rows_nat(q_ref):
    return jnp.concatenate([q_ref[0, h] for h in range(NH)], axis=1)


def _block_diag_q(qnat, T):
    q4 = jnp.concatenate([qnat] * NH, axis=0)
    r = lax.broadcasted_iota(jnp.int32, q4.shape, 0) // T
    c = lax.broadcasted_iota(jnp.int32, q4.shape, 1) // HEAD_DIM
    return jnp.where(r == c, q4, 0.0)


def _diag_heads(x, T):
    return jnp.concatenate([x[h * T:(h + 1) * T, h * HEAD_DIM:(h + 1) * HEAD_DIM] for h in range(NH)], axis=1)


def _new_causal(R, T):
    tq = lax.broadcasted_iota(jnp.int32, (R, T), 0) % T
    tk = lax.broadcasted_iota(jnp.int32, (R, T), 1)
    return tk <= tq


def _fox_decode_kernel(pt_ref, q_ref, new_ref, cq_ref, cn_ref, ck_ref, *rest, PG, T):
    page_refs, o_ref = rest[:PG], rest[PG]
    m_scr, l_scr, acc_scr = rest[PG + 1:]
    s = pl.program_id(1)
    R = NH * T

    @pl.when(s == 0)
    def _():
        m_scr[...] = jnp.full_like(m_scr, NEG_INF)
        l_scr[...] = jnp.zeros_like(l_scr)
        acc_scr[...] = jnp.zeros_like(acc_scr)

    qbd = _block_diag_q(_rows_nat(q_ref), T).astype(BF)
    cq = cq_ref[0]
    m_i, l_i, acc = m_scr[...], l_scr[...], acc_scr[...]
    for j in range(PG):
        page = page_refs[j][0]
        k = page[:, :HW].astype(BF)
        v = page[:, HW:].astype(BF)
        ck = jnp.broadcast_to(ck_ref[0, j][:, None, :], (NH, T, PAGE_SIZE)).reshape(R, PAGE_SIZE)
        sc = _dot_nt(qbd, k) + cq - ck
        m_new = jnp.maximum(m_i, jnp.max(sc, axis=-1, keepdims=True))
        p = jnp.exp(sc - m_new)
        alpha = jnp.exp(m_i - m_new)
        l_i = alpha * l_i + jnp.sum(p, axis=-1, keepdims=True)
        acc = alpha * acc + _dot(p.astype(BF), v)
        m_i = m_new
    m_scr[...] = m_i
    l_scr[...] = l_i
    acc_scr[...] = acc

    @pl.when(s == pl.num_programs(1) - 1)
    def _():
        new = new_ref[...]
        sc = _dot_nt(qbd, new[:, :HW].astype(BF)) + cq - cn_ref[0]
        _, l_f, acc_f = _softmax_update(sc, _new_causal(R, T), m_i, l_i, acc, new[:, HW:].astype(BF), R)
        o_ref[...] = _diag_heads(acc_f / jnp.maximum(l_f, 1.0), T)


def fox_decode(sp, pool, page_table, c, row_new, B, T):
    n_pages = page_table.shape[1]
    PG = min(DEC_PG, n_pages)
    past = n_pages * PAGE_SIZE
    R = NH * T
    view = pool.reshape(pool.shape[0], PAGE_SIZE, 2 * HW)
    ch = jnp.transpose(c, (0, 2, 1))
    ck = ch[:, :, :past].reshape(B, NH, n_pages, PAGE_SIZE).transpose(0, 2, 1, 3)
    cnew = ch[:, :, past:]
    cq = cnew.reshape(B, R, 1)
    cn = jnp.broadcast_to(cnew[:, :, None, :], (B, NH, T, T)).reshape(B, R, T)
    return pl.pallas_call(
        functools.partial(_fox_decode_kernel, PG=PG, T=T),
        grid_spec=pltpu.PrefetchScalarGridSpec(
            num_scalar_prefetch=1, grid=(B, n_pages // PG),
            in_specs=[pl.BlockSpec((1, NH, T, HEAD_DIM), lambda b, s, pt: (0, 0, b, 0)),
                      pl.BlockSpec((T, 2 * HW), lambda b, s, pt: (b, 0)),
                      pl.BlockSpec((1, R, 1), lambda b, s, pt: (b, 0, 0)),
                      pl.BlockSpec((1, R, T), lambda b, s, pt: (b, 0, 0)),
                      pl.BlockSpec((1, PG, NH, PAGE_SIZE), lambda b, s, pt: (b, s, 0, 0))]
            + _page_specs((PAGE_SIZE, 2 * HW), PG),
            out_specs=pl.BlockSpec((T, HW), lambda b, s, pt: (b, 0)),
            scratch_shapes=[pltpu.VMEM((R, 1), F32), pltpu.VMEM((R, 1), F32), pltpu.VMEM((R, HW), F32)]),
        out_shape=jax.ShapeDtypeStruct((B * T, HW), F32),
        compiler_params=_cparams(("parallel", "arbitrary")),
        name="fox_decode",
    )(page_table, sp["fqn"], row_new, cq, cn, ck, *([view] * PG))


def _moba_decode_kernel(pt_ref, q_ref, new_ref, *rest, PG, T, n_pages):
    page_refs, o_ref = rest[:PG], rest[PG]
    m_scr, l_scr, rs_scr, acc_scr = rest[PG + 1:]
    s = pl.program_id(1)
    R = NH * T
    lane = lax.broadcasted_iota(jnp.int32, (R, LANES), 1)

    @pl.when(s == 0)
    def _():
        m_scr[...] = jnp.full_like(m_scr, NEG_INF)
        l_scr[...] = jnp.zeros_like(l_scr)
        rs_scr[...] = jnp.zeros_like(rs_scr)

    qbd = _block_diag_q(_rows_nat(q_ref), T).astype(BF)
    mm, ll, rs = m_scr[...], l_scr[...], rs_scr[...]
    for j in range(PG):
        pid = s * PG + j
        page = page_refs[j][0]
        sc = _dot_nt(qbd, page[:, :HW].astype(BF))
        m = jnp.max(sc, axis=-1, keepdims=True)
        p = jnp.exp(sc - m)
        acc_scr[pid] = _dot(p.astype(BF), page[:, HW:].astype(BF))
        hit = lane == pid
        mm = jnp.where(hit, m, mm)
        ll = jnp.where(hit, jnp.sum(p, axis=-1, keepdims=True), ll)
        rs = jnp.where(hit, jnp.sum(sc, axis=-1, keepdims=True), rs)
    m_scr[...] = mm
    l_scr[...] = ll
    rs_scr[...] = rs

    @pl.when(s == pl.num_programs(1) - 1)
    def _():
        block_lane = (lane % 2 == 0) & (lane < n_pages)
        g = jnp.where(block_lane, (rs + pltpu.roll(rs, LANES - 1, 1)) * (1.0 / MOBA_BLOCK), NEG_INF)
        sel = jnp.zeros((R, LANES), F32)
        for _ in range(MOBA_TOPK):
            mx = jnp.max(g, axis=1, keepdims=True)
            idx = jnp.min(jnp.where(g == mx, lane, LANES), axis=1, keepdims=True)
            hit = lane == idx
            sel = jnp.where(hit & block_lane, 1.0, sel)
            g = jnp.where(hit, NEG_INF, g)
        picked = (sel + pltpu.roll(sel, 1, 1)) > 0.5
        new = new_ref[...]
        mask_n = _new_causal(R, T)
        sn = jnp.where(mask_n, _dot_nt(qbd, new[:, :HW].astype(BF)), NEG_INF)
        m_n = jnp.max(sn, axis=-1, keepdims=True)
        p_n = jnp.where(mask_n, jnp.exp(sn - m_n), 0.0)
        m_tot = jnp.maximum(jnp.max(jnp.where(picked, mm, NEG_INF), axis=1, keepdims=True), m_n)
        w = jnp.where(picked, jnp.exp(mm - m_tot), 0.0)
        w_n = jnp.exp(m_n - m_tot)
        l_tot = jnp.sum(w * ll, axis=1, keepdims=True) + w_n * jnp.sum(p_n, axis=-1, keepdims=True)
        acc = w_n * _dot(p_n.astype(BF), new[:, HW:].astype(BF))
        for pg in range(n_pages):
            acc = acc + w[:, pg:pg + 1] * acc_scr[pg]
        o_ref[...] = _diag_heads(acc / jnp.maximum(l_tot, 1.0), T)


def moba_decode(sp, pool, page_table, row_new, B, T):
    n_pages = page_table.shape[1]
    assert MOBA_BLOCK == 2 * PAGE_SIZE and n_pages % 2 == 0 and n_pages <= LANES and T <= MOBA_BLOCK
    PG = min(DEC_PG, n_pages)
    R = NH * T
    view = pool.reshape(pool.shape[0], PAGE_SIZE, 2 * HW)
    return pl.pallas_call(
        functools.partial(_moba_decode_kernel, PG=PG, T=T, n_pages=n_pages),
        grid_spec=pltpu.PrefetchScalarGridSpec(
            num_scalar_prefetch=1, grid=(B, n_pages // PG),
            in_specs=[pl.BlockSpec((1, NH, T, HEAD_DIM), lambda b, s, pt: (0, 0, b, 0)),
                      pl.BlockSpec((T, 2 * HW), lambda b, s, pt: (b, 0))]
            + _page_specs((PAGE_SIZE, 2 * HW), PG),
            out_specs=pl.BlockSpec((T, HW), lambda b, s, pt: (b, 0)),
            scratch_shapes=[pltpu.VMEM((R, LANES), F32), pltpu.VMEM((R, LANES), F32), pltpu.VMEM((R, LANES), F32),
                            pltpu.VMEM((n_pages, R, HW), F32)]),
        out_shape=jax.ShapeDtypeStruct((B * T, HW), F32),
        compiler_params=_cparams(("parallel", "arbitrary")),
        name="moba_decode",
    )(page_table, sp["mqr"], row_new, *([view] * PG))


def _nsa_cmp_phys_kernel(x_ref, pe_ref, w_ref, y_ref):
    x = x_ref[...]
    a = _dot((x + pe_ref[0:1]).astype(BF), w_ref[0])
    b = _dot((x + pe_ref[1:2]).astype(BF), w_ref[1])
    y_ref[...] = jnp.concatenate([a, b], axis=1)


def nsa_compress_pool(pool, cmp_pe, cmp_w):
    S = NSA_CMP_STRIDE
    n_phys = pool.shape[0]
    rows = n_phys * (PAGE_SIZE // S)
    cw = S * NSA_ROWS * HEAD_DIM
    pe = jnp.transpose(cmp_pe.reshape(2, 2, S, HEAD_DIM), (1, 2, 0, 3))
    pe = jnp.pad(pe, ((0, 0), (0, 0), (0, 2), (0, 0))).reshape(2, cw)
    w = cmp_w.reshape(2, 2, S, HEAD_DIM, HEAD_DIM)
    wz = jnp.zeros((2, S, NSA_ROWS, HEAD_DIM, 2, HEAD_DIM), F32)
    wz = wz.at[:, :, 0, :, 0, :].set(w[0]).at[:, :, 1, :, 1, :].set(w[1])
    wz = wz.reshape(2, cw, 2 * HEAD_DIM).astype(BF)
    tm = next(t for t in (512, 256, 128, 64, 32, 16, 8) if rows % t == 0)
    return pl.pallas_call(
        _nsa_cmp_phys_kernel,
        grid=(rows // tm,),
        in_specs=[pl.BlockSpec((tm, cw), lambda i: (i, 0)),
                  pl.BlockSpec((2, cw), lambda i: (0, 0)),
                  pl.BlockSpec((2, cw, 2 * HEAD_DIM), lambda i: (0, 0, 0))],
        out_specs=pl.BlockSpec((tm, HW), lambda i: (i, 0)),
        out_shape=jax.ShapeDtypeStruct((rows, HW), F32),
        compiler_params=_cparams(("parallel",)),
        name="nsa_compress_pool",
    )(pool.reshape(rows, cw), pe, wz)


def _nsa_sel_decode_kernel(y_ref, q_ref, gain_ref, cover_ref, ocmp_ref, sel_ref, *, T, offset, n_cmp, n_sel):
    R = NH * T
    y = y_ref[0]
    nch = y.shape[0]
    c = y[:, :2 * HEAD_DIM] + pltpu.roll(y[:, 2 * HEAD_DIM:], nch - 1, 0)
    lane = lax.broadcasted_iota(jnp.int32, c.shape, 1)
    ms = jnp.sum(jnp.where(lane < HEAD_DIM, c * c, 0.0), axis=1, keepdims=True) / HEAD_DIM
    kn = c * lax.rsqrt(ms + RMS_EPS) * gain_ref[...]
    kc = kn[:, :HEAD_DIM].astype(BF)
    vc = c[:, HEAD_DIM:].astype(BF)
    q = q_ref[0].reshape(R, HEAD_DIM).astype(BF)
    pos = offset + lax.broadcasted_iota(jnp.int32, (R, 1), 0) % T
    n_idx = lax.broadcasted_iota(jnp.int32, (R, nch), 1)
    mask = (n_idx * NSA_CMP_STRIDE + (NSA_CMP_BLOCK - 1) <= pos) & (n_idx < n_cmp)
    s_c = jnp.where(mask, _dot_nt(q, kc), NEG_INF)
    m_c = jnp.max(s_c, axis=-1, keepdims=True)
    p_c = jnp.where(mask, jnp.exp(s_c - m_c), 0.0)
    p_c = (p_c / jnp.maximum(jnp.sum(p_c, axis=-1, keepdims=True), 1.0)).astype(BF)
    ocmp_ref[0] = _dot(p_c, vc)
    imp4 = _dot(p_c, cover_ref[...])
    imp = imp4[0:T]
    for h in range(1, NH):
        imp = imp + imp4[h * T:(h + 1) * T]
    j = lax.broadcasted_iota(jnp.int32, imp.shape, 1)
    cur = (offset + lax.broadcasted_iota(jnp.int32, (T, 1), 0)) // NSA_SEL_BLOCK
    forced = (j == 0) | (j == cur) | (j == cur - 1)
    imp = jnp.where(j > cur, NEG_INF, jnp.where(forced, SEL_FORCE, imp))
    rank = jnp.zeros(imp.shape, F32)
    for jp in range(n_sel):
        col = imp[:, jp:jp + 1]
        rank = rank + jnp.where((col > imp) | ((col == imp) & (j > jp)), 1.0, 0.0)
    sel_ref[0] = jnp.where((rank < NSA_TOPN) & (j <= cur), 1.0, 0.0)


def _nsa_decode_kernel(pt_ref, q_ref, sel_ref, ocmp_ref, newr_ref, win_ref, neww_ref, gl_ref, *rest,
                       PG, T, offset):
    page_refs, o_ref = rest[:PG], rest[PG]
    m_scr, l_scr, acc_scr = rest[PG + 1:]
    s = pl.program_id(1)
    R = NH * T
    D = HEAD_DIM

    @pl.when(s == 0)
    def _():
        m_scr[...] = jnp.full_like(m_scr, NEG_INF)
        l_scr[...] = jnp.zeros_like(l_scr)
        acc_scr[...] = jnp.zeros_like(acc_scr)

    q = q_ref[0].reshape(R, D).astype(BF)
    sel = sel_ref[0]
    jl = lax.broadcasted_iota(jnp.int32, sel.shape, 1)
    half = lax.broadcasted_iota(jnp.int32, (T, PAGE_SIZE), 1) < NSA_SEL_BLOCK
    m_i, l_i, acc = m_scr[...], l_scr[...], acc_scr[...]
    for j in range(PG):
        pid = s * PG + j
        page = page_refs[j][0]
        sc = _dot_nt(q, page[:, 2 * D:3 * D].astype(BF))
        sa = jnp.max(jnp.where(jl == 2 * pid, sel, 0.0), axis=1, keepdims=True)
        sb = jnp.max(jnp.where(jl == 2 * pid + 1, sel, 0.0), axis=1, keepdims=True)
        mt = jnp.where(half, sa, sb) > 0.5
        mask = jnp.broadcast_to(mt[None], (NH, T, PAGE_SIZE)).reshape(R, PAGE_SIZE)
        m_i, l_i, acc = _softmax_update(sc, mask, m_i, l_i, acc, page[:, 3 * D:].astype(BF), R)
    m_scr[...] = m_i
    l_scr[...] = l_i
    acc_scr[...] = acc

    @pl.when(s == pl.num_programs(1) - 1)
    def _():
        causal = _new_causal(R, T)
        newr = newr_ref[...]
        cb = offset // NSA_SEL_BLOCK
        own = jnp.broadcast_to((sel[:, cb:cb + 1] > 0.5)[None], (NH, T, T)).reshape(R, T)
        _, l_s, acc_s = _softmax_update(_dot_nt(q, newr[:, 2 * D:3 * D].astype(BF)), causal & own, m_i, l_i, acc,
                                        newr[:, 3 * D:].astype(BF), R)
        o_sel = acc_s / jnp.maximum(l_s, 1.0)
        win = win_ref[0]
        neww = neww_ref[...]
        wb = win.shape[0]
        tq = lax.broadcasted_iota(jnp.int32, (R, wb), 0) % T
        rk = lax.broadcasted_iota(jnp.int32, (R, wb), 1)
        mask1 = rk > wb + tq - NSA_WINDOW
        s1 = jnp.where(mask1, _dot_nt(q, win[:, :D].astype(BF)), NEG_INF)
        s2 = jnp.where(causal, _dot_nt(q, neww[:, :D].astype(BF)), NEG_INF)
        m_w = jnp.maximum(jnp.max(s1, axis=-1, keepdims=True), jnp.max(s2, axis=-1, keepdims=True))
        p1 = jnp.where(mask1, jnp.exp(s1 - m_w), 0.0)
        p2 = jnp.where(causal, jnp.exp(s2 - m_w), 0.0)
        l_w = jnp.sum(p1, axis=-1, keepdims=True) + jnp.sum(p2, axis=-1, keepdims=True)
        o_win = (_dot(p1.astype(BF), win[:, D:].astype(BF)) + _dot(p2.astype(BF), neww[:, D:].astype(BF))) \
            / jnp.maximum(l_w, 1.0)
        o_cmp = ocmp_ref[0]
        sig = jax.nn.sigmoid(gl_ref[...])
        outs = []
        for h in range(NH):
            c = MISC_LANE + 3 * h
            rs = slice(h * T, (h + 1) * T)
            outs.append(sig[:, c:c + 1] * o_cmp[rs] + sig[:, c + 1:c + 2] * o_sel[rs] + sig[:, c + 2:c + 3] * o_win[rs])
        o_ref[...] = jnp.concatenate(outs, axis=1)


def nsa_decode(sp, z, row0, pool, page_table, win_state, cmp_pe, cmp_w, gain1, B, T):
    n_pages = page_table.shape[1]
    past = n_pages * PAGE_SIZE
    S = past + T
    R = NH * T
    n_cmp = (S - NSA_CMP_BLOCK) // NSA_CMP_STRIDE + 1
    n_sel = -(-S // NSA_SEL_BLOCK)
    nch = past // NSA_CMP_STRIDE
    assert (n_cmp - 1) * NSA_CMP_STRIDE + NSA_CMP_BLOCK <= past, "compressed blocks must lie inside the cache"
    assert past % NSA_SEL_BLOCK == 0 and T <= NSA_SEL_BLOCK and n_sel <= HW
    PG = min(DEC_PG, n_pages)
    cpp = PAGE_SIZE // NSA_CMP_STRIDE
    y = nsa_compress_pool(pool, cmp_pe, cmp_w).reshape(pool.shape[0], cpp, HW)
    yb = y[page_table].reshape(B, nch, HW)
    ci = np.arange(nch)[:, None] * NSA_CMP_STRIDE
    sj = np.arange(HW)[None, :] * NSA_SEL_BLOCK
    cover = (ci < sj + NSA_SEL_BLOCK) & (ci + NSA_CMP_BLOCK > sj) & (np.arange(nch)[:, None] < n_cmp) \
        & (np.arange(HW)[None, :] < n_sel)
    cover = jnp.asarray(cover.astype(np.float32), dtype=BF)
    gain = jnp.concatenate([gain1, jnp.ones((HEAD_DIM,), F32)])[None, :]
    qspec = lambda nargs: pl.BlockSpec((1, NH, T, HEAD_DIM), (lambda b: (0, 0, b, 0)) if nargs == 1
                                       else (lambda b, s, pt: (0, 0, b, 0)))
    ocmp, sel = pl.pallas_call(
        functools.partial(_nsa_sel_decode_kernel, T=T, offset=past, n_cmp=n_cmp, n_sel=n_sel),
        grid=(B,),
        in_specs=[pl.BlockSpec((1, nch, HW), lambda b: (b, 0, 0)), qspec(1),
                  pl.BlockSpec((1, 2 * HEAD_DIM), lambda b: (0, 0)),
                  pl.BlockSpec((nch, HW), lambda b: (0, 0))],
        out_specs=[pl.BlockSpec((1, R, HEAD_DIM), lambda b: (b, 0, 0)), pl.BlockSpec((1, T, HW), lambda b: (b, 0, 0))],
        out_shape=[jax.ShapeDtypeStruct((B, R, HEAD_DIM), F32), jax.ShapeDtypeStruct((B, T, HW), F32)],
        compiler_params=_cparams(("parallel",)),
        name="nsa_select_decode",
    )(yb, sp["nqn"], gain, cover)
    view = pool.reshape(pool.shape[0], PAGE_SIZE, HW)
    wb = win_state.shape[1]
    rb0 = row0 // T
    return pl.pallas_call(
        functools.partial(_nsa_decode_kernel, PG=PG, T=T, offset=past),
        grid_spec=pltpu.PrefetchScalarGridSpec(
            num_scalar_prefetch=1, grid=(B, n_pages // PG),
            in_specs=[qspec(3),
                      pl.BlockSpec((1, T, HW), lambda b, s, pt: (b, 0, 0)),
                      pl.BlockSpec((1, R, HEAD_DIM), lambda b, s, pt: (b, 0, 0)),
                      pl.BlockSpec((T, HW), lambda b, s, pt: (b, 0)),
                      pl.BlockSpec((1, wb, 2 * HEAD_DIM), lambda b, s, pt: (b, 0, 0)),
                      pl.BlockSpec((T, 2 * HEAD_DIM), lambda b, s, pt: (b, 0)),
                      pl.BlockSpec((T, HW), lambda b, s, pt: (rb0 + b, SEC_S2))]
            + _page_specs((PAGE_SIZE, HW), PG),
            out_specs=pl.BlockSpec((T, HW), lambda b, s, pt: (b, 0)),
            scratch_shapes=[pltpu.VMEM((R, 1), F32), pltpu.VMEM((R, 1), F32), pltpu.VMEM((R, HEAD_DIM), F32)]),
        out_shape=jax.ShapeDtypeStruct((B * T, HW), F32),
        compiler_params=_cparams(("parallel", "arbitrary")),
        name="nsa_decode",
    )(page_table, sp["nqr"], sel, ocmp, sp["nsa_rows"], win_state.reshape(B, wb, 2 * HEAD_DIM), sp["nsa_win"], z,
      *([view] * PG))


def rmsnorm(x, g):
    xf = x.astype(jnp.float32)
    y = xf * lax.rsqrt(jnp.mean(xf * xf, axis=-1, keepdims=True) + RMS_EPS)
    return (y * g.astype(jnp.float32)).astype(x.dtype)


def rope(x, pos):
    half = HEAD_DIM // 2
    inv = ROPE_THETA ** (-jnp.arange(half, dtype=jnp.float32) / half)
    ang = pos.astype(jnp.float32)[:, None] * inv[None, :]
    cos = jnp.cos(ang)[:, None, :]
    sin = jnp.sin(ang)[:, None, :]
    xf = x.astype(jnp.float32)
    x1, x2 = xf[..., :half], xf[..., half:]
    return jnp.concatenate([x1 * cos - x2 * sin, x2 * cos + x1 * sin], axis=-1).astype(x.dtype)


def masked_softmax(s, mask):
    s = jnp.where(mask, s.astype(jnp.float32), NEG_INF)
    m = jnp.max(s, axis=-1, keepdims=True)
    p = jnp.where(mask, jnp.exp(s - m), 0.0)
    return p / jnp.maximum(jnp.sum(p, axis=-1, keepdims=True), 1.0)


def sweep(fn, blk, *arrays):
    B, T = arrays[0].shape[:2]
    nb = -(-T // blk)
    Tp = nb * blk
    blocks = []
    for a in arrays:
        a = jnp.pad(a, [(0, 0), (0, Tp - T)] + [(0, 0)] * (a.ndim - 2))
        blocks.append(jnp.moveaxis(a.reshape((B, nb, blk) + a.shape[2:]), 1, 0))
    starts = jnp.arange(nb, dtype=jnp.int32) * blk
    out = lax.map(lambda args: fn(args[0], *args[1]), (starts, tuple(blocks)))
    out = jnp.moveaxis(out, 0, 1).reshape((B, Tp) + out.shape[3:])
    return out[:, :T]


def window_attend(q, rows, buf_len):
    B, T, H, Dh = q.shape
    dt = q.dtype
    W = NSA_WINDOW
    qb = min(Q_BLOCK, T)
    nb = -(-T // qb)
    Tp = nb * qb
    band = W + qb
    rp = jnp.pad(rows, ((0, 0), (W, Tp - T), (0, 0), (0, 0)))
    kidx = buf_len + np.arange(nb)[:, None] * qb + np.arange(band)[None, :]
    kb = rp[:, kidx]
    qp = jnp.pad(q, ((0, 0), (0, Tp - T), (0, 0), (0, 0))).reshape(B, nb, qb, H, Dh)
    s = jnp.einsum('bnqhd,bnkd->bhnqk', qp, kb[..., 0, :], preferred_element_type=jnp.float32) * ATTN_SCALE
    qq = buf_len + np.arange(nb)[:, None] * qb + np.arange(qb)[None, :]
    kk = kidx - W
    mask = (kk[:, None, :] >= 0) & (kk[:, None, :] <= qq[:, :, None]) & (kk[:, None, :] > qq[:, :, None] - W)
    p = masked_softmax(s, mask)
    o = jnp.einsum('bhnqk,bnkd->bnqhd', p.astype(dt), kb[..., 1, :]).reshape(B, Tp, H, Dh)
    return o[:, :T]


def nsa_mixer(q, kv_c, kv_s, kv_w, gate_logits, past_rows, win_buf, qk_gain, cmp_pe, cmp_w, offset):
    B, T, H, Dh = q.shape
    dt = q.dtype
    pos = offset + jnp.arange(T, dtype=jnp.int32)
    qn = rmsnorm(q, qk_gain[0])
    qr = rope(qn, pos)
    k_s = rope(rmsnorm(kv_s[:, :, 0:1], qk_gain[2]), pos)[:, :, 0]
    k_w = rope(rmsnorm(kv_w[:, :, 0:1], qk_gain[3]), pos)[:, :, 0]
    new_rows = jnp.stack([kv_c[:, :, 0], kv_c[:, :, 1], k_s, kv_s[:, :, 1]], axis=2)
    rows = jnp.concatenate([past_rows, new_rows], axis=1)
    S = offset + T
    n_cmp = (S - NSA_CMP_BLOCK) // NSA_CMP_STRIDE + 1
    cidx = np.arange(n_cmp)[:, None] * NSA_CMP_STRIDE + np.arange(NSA_CMP_BLOCK)[None, :]

    def compress(r):
        blocks = rows[:, :, r][:, cidx] + cmp_pe[r]
        return blocks.reshape(B, n_cmp, NSA_CMP_BLOCK * Dh) @ cmp_w[r]

    k_cmp = rmsnorm(compress(0), qk_gain[1])
    v_cmp = compress(1)
    s_c = jnp.einsum('bthd,bnd->bhtn', qn, k_cmp, preferred_element_type=jnp.float32) * ATTN_SCALE
    cmp_end = np.arange(n_cmp) * NSA_CMP_STRIDE + NSA_CMP_BLOCK - 1
    p_c = masked_softmax(s_c, cmp_end[None, :] <= pos[:, None])
    o_cmp = jnp.einsum('bhtn,bnd->bthd', p_c.astype(dt), v_cmp)
    n_sel = -(-S // NSA_SEL_BLOCK)
    ci = np.arange(n_cmp)[:, None] * NSA_CMP_STRIDE
    sj = np.arange(n_sel)[None, :] * NSA_SEL_BLOCK
    cover = ((ci < sj + NSA_SEL_BLOCK) & (ci + NSA_CMP_BLOCK > sj)).astype(np.float32)
    imp = jnp.einsum('bhtn,nj->btj', p_c, jnp.asarray(cover))
    cur = (pos // NSA_SEL_BLOCK)[:, None]
    jj = jnp.arange(n_sel)[None, :]
    forced = (jj == 0) | (jj == cur) | (jj == cur - 1)
    imp = jnp.where(jj > cur, NEG_INF, jnp.where(forced, SEL_FORCE, imp))
    _, sel_idx = lax.top_k(imp, min(NSA_TOPN, n_sel))
    kv_sel = jnp.pad(rows[:, :, 2:4], ((0, 0), (0, n_sel * NSA_SEL_BLOCK - S), (0, 0), (0, 0)))
    kv_sel = kv_sel.reshape(B, n_sel, NSA_SEL_BLOCK, 2, Dh)
    bidx = jnp.arange(B)[:, None, None]

    def sel_block(start, q_blk, idx_blk):
        qb = q_blk.shape[1]
        tp = offset + start + jnp.arange(qb)
        g = kv_sel[bidx, idx_blk]
        kpos = idx_blk[..., None] * NSA_SEL_BLOCK + jnp.arange(NSA_SEL_BLOCK)
        mask = (kpos <= tp[None, :, None, None]).reshape(B, 1, qb, -1)
        g = g.reshape(B, qb, -1, 2, Dh)
        s = jnp.einsum('bqhd,bqkd->bhqk', q_blk, g[..., 0, :], preferred_element_type=jnp.float32) * ATTN_SCALE
        p = masked_softmax(s, mask)
        return jnp.einsum('bhqk,bqkd->bqhd', p.astype(dt), g[..., 1, :])

    o_sel = sweep(sel_block, min(GATHER_Q_BLOCK, T), qr, sel_idx)
    win_rows = jnp.concatenate([win_buf, jnp.stack([k_w, kv_w[:, :, 1]], axis=2)], axis=1)
    o_win = window_attend(qr, win_rows, win_buf.shape[1])
    gates = jax.nn.sigmoid(gate_logits.astype(jnp.float32)).astype(dt)
    o = gates[..., 0:1] * o_cmp + gates[..., 1:2] * o_sel + gates[..., 2:3] * o_win
    new_win = win_rows[:, -min(NSA_WINDOW, win_rows.shape[1]):]
    return o, new_rows, new_win


def gated_recurrence(q, k, v, logf, S0):
    B, T, H, DK = q.shape
    DV = v.shape[-1]
    C = math.gcd(T, HGRN_CHUNK)
    nc = T // C

    def chunks(a):
        return jnp.moveaxis(a.reshape((B, nc, C) + a.shape[2:]), 1, 0).swapaxes(2, 3)

    causal = jnp.tril(jnp.ones((C, C), dtype=bool))[:, :, None]

    def step(S, inp):
        qc, kc, vc, gc = inp
        b = jnp.cumsum(gc, axis=2)
        o_inter = jnp.einsum('bhtk,bhkv->bhtv', qc * jnp.exp(b), S)
        diff = b[:, :, :, None, :] - b[:, :, None, :, :]
        decay = jnp.where(causal, jnp.exp(jnp.where(causal, diff, 0.0)), 0.0)
        A = jnp.einsum('bhtk,bhsk,bhtsk->bhts', qc, kc, decay)
        o = o_inter + jnp.einsum('bhts,bhsv->bhtv', A, vc)
        b_last = b[:, :, -1:, :]
        S_new = jnp.exp(b_last[:, :, 0, :])[..., None] * S + jnp.einsum('bhsk,bhsv->bhkv', kc * jnp.exp(b_last - b), vc)
        return S_new, o

    S, o = lax.scan(step, S0, (chunks(q), chunks(k), chunks(v), chunks(logf)))
    o = jnp.moveaxis(o.swapaxes(2, 3), 0, 1).reshape(B, T, H, DV)
    return o, S


def hgrn2_mixer(q, f, i, g, S0, lb, out_gain):
    dt = q.dtype
    H = q.shape[2]
    lb = lb.reshape(H, HGRN_DK)
    z = f.astype(jnp.float32)
    logf = jnp.log(lb + (1.0 - lb) * jax.nn.sigmoid(z))
    k = (1.0 - lb) * jax.nn.sigmoid(-z)
    qf = jax.nn.silu(q.astype(jnp.float32))
    o, S = gated_recurrence(qf, k, i.astype(jnp.float32), logf, S0)
    o = rmsnorm(o, out_gain) * jax.nn.silu(g.astype(jnp.float32))
    return o.astype(dt), S


def moba_mixer(q, k, v, past_rows, qk_gain, offset):
    B, T, H, Dh = q.shape
    dt = q.dtype
    pos = offset + jnp.arange(T, dtype=jnp.int32)
    qr = rope(rmsnorm(q, qk_gain[0]), pos)
    kr = rope(rmsnorm(k, qk_gain[1]), pos)
    new_rows = jnp.stack([kr, v], axis=2)
    rows = jnp.concatenate([past_rows, new_rows], axis=1)
    S = offset + T
    nblk = -(-S // MOBA_BLOCK)
    kvb = jnp.pad(rows, ((0, 0), (0, nblk * MOBA_BLOCK - S), (0, 0), (0, 0), (0, 0)))
    kvb = jnp.transpose(kvb.reshape(B, nblk, MOBA_BLOCK, 2, H, Dh), (0, 4, 1, 2, 3, 5))
    kmean = jnp.mean(kvb[..., 0, :].astype(jnp.float32), axis=3)
    gate = jnp.einsum('bthd,bhnd->bthn', qr.astype(jnp.float32), kmean)
    own = pos // MOBA_BLOCK
    past_ok = jnp.arange(nblk)[None, :] < own[:, None]
    _, top = lax.top_k(jnp.where(past_ok[None, :, None, :], gate, NEG_INF), min(MOBA_TOPK, nblk))
    valid = top < own[None, :, None, None]
    idx = jnp.concatenate([top, jnp.broadcast_to(own[None, :, None, None], (B, T, H, 1)).astype(top.dtype)], axis=-1)
    ok = jnp.concatenate([valid, jnp.ones((B, T, H, 1), dtype=bool)], axis=-1)
    bidx = jnp.arange(B)[:, None, None, None]
    hidx = jnp.arange(H)[None, None, :, None]

    def blk_fn(start, q_blk, idx_blk, ok_blk):
        qb = q_blk.shape[1]
        tp = offset + start + jnp.arange(qb)
        g = kvb[bidx, hidx, idx_blk]
        kpos = idx_blk[..., None] * MOBA_BLOCK + jnp.arange(MOBA_BLOCK)
        mask = (ok_blk[..., None] & (kpos <= tp[None, :, None, None, None])).reshape(B, qb, H, -1)
        g = g.reshape(B, qb, H, -1, 2, Dh)
        s = jnp.einsum('bqhd,bqhkd->bqhk', q_blk, g[..., 0, :], preferred_element_type=jnp.float32) * ATTN_SCALE
        p = masked_softmax(s, mask)
        return jnp.einsum('bqhk,bqhkd->bqhd', p.astype(dt), g[..., 1, :])

    o = sweep(blk_fn, min(GATHER_Q_BLOCK, T), qr, idx, ok)
    return o, new_rows


def fox_mixer(q, k, v, f_logit, past_kv, past_logf, qk_gain, f_bias, offset):
    B, T, H, Dh = q.shape
    dt = q.dtype
    qn = rmsnorm(q, qk_gain[0])
    kn = rmsnorm(k, qk_gain[1])
    logf_new = jax.nn.log_sigmoid(f_logit.astype(jnp.float32) + f_bias.astype(jnp.float32))
    new_rows = jnp.stack([kn, v], axis=2)
    rows = jnp.concatenate([past_kv, new_rows], axis=1)
    c = jnp.cumsum(jnp.concatenate([past_logf.astype(jnp.float32), logf_new], axis=1), axis=1)
    S = offset + T
    K = rows[:, :, 0]
    V = rows[:, :, 1]
    c_k = jnp.moveaxis(c, 1, 2)[:, :, None, :]
    kpos = jnp.arange(S)

    def blk_fn(start, q_blk, cq_blk):
        qb = q_blk.shape[1]
        tp = offset + start + jnp.arange(qb)
        s = jnp.einsum('bqhd,bkhd->bhqk', q_blk, K, preferred_element_type=jnp.float32) * ATTN_SCALE
        s = s + jnp.moveaxis(cq_blk, 1, 2)[..., None] - c_k
        p = masked_softmax(s, kpos[None, :] <= tp[:, None])
        return jnp.einsum('bhqk,bkhd->bqhd', p.astype(dt), V)

    o = sweep(blk_fn, min(Q_BLOCK, T), qn, c[:, offset:])
    return o, new_rows, logf_new.astype(dt)


def _ffn_up_kernel(be_ref, new_ref, x_ref, w1_ref, w3_ref, u_ref, w1_scr, w3_scr):
    i = pl.program_id(1)

    @pl.when(new_ref[i] == 1)
    def _():
        w1_scr[...] = w1_ref[0].astype(BF)
        w3_scr[...] = w3_ref[0].astype(BF)

    x = x_ref[...].astype(BF)
    a = _dot(x, w1_scr[...])
    b = _dot(x, w3_scr[...])
    u_ref[...] = (a * jax.nn.sigmoid(a) * b).astype(u_ref.dtype)


def _ffn_down_kernel(be_ref, new_ref, u_ref, w2_ref, y_ref, w2_scr):
    i = pl.program_id(1)

    @pl.when(new_ref[i] == 1)
    def _():
        w2_scr[...] = w2_ref[0].astype(BF)

    y_ref[...] = _dot(u_ref[...], w2_scr[...])


def grouped_swiglu(x, block_exp, w1, w3, w2, tm, tf, tn):
    R, D = x.shape
    F = w1.shape[2]
    nblk = R // tm
    block_exp = block_exp.astype(jnp.int32)
    new = jnp.concatenate([jnp.ones((1,), jnp.int32), (block_exp[1:] != block_exp[:-1]).astype(jnp.int32)])
    u = pl.pallas_call(
        _ffn_up_kernel,
        grid_spec=pltpu.PrefetchScalarGridSpec(
            num_scalar_prefetch=2, grid=(F // tf, nblk),
            in_specs=[pl.BlockSpec((tm, D), lambda j, i, be, nw: (i, 0)),
                      pl.BlockSpec((1, D, tf), lambda j, i, be, nw: (be[i], 0, j)),
                      pl.BlockSpec((1, D, tf), lambda j, i, be, nw: (be[i], 0, j))],
            out_specs=pl.BlockSpec((tm, tf), lambda j, i, be, nw: (i, j)),
            scratch_shapes=[pltpu.VMEM((D, tf), BF), pltpu.VMEM((D, tf), BF)]),
        out_shape=jax.ShapeDtypeStruct((R, F), BF),
        compiler_params=_cparams(("arbitrary", "arbitrary")),
        name="swiglu_up",
    )(block_exp, new, x, w1, w3)
    return pl.pallas_call(
        _ffn_down_kernel,
        grid_spec=pltpu.PrefetchScalarGridSpec(
            num_scalar_prefetch=2, grid=(D // tn, nblk),
            in_specs=[pl.BlockSpec((tm, F), lambda n, i, be, nw: (i, 0)),
                      pl.BlockSpec((1, F, tn), lambda n, i, be, nw: (be[i], 0, n))],
            out_specs=pl.BlockSpec((tm, tn), lambda n, i, be, nw: (i, n)),
            scratch_shapes=[pltpu.VMEM((F, tn), BF)]),
        out_shape=jax.ShapeDtypeStruct((R, D), F32),
        compiler_params=_cparams(("arbitrary", "arbitrary")),
        name="swiglu_down",
    )(block_exp, new, u, w2)


def swiglu_dense(x, w1, w3, w2):
    tm = 640 if x.shape[0] % 640 == 0 else 256
    be = jnp.zeros((x.shape[0] // tm,), jnp.int32)
    return grouped_swiglu(x, be, w1[None], w3[None], w2[None], tm, DENSE_TF, FFN_TN)


def moe_ffn_grouped(xf, router, w1, w3, w2):
    N, D = xf.shape
    tm = MOE_TM
    rpad = jnp.pad(router, ((0, 0), (0, LANES - N_EXPERTS)))
    logits = matmul(xf, rpad, tm=256, tn=LANES)[:, :N_EXPERTS]
    top_v, top_e = lax.top_k(logits, TOP_K)
    gates = jax.nn.softmax(top_v, axis=-1)
    NK = N * TOP_K
    flat_e = top_e.reshape(NK)
    order = jnp.argsort(flat_e)
    e_sorted = flat_e[order]
    tok_sorted = (order // TOP_K).astype(jnp.int32)
    counts = jnp.sum((flat_e[:, None] == jnp.arange(N_EXPERTS)[None, :]).astype(jnp.int32), axis=0)
    padded = (counts + tm - 1) // tm * tm
    pend = jnp.cumsum(padded)
    pstart = pend - padded
    start = jnp.cumsum(counts) - counts
    dest_sorted = pstart[e_sorted] + (jnp.arange(NK, dtype=jnp.int32) - start[e_sorted])
    n_blocks = -(-NK // tm) + N_EXPERTS
    slot_tok = jnp.full((n_blocks * tm,), N, jnp.int32).at[dest_sorted].set(tok_sorted)
    block_exp = jnp.clip(jnp.searchsorted(pend, jnp.arange(n_blocks) * tm, side='right'), 0, N_EXPERTS - 1)
    xpad = jnp.concatenate([xf, jnp.zeros((1, D), xf.dtype)], axis=0)
    xb = xpad[slot_tok]
    yb = grouped_swiglu(xb, block_exp, w1, w3, w2, tm, MOE_TF, FFN_TN)
    dest = jnp.zeros((NK,), jnp.int32).at[order].set(dest_sorted).reshape(N, TOP_K)
    return yb[dest[:, 0]] * gates[:, 0:1] + yb[dest[:, 1]] * gates[:, 1:2]


def swiglu(h, w1, w3, w2):
    return (jax.nn.silu(h @ w1) * (h @ w3)) @ w2


def moe_ffn(xf, router, w1, w3, w2):
    N, D = xf.shape
    dt = xf.dtype
    logits = (xf @ router).astype(jnp.float32)
    top_v, top_e = lax.top_k(logits, TOP_K)
    gates = jax.nn.softmax(top_v, axis=-1)
    NK = N * TOP_K
    flat_e = top_e.reshape(NK)
    flat_tok = jnp.arange(NK, dtype=jnp.int32) // TOP_K
    order = jnp.argsort(flat_e)
    e_sorted = flat_e[order]
    tok_sorted = flat_tok[order]
    counts = jnp.zeros((N_EXPERTS,), jnp.int32).at[flat_e].add(1)
    padded = (counts + MOE_BLOCK - 1) // MOE_BLOCK * MOE_BLOCK
    pend = jnp.cumsum(padded)
    pstart = pend - padded
    start = jnp.cumsum(counts) - counts
    dest = pstart[e_sorted] + (jnp.arange(NK, dtype=jnp.int32) - start[e_sorted])
    n_blocks = -(-NK // MOE_BLOCK) + N_EXPERTS
    slot_tok = jnp.full((n_blocks * MOE_BLOCK,), N, jnp.int32).at[dest].set(tok_sorted)
    block_exp = jnp.clip(jnp.searchsorted(pend, jnp.arange(n_blocks) * MOE_BLOCK, side='right'), 0, N_EXPERTS - 1)
    xpad = jnp.concatenate([xf, jnp.zeros((1, D), dt)], axis=0)
    xb = xpad[slot_tok].reshape(n_blocks, MOE_BLOCK, D)

    def expert_block(args):
        xblk, e = args
        return swiglu(xblk, w1[e], w3[e], w2[e])

    yb = lax.map(expert_block, (xb, block_exp)).reshape(n_blocks * MOE_BLOCK, D)
    y_assign = yb[dest] * gates.reshape(NK)[order][:, None].astype(dt)
    return jnp.zeros((N, D), dt).at[tok_sorted].add(y_assign)


def z_sections(z):
    s = lambda c, a=0, b=HW: z[..., c * HW + a:c * HW + b]
    d = HEAD_DIM
    return dict(nq=s(SEC_NQ), nkc=s(SEC_S1, 0, 2 * d), nks=s(SEC_S1, 2 * d, 4 * d), nkw=s(SEC_S2, 0, 2 * d),
                ngate=s(SEC_S2, MISC_LANE, MISC_LANE + 12), ff=s(SEC_S2, MISC_LANE + 12, MISC_LANE + 16),
                hq=s(SEC_HQ), hf=s(SEC_HF), hi=s(SEC_HI), hg=s(SEC_HG), mq=s(SEC_MQ), mk=s(SEC_MK), mv=s(SEC_MV),
                fq=s(SEC_FQ), fk=s(SEC_FK), fv=s(SEC_FV))


def kernel(x_prompt, x_sample, cache_nsa, state_nsa_win, state_hgrn, cache_moba, cache_fox_kv, cache_fox_logf,
           page_table, g_mix, g_ffn, w_in, w_out, nsa_qk_gain, nsa_cmp_pe, nsa_cmp_w, hgrn_lb_logits,
           hgrn_out_gain, moba_qk_gain, fox_qk_gain, fox_f_bias, ffn_w1, ffn_w3, ffn_w2, moe_router,
           moe_w1, moe_w3, moe_w2):
    dt = x_prompt.dtype
    Bp, Tp, D = x_prompt.shape
    Bs, Ts, _ = x_sample.shape
    Np, Ns = Bp * Tp, Bs * Ts
    past_len = page_table.shape[1] * PAGE_SIZE
    lb_w = jax.nn.softmax(hgrn_lb_logits.astype(jnp.float32), axis=0)
    lower_bounds = jnp.cumsum(lb_w, axis=0) - lb_w[0:1]

    def gather_pages(pool):
        g = pool[page_table]
        return g.reshape((Bs, past_len) + pool.shape[2:])

    cos_p, sin_p = rope_tables(jnp.arange(Tp, dtype=jnp.int32))
    cos_s, sin_s = rope_tables(past_len + jnp.arange(Ns, dtype=jnp.int32) % Ts)
    assert Ns == PREP_TQ and Np % PREP_TQ == 0
    gmat = group_mean_matrix()
    cover, expand = nsa_constants(Tp)

    x = jnp.concatenate([x_prompt.reshape(Np, D), x_sample.reshape(Ns, D)], axis=0)
    st_p, st_s = [], []
    for l in range(DEPTH):
        i = l // 2
        z = in_projection(x, g_mix[l][None, :], relayout_w_in(w_in[l]))

        gains = head_gains(nsa_qk_gain[l], moba_qk_gain[l], fox_qk_gain[l])
        pp = prep_prompt(z, 0, Bp, Tp, cos_p, sin_p, gains, gmat)
        pe_flat, w_flat, cgain = nsa_compress_weights(nsa_cmp_pe[l], nsa_cmp_w[l], nsa_qk_gain[l][1])
        kc, vc = nsa_compress(pp["nsa_kc"], Bp, Tp, pe_flat, w_flat, cgain)
        o_nsa_p = nsa_attention_prompt(pp, kc, vc, z, 0, Bp, Tp, cover, expand)
        o_mb_p = moba_attention_prompt(pp, Bp, Tp)
        zp = z_sections(z[:Np].reshape(Bp, Tp, N_INP))
        logf_p = jax.nn.log_sigmoid(zp["ff"] + fox_f_bias[l].astype(F32))
        o_fx_p = fox_attention_prompt(pp, jnp.cumsum(logf_p, axis=1), Bp, Tp)
        hd = lambda a, n, d, B, T: a.reshape(B, T, n, d)
        o_hg_p, hg_state_p = hgrn_mixer(z, 0, Bp, Tp, jnp.zeros((Bp, NH, HGRN_DK, HGRN_DV), F32), lower_bounds[l],
                                        hgrn_out_gain[l], gmat)
        o_p = jnp.concatenate([o_nsa_p.astype(F32), o_hg_p, o_mb_p.astype(F32),
                               o_fx_p.astype(F32)], axis=1)
        nsa_win_p = pp["nsa_win"].reshape(Bp, Tp, 2, HEAD_DIM)[:, -min(NSA_WINDOW, Tp):]
        st_p.append((pp["nsa_rows"].reshape(Bp, Tp, NSA_ROWS, HEAD_DIM), nsa_win_p, hg_state_p.astype(dt),
                     pp["moba_rows"].reshape(Bp, Tp, 2, NH, HEAD_DIM), pp["fox_rows"].reshape(Bp, Tp, 2, NH, HEAD_DIM),
                     logf_p.astype(dt)))

        sp = prep_prompt(z, Np, 1, Ns, cos_s, sin_s, gains, gmat, qdt=F32)
        o_nsa_s = nsa_decode(sp, z, Np, cache_nsa[l], page_table, state_nsa_win[l], nsa_cmp_pe[l], nsa_cmp_w[l],
                             nsa_qk_gain[l][1], Bs, Ts)
        o_hg_s, hg_state = hgrn_mixer(z, Np, Bs, Ts, state_hgrn[l].astype(F32), lower_bounds[l], hgrn_out_gain[l], gmat)
        o_mb_s = moba_decode(sp, cache_moba[l], page_table, sp["moba_rows"], Bs, Ts)
        ff_s = z[Np:, SEC_S2 * HW + MISC_LANE + 12:SEC_S2 * HW + MISC_LANE + 16].reshape(Bs, Ts, NH)
        logf_s = jax.nn.log_sigmoid(ff_s + fox_f_bias[l].astype(F32))
        c_s = jnp.cumsum(jnp.concatenate([gather_pages(cache_fox_logf[l]).astype(F32), logf_s], axis=1), axis=1)
        o_fx_s = fox_decode(sp, cache_fox_kv[l], page_table, c_s, sp["fox_rows"], Bs, Ts)
        o_s = jnp.concatenate([o_nsa_s, o_hg_s, o_mb_s, o_fx_s], axis=1)
        win_rows = jnp.concatenate([state_nsa_win[l], sp["nsa_win"].reshape(Bs, Ts, 2, HEAD_DIM)], axis=1)
        st_s.append((sp["nsa_rows"].reshape(Bs, Ts, NSA_ROWS, HEAD_DIM),
                     win_rows[:, -min(NSA_WINDOW, win_rows.shape[1]):], hg_state.astype(dt),
                     sp["moba_rows"].reshape(Bs, Ts, 2, NH, HEAD_DIM), sp["fox_rows"].reshape(Bs, Ts, 2, NH, HEAD_DIM),
                     logf_s.astype(dt)))

        o = jnp.concatenate([o_p, o_s], axis=0)
        x = x + matmul(o, w_out[l])
        hn = rmsnorm(x, g_ffn[l])
        if l % 2 == 0:
            x = x + swiglu_dense(hn, ffn_w1[i], ffn_w3[i], ffn_w2[i])
        else:
            x = x + moe_ffn_grouped(hn, moe_router[i], moe_w1[i], moe_w3[i], moe_w2[i])

    def stk(states, j):
        return jnp.stack([s[j] for s in states], axis=0)

    return (x[:Np].reshape(Bp, Tp, D), x[Np:].reshape(Bs, Ts, D),
            stk(st_p, 0), stk(st_s, 0), stk(st_p, 1), stk(st_s, 1), stk(st_p, 2), stk(st_s, 2),
            stk(st_p, 3), stk(st_s, 3), stk(st_p, 4), stk(st_s, 4), stk(st_p, 5), stk(st_s, 5))
```

```python
import math, functools
import jax, jax.numpy as jnp
from jax import lax
import numpy as np
from jax.experimental import pallas as pl
from jax.experimental.pallas import tpu as pltpu

D_MODEL = 1024
DEPTH = 2
PAGE_SIZE = 128
HEAD_DIM = 64
H_NSA = 4
H_HGRN = 4
H_MOBA = 4
H_FOX = 4
NH = 4
HW = NH * HEAD_DIM
MIX_WIDTH = (H_NSA + H_HGRN + H_MOBA + H_FOX) * HEAD_DIM
HGRN_DK = 64
HGRN_DV = HEAD_DIM
HGRN_CHUNK = 64
NSA_CMP_BLOCK = 32
NSA_CMP_STRIDE = 16
NSA_SEL_BLOCK = 64
NSA_TOPN = 16
NSA_WINDOW = 512
NSA_ROWS = 4
MOBA_BLOCK = 256
MOBA_TOPK = 3
ROPE_THETA = 10000.0
Q_BLOCK = 128
GATHER_Q_BLOCK = 32
N_EXPERTS = 8
TOP_K = 2
MOE_BLOCK = 128
RMS_EPS = 1e-6
NEG_INF = -1e30
SEL_FORCE = 1e6
ATTN_SCALE = HEAD_DIM ** -0.5
IN_SIZES = (H_NSA * HEAD_DIM, 2 * HEAD_DIM, 2 * HEAD_DIM, 2 * HEAD_DIM, 3 * H_NSA,
            H_HGRN * HGRN_DK, H_HGRN * HGRN_DK, H_HGRN * HGRN_DV, H_HGRN * HGRN_DV,
            H_MOBA * HEAD_DIM, H_MOBA * HEAD_DIM, H_MOBA * HEAD_DIM,
            H_FOX * HEAD_DIM, H_FOX * HEAD_DIM, H_FOX * HEAD_DIM, H_FOX)
N_IN = sum(IN_SIZES)
IN_OFFS = tuple(int(v) for v in np.cumsum((0,) + IN_SIZES))

N_SEC = 13
N_INP = N_SEC * HW
SEC_NQ, SEC_S1, SEC_S2, SEC_HQ, SEC_HF, SEC_HI, SEC_HG = 0, 1, 2, 3, 4, 5, 6
SEC_MQ, SEC_MK, SEC_MV, SEC_FQ, SEC_FK, SEC_FV = 7, 8, 9, 10, 11, 12
MISC_LANE = 128

LANES = 128
VMEM_LIMIT = 48 * 1024 * 1024
PREP_TQ = 256
NSA_TQ = 128
NSA_TK = 512
ATT_SB = 2
M_FLOOR = -1e20
MOE_TM = 256
MOE_TF = 896
DENSE_TF = 1408
FFN_TN = 512
BF = jnp.bfloat16
F32 = jnp.float32


def _round_up(x, m):
    return -(-x // m) * m


def _cparams(sem):
    return pltpu.CompilerParams(dimension_semantics=sem, vmem_limit_bytes=VMEM_LIMIT)


def _dot(a, b):
    return jnp.dot(a, b, preferred_element_type=F32)


def _dot_nt(a, b):
    return lax.dot_general(a, b, (((1,), (1,)), ((), ())), preferred_element_type=F32)


def _mm_kernel(a_ref, b_ref, o_ref):
    k = pl.program_id(2)
    acc = _dot(a_ref[...].astype(BF), b_ref[...].astype(BF))

    @pl.when(k == 0)
    def _():
        o_ref[...] = acc

    @pl.when(k != 0)
    def _():
        o_ref[...] += acc


def matmul(a, b, tm=512, tn=512, tk=1024):
    M, K = a.shape
    _, N = b.shape
    tm = min(tm, _round_up(M, 8))
    Mp, Np = _round_up(M, tm), _round_up(N, tn)
    if K % tk:
        tk = K
    if Mp != M:
        a = jnp.pad(a, ((0, Mp - M), (0, 0)))
    if Np != N:
        b = jnp.pad(b, ((0, 0), (0, Np - N)))
    out = pl.pallas_call(
        _mm_kernel,
        grid=(Mp // tm, Np // tn, K // tk),
        in_specs=[pl.BlockSpec((tm, tk), lambda i, j, k: (i, k)),
                  pl.BlockSpec((tk, tn), lambda i, j, k: (k, j))],
        out_specs=pl.BlockSpec((tm, tn), lambda i, j, k: (i, j)),
        out_shape=jax.ShapeDtypeStruct((Mp, Np), F32),
        compiler_params=_cparams(("parallel", "parallel", "arbitrary")),
        name="dense_matmul",
    )(a, b)
    return out[:M, :N]


def _inproj_kernel(x_ref, g_ref, w_ref, o_ref):
    x = x_ref[...]
    y = x * lax.rsqrt(jnp.mean(x * x, axis=-1, keepdims=True) + RMS_EPS) * g_ref[...]
    o_ref[...] = _dot(y.astype(BF), w_ref[...])


def in_projection(x, g, w_bf, tm=256):
    N, D = x.shape
    return pl.pallas_call(
        _inproj_kernel,
        grid=(N // tm,),
        in_specs=[pl.BlockSpec((tm, D), lambda i: (i, 0)),
                  pl.BlockSpec((1, D), lambda i: (0, 0)),
                  pl.BlockSpec((D, N_INP), lambda i: (0, 0))],
        out_specs=pl.BlockSpec((tm, N_INP), lambda i: (i, 0)),
        out_shape=jax.ShapeDtypeStruct((N, N_INP), F32),
        compiler_params=_cparams(("parallel",)),
        name="rmsnorm_in_projection",
    )(x, g, w_bf)


def relayout_w_in(w):
    def cols(i):
        return w[:, IN_OFFS[i]:IN_OFFS[i + 1]]
    pad = jnp.zeros((w.shape[0], HW - 2 * HEAD_DIM - IN_SIZES[4] - IN_SIZES[15]), w.dtype)
    parts = [cols(0), cols(1), cols(2), cols(3), cols(4), cols(15), pad] + [cols(i) for i in range(5, 15)]
    return jnp.concatenate(parts, axis=1).astype(BF)


def _head_meansq(x, gmat):
    sq = x * x
    hi = sq.astype(BF)
    lo = (sq - hi.astype(F32)).astype(BF)
    return _dot(hi, gmat) + _dot(lo, gmat)


def _head_rmsnorm(x, gain, gmat):
    return x * lax.rsqrt(_head_meansq(x, gmat) + RMS_EPS) * gain


def _rope(x, cos, sin_signed, lo_half):
    w = x.shape[1]
    swapped = jnp.where(lo_half, pltpu.roll(x, w - HEAD_DIM // 2, 1), pltpu.roll(x, HEAD_DIM // 2, 1))
    return x * cos + swapped * sin_signed


def _store_heads(ref, x):
    for h in range(NH):
        ref[0, h] = x[:, h * HEAD_DIM:(h + 1) * HEAD_DIM].astype(ref.dtype)


def _prep_kernel(nq_ref, s1_ref, s2_ref, mq_ref, mk_ref, mv_ref, fq_ref, fk_ref, fv_ref,
                 cos_ref, sin_ref, gains_ref, gmat_ref,
                 nsa_rows_ref, nsa_kc_ref, nsa_win_ref, moba_rows_ref, fox_rows_ref,
                 nqn_ref, nqr_ref, nks_ref, nvs_ref, nkw_ref, nvw_ref,
                 mqr_ref, mkr_ref, mvv_ref, kmean_ref, fqn_ref, fkn_ref, fvv_ref):
    cos = cos_ref[...]
    sin = sin_ref[...]
    gmat = gmat_ref[...]
    t = cos.shape[0]
    lane = lax.broadcasted_iota(jnp.int32, (t, HW), 1)
    lo_half = (lane % HEAD_DIM) < (HEAD_DIM // 2)
    gains = gains_ref[...]

    qn = _head_rmsnorm(nq_ref[...], gains[0:1], gmat)
    qr = _rope(qn, cos, sin, lo_half)
    _store_heads(nqn_ref, qn * ATTN_SCALE)
    _store_heads(nqr_ref, qr * ATTN_SCALE)

    s1 = s1_ref[...]
    s1r = _rope(_head_rmsnorm(s1, gains[1:2], gmat), cos, sin, lo_half)
    third = (lane >= 2 * HEAD_DIM) & (lane < 3 * HEAD_DIM)
    rows = jnp.where(third, s1r, s1)
    nsa_rows_ref[...] = rows
    nsa_kc_ref[...] = rows[:, :2 * HEAD_DIM]
    nks_ref[0] = rows[:, 2 * HEAD_DIM:3 * HEAD_DIM].astype(nks_ref.dtype)
    nvs_ref[0] = rows[:, 3 * HEAD_DIM:].astype(nvs_ref.dtype)

    s2 = s2_ref[...]
    s2r = _rope(_head_rmsnorm(s2, gains[2:3], gmat), cos, sin, lo_half)
    wrows = jnp.where(lane < HEAD_DIM, s2r, s2)
    nsa_win_ref[...] = wrows[:, :2 * HEAD_DIM]
    nkw_ref[0] = wrows[:, :HEAD_DIM].astype(nkw_ref.dtype)
    nvw_ref[0] = wrows[:, HEAD_DIM:2 * HEAD_DIM].astype(nvw_ref.dtype)

    mq = _rope(_head_rmsnorm(mq_ref[...], gains[3:4], gmat), cos, sin, lo_half)
    mk = _rope(_head_rmsnorm(mk_ref[...], gains[4:5], gmat), cos, sin, lo_half)
    mv = mv_ref[...]
    _store_heads(mqr_ref, mq * ATTN_SCALE)
    _store_heads(mkr_ref, mk)
    _store_heads(mvv_ref, mv)
    moba_rows_ref[:, :HW] = mk
    moba_rows_ref[:, HW:] = mv
    kmean_ref[0, 0] = jnp.mean(mk, axis=0, keepdims=True)

    fq = _head_rmsnorm(fq_ref[...], gains[5:6], gmat)
    fk = _head_rmsnorm(fk_ref[...], gains[6:7], gmat)
    fv = fv_ref[...]
    _store_heads(fqn_ref, fq * ATTN_SCALE)
    _store_heads(fkn_ref, fk)
    _store_heads(fvv_ref, fv)
    fox_rows_ref[:, :HW] = fk
    fox_rows_ref[:, HW:] = fv


def rope_tables(pos):
    half = HEAD_DIM // 2
    inv = ROPE_THETA ** (-jnp.arange(half, dtype=F32) / half)
    ang = pos.astype(F32)[:, None] * inv[None, :]
    cos = jnp.cos(ang)
    sin = jnp.sin(ang)
    cos_h = jnp.concatenate([cos, cos], axis=1)
    sin_h = jnp.concatenate([-sin, sin], axis=1)
    return jnp.tile(cos_h, (1, NH)), jnp.tile(sin_h, (1, NH))


def head_gains(nsa_gain, moba_gain, fox_gain):
    one = jnp.ones((HEAD_DIM,), F32)
    t4 = lambda g: jnp.tile(g, NH)
    rows = [t4(nsa_gain[0]),
            jnp.concatenate([one, one, nsa_gain[2], one]),
            jnp.concatenate([nsa_gain[3], one, one, one]),
            t4(moba_gain[0]), t4(moba_gain[1]), t4(fox_gain[0]), t4(fox_gain[1]), t4(one)]
    return jnp.stack(rows, axis=0)


def group_mean_matrix():
    idx = np.arange(HW) // HEAD_DIM
    return jnp.asarray((idx[:, None] == idx[None, :]).astype(np.float32) / HEAD_DIM, dtype=BF)


def prep_prompt(z, row0, B, T, cos, sin, gains, gmat, qdt=None):
    qdt = BF if qdt is None else qdt
    tq = PREP_TQ
    nq = T // tq
    rb0 = row0 // tq

    def sec(c):
        return pl.BlockSpec((tq, HW), lambda b, i, c=c: (rb0 + b * nq + i, c))

    flat = lambda w: pl.BlockSpec((tq, w), lambda b, i: (b * nq + i, 0))
    headmaj = pl.BlockSpec((1, NH, tq, HEAD_DIM), lambda b, i: (b, 0, i, 0))
    single = pl.BlockSpec((1, tq, HEAD_DIM), lambda b, i: (b, i, 0))
    N = B * T
    sd = jax.ShapeDtypeStruct
    hm_shape = sd((B, NH, T, HEAD_DIM), qdt)
    sg_shape = sd((B, T, HEAD_DIM), qdt)
    outs = pl.pallas_call(
        _prep_kernel,
        grid=(B, nq),
        in_specs=[sec(SEC_NQ), sec(SEC_S1), sec(SEC_S2), sec(SEC_MQ), sec(SEC_MK), sec(SEC_MV),
                  sec(SEC_FQ), sec(SEC_FK), sec(SEC_FV),
                  pl.BlockSpec((tq, HW), lambda b, i: (i, 0)),
                  pl.BlockSpec((tq, HW), lambda b, i: (i, 0)),
                  pl.BlockSpec((8, HW), lambda b, i: (0, 0)),
                  pl.BlockSpec((HW, HW), lambda b, i: (0, 0))],
        out_specs=[flat(HW), flat(2 * HEAD_DIM), flat(2 * HEAD_DIM), flat(2 * HW), flat(2 * HW),
                   headmaj, headmaj, single, single, single, single,
                   headmaj, headmaj, headmaj,
                   pl.BlockSpec((1, 1, 1, HW), lambda b, i: (b, i, 0, 0)),
                   headmaj, headmaj, headmaj],
        out_shape=[sd((N, HW), F32), sd((N, 2 * HEAD_DIM), F32), sd((N, 2 * HEAD_DIM), F32),
                   sd((N, 2 * HW), F32), sd((N, 2 * HW), F32),
                   hm_shape, hm_shape, sg_shape, sg_shape, sg_shape, sg_shape,
                   hm_shape, hm_shape, hm_shape,
                   sd((B, nq, 1, HW), F32),
                   hm_shape, hm_shape, hm_shape],
        compiler_params=_cparams(("parallel", "parallel")),
        name="mixer_prep",
    )(z, z, z, z, z, z, z, z, z, cos, sin, gains, gmat)
    keys = ("nsa_rows", "nsa_kc", "nsa_win", "moba_rows", "fox_rows",
            "nqn", "nqr", "nks", "nvs", "nkw", "nvw", "mqr", "mkr", "mvv", "kmean", "fqn", "fkn", "fvv")
    return dict(zip(keys, outs))


def _nsa_compress_kernel(x_ref, pe_ref, w_ref, gain_ref, k_ref, v_ref):
    x = x_ref[0]
    a = _dot((x + pe_ref[0:1]).astype(BF), w_ref[0])
    b = _dot((x + pe_ref[1:2]).astype(BF), w_ref[1])
    nch = x.shape[0]
    y = a + pltpu.roll(b, nch - 1, 0)
    lane = lax.broadcasted_iota(jnp.int32, y.shape, 1)
    ms = jnp.sum(jnp.where(lane < HEAD_DIM, y * y, 0.0), axis=1, keepdims=True) / HEAD_DIM
    kn = y * lax.rsqrt(ms + RMS_EPS) * gain_ref[...]
    k_ref[0] = kn[:, :HEAD_DIM].astype(BF)
    v_ref[0] = y[:, HEAD_DIM:].astype(BF)


def nsa_compress_weights(cmp_pe, cmp_w, gain1):
    S = NSA_CMP_STRIDE
    pe = cmp_pe.reshape(2, 2, S, HEAD_DIM)
    pe_flat = jnp.transpose(pe, (1, 2, 0, 3)).reshape(2, S * 2 * HEAD_DIM)
    w = cmp_w.reshape(2, 2, S, HEAD_DIM, HEAD_DIM)
    wz = jnp.zeros((2, S, 2, HEAD_DIM, 2, HEAD_DIM), F32)
    wz = wz.at[:, :, 0, :, 0, :].set(w[0]).at[:, :, 1, :, 1, :].set(w[1])
    w_flat = wz.reshape(2, S * 2 * HEAD_DIM, 2 * HEAD_DIM).astype(BF)
    gain = jnp.concatenate([gain1, jnp.ones((HEAD_DIM,), F32)])[None, :]
    return pe_flat, w_flat, gain


def nsa_compress(kc, B, T, pe_flat, w_flat, gain):
    nch = T // NSA_CMP_STRIDE
    cw = NSA_CMP_STRIDE * 2 * HEAD_DIM
    x = kc.reshape(B, nch, cw)
    out_spec = pl.BlockSpec((1, nch, HEAD_DIM), lambda b: (b, 0, 0))
    return pl.pallas_call(
        _nsa_compress_kernel,
        grid=(B,),
        in_specs=[pl.BlockSpec((1, nch, cw), lambda b: (b, 0, 0)),
                  pl.BlockSpec((2, cw), lambda b: (0, 0)),
                  pl.BlockSpec((2, cw, 2 * HEAD_DIM), lambda b: (0, 0, 0)),
                  pl.BlockSpec((1, 2 * HEAD_DIM), lambda b: (0, 0))],
        out_specs=[out_spec, out_spec],
        out_shape=[jax.ShapeDtypeStruct((B, nch, HEAD_DIM), BF)] * 2,
        compiler_params=_cparams(("parallel",)),
        name="nsa_compress",
    )(x, pe_flat, w_flat, gain)


def _softmax_update(s, mask, m_i, l_i, acc, v, lead):
    s = jnp.where(mask, s, NEG_INF)
    m_new = jnp.maximum(m_i, jnp.max(s, axis=-1, keepdims=True))
    p = jnp.where(mask, jnp.exp(s - m_new), 0.0)
    alpha = jnp.exp(m_i - m_new)
    l_new = alpha * l_i + jnp.sum(p, axis=-1, keepdims=True)
    pv = _dot(p.astype(BF).reshape(lead, p.shape[-1]), v).reshape(acc.shape)
    return m_new, l_new, alpha * acc + pv


def _nsa_attn_kernel(qn_ref, qr_ref, kc_ref, vc_ref, ks_ref, vs_ref, kw_ref, vw_ref, gl_ref,
                     cover_ref, expand_ref, o_ref, m_scr, l_scr, acc_scr, *, T):
    tq, tk = NSA_TQ, NSA_TK
    M = NH * tq
    i = pl.program_id(1)
    p0 = i * tq
    qn = qn_ref[0].reshape(M, HEAD_DIM)
    qr = qr_ref[0].reshape(M, HEAD_DIM)
    pos = p0 + lax.broadcasted_iota(jnp.int32, (tq, 1), 0)

    nch = kc_ref.shape[1]
    s_c = _dot_nt(qn, kc_ref[0]).reshape(NH, tq, nch)
    n_idx = lax.broadcasted_iota(jnp.int32, (tq, nch), 1)
    mask_c = (n_idx * NSA_CMP_STRIDE + (NSA_CMP_BLOCK - 1) <= pos)[None]
    s_c = jnp.where(mask_c, s_c, NEG_INF)
    m_c = jnp.max(s_c, axis=-1, keepdims=True)
    p_c = jnp.where(mask_c, jnp.exp(s_c - m_c), 0.0)
    p_c = p_c / jnp.maximum(jnp.sum(p_c, axis=-1, keepdims=True), 1.0)
    p_cb = p_c.astype(BF).reshape(M, nch)
    o_cmp = _dot(p_cb, vc_ref[0]).reshape(NH, tq, HEAD_DIM)
    imp = jnp.sum(_dot(p_cb, cover_ref[...]).reshape(NH, tq, LANES), axis=0)

    j = lax.broadcasted_iota(jnp.int32, (tq, LANES), 1)
    cur = pos // NSA_SEL_BLOCK
    forced = (j == 0) | (j == cur) | (j == cur - 1)
    imp = jnp.where(j > cur, NEG_INF, jnp.where(forced, SEL_FORCE, imp))
    n_sel = T // NSA_SEL_BLOCK
    rank = jnp.zeros((tq, LANES), F32)
    for jp in range(n_sel):
        col = imp[:, jp:jp + 1]
        beats = (col > imp) | ((col == imp) & (j > jp))
        rank = rank + jnp.where(beats, 1.0, 0.0)
    sel = jnp.where((rank < NSA_TOPN) & (j <= cur), 1.0, 0.0).astype(BF)

    _flash_init(m_scr, l_scr, acc_scr)
    kcol = lax.broadcasted_iota(jnp.int32, (tq, tk), 1)
    for kj in range(T // tk):
        @pl.when(kj * tk <= p0 + (tq - 1))
        def _(kj=kj):
            k = ks_ref[0, kj * tk:(kj + 1) * tk, :]
            v = vs_ref[0, kj * tk:(kj + 1) * tk, :]
            e = _dot(sel, expand_ref[kj])
            mask = ((e > 0.5) & (kcol + kj * tk <= pos))[None]
            s = jnp.where(mask, _dot_nt(qr, k).reshape(NH, tq, tk), NEG_INF)
            _flash_step(s, v, m_scr, l_scr, acc_scr)
    o_sel = acc_scr[...] / jnp.maximum(l_scr[...], 1.0)

    band = NSA_WINDOW + tq
    start = pl.multiple_of(jnp.maximum(p0 - NSA_WINDOW, 0), tq)
    kw = kw_ref[0, pl.ds(start, band), :]
    vw = vw_ref[0, pl.ds(start, band), :]
    s_w = _dot_nt(qr, kw).reshape(NH, tq, band)
    kpos = start + lax.broadcasted_iota(jnp.int32, (tq, band), 1)
    mask_w = ((kpos <= pos) & (kpos > pos - NSA_WINDOW))[None]
    s_w = jnp.where(mask_w, s_w, NEG_INF)
    m_w = jnp.max(s_w, axis=-1, keepdims=True)
    p_w = jnp.where(mask_w, jnp.exp(s_w - m_w), 0.0)
    l_w = jnp.sum(p_w, axis=-1, keepdims=True)
    o_win = _dot(p_w.astype(BF).reshape(M, band), vw).reshape(NH, tq, HEAD_DIM) / jnp.maximum(l_w, 1.0)

    sig = jax.nn.sigmoid(gl_ref[...])
    outs = []
    for h in range(NH):
        c = MISC_LANE + 3 * h
        outs.append(sig[:, c:c + 1] * o_cmp[h] + sig[:, c + 1:c + 2] * o_sel[h] + sig[:, c + 2:c + 3] * o_win[h])
    o_ref[...] = jnp.concatenate(outs, axis=1).astype(o_ref.dtype)


def nsa_constants(T):
    nch = T // NSA_CMP_STRIDE
    n_cmp = (T - NSA_CMP_BLOCK) // NSA_CMP_STRIDE + 1
    ci = np.arange(nch)[:, None] * NSA_CMP_STRIDE
    sj = np.arange(LANES)[None, :] * NSA_SEL_BLOCK
    cover = (ci < sj + NSA_SEL_BLOCK) & (ci + NSA_CMP_BLOCK > sj) & (np.arange(nch)[:, None] < n_cmp)
    nkt = T // NSA_TK
    blk = (np.arange(nkt)[:, None, None] * NSA_TK + np.arange(NSA_TK)[None, None, :]) // NSA_SEL_BLOCK
    expand = blk == np.arange(LANES)[None, :, None]
    return jnp.asarray(cover.astype(np.float32), dtype=BF), jnp.asarray(expand.astype(np.float32), dtype=BF)


def nsa_attention_prompt(pp, kc, vc, z, row0, B, T, cover, expand):
    tq = NSA_TQ
    nq = T // tq
    rb0 = row0 // tq
    nch = T // NSA_CMP_STRIDE
    assert T >= NSA_WINDOW + tq and T % NSA_TK == 0
    headmaj = pl.BlockSpec((1, NH, tq, HEAD_DIM), lambda b, i: (b, 0, i, 0))
    full1 = pl.BlockSpec((1, T, HEAD_DIM), lambda b, i: (b, 0, 0))
    cmp1 = pl.BlockSpec((1, nch, HEAD_DIM), lambda b, i: (b, 0, 0))
    return pl.pallas_call(
        functools.partial(_nsa_attn_kernel, T=T),
        grid=(B, nq),
        in_specs=[headmaj, headmaj, cmp1, cmp1, full1, full1, full1, full1,
                  pl.BlockSpec((tq, HW), lambda b, i: (rb0 + b * nq + i, SEC_S2)),
                  pl.BlockSpec((nch, LANES), lambda b, i: (0, 0)),
                  pl.BlockSpec((T // NSA_TK, LANES, NSA_TK), lambda b, i: (0, 0, 0))],
        out_specs=pl.BlockSpec((tq, HW), lambda b, i: (b * nq + i, 0)),
        out_shape=jax.ShapeDtypeStruct((B * T, HW), BF),
        scratch_shapes=[pltpu.VMEM((NH, tq, 1), F32), pltpu.VMEM((NH, tq, 1), F32),
                        pltpu.VMEM((NH, tq, HEAD_DIM), F32)],
        compiler_params=_cparams(("parallel", "parallel")),
        name="nsa_attention",
    )(pp["nqn"], pp["nqr"], kc, vc, pp["nks"], pp["nvs"], pp["nkw"], pp["nvw"], z, cover, expand)


def _flash_init(m_scr, l_scr, acc_scr):
    m_scr[...] = jnp.full_like(m_scr, M_FLOOR)
    l_scr[...] = jnp.zeros_like(l_scr)
    acc_scr[...] = jnp.zeros_like(acc_scr)


def _flash_step(s, v, m_scr, l_scr, acc_scr):
    m_i = m_scr[...]
    m_new = jnp.maximum(m_i, jnp.max(s, axis=-1, keepdims=True))
    p = jnp.exp(s - m_new)
    alpha = jnp.exp(m_i - m_new)
    l_scr[...] = alpha * l_scr[...] + jnp.sum(p, axis=-1, keepdims=True)
    pv = _dot(p.astype(BF).reshape(-1, p.shape[-1]), v)
    acc_scr[...] = alpha * acc_scr[...] + pv.reshape(acc_scr.shape)
    m_scr[...] = m_new


def _moba_attn_kernel(q_ref, k_ref, v_ref, km_ref, o_ref, m_scr, l_scr, acc_scr):
    tq = MOBA_BLOCK
    tks = ATT_SB * MOBA_BLOCK
    n_sb = k_ref.shape[2] // tks
    qi = pl.program_id(1)
    lane = lax.broadcasted_iota(jnp.int32, (tq, LANES), 1)
    qrow = lax.broadcasted_iota(jnp.int32, (tq, tks), 0)
    kcol = lax.broadcasted_iota(jnp.int32, (tq, tks), 1)
    sels = []
    for h in range(NH):
        g = jnp.where(lane < qi, _dot_nt(q_ref[0, h], km_ref[0, h]), NEG_INF)
        sel = jnp.where(lane == qi, 1.0, 0.0)
        for _ in range(MOBA_TOPK):
            m = jnp.max(g, axis=1, keepdims=True)
            idx = jnp.min(jnp.where(g == m, lane, LANES), axis=1, keepdims=True)
            hit = lane == idx
            sel = jnp.where(hit & (lane < qi), 1.0, sel)
            g = jnp.where(hit, NEG_INF, g)
        sels.append(sel)
    _flash_init(m_scr, l_scr, acc_scr)
    for sb in range(n_sb):
        @pl.when(sb * ATT_SB <= qi)
        def _(sb=sb):
            causal = kcol + (sb * tks) <= qrow + qi * tq
            for h in range(NH):
                k = k_ref[0, h, sb * tks:(sb + 1) * tks, :]
                v = v_ref[0, h, sb * tks:(sb + 1) * tks, :]
                picked = jnp.concatenate(
                    [jnp.broadcast_to(sels[h][:, j:j + 1] > 0.5, (tq, MOBA_BLOCK))
                     for j in range(sb * ATT_SB, (sb + 1) * ATT_SB)], axis=1)
                s = jnp.where(picked & causal, _dot_nt(q_ref[0, h], k), NEG_INF)
                _flash_step(s, v, m_scr.at[h], l_scr.at[h], acc_scr.at[h])
    o = acc_scr[...] / jnp.maximum(l_scr[...], 1.0)
    o_ref[...] = jnp.concatenate([o[h] for h in range(NH)], axis=1).astype(o_ref.dtype)


def moba_attention_prompt(pp, B, T):
    tq = MOBA_BLOCK
    nq = T // tq
    km = pp["kmean"].reshape(B, nq, NH, HEAD_DIM).transpose(0, 2, 1, 3)
    km = jnp.pad(km, ((0, 0), (0, 0), (0, LANES - nq), (0, 0))).astype(BF)
    headq = pl.BlockSpec((1, NH, tq, HEAD_DIM), lambda b, i: (b, 0, i, 0))
    headfull = pl.BlockSpec((1, NH, T, HEAD_DIM), lambda b, i: (b, 0, 0, 0))
    return pl.pallas_call(
        _moba_attn_kernel,
        grid=(B, nq),
        in_specs=[headq, headfull, headfull,
                  pl.BlockSpec((1, NH, LANES, HEAD_DIM), lambda b, i: (b, 0, 0, 0))],
        out_specs=pl.BlockSpec((tq, HW), lambda b, i: (b * nq + i, 0)),
        out_shape=jax.ShapeDtypeStruct((B * T, HW), BF),
        scratch_shapes=[pltpu.VMEM((NH, tq, 1), F32), pltpu.VMEM((NH, tq, 1), F32),
                        pltpu.VMEM((NH, tq, HEAD_DIM), F32)],
        compiler_params=_cparams(("parallel", "parallel")),
        name="moba_attention",
    )(pp["mqr"], pp["mkr"], pp["mvv"], km)


def _fox_attn_kernel(q_ref, k_ref, v_ref, cq_ref, ck_ref, o_ref, m_scr, l_scr, acc_scr):
    tq = MOBA_BLOCK
    tks = ATT_SB * MOBA_BLOCK
    n_sb = k_ref.shape[2] // tks
    qi = pl.program_id(1)
    qrow = lax.broadcasted_iota(jnp.int32, (tq, tks), 0)
    kcol = lax.broadcasted_iota(jnp.int32, (tq, tks), 1)
    _flash_init(m_scr, l_scr, acc_scr)
    for sb in range(n_sb):
        last = (sb + 1) * ATT_SB - 1

        def region(causal, sb=sb):
            for h in range(NH):
                k = k_ref[0, h, sb * tks:(sb + 1) * tks, :]
                v = v_ref[0, h, sb * tks:(sb + 1) * tks, :]
                s = _dot_nt(q_ref[0, h], k) + cq_ref[0, h] - ck_ref[0, h, sb]
                if causal:
                    s = jnp.where(kcol + (sb * tks) <= qrow + qi * tq, s, NEG_INF)
                _flash_step(s, v, m_scr.at[h], l_scr.at[h], acc_scr.at[h])

        pl.when(last < qi)(functools.partial(region, False))
        pl.when((sb * ATT_SB <= qi) & (last >= qi))(functools.partial(region, True))
    o = acc_scr[...] / jnp.maximum(l_scr[...], 1.0)
    o_ref[...] = jnp.concatenate([o[h] for h in range(NH)], axis=1).astype(o_ref.dtype)


def fox_attention_prompt(pp, c, B, T):
    tq = MOBA_BLOCK
    nq = T // tq
    ch = jnp.transpose(c, (0, 2, 1))
    cq = ch[..., None]
    tks = ATT_SB * MOBA_BLOCK
    ck = ch.reshape(B, NH, T // tks, 1, tks)
    headq = pl.BlockSpec((1, NH, tq, HEAD_DIM), lambda b, i: (b, 0, i, 0))
    headfull = pl.BlockSpec((1, NH, T, HEAD_DIM), lambda b, i: (b, 0, 0, 0))
    return pl.pallas_call(
        _fox_attn_kernel,
        grid=(B, nq),
        in_specs=[headq, headfull, headfull,
                  pl.BlockSpec((1, NH, tq, 1), lambda b, i: (b, 0, i, 0)),
                  pl.BlockSpec((1, NH, T // tks, 1, tks), lambda b, i: (b, 0, 0, 0, 0))],
        out_specs=pl.BlockSpec((tq, HW), lambda b, i: (b * nq + i, 0)),
        out_shape=jax.ShapeDtypeStruct((B * T, HW), BF),
        scratch_shapes=[pltpu.VMEM((NH, tq, 1), F32), pltpu.VMEM((NH, tq, 1), F32),
                        pltpu.VMEM((NH, tq, HEAD_DIM), F32)],
        compiler_params=_cparams(("parallel", "parallel")),
        name="fox_attention",
    )(pp["fqn"], pp["fkn"], pp["fvv"], cq, ck)


def _hgrn_kernel(q_ref, f_ref, i_ref, g_ref, s0_ref, lb_ref, gain_ref, bd_ref, gmat_ref,
                 o_ref, sout_ref, st_scr, *, C):
    c = pl.program_id(1)

    @pl.when(c == 0)
    def _():
        st_scr[...] = jnp.zeros_like(st_scr)
        for h in range(NH):
            st_scr[h * HEAD_DIM:(h + 1) * HEAD_DIM, h * HEAD_DIM:(h + 1) * HEAD_DIM] = s0_ref[0, h]

    lb = lb_ref[...]
    z = f_ref[...]
    logf = jnp.log(lb + (1.0 - lb) * jax.nn.sigmoid(z))
    kk = (1.0 - lb) * jax.nn.sigmoid(-z)
    q = q_ref[...]
    qf = q * jax.nn.sigmoid(q)
    v = i_ref[...]
    row = lax.broadcasted_iota(jnp.int32, (C, HW), 0)
    b = logf
    sh = 1
    while sh < C:
        b = b + jnp.where(row >= sh, pltpu.roll(b, sh, 0), 0.0)
        sh *= 2
    bd = bd_ref[...]
    st = st_scr[...]
    o_ref[...] = _dot_nt((qf * jnp.exp(b)).astype(BF), st.astype(BF))
    for s in range(C):
        causal = row >= s
        e = jnp.exp(jnp.where(causal, b - b[s:s + 1], 0.0))
        fz = jnp.where(causal, qf * kk[s:s + 1] * e, 0.0)
        o_ref[...] += _dot(fz.astype(BF), bd) * v[s:s + 1]
    o = o_ref[...]
    b_last = b[C - 1:C, :]
    kt = kk * jnp.exp(b_last - b)
    upd = lax.dot_general(v.astype(BF), kt.astype(BF), (((0,), (0,)), ((), ())), preferred_element_type=F32)
    r2 = lax.broadcasted_iota(jnp.int32, (HW, HW), 0) // HEAD_DIM
    c2 = lax.broadcasted_iota(jnp.int32, (HW, HW), 1) // HEAD_DIM
    st_new = st * jnp.exp(b_last) + jnp.where(r2 == c2, upd, 0.0)
    st_scr[...] = st_new
    g = g_ref[...]
    o_ref[...] = _head_rmsnorm(o, gain_ref[...], gmat_ref[...]) * (g * jax.nn.sigmoid(g))

    @pl.when(c == pl.num_programs(1) - 1)
    def _():
        for h in range(NH):
            sout_ref[0, h] = st_new[h * HEAD_DIM:(h + 1) * HEAD_DIM, h * HEAD_DIM:(h + 1) * HEAD_DIM]


def hgrn_mixer(z, row0, B, T, s0, lb, out_gain, gmat):
    C = math.gcd(T, HGRN_CHUNK)
    nc = T // C
    rb0 = row0 // C
    idx = np.arange(HW) // HEAD_DIM
    bd = jnp.asarray((idx[:, None] == idx[None, :]).astype(np.float32), dtype=BF)

    def sec(cidx):
        return pl.BlockSpec((C, HW), lambda b, c, cidx=cidx: (rb0 + b * nc + c, cidx))

    state = pl.BlockSpec((1, NH, HGRN_DV, HGRN_DK), lambda b, c: (b, 0, 0, 0))
    vec = pl.BlockSpec((1, HW), lambda b, c: (0, 0))
    mat = pl.BlockSpec((HW, HW), lambda b, c: (0, 0))
    o, st = pl.pallas_call(
        functools.partial(_hgrn_kernel, C=C),
        grid=(B, nc),
        in_specs=[sec(SEC_HQ), sec(SEC_HF), sec(SEC_HI), sec(SEC_HG), state, vec, vec, mat, mat],
        out_specs=[pl.BlockSpec((C, HW), lambda b, c: (b * nc + c, 0)), state],
        out_shape=[jax.ShapeDtypeStruct((B * T, HW), F32),
                   jax.ShapeDtypeStruct((B, NH, HGRN_DV, HGRN_DK), F32)],
        scratch_shapes=[pltpu.VMEM((HW, HW), F32)],
        compiler_params=_cparams(("parallel", "arbitrary")),
        name="hgrn_recurrence",
    )(z, z, z, z, jnp.swapaxes(s0, 2, 3), lb[None, :], jnp.tile(out_gain, NH)[None, :], bd, gmat)
    return o, jnp.swapaxes(st, 2, 3)


DEC_PG = 8


def _page_specs(tail, PG, base):
    return [pl.BlockSpec((1,) + tail, lambda b, s, pt, j=j: (base + pt[b, s * PG + j], 0, 0)) for j in range(PG)]


def _pool_view(pools, l, width):
    return pools.reshape(pools.shape[0] * pools.shape[1], PAGE_SIZE, width), l * pools.shape[1]


def _rows_nat(q_ref):
    return jnp.concatenate([q_ref[0, h] for h in range(NH)], axis=1)


def _block_diag_q(qnat, T):
    q4 = jnp.concatenate([qnat] * NH, axis=0)
    r = lax.broadcasted_iota(jnp.int32, q4.shape, 0) // T
    c = lax.broadcasted_iota(jnp.int32, q4.shape, 1) // HEAD_DIM
    return jnp.where(r == c, q4, 0.0)


def _diag_heads(x, T):
    return jnp.concatenate([x[h * T:(h + 1) * T, h * HEAD_DIM:(h + 1) * HEAD_DIM] for h in range(NH)], axis=1)


def _new_causal(R, T):
    tq = lax.broadcasted_iota(jnp.int32, (R, T), 0) % T
    tk = lax.broadcasted_iota(jnp.int32, (R, T), 1)
    return tk <= tq


def _fox_decode_kernel(pt_ref, q_ref, new_ref, cq_ref, cn_ref, ck_ref, *rest, PG, T):
    page_refs, o_ref = rest[:PG], rest[PG]
    m_scr, l_scr, acc_scr = rest[PG + 1:]
    s = pl.program_id(1)
    R = NH * T

    @pl.when(s == 0)
    def _():
        m_scr[...] = jnp.full_like(m_scr, NEG_INF)
        l_scr[...] = jnp.zeros_like(l_scr)
        acc_scr[...] = jnp.zeros_like(acc_scr)

    qbd = _block_diag_q(_rows_nat(q_ref), T).astype(BF)
    cq = cq_ref[0]
    m_i, l_i, acc = m_scr[...], l_scr[...], acc_scr[...]
    for j in range(PG):
        page = page_refs[j][0]
        k = page[:, :HW].astype(BF)
        v = page[:, HW:].astype(BF)
        ck = jnp.broadcast_to(ck_ref[0, j][:, None, :], (NH, T, PAGE_SIZE)).reshape(R, PAGE_SIZE)
        sc = _dot_nt(qbd, k) + cq - ck
        m_new = jnp.maximum(m_i, jnp.max(sc, axis=-1, keepdims=True))
        p = jnp.exp(sc - m_new)
        alpha = jnp.exp(m_i - m_new)
        l_i = alpha * l_i + jnp.sum(p, axis=-1, keepdims=True)
        acc = alpha * acc + _dot(p.astype(BF), v)
        m_i = m_new
    m_scr[...] = m_i
    l_scr[...] = l_i
    acc_scr[...] = acc

    @pl.when(s == pl.num_programs(1) - 1)
    def _():
        new = new_ref[...]
        sc = _dot_nt(qbd, new[:, :HW].astype(BF)) + cq - cn_ref[0]
        _, l_f, acc_f = _softmax_update(sc, _new_causal(R, T), m_i, l_i, acc, new[:, HW:].astype(BF), R)
        o_ref[...] = _diag_heads(acc_f / jnp.maximum(l_f, 1.0), T)


def fox_decode(sp, pools, l, page_table, c, row_new, B, T):
    n_pages = page_table.shape[1]
    PG = min(DEC_PG, n_pages)
    past = n_pages * PAGE_SIZE
    R = NH * T
    view, base = _pool_view(pools, l, 2 * HW)
    ch = jnp.transpose(c, (0, 2, 1))
    ck = ch[:, :, :past].reshape(B, NH, n_pages, PAGE_SIZE).transpose(0, 2, 1, 3)
    cnew = ch[:, :, past:]
    cq = cnew.reshape(B, R, 1)
    cn = jnp.broadcast_to(cnew[:, :, None, :], (B, NH, T, T)).reshape(B, R, T)
    return pl.pallas_call(
        functools.partial(_fox_decode_kernel, PG=PG, T=T),
        grid_spec=pltpu.PrefetchScalarGridSpec(
            num_scalar_prefetch=1, grid=(B, n_pages // PG),
            in_specs=[pl.BlockSpec((1, NH, T, HEAD_DIM), lambda b, s, pt: (0, 0, b, 0)),
                      pl.BlockSpec((T, 2 * HW), lambda b, s, pt: (b, 0)),
                      pl.BlockSpec((1, R, 1), lambda b, s, pt: (b, 0, 0)),
                      pl.BlockSpec((1, R, T), lambda b, s, pt: (b, 0, 0)),
                      pl.BlockSpec((1, PG, NH, PAGE_SIZE), lambda b, s, pt: (b, s, 0, 0))]
            + _page_specs((PAGE_SIZE, 2 * HW), PG, base),
            out_specs=pl.BlockSpec((T, HW), lambda b, s, pt: (b, 0)),
            scratch_shapes=[pltpu.VMEM((R, 1), F32), pltpu.VMEM((R, 1), F32), pltpu.VMEM((R, HW), F32)]),
        out_shape=jax.ShapeDtypeStruct((B * T, HW), F32),
        compiler_params=_cparams(("parallel", "arbitrary")),
        name="fox_decode",
    )(page_table, sp["fqn"], row_new, cq, cn, ck, *([view] * PG))


def _moba_decode_kernel(pt_ref, q_ref, new_ref, *rest, PG, T, n_pages):
    page_refs, o_ref = rest[:PG], rest[PG]
    m_scr, l_scr, rs_scr, acc_scr = rest[PG + 1:]
    s = pl.program_id(1)
    R = NH * T
    lane = lax.broadcasted_iota(jnp.int32, (R, LANES), 1)

    @pl.when(s == 0)
    def _():
        m_scr[...] = jnp.full_like(m_scr, NEG_INF)
        l_scr[...] = jnp.zeros_like(l_scr)
        rs_scr[...] = jnp.zeros_like(rs_scr)

    qbd = _block_diag_q(_rows_nat(q_ref), T).astype(BF)
    mm, ll, rs = m_scr[...], l_scr[...], rs_scr[...]
    for j in range(PG):
        pid = s * PG + j
        page = page_refs[j][0]
        sc = _dot_nt(qbd, page[:, :HW].astype(BF))
        m = jnp.max(sc, axis=-1, keepdims=True)
        p = jnp.exp(sc - m)
        acc_scr[pid] = _dot(p.astype(BF), page[:, HW:].astype(BF))
        hit = lane == pid
        mm = jnp.where(hit, m, mm)
        ll = jnp.where(hit, jnp.sum(p, axis=-1, keepdims=True), ll)
        rs = jnp.where(hit, jnp.sum(sc, axis=-1, keepdims=True), rs)
    m_scr[...] = mm
    l_scr[...] = ll
    rs_scr[...] = rs

    @pl.when(s == pl.num_programs(1) - 1)
    def _():
        block_lane = (lane % 2 == 0) & (lane < n_pages)
        g = jnp.where(block_lane, (rs + pltpu.roll(rs, LANES - 1, 1)) * (1.0 / MOBA_BLOCK), NEG_INF)
        sel = jnp.zeros((R, LANES), F32)
        for _ in range(MOBA_TOPK):
            mx = jnp.max(g, axis=1, keepdims=True)
            idx = jnp.min(jnp.where(g == mx, lane, LANES), axis=1, keepdims=True)
            hit = lane == idx
            sel = jnp.where(hit & block_lane, 1.0, sel)
            g = jnp.where(hit, NEG_INF, g)
        picked = (sel + pltpu.roll(sel, 1, 1)) > 0.5
        new = new_ref[...]
        mask_n = _new_causal(R, T)
        sn = jnp.where(mask_n, _dot_nt(qbd, new[:, :HW].astype(BF)), NEG_INF)
        m_n = jnp.max(sn, axis=-1, keepdims=True)
        p_n = jnp.where(mask_n, jnp.exp(sn - m_n), 0.0)
        m_tot = jnp.maximum(jnp.max(jnp.where(picked, mm, NEG_INF), axis=1, keepdims=True), m_n)
        w = jnp.where(picked, jnp.exp(mm - m_tot), 0.0)
        w_n = jnp.exp(m_n - m_tot)
        l_tot = jnp.sum(w * ll, axis=1, keepdims=True) + w_n * jnp.sum(p_n, axis=-1, keepdims=True)
        acc = w_n * _dot(p_n.astype(BF), new[:, HW:].astype(BF))
        for pg in range(n_pages):
            acc = acc + w[:, pg:pg + 1] * acc_scr[pg]
        o_ref[...] = _diag_heads(acc / jnp.maximum(l_tot, 1.0), T)


def moba_decode(sp, pools, l, page_table, row_new, B, T):
    n_pages = page_table.shape[1]
    assert MOBA_BLOCK == 2 * PAGE_SIZE and n_pages % 2 == 0 and n_pages <= LANES and T <= MOBA_BLOCK
    PG = min(DEC_PG, n_pages)
    R = NH * T
    view, base = _pool_view(pools, l, 2 * HW)
    return pl.pallas_call(
        functools.partial(_moba_decode_kernel, PG=PG, T=T, n_pages=n_pages),
        grid_spec=pltpu.PrefetchScalarGridSpec(
            num_scalar_prefetch=1, grid=(B, n_pages // PG),
            in_specs=[pl.BlockSpec((1, NH, T, HEAD_DIM), lambda b, s, pt: (0, 0, b, 0)),
                      pl.BlockSpec((T, 2 * HW), lambda b, s, pt: (b, 0))]
            + _page_specs((PAGE_SIZE, 2 * HW), PG, base),
            out_specs=pl.BlockSpec((T, HW), lambda b, s, pt: (b, 0)),
            scratch_shapes=[pltpu.VMEM((R, LANES), F32), pltpu.VMEM((R, LANES), F32), pltpu.VMEM((R, LANES), F32),
                            pltpu.VMEM((n_pages, R, HW), F32)]),
        out_shape=jax.ShapeDtypeStruct((B * T, HW), F32),
        compiler_params=_cparams(("parallel", "arbitrary")),
        name="moba_decode",
    )(page_table, sp["mqr"], row_new, *([view] * PG))


def _nsa_cmp_phys_kernel(x_ref, pe_ref, w_ref, y_ref):
    x = x_ref[...]
    a = _dot((x + pe_ref[0:1]).astype(BF), w_ref[0])
    b = _dot((x + pe_ref[1:2]).astype(BF), w_ref[1])
    y_ref[...] = jnp.concatenate([a, b], axis=1)


def nsa_compress_pool(pools, l, cmp_pe, cmp_w):
    S = NSA_CMP_STRIDE
    n_phys = pools.shape[1]
    rows = n_phys * (PAGE_SIZE // S)
    cw = S * NSA_ROWS * HEAD_DIM
    pe = jnp.transpose(cmp_pe.reshape(2, 2, S, HEAD_DIM), (1, 2, 0, 3))
    pe = jnp.pad(pe, ((0, 0), (0, 0), (0, 2), (0, 0))).reshape(2, cw)
    w = cmp_w.reshape(2, 2, S, HEAD_DIM, HEAD_DIM)
    wz = jnp.zeros((2, S, NSA_ROWS, HEAD_DIM, 2, HEAD_DIM), F32)
    wz = wz.at[:, :, 0, :, 0, :].set(w[0]).at[:, :, 1, :, 1, :].set(w[1])
    wz = wz.reshape(2, cw, 2 * HEAD_DIM).astype(BF)
    tm = next(t for t in (512, 256, 128, 64, 32, 16, 8) if rows % t == 0)
    return pl.pallas_call(
        _nsa_cmp_phys_kernel,
        grid=(rows // tm,),
        in_specs=[pl.BlockSpec((tm, cw), lambda i: (l * (rows // tm) + i, 0)),
                  pl.BlockSpec((2, cw), lambda i: (0, 0)),
                  pl.BlockSpec((2, cw, 2 * HEAD_DIM), lambda i: (0, 0, 0))],
        out_specs=pl.BlockSpec((tm, HW), lambda i: (i, 0)),
        out_shape=jax.ShapeDtypeStruct((rows, HW), F32),
        compiler_params=_cparams(("parallel",)),
        name="nsa_compress_pool",
    )(pools.reshape(pools.shape[0] * rows, cw), pe, wz)


def _nsa_sel_decode_kernel(y_ref, q_ref, gain_ref, cover_ref, ocmp_ref, sel_ref, *, T, offset, n_cmp, n_sel):
    R = NH * T
    y = y_ref[0]
    nch = y.shape[0]
    c = y[:, :2 * HEAD_DIM] + pltpu.roll(y[:, 2 * HEAD_DIM:], nch - 1, 0)
    lane = lax.broadcasted_iota(jnp.int32, c.shape, 1)
    ms = jnp.sum(jnp.where(lane < HEAD_DIM, c * c, 0.0), axis=1, keepdims=True) / HEAD_DIM
    kn = c * lax.rsqrt(ms + RMS_EPS) * gain_ref[...]
    kc = kn[:, :HEAD_DIM].astype(BF)
    vc = c[:, HEAD_DIM:].astype(BF)
    q = q_ref[0].reshape(R, HEAD_DIM).astype(BF)
    pos = offset + lax.broadcasted_iota(jnp.int32, (R, 1), 0) % T
    n_idx = lax.broadcasted_iota(jnp.int32, (R, nch), 1)
    mask = (n_idx * NSA_CMP_STRIDE + (NSA_CMP_BLOCK - 1) <= pos) & (n_idx < n_cmp)
    s_c = jnp.where(mask, _dot_nt(q, kc), NEG_INF)
    m_c = jnp.max(s_c, axis=-1, keepdims=True)
    p_c = jnp.where(mask, jnp.exp(s_c - m_c), 0.0)
    p_c = (p_c / jnp.maximum(jnp.sum(p_c, axis=-1, keepdims=True), 1.0)).astype(BF)
    ocmp_ref[0] = _dot(p_c, vc)
    imp4 = _dot(p_c, cover_ref[...])
    imp = imp4[0:T]
    for h in range(1, NH):
        imp = imp + imp4[h * T:(h + 1) * T]
    j = lax.broadcasted_iota(jnp.int32, imp.shape, 1)
    cur = (offset + lax.broadcasted_iota(jnp.int32, (T, 1), 0)) // NSA_SEL_BLOCK
    forced = (j == 0) | (j == cur) | (j == cur - 1)
    imp = jnp.where(j > cur, NEG_INF, jnp.where(forced, SEL_FORCE, imp))
    rank = jnp.zeros(imp.shape, F32)
    for jp in range(n_sel):
        col = imp[:, jp:jp + 1]
        rank = rank + jnp.where((col > imp) | ((col == imp) & (j > jp)), 1.0, 0.0)
    sel_ref[0] = jnp.where((rank < NSA_TOPN) & (j <= cur), 1.0, 0.0)


def _nsa_decode_kernel(pt_ref, q_ref, sel_ref, ocmp_ref, newr_ref, win_ref, neww_ref, gl_ref, *rest,
                       PG, T, offset):
    page_refs, o_ref = rest[:PG], rest[PG]
    m_scr, l_scr, acc_scr = rest[PG + 1:]
    s = pl.program_id(1)
    R = NH * T
    D = HEAD_DIM

    @pl.when(s == 0)
    def _():
        m_scr[...] = jnp.full_like(m_scr, NEG_INF)
        l_scr[...] = jnp.zeros_like(l_scr)
        acc_scr[...] = jnp.zeros_like(acc_scr)

    q = q_ref[0].reshape(R, D).astype(BF)
    sel = sel_ref[0]
    jl = lax.broadcasted_iota(jnp.int32, sel.shape, 1)
    half = lax.broadcasted_iota(jnp.int32, (T, PAGE_SIZE), 1) < NSA_SEL_BLOCK
    m_i, l_i, acc = m_scr[...], l_scr[...], acc_scr[...]
    for j in range(PG):
        pid = s * PG + j
        page = page_refs[j][0]
        sc = _dot_nt(q, page[:, 2 * D:3 * D].astype(BF))
        sa = jnp.max(jnp.where(jl == 2 * pid, sel, 0.0), axis=1, keepdims=True)
        sb = jnp.max(jnp.where(jl == 2 * pid + 1, sel, 0.0), axis=1, keepdims=True)
        mt = jnp.where(half, sa, sb) > 0.5
        mask = jnp.broadcast_to(mt[None], (NH, T, PAGE_SIZE)).reshape(R, PAGE_SIZE)
        m_i, l_i, acc = _softmax_update(sc, mask, m_i, l_i, acc, page[:, 3 * D:].astype(BF), R)
    m_scr[...] = m_i
    l_scr[...] = l_i
    acc_scr[...] = acc

    @pl.when(s == pl.num_programs(1) - 1)
    def _():
        causal = _new_causal(R, T)
        newr = newr_ref[...]
        cb = offset // NSA_SEL_BLOCK
        own = jnp.broadcast_to((sel[:, cb:cb + 1] > 0.5)[None], (NH, T, T)).reshape(R, T)
        _, l_s, acc_s = _softmax_update(_dot_nt(q, newr[:, 2 * D:3 * D].astype(BF)), causal & own, m_i, l_i, acc,
                                        newr[:, 3 * D:].astype(BF), R)
        o_sel = acc_s / jnp.maximum(l_s, 1.0)
        win = win_ref[0]
        neww = neww_ref[...]
        wb = win.shape[0]
        tq = lax.broadcasted_iota(jnp.int32, (R, wb), 0) % T
        rk = lax.broadcasted_iota(jnp.int32, (R, wb), 1)
        mask1 = rk > wb + tq - NSA_WINDOW
        s1 = jnp.where(mask1, _dot_nt(q, win[:, :D].astype(BF)), NEG_INF)
        s2 = jnp.where(causal, _dot_nt(q, neww[:, :D].astype(BF)), NEG_INF)
        m_w = jnp.maximum(jnp.max(s1, axis=-1, keepdims=True), jnp.max(s2, axis=-1, keepdims=True))
        p1 = jnp.where(mask1, jnp.exp(s1 - m_w), 0.0)
        p2 = jnp.where(causal, jnp.exp(s2 - m_w), 0.0)
        l_w = jnp.sum(p1, axis=-1, keepdims=True) + jnp.sum(p2, axis=-1, keepdims=True)
        o_win = (_dot(p1.astype(BF), win[:, D:].astype(BF)) + _dot(p2.astype(BF), neww[:, D:].astype(BF))) \
            / jnp.maximum(l_w, 1.0)
        o_cmp = ocmp_ref[0]
        sig = jax.nn.sigmoid(gl_ref[...])
        outs = []
        for h in range(NH):
            c = MISC_LANE + 3 * h
            rs = slice(h * T, (h + 1) * T)
            outs.append(sig[:, c:c + 1] * o_cmp[rs] + sig[:, c + 1:c + 2] * o_sel[rs] + sig[:, c + 2:c + 3] * o_win[rs])
        o_ref[...] = jnp.concatenate(outs, axis=1)


def nsa_decode(sp, z, row0, pools, l, page_table, win_state, cmp_pe, cmp_w, gain1, B, T):
    n_pages = page_table.shape[1]
    past = n_pages * PAGE_SIZE
    S = past + T
    R = NH * T
    n_cmp = (S - NSA_CMP_BLOCK) // NSA_CMP_STRIDE + 1
    n_sel = -(-S // NSA_SEL_BLOCK)
    nch = past // NSA_CMP_STRIDE
    assert (n_cmp - 1) * NSA_CMP_STRIDE + NSA_CMP_BLOCK <= past, "compressed blocks must lie inside the cache"
    assert past % NSA_SEL_BLOCK == 0 and T <= NSA_SEL_BLOCK and n_sel <= HW
    PG = min(DEC_PG, n_pages)
    cpp = PAGE_SIZE // NSA_CMP_STRIDE
    y = nsa_compress_pool(pools, l, cmp_pe, cmp_w).reshape(pools.shape[1], cpp, HW)
    yb = y[page_table].reshape(B, nch, HW)
    ci = np.arange(nch)[:, None] * NSA_CMP_STRIDE
    sj = np.arange(HW)[None, :] * NSA_SEL_BLOCK
    cover = (ci < sj + NSA_SEL_BLOCK) & (ci + NSA_CMP_BLOCK > sj) & (np.arange(nch)[:, None] < n_cmp) \
        & (np.arange(HW)[None, :] < n_sel)
    cover = jnp.asarray(cover.astype(np.float32), dtype=BF)
    gain = jnp.concatenate([gain1, jnp.ones((HEAD_DIM,), F32)])[None, :]
    qspec = lambda nargs: pl.BlockSpec((1, NH, T, HEAD_DIM), (lambda b: (0, 0, b, 0)) if nargs == 1
                                       else (lambda b, s, pt: (0, 0, b, 0)))
    ocmp, sel = pl.pallas_call(
        functools.partial(_nsa_sel_decode_kernel, T=T, offset=past, n_cmp=n_cmp, n_sel=n_sel),
        grid=(B,),
        in_specs=[pl.BlockSpec((1, nch, HW), lambda b: (b, 0, 0)), qspec(1),
                  pl.BlockSpec((1, 2 * HEAD_DIM), lambda b: (0, 0)),
                  pl.BlockSpec((nch, HW), lambda b: (0, 0))],
        out_specs=[pl.BlockSpec((1, R, HEAD_DIM), lambda b: (b, 0, 0)), pl.BlockSpec((1, T, HW), lambda b: (b, 0, 0))],
        out_shape=[jax.ShapeDtypeStruct((B, R, HEAD_DIM), F32), jax.ShapeDtypeStruct((B, T, HW), F32)],
        compiler_params=_cparams(("parallel",)),
        name="nsa_select_decode",
    )(yb, sp["nqn"], gain, cover)
    view, base = _pool_view(pools, l, HW)
    wb = win_state.shape[1]
    rb0 = row0 // T
    return pl.pallas_call(
        functools.partial(_nsa_decode_kernel, PG=PG, T=T, offset=past),
        grid_spec=pltpu.PrefetchScalarGridSpec(
            num_scalar_prefetch=1, grid=(B, n_pages // PG),
            in_specs=[qspec(3),
                      pl.BlockSpec((1, T, HW), lambda b, s, pt: (b, 0, 0)),
                      pl.BlockSpec((1, R, HEAD_DIM), lambda b, s, pt: (b, 0, 0)),
                      pl.BlockSpec((T, HW), lambda b, s, pt: (b, 0)),
                      pl.BlockSpec((1, wb, 2 * HEAD_DIM), lambda b, s, pt: (b, 0, 0)),
                      pl.BlockSpec((T, 2 * HEAD_DIM), lambda b, s, pt: (b, 0)),
                      pl.BlockSpec((T, HW), lambda b, s, pt: (rb0 + b, SEC_S2))]
            + _page_specs((PAGE_SIZE, HW), PG, base),
            out_specs=pl.BlockSpec((T, HW), lambda b, s, pt: (b, 0)),
            scratch_shapes=[pltpu.VMEM((R, 1), F32), pltpu.VMEM((R, 1), F32), pltpu.VMEM((R, HEAD_DIM), F32)]),
        out_shape=jax.ShapeDtypeStruct((B * T, HW), F32),
        compiler_params=_cparams(("parallel", "arbitrary")),
        name="nsa_decode",
    )(page_table, sp["nqr"], sel, ocmp, sp["nsa_rows"], win_state.reshape(B, wb, 2 * HEAD_DIM), sp["nsa_win"], z,
      *([view] * PG))


def rmsnorm(x, g):
    xf = x.astype(jnp.float32)
    y = xf * lax.rsqrt(jnp.mean(xf * xf, axis=-1, keepdims=True) + RMS_EPS)
    return (y * g.astype(jnp.float32)).astype(x.dtype)


def rope(x, pos):
    half = HEAD_DIM // 2
    inv = ROPE_THETA ** (-jnp.arange(half, dtype=jnp.float32) / half)
    ang = pos.astype(jnp.float32)[:, None] * inv[None, :]
    cos = jnp.cos(ang)[:, None, :]
    sin = jnp.sin(ang)[:, None, :]
    xf = x.astype(jnp.float32)
    x1, x2 = xf[..., :half], xf[..., half:]
    return jnp.concatenate([x1 * cos - x2 * sin, x2 * cos + x1 * sin], axis=-1).astype(x.dtype)


def masked_softmax(s, mask):
    s = jnp.where(mask, s.astype(jnp.float32), NEG_INF)
    m = jnp.max(s, axis=-1, keepdims=True)
    p = jnp.where(mask, jnp.exp(s - m), 0.0)
    return p / jnp.maximum(jnp.sum(p, axis=-1, keepdims=True), 1.0)


def sweep(fn, blk, *arrays):
    B, T = arrays[0].shape[:2]
    nb = -(-T // blk)
    Tp = nb * blk
    blocks = []
    for a in arrays:
        a = jnp.pad(a, [(0, 0), (0, Tp - T)] + [(0, 0)] * (a.ndim - 2))
        blocks.append(jnp.moveaxis(a.reshape((B, nb, blk) + a.shape[2:]), 1, 0))
    starts = jnp.arange(nb, dtype=jnp.int32) * blk
    out = lax.map(lambda args: fn(args[0], *args[1]), (starts, tuple(blocks)))
    out = jnp.moveaxis(out, 0, 1).reshape((B, Tp) + out.shape[3:])
    return out[:, :T]


def window_attend(q, rows, buf_len):
    B, T, H, Dh = q.shape
    dt = q.dtype
    W = NSA_WINDOW
    qb = min(Q_BLOCK, T)
    nb = -(-T // qb)
    Tp = nb * qb
    band = W + qb
    rp = jnp.pad(rows, ((0, 0), (W, Tp - T), (0, 0), (0, 0)))
    kidx = buf_len + np.arange(nb)[:, None] * qb + np.arange(band)[None, :]
    kb = rp[:, kidx]
    qp = jnp.pad(q, ((0, 0), (0, Tp - T), (0, 0), (0, 0))).reshape(B, nb, qb, H, Dh)
    s = jnp.einsum('bnqhd,bnkd->bhnqk', qp, kb[..., 0, :], preferred_element_type=jnp.float32) * ATTN_SCALE
    qq = buf_len + np.arange(nb)[:, None] * qb + np.arange(qb)[None, :]
    kk = kidx - W
    mask = (kk[:, None, :] >= 0) & (kk[:, None, :] <= qq[:, :, None]) & (kk[:, None, :] > qq[:, :, None] - W)
    p = masked_softmax(s, mask)
    o = jnp.einsum('bhnqk,bnkd->bnqhd', p.astype(dt), kb[..., 1, :]).reshape(B, Tp, H, Dh)
    return o[:, :T]


def nsa_mixer(q, kv_c, kv_s, kv_w, gate_logits, past_rows, win_buf, qk_gain, cmp_pe, cmp_w, offset):
    B, T, H, Dh = q.shape
    dt = q.dtype
    pos = offset + jnp.arange(T, dtype=jnp.int32)
    qn = rmsnorm(q, qk_gain[0])
    qr = rope(qn, pos)
    k_s = rope(rmsnorm(kv_s[:, :, 0:1], qk_gain[2]), pos)[:, :, 0]
    k_w = rope(rmsnorm(kv_w[:, :, 0:1], qk_gain[3]), pos)[:, :, 0]
    new_rows = jnp.stack([kv_c[:, :, 0], kv_c[:, :, 1], k_s, kv_s[:, :, 1]], axis=2)
    rows = jnp.concatenate([past_rows, new_rows], axis=1)
    S = offset + T
    n_cmp = (S - NSA_CMP_BLOCK) // NSA_CMP_STRIDE + 1
    cidx = np.arange(n_cmp)[:, None] * NSA_CMP_STRIDE + np.arange(NSA_CMP_BLOCK)[None, :]

    def compress(r):
        blocks = rows[:, :, r][:, cidx] + cmp_pe[r]
        return blocks.reshape(B, n_cmp, NSA_CMP_BLOCK * Dh) @ cmp_w[r]

    k_cmp = rmsnorm(compress(0), qk_gain[1])
    v_cmp = compress(1)
    s_c = jnp.einsum('bthd,bnd->bhtn', qn, k_cmp, preferred_element_type=jnp.float32) * ATTN_SCALE
    cmp_end = np.arange(n_cmp) * NSA_CMP_STRIDE + NSA_CMP_BLOCK - 1
    p_c = masked_softmax(s_c, cmp_end[None, :] <= pos[:, None])
    o_cmp = jnp.einsum('bhtn,bnd->bthd', p_c.astype(dt), v_cmp)
    n_sel = -(-S // NSA_SEL_BLOCK)
    ci = np.arange(n_cmp)[:, None] * NSA_CMP_STRIDE
    sj = np.arange(n_sel)[None, :] * NSA_SEL_BLOCK
    cover = ((ci < sj + NSA_SEL_BLOCK) & (ci + NSA_CMP_BLOCK > sj)).astype(np.float32)
    imp = jnp.einsum('bhtn,nj->btj', p_c, jnp.asarray(cover))
    cur = (pos // NSA_SEL_BLOCK)[:, None]
    jj = jnp.arange(n_sel)[None, :]
    forced = (jj == 0) | (jj == cur) | (jj == cur - 1)
    imp = jnp.where(jj > cur, NEG_INF, jnp.where(forced, SEL_FORCE, imp))
    _, sel_idx = lax.top_k(imp, min(NSA_TOPN, n_sel))
    kv_sel = jnp.pad(rows[:, :, 2:4], ((0, 0), (0, n_sel * NSA_SEL_BLOCK - S), (0, 0), (0, 0)))
    kv_sel = kv_sel.reshape(B, n_sel, NSA_SEL_BLOCK, 2, Dh)
    bidx = jnp.arange(B)[:, None, None]

    def sel_block(start, q_blk, idx_blk):
        qb = q_blk.shape[1]
        tp = offset + start + jnp.arange(qb)
        g = kv_sel[bidx, idx_blk]
        kpos = idx_blk[..., None] * NSA_SEL_BLOCK + jnp.arange(NSA_SEL_BLOCK)
        mask = (kpos <= tp[None, :, None, None]).reshape(B, 1, qb, -1)
        g = g.reshape(B, qb, -1, 2, Dh)
        s = jnp.einsum('bqhd,bqkd->bhqk', q_blk, g[..., 0, :], preferred_element_type=jnp.float32) * ATTN_SCALE
        p = masked_softmax(s, mask)
        return jnp.einsum('bhqk,bqkd->bqhd', p.astype(dt), g[..., 1, :])

    o_sel = sweep(sel_block, min(GATHER_Q_BLOCK, T), qr, sel_idx)
    win_rows = jnp.concatenate([win_buf, jnp.stack([k_w, kv_w[:, :, 1]], axis=2)], axis=1)
    o_win = window_attend(qr, win_rows, win_buf.shape[1])
    gates = jax.nn.sigmoid(gate_logits.astype(jnp.float32)).astype(dt)
    o = gates[..., 0:1] * o_cmp + gates[..., 1:2] * o_sel + gates[..., 2:3] * o_win
    new_win = win_rows[:, -min(NSA_WINDOW, win_rows.shape[1]):]
    return o, new_rows, new_win


def gated_recurrence(q, k, v, logf, S0):
    B, T, H, DK = q.shape
    DV = v.shape[-1]
    C = math.gcd(T, HGRN_CHUNK)
    nc = T // C

    def chunks(a):
        return jnp.moveaxis(a.reshape((B, nc, C) + a.shape[2:]), 1, 0).swapaxes(2, 3)

    causal = jnp.tril(jnp.ones((C, C), dtype=bool))[:, :, None]

    def step(S, inp):
        qc, kc, vc, gc = inp
        b = jnp.cumsum(gc, axis=2)
        o_inter = jnp.einsum('bhtk,bhkv->bhtv', qc * jnp.exp(b), S)
        diff = b[:, :, :, None, :] - b[:, :, None, :, :]
        decay = jnp.where(causal, jnp.exp(jnp.where(causal, diff, 0.0)), 0.0)
        A = jnp.einsum('bhtk,bhsk,bhtsk->bhts', qc, kc, decay)
        o = o_inter + jnp.einsum('bhts,bhsv->bhtv', A, vc)
        b_last = b[:, :, -1:, :]
        S_new = jnp.exp(b_last[:, :, 0, :])[..., None] * S + jnp.einsum('bhsk,bhsv->bhkv', kc * jnp.exp(b_last - b), vc)
        return S_new, o

    S, o = lax.scan(step, S0, (chunks(q), chunks(k), chunks(v), chunks(logf)))
    o = jnp.moveaxis(o.swapaxes(2, 3), 0, 1).reshape(B, T, H, DV)
    return o, S


def hgrn2_mixer(q, f, i, g, S0, lb, out_gain):
    dt = q.dtype
    H = q.shape[2]
    lb = lb.reshape(H, HGRN_DK)
    z = f.astype(jnp.float32)
    logf = jnp.log(lb + (1.0 - lb) * jax.nn.sigmoid(z))
    k = (1.0 - lb) * jax.nn.sigmoid(-z)
    qf = jax.nn.silu(q.astype(jnp.float32))
    o, S = gated_recurrence(qf, k, i.astype(jnp.float32), logf, S0)
    o = rmsnorm(o, out_gain) * jax.nn.silu(g.astype(jnp.float32))
    return o.astype(dt), S


def moba_mixer(q, k, v, past_rows, qk_gain, offset):
    B, T, H, Dh = q.shape
    dt = q.dtype
    pos = offset + jnp.arange(T, dtype=jnp.int32)
    qr = rope(rmsnorm(q, qk_gain[0]), pos)
    kr = rope(rmsnorm(k, qk_gain[1]), pos)
    new_rows = jnp.stack([kr, v], axis=2)
    rows = jnp.concatenate([past_rows, new_rows], axis=1)
    S = offset + T
    nblk = -(-S // MOBA_BLOCK)
    kvb = jnp.pad(rows, ((0, 0), (0, nblk * MOBA_BLOCK - S), (0, 0), (0, 0), (0, 0)))
    kvb = jnp.transpose(kvb.reshape(B, nblk, MOBA_BLOCK, 2, H, Dh), (0, 4, 1, 2, 3, 5))
    kmean = jnp.mean(kvb[..., 0, :].astype(jnp.float32), axis=3)
    gate = jnp.einsum('bthd,bhnd->bthn', qr.astype(jnp.float32), kmean)
    own = pos // MOBA_BLOCK
    past_ok = jnp.arange(nblk)[None, :] < own[:, None]
    _, top = lax.top_k(jnp.where(past_ok[None, :, None, :], gate, NEG_INF), min(MOBA_TOPK, nblk))
    valid = top < own[None, :, None, None]
    idx = jnp.concatenate([top, jnp.broadcast_to(own[None, :, None, None], (B, T, H, 1)).astype(top.dtype)], axis=-1)
    ok = jnp.concatenate([valid, jnp.ones((B, T, H, 1), dtype=bool)], axis=-1)
    bidx = jnp.arange(B)[:, None, None, None]
    hidx = jnp.arange(H)[None, None, :, None]

    def blk_fn(start, q_blk, idx_blk, ok_blk):
        qb = q_blk.shape[1]
        tp = offset + start + jnp.arange(qb)
        g = kvb[bidx, hidx, idx_blk]
        kpos = idx_blk[..., None] * MOBA_BLOCK + jnp.arange(MOBA_BLOCK)
        mask = (ok_blk[..., None] & (kpos <= tp[None, :, None, None, None])).reshape(B, qb, H, -1)
        g = g.reshape(B, qb, H, -1, 2, Dh)
        s = jnp.einsum('bqhd,bqhkd->bqhk', q_blk, g[..., 0, :], preferred_element_type=jnp.float32) * ATTN_SCALE
        p = masked_softmax(s, mask)
        return jnp.einsum('bqhk,bqhkd->bqhd', p.astype(dt), g[..., 1, :])

    o = sweep(blk_fn, min(GATHER_Q_BLOCK, T), qr, idx, ok)
    return o, new_rows


def fox_mixer(q, k, v, f_logit, past_kv, past_logf, qk_gain, f_bias, offset):
    B, T, H, Dh = q.shape
    dt = q.dtype
    qn = rmsnorm(q, qk_gain[0])
    kn = rmsnorm(k, qk_gain[1])
    logf_new = jax.nn.log_sigmoid(f_logit.astype(jnp.float32) + f_bias.astype(jnp.float32))
    new_rows = jnp.stack([kn, v], axis=2)
    rows = jnp.concatenate([past_kv, new_rows], axis=1)
    c = jnp.cumsum(jnp.concatenate([past_logf.astype(jnp.float32), logf_new], axis=1), axis=1)
    S = offset + T
    K = rows[:, :, 0]
    V = rows[:, :, 1]
    c_k = jnp.moveaxis(c, 1, 2)[:, :, None, :]
    kpos = jnp.arange(S)

    def blk_fn(start, q_blk, cq_blk):
        qb = q_blk.shape[1]
        tp = offset + start + jnp.arange(qb)
        s = jnp.einsum('bqhd,bkhd->bhqk', q_blk, K, preferred_element_type=jnp.float32) * ATTN_SCALE
        s = s + jnp.moveaxis(cq_blk, 1, 2)[..., None] - c_k
        p = masked_softmax(s, kpos[None, :] <= tp[:, None])
        return jnp.einsum('bhqk,bkhd->bqhd', p.astype(dt), V)

    o = sweep(blk_fn, min(Q_BLOCK, T), qn, c[:, offset:])
    return o, new_rows, logf_new.astype(dt)


def _ffn_up_kernel(be_ref, new_ref, x_ref, w1_ref, w3_ref, u_ref, w1_scr, w3_scr):
    i = pl.program_id(1)

    @pl.when(new_ref[i] == 1)
    def _():
        w1_scr[...] = w1_ref[0].astype(BF)
        w3_scr[...] = w3_ref[0].astype(BF)

    x = x_ref[...].astype(BF)
    a = _dot(x, w1_scr[...])
    b = _dot(x, w3_scr[...])
    u_ref[...] = (a * jax.nn.sigmoid(a) * b).astype(u_ref.dtype)


def _ffn_down_kernel(be_ref, new_ref, u_ref, w2_ref, y_ref, w2_scr):
    i = pl.program_id(1)

    @pl.when(new_ref[i] == 1)
    def _():
        w2_scr[...] = w2_ref[0].astype(BF)

    y_ref[...] = _dot(u_ref[...], w2_scr[...])


def grouped_swiglu(x, block_exp, w1, w3, w2, tm, tf, tn):
    R, D = x.shape
    F = w1.shape[2]
    nblk = R // tm
    block_exp = block_exp.astype(jnp.int32)
    new = jnp.concatenate([jnp.ones((1,), jnp.int32), (block_exp[1:] != block_exp[:-1]).astype(jnp.int32)])
    u = pl.pallas_call(
        _ffn_up_kernel,
        grid_spec=pltpu.PrefetchScalarGridSpec(
            num_scalar_prefetch=2, grid=(F // tf, nblk),
            in_specs=[pl.BlockSpec((tm, D), lambda j, i, be, nw: (i, 0)),
                      pl.BlockSpec((1, D, tf), lambda j, i, be, nw: (be[i], 0, j)),
                      pl.BlockSpec((1, D, tf), lambda j, i, be, nw: (be[i], 0, j))],
            out_specs=pl.BlockSpec((tm, tf), lambda j, i, be, nw: (i, j)),
            scratch_shapes=[pltpu.VMEM((D, tf), BF), pltpu.VMEM((D, tf), BF)]),
        out_shape=jax.ShapeDtypeStruct((R, F), BF),
        compiler_params=_cparams(("arbitrary", "arbitrary")),
        name="swiglu_up",
    )(block_exp, new, x, w1, w3)
    return pl.pallas_call(
        _ffn_down_kernel,
        grid_spec=pltpu.PrefetchScalarGridSpec(
            num_scalar_prefetch=2, grid=(D // tn, nblk),
            in_specs=[pl.BlockSpec((tm, F), lambda n, i, be, nw: (i, 0)),
                      pl.BlockSpec((1, F, tn), lambda n, i, be, nw: (be[i], 0, n))],
            out_specs=pl.BlockSpec((tm, tn), lambda n, i, be, nw: (i, n)),
            scratch_shapes=[pltpu.VMEM((F, tn), BF)]),
        out_shape=jax.ShapeDtypeStruct((R, D), F32),
        compiler_params=_cparams(("arbitrary", "arbitrary")),
        name="swiglu_down",
    )(block_exp, new, u, w2)


def swiglu_dense(x, w1, w3, w2):
    tm = 640 if x.shape[0] % 640 == 0 else 256
    be = jnp.zeros((x.shape[0] // tm,), jnp.int32)
    return grouped_swiglu(x, be, w1[None], w3[None], w2[None], tm, DENSE_TF, FFN_TN)


def moe_ffn_grouped(xf, router, w1, w3, w2):
    N, D = xf.shape
    tm = MOE_TM
    rpad = jnp.pad(router, ((0, 0), (0, LANES - N_EXPERTS)))
    logits = matmul(xf, rpad, tm=256, tn=LANES)[:, :N_EXPERTS]
    top_v, top_e = lax.top_k(logits, TOP_K)
    gates = jax.nn.softmax(top_v, axis=-1)
    NK = N * TOP_K
    flat_e = top_e.reshape(NK)
    order = jnp.argsort(flat_e)
    e_sorted = flat_e[order]
    tok_sorted = (order // TOP_K).astype(jnp.int32)
    counts = jnp.sum((flat_e[:, None] == jnp.arange(N_EXPERTS)[None, :]).astype(jnp.int32), axis=0)
    padded = (counts + tm - 1) // tm * tm
    pend = jnp.cumsum(padded)
    pstart = pend - padded
    start = jnp.cumsum(counts) - counts
    dest_sorted = pstart[e_sorted] + (jnp.arange(NK, dtype=jnp.int32) - start[e_sorted])
    n_blocks = -(-NK // tm) + N_EXPERTS
    slot_tok = jnp.full((n_blocks * tm,), N, jnp.int32).at[dest_sorted].set(tok_sorted)
    block_exp = jnp.clip(jnp.searchsorted(pend, jnp.arange(n_blocks) * tm, side='right'), 0, N_EXPERTS - 1)
    xpad = jnp.concatenate([xf, jnp.zeros((1, D), xf.dtype)], axis=0)
    xb = xpad[slot_tok]
    yb = grouped_swiglu(xb, block_exp, w1, w3, w2, tm, MOE_TF, FFN_TN)
    dest = jnp.zeros((NK,), jnp.int32).at[order].set(dest_sorted).reshape(N, TOP_K)
    return yb[dest[:, 0]] * gates[:, 0:1] + yb[dest[:, 1]] * gates[:, 1:2]


def swiglu(h, w1, w3, w2):
    return (jax.nn.silu(h @ w1) * (h @ w3)) @ w2


def moe_ffn(xf, router, w1, w3, w2):
    N, D = xf.shape
    dt = xf.dtype
    logits = (xf @ router).astype(jnp.float32)
    top_v, top_e = lax.top_k(logits, TOP_K)
    gates = jax.nn.softmax(top_v, axis=-1)
    NK = N * TOP_K
    flat_e = top_e.reshape(NK)
    flat_tok = jnp.arange(NK, dtype=jnp.int32) // TOP_K
    order = jnp.argsort(flat_e)
    e_sorted = flat_e[order]
    tok_sorted = flat_tok[order]
    counts = jnp.zeros((N_EXPERTS,), jnp.int32).at[flat_e].add(1)
    padded = (counts + MOE_BLOCK - 1) // MOE_BLOCK * MOE_BLOCK
    pend = jnp.cumsum(padded)
    pstart = pend - padded
    start = jnp.cumsum(counts) - counts
    dest = pstart[e_sorted] + (jnp.arange(NK, dtype=jnp.int32) - start[e_sorted])
    n_blocks = -(-NK // MOE_BLOCK) + N_EXPERTS
    slot_tok = jnp.full((n_blocks * MOE_BLOCK,), N, jnp.int32).at[dest].set(tok_sorted)
    block_exp = jnp.clip(jnp.searchsorted(pend, jnp.arange(n_blocks) * MOE_BLOCK, side='right'), 0, N_EXPERTS - 1)
    xpad = jnp.concatenate([xf, jnp.zeros((1, D), dt)], axis=0)
    xb = xpad[slot_tok].reshape(n_blocks, MOE_BLOCK, D)

    def expert_block(args):
        xblk, e = args
        return swiglu(xblk, w1[e], w3[e], w2[e])

    yb = lax.map(expert_block, (xb, block_exp)).reshape(n_blocks * MOE_BLOCK, D)
    y_assign = yb[dest] * gates.reshape(NK)[order][:, None].astype(dt)
    return jnp.zeros((N, D), dt).at[tok_sorted].add(y_assign)


def z_sections(z):
    s = lambda c, a=0, b=HW: z[..., c * HW + a:c * HW + b]
    d = HEAD_DIM
    return dict(nq=s(SEC_NQ), nkc=s(SEC_S1, 0, 2 * d), nks=s(SEC_S1, 2 * d, 4 * d), nkw=s(SEC_S2, 0, 2 * d),
                ngate=s(SEC_S2, MISC_LANE, MISC_LANE + 12), ff=s(SEC_S2, MISC_LANE + 12, MISC_LANE + 16),
                hq=s(SEC_HQ), hf=s(SEC_HF), hi=s(SEC_HI), hg=s(SEC_HG), mq=s(SEC_MQ), mk=s(SEC_MK), mv=s(SEC_MV),
                fq=s(SEC_FQ), fk=s(SEC_FK), fv=s(SEC_FV))


def kernel(x_prompt, x_sample, cache_nsa, state_nsa_win, state_hgrn, cache_moba, cache_fox_kv, cache_fox_logf,
           page_table, g_mix, g_ffn, w_in, w_out, nsa_qk_gain, nsa_cmp_pe, nsa_cmp_w, hgrn_lb_logits,
           hgrn_out_gain, moba_qk_gain, fox_qk_gain, fox_f_bias, ffn_w1, ffn_w3, ffn_w2, moe_router,
           moe_w1, moe_w3, moe_w2):
    dt = x_prompt.dtype
    Bp, Tp, D = x_prompt.shape
    Bs, Ts, _ = x_sample.shape
    Np, Ns = Bp * Tp, Bs * Ts
    past_len = page_table.shape[1] * PAGE_SIZE
    lb_w = jax.nn.softmax(hgrn_lb_logits.astype(jnp.float32), axis=0)
    lower_bounds = jnp.cumsum(lb_w, axis=0) - lb_w[0:1]

    def gather_pages(pool):
        g = pool[page_table]
        return g.reshape((Bs, past_len) + pool.shape[2:])

    cos_p, sin_p = rope_tables(jnp.arange(Tp, dtype=jnp.int32))
    cos_s, sin_s = rope_tables(past_len + jnp.arange(Ns, dtype=jnp.int32) % Ts)
    assert Ns == PREP_TQ and Np % PREP_TQ == 0
    gmat = group_mean_matrix()
    cover, expand = nsa_constants(Tp)

    x = jnp.concatenate([x_prompt.reshape(Np, D), x_sample.reshape(Ns, D)], axis=0)
    st_p, st_s = [], []
    for l in range(DEPTH):
        i = l // 2
        z = in_projection(x, g_mix[l][None, :], relayout_w_in(w_in[l]))

        gains = head_gains(nsa_qk_gain[l], moba_qk_gain[l], fox_qk_gain[l])
        pp = prep_prompt(z, 0, Bp, Tp, cos_p, sin_p, gains, gmat)
        pe_flat, w_flat, cgain = nsa_compress_weights(nsa_cmp_pe[l], nsa_cmp_w[l], nsa_qk_gain[l][1])
        kc, vc = nsa_compress(pp["nsa_kc"], Bp, Tp, pe_flat, w_flat, cgain)
        o_nsa_p = nsa_attention_prompt(pp, kc, vc, z, 0, Bp, Tp, cover, expand)
        o_mb_p = moba_attention_prompt(pp, Bp, Tp)
        zp = z_sections(z[:Np].reshape(Bp, Tp, N_INP))
        logf_p = jax.nn.log_sigmoid(zp["ff"] + fox_f_bias[l].astype(F32))
        o_fx_p = fox_attention_prompt(pp, jnp.cumsum(logf_p, axis=1), Bp, Tp)
        hd = lambda a, n, d, B, T: a.reshape(B, T, n, d)
        o_hg_p, hg_state_p = hgrn_mixer(z, 0, Bp, Tp, jnp.zeros((Bp, NH, HGRN_DK, HGRN_DV), F32), lower_bounds[l],
                                        hgrn_out_gain[l], gmat)
        o_p = jnp.concatenate([o_nsa_p.astype(F32), o_hg_p, o_mb_p.astype(F32),
                               o_fx_p.astype(F32)], axis=1)
        nsa_win_p = pp["nsa_win"].reshape(Bp, Tp, 2, HEAD_DIM)[:, -min(NSA_WINDOW, Tp):]
        st_p.append((pp["nsa_rows"].reshape(Bp, Tp, NSA_ROWS, HEAD_DIM), nsa_win_p, hg_state_p.astype(dt),
                     pp["moba_rows"].reshape(Bp, Tp, 2, NH, HEAD_DIM), pp["fox_rows"].reshape(Bp, Tp, 2, NH, HEAD_DIM),
                     logf_p.astype(dt)))

        sp = prep_prompt(z, Np, 1, Ns, cos_s, sin_s, gains, gmat, qdt=F32)
        o_nsa_s = nsa_decode(sp, z, Np, cache_nsa, l, page_table, state_nsa_win[l], nsa_cmp_pe[l], nsa_cmp_w[l],
                             nsa_qk_gain[l][1], Bs, Ts)
        o_hg_s, hg_state = hgrn_mixer(z, Np, Bs, Ts, state_hgrn[l].astype(F32), lower_bounds[l], hgrn_out_gain[l], gmat)
        o_mb_s = moba_decode(sp, cache_moba, l, page_table, sp["moba_rows"], Bs, Ts)
        ff_s = z[Np:, SEC_S2 * HW + MISC_LANE + 12:SEC_S2 * HW + MISC_LANE + 16].reshape(Bs, Ts, NH)
        logf_s = jax.nn.log_sigmoid(ff_s + fox_f_bias[l].astype(F32))
        c_s = jnp.cumsum(jnp.concatenate([gather_pages(cache_fox_logf[l]).astype(F32), logf_s], axis=1), axis=1)
        o_fx_s = fox_decode(sp, cache_fox_kv, l, page_table, c_s, sp["fox_rows"], Bs, Ts)
        o_s = jnp.concatenate([o_nsa_s, o_hg_s, o_mb_s, o_fx_s], axis=1)
        win_rows = jnp.concatenate([state_nsa_win[l], sp["nsa_win"].reshape(Bs, Ts, 2, HEAD_DIM)], axis=1)
        st_s.append((sp["nsa_rows"].reshape(Bs, Ts, NSA_ROWS, HEAD_DIM),
                     win_rows[:, -min(NSA_WINDOW, win_rows.shape[1]):], hg_state.astype(dt),
                     sp["moba_rows"].reshape(Bs, Ts, 2, NH, HEAD_DIM), sp["fox_rows"].reshape(Bs, Ts, 2, NH, HEAD_DIM),
                     logf_s.astype(dt)))

        o = jnp.concatenate([o_p, o_s], axis=0)
        x = x + matmul(o, w_out[l])
        hn = rmsnorm(x, g_ffn[l])
        if l % 2 == 0:
            x = x + swiglu_dense(hn, ffn_w1[i], ffn_w3[i], ffn_w2[i])
        else:
            x = x + moe_ffn_grouped(hn, moe_router[i], moe_w1[i], moe_w3[i], moe_w2[i])

    def stk(states, j):
        return jnp.stack([s[j] for s in states], axis=0)

    return (x[:Np].reshape(Bp, Tp, D), x[Np:].reshape(Bs, Ts, D),
            stk(st_p, 0), stk(st_s, 0), stk(st_p, 1), stk(st_s, 1), stk(st_p, 2), stk(st_s, 2),
            stk(st_p, 3), stk(st_s, 3), stk(st_p, 4), stk(st_s, 4), stk(st_p, 5), stk(st_s, 5))
```

```python
import math, functools
import jax, jax.numpy as jnp
from jax import lax
import numpy as np
from jax.experimental import pallas as pl
from jax.experimental.pallas import tpu as pltpu

D_MODEL = 1024
DEPTH = 2
PAGE_SIZE = 128
HEAD_DIM = 64
H_NSA = 4
H_HGRN = 4
H_MOBA = 4
H_FOX = 4
NH = 4
HW = NH * HEAD_DIM
MIX_WIDTH = (H_NSA + H_HGRN + H_MOBA + H_FOX) * HEAD_DIM
HGRN_DK = 64
HGRN_DV = HEAD_DIM
HGRN_CHUNK = 64
NSA_CMP_BLOCK = 32
NSA_CMP_STRIDE = 16
NSA_SEL_BLOCK = 64
NSA_TOPN = 16
NSA_WINDOW = 512
NSA_ROWS = 4
MOBA_BLOCK = 256
MOBA_TOPK = 3
ROPE_THETA = 10000.0
Q_BLOCK = 128
GATHER_Q_BLOCK = 32
N_EXPERTS = 8
TOP_K = 2
MOE_BLOCK = 128
RMS_EPS = 1e-6
NEG_INF = -1e30
SEL_FORCE = 1e6
ATTN_SCALE = HEAD_DIM ** -0.5
IN_SIZES = (H_NSA * HEAD_DIM, 2 * HEAD_DIM, 2 * HEAD_DIM, 2 * HEAD_DIM, 3 * H_NSA,
            H_HGRN * HGRN_DK, H_HGRN * HGRN_DK, H_HGRN * HGRN_DV, H_HGRN * HGRN_DV,
            H_MOBA * HEAD_DIM, H_MOBA * HEAD_DIM, H_MOBA * HEAD_DIM,
            H_FOX * HEAD_DIM, H_FOX * HEAD_DIM, H_FOX * HEAD_DIM, H_FOX)
N_IN = sum(IN_SIZES)
IN_OFFS = tuple(int(v) for v in np.cumsum((0,) + IN_SIZES))

N_SEC = 13
N_INP = N_SEC * HW
SEC_NQ, SEC_S1, SEC_S2, SEC_HQ, SEC_HF, SEC_HI, SEC_HG = 0, 1, 2, 3, 4, 5, 6
SEC_MQ, SEC_MK, SEC_MV, SEC_FQ, SEC_FK, SEC_FV = 7, 8, 9, 10, 11, 12
MISC_LANE = 128

LANES = 128
VMEM_LIMIT = 48 * 1024 * 1024
PREP_TQ = 256
NSA_TQ = 128
NSA_TK = 512
ATT_SB = 2
M_FLOOR = -1e20
MOE_TM = 256
MOE_TF = 896
DENSE_TF = 1408
FFN_TN = 512
BF = jnp.bfloat16
F32 = jnp.float32


def _round_up(x, m):
    return -(-x // m) * m


def _cparams(sem):
    return pltpu.CompilerParams(dimension_semantics=sem, vmem_limit_bytes=VMEM_LIMIT)


def _dot(a, b):
    return jnp.dot(a, b, preferred_element_type=F32)


def _dot_nt(a, b):
    return lax.dot_general(a, b, (((1,), (1,)), ((), ())), preferred_element_type=F32)


def _mm_kernel(a_ref, b_ref, o_ref):
    k = pl.program_id(2)
    acc = _dot(a_ref[...].astype(BF), b_ref[...].astype(BF))

    @pl.when(k == 0)
    def _():
        o_ref[...] = acc

    @pl.when(k != 0)
    def _():
        o_ref[...] += acc


def matmul(a, b, tm=512, tn=512, tk=1024):
    M, K = a.shape
    _, N = b.shape
    tm = min(tm, _round_up(M, 8))
    Mp, Np = _round_up(M, tm), _round_up(N, tn)
    if K % tk:
        tk = K
    if Mp != M:
        a = jnp.pad(a, ((0, Mp - M), (0, 0)))
    if Np != N:
        b = jnp.pad(b, ((0, 0), (0, Np - N)))
    out = pl.pallas_call(
        _mm_kernel,
        grid=(Mp // tm, Np // tn, K // tk),
        in_specs=[pl.BlockSpec((tm, tk), lambda i, j, k: (i, k)),
                  pl.BlockSpec((tk, tn), lambda i, j, k: (k, j))],
        out_specs=pl.BlockSpec((tm, tn), lambda i, j, k: (i, j)),
        out_shape=jax.ShapeDtypeStruct((Mp, Np), F32),
        compiler_params=_cparams(("parallel", "parallel", "arbitrary")),
        name="dense_matmul",
    )(a, b)
    return out[:M, :N]


def _inproj_kernel(x_ref, g_ref, w_ref, o_ref):
    x = x_ref[...]
    y = x * lax.rsqrt(jnp.mean(x * x, axis=-1, keepdims=True) + RMS_EPS) * g_ref[...]
    o_ref[...] = _dot(y.astype(BF), w_ref[...])


def in_projection(x, g, w_bf, tm=256):
    N, D = x.shape
    return pl.pallas_call(
        _inproj_kernel,
        grid=(N // tm,),
        in_specs=[pl.BlockSpec((tm, D), lambda i: (i, 0)),
                  pl.BlockSpec((1, D), lambda i: (0, 0)),
                  pl.BlockSpec((D, N_INP), lambda i: (0, 0))],
        out_specs=pl.BlockSpec((tm, N_INP), lambda i: (i, 0)),
        out_shape=jax.ShapeDtypeStruct((N, N_INP), F32),
        compiler_params=_cparams(("parallel",)),
        name="rmsnorm_in_projection",
    )(x, g, w_bf)


def relayout_w_in(w):
    def cols(i):
        return w[:, IN_OFFS[i]:IN_OFFS[i + 1]]
    pad = jnp.zeros((w.shape[0], HW - 2 * HEAD_DIM - IN_SIZES[4] - IN_SIZES[15]), w.dtype)
    parts = [cols(0), cols(1), cols(2), cols(3), cols(4), cols(15), pad] + [cols(i) for i in range(5, 15)]
    return jnp.concatenate(parts, axis=1).astype(BF)


def _head_meansq(x, gmat):
    sq = x * x
    hi = sq.astype(BF)
    lo = (sq - hi.astype(F32)).astype(BF)
    return _dot(hi, gmat) + _dot(lo, gmat)


def _head_rmsnorm(x, gain, gmat):
    return x * lax.rsqrt(_head_meansq(x, gmat) + RMS_EPS) * gain


def _rope(x, cos, sin_signed, lo_half):
    w = x.shape[1]
    swapped = jnp.where(lo_half, pltpu.roll(x, w - HEAD_DIM // 2, 1), pltpu.roll(x, HEAD_DIM // 2, 1))
    return x * cos + swapped * sin_signed


def _store_heads(ref, x):
    for h in range(NH):
        ref[0, h] = x[:, h * HEAD_DIM:(h + 1) * HEAD_DIM].astype(ref.dtype)


def _prep_kernel(nq_ref, s1_ref, s2_ref, mq_ref, mk_ref, mv_ref, fq_ref, fk_ref, fv_ref,
                 cos_ref, sin_ref, gains_ref, gmat_ref,
                 nsa_rows_ref, nsa_kc_ref, nsa_win_ref, moba_rows_ref, fox_rows_ref,
                 nqn_ref, nqr_ref, nks_ref, nvs_ref, nkw_ref, nvw_ref,
                 mqr_ref, mkr_ref, mvv_ref, kmean_ref, fqn_ref, fkn_ref, fvv_ref):
    cos = cos_ref[...]
    sin = sin_ref[...]
    gmat = gmat_ref[...]
    t = cos.shape[0]
    lane = lax.broadcasted_iota(jnp.int32, (t, HW), 1)
    lo_half = (lane % HEAD_DIM) < (HEAD_DIM // 2)
    gains = gains_ref[...]

    qn = _head_rmsnorm(nq_ref[...], gains[0:1], gmat)
    qr = _rope(qn, cos, sin, lo_half)
    _store_heads(nqn_ref, qn * ATTN_SCALE)
    _store_heads(nqr_ref, qr * ATTN_SCALE)

    s1 = s1_ref[...]
    s1r = _rope(_head_rmsnorm(s1, gains[1:2], gmat), cos, sin, lo_half)
    third = (lane >= 2 * HEAD_DIM) & (lane < 3 * HEAD_DIM)
    rows = jnp.where(third, s1r, s1)
    nsa_rows_ref[...] = rows
    nsa_kc_ref[...] = rows[:, :2 * HEAD_DIM]
    nks_ref[0] = rows[:, 2 * HEAD_DIM:3 * HEAD_DIM].astype(nks_ref.dtype)
    nvs_ref[0] = rows[:, 3 * HEAD_DIM:].astype(nvs_ref.dtype)

    s2 = s2_ref[...]
    s2r = _rope(_head_rmsnorm(s2, gains[2:3], gmat), cos, sin, lo_half)
    wrows = jnp.where(lane < HEAD_DIM, s2r, s2)
    nsa_win_ref[...] = wrows[:, :2 * HEAD_DIM]
    nkw_ref[0] = wrows[:, :HEAD_DIM].astype(nkw_ref.dtype)
    nvw_ref[0] = wrows[:, HEAD_DIM:2 * HEAD_DIM].astype(nvw_ref.dtype)

    mq = _rope(_head_rmsnorm(mq_ref[...], gains[3:4], gmat), cos, sin, lo_half)
    mk = _rope(_head_rmsnorm(mk_ref[...], gains[4:5], gmat), cos, sin, lo_half)
    mv = mv_ref[...]
    _store_heads(mqr_ref, mq * ATTN_SCALE)
    _store_heads(mkr_ref, mk)
    _store_heads(mvv_ref, mv)
    moba_rows_ref[:, :HW] = mk
    moba_rows_ref[:, HW:] = mv
    kmean_ref[0, 0] = jnp.mean(mk, axis=0, keepdims=True)

    fq = _head_rmsnorm(fq_ref[...], gains[5:6], gmat)
    fk = _head_rmsnorm(fk_ref[...], gains[6:7], gmat)
    fv = fv_ref[...]
    _store_heads(fqn_ref, fq * ATTN_SCALE)
    _store_heads(fkn_ref, fk)
    _store_heads(fvv_ref, fv)
    fox_rows_ref[:, :HW] = fk
    fox_rows_ref[:, HW:] = fv


def rope_tables(pos):
    half = HEAD_DIM // 2
    inv = ROPE_THETA ** (-jnp.arange(half, dtype=F32) / half)
    ang = pos.astype(F32)[:, None] * inv[None, :]
    cos = jnp.cos(ang)
    sin = jnp.sin(ang)
    cos_h = jnp.concatenate([cos, cos], axis=1)
    sin_h = jnp.concatenate([-sin, sin], axis=1)
    return jnp.tile(cos_h, (1, NH)), jnp.tile(sin_h, (1, NH))


def head_gains(nsa_gain, moba_gain, fox_gain):
    one = jnp.ones((HEAD_DIM,), F32)
    t4 = lambda g: jnp.tile(g, NH)
    rows = [t4(nsa_gain[0]),
            jnp.concatenate([one, one, nsa_gain[2], one]),
            jnp.concatenate([nsa_gain[3], one, one, one]),
            t4(moba_gain[0]), t4(moba_gain[1]), t4(fox_gain[0]), t4(fox_gain[1]), t4(one)]
    return jnp.stack(rows, axis=0)


def group_mean_matrix():
    idx = np.arange(HW) // HEAD_DIM
    return jnp.asarray((idx[:, None] == idx[None, :]).astype(np.float32) / HEAD_DIM, dtype=BF)


def prep_prompt(z, row0, B, T, cos, sin, gains, gmat, qdt=None):
    qdt = BF if qdt is None else qdt
    tq = PREP_TQ
    nq = T // tq
    rb0 = row0 // tq

    def sec(c):
        return pl.BlockSpec((tq, HW), lambda b, i, c=c: (rb0 + b * nq + i, c))

    flat = lambda w: pl.BlockSpec((tq, w), lambda b, i: (b * nq + i, 0))
    headmaj = pl.BlockSpec((1, NH, tq, HEAD_DIM), lambda b, i: (b, 0, i, 0))
    single = pl.BlockSpec((1, tq, HEAD_DIM), lambda b, i: (b, i, 0))
    N = B * T
    sd = jax.ShapeDtypeStruct
    hm_shape = sd((B, NH, T, HEAD_DIM), qdt)
    sg_shape = sd((B, T, HEAD_DIM), qdt)
    outs = pl.pallas_call(
        _prep_kernel,
        grid=(B, nq),
        in_specs=[sec(SEC_NQ), sec(SEC_S1), sec(SEC_S2), sec(SEC_MQ), sec(SEC_MK), sec(SEC_MV),
                  sec(SEC_FQ), sec(SEC_FK), sec(SEC_FV),
                  pl.BlockSpec((tq, HW), lambda b, i: (i, 0)),
                  pl.BlockSpec((tq, HW), lambda b, i: (i, 0)),
                  pl.BlockSpec((8, HW), lambda b, i: (0, 0)),
                  pl.BlockSpec((HW, HW), lambda b, i: (0, 0))],
        out_specs=[flat(HW), flat(2 * HEAD_DIM), flat(2 * HEAD_DIM), flat(2 * HW), flat(2 * HW),
                   headmaj, headmaj, single, single, single, single,
                   headmaj, headmaj, headmaj,
                   pl.BlockSpec((1, 1, 1, HW), lambda b, i: (b, i, 0, 0)),
                   headmaj, headmaj, headmaj],
        out_shape=[sd((N, HW), F32), sd((N, 2 * HEAD_DIM), F32), sd((N, 2 * HEAD_DIM), F32),
                   sd((N, 2 * HW), F32), sd((N, 2 * HW), F32),
                   hm_shape, hm_shape, sg_shape, sg_shape, sg_shape, sg_shape,
                   hm_shape, hm_shape, hm_shape,
                   sd((B, nq, 1, HW), F32),
                   hm_shape, hm_shape, hm_shape],
        compiler_params=_cparams(("parallel", "parallel")),
        name="mixer_prep",
    )(z, z, z, z, z, z, z, z, z, cos, sin, gains, gmat)
    keys = ("nsa_rows", "nsa_kc", "nsa_win", "moba_rows", "fox_rows",
            "nqn", "nqr", "nks", "nvs", "nkw", "nvw", "mqr", "mkr", "mvv", "kmean", "fqn", "fkn", "fvv")
    return dict(zip(keys, outs))


def _nsa_compress_kernel(x_ref, pe_ref, w_ref, gain_ref, k_ref, v_ref):
    x = x_ref[0]
    a = _dot((x + pe_ref[0:1]).astype(BF), w_ref[0])
    b = _dot((x + pe_ref[1:2]).astype(BF), w_ref[1])
    nch = x.shape[0]
    y = a + pltpu.roll(b, nch - 1, 0)
    lane = lax.broadcasted_iota(jnp.int32, y.shape, 1)
    ms = jnp.sum(jnp.where(lane < HEAD_DIM, y * y, 0.0), axis=1, keepdims=True) / HEAD_DIM
    kn = y * lax.rsqrt(ms + RMS_EPS) * gain_ref[...]
    k_ref[0] = kn[:, :HEAD_DIM].astype(BF)
    v_ref[0] = y[:, HEAD_DIM:].astype(BF)


def nsa_compress_weights(cmp_pe, cmp_w, gain1):
    S = NSA_CMP_STRIDE
    pe = cmp_pe.reshape(2, 2, S, HEAD_DIM)
    pe_flat = jnp.transpose(pe, (1, 2, 0, 3)).reshape(2, S * 2 * HEAD_DIM)
    w = cmp_w.reshape(2, 2, S, HEAD_DIM, HEAD_DIM)
    wz = jnp.zeros((2, S, 2, HEAD_DIM, 2, HEAD_DIM), F32)
    wz = wz.at[:, :, 0, :, 0, :].set(w[0]).at[:, :, 1, :, 1, :].set(w[1])
    w_flat = wz.reshape(2, S * 2 * HEAD_DIM, 2 * HEAD_DIM).astype(BF)
    gain = jnp.concatenate([gain1, jnp.ones((HEAD_DIM,), F32)])[None, :]
    return pe_flat, w_flat, gain


def nsa_compress(kc, B, T, pe_flat, w_flat, gain):
    nch = T // NSA_CMP_STRIDE
    cw = NSA_CMP_STRIDE * 2 * HEAD_DIM
    x = kc.reshape(B, nch, cw)
    out_spec = pl.BlockSpec((1, nch, HEAD_DIM), lambda b: (b, 0, 0))
    return pl.pallas_call(
        _nsa_compress_kernel,
        grid=(B,),
        in_specs=[pl.BlockSpec((1, nch, cw), lambda b: (b, 0, 0)),
                  pl.BlockSpec((2, cw), lambda b: (0, 0)),
                  pl.BlockSpec((2, cw, 2 * HEAD_DIM), lambda b: (0, 0, 0)),
                  pl.BlockSpec((1, 2 * HEAD_DIM), lambda b: (0, 0))],
        out_specs=[out_spec, out_spec],
        out_shape=[jax.ShapeDtypeStruct((B, nch, HEAD_DIM), BF)] * 2,
        compiler_params=_cparams(("parallel",)),
        name="nsa_compress",
    )(x, pe_flat, w_flat, gain)


def _softmax_update(s, mask, m_i, l_i, acc, v, lead, v_feature_major=False):
    s = jnp.where(mask, s, NEG_INF)
    m_new = jnp.maximum(m_i, jnp.max(s, axis=-1, keepdims=True))
    p = jnp.where(mask, jnp.exp(s - m_new), 0.0)
    alpha = jnp.exp(m_i - m_new)
    l_new = alpha * l_i + jnp.sum(p, axis=-1, keepdims=True)
    pb = p.astype(BF).reshape(lead, p.shape[-1])
    pv = (_dot_nt(pb, v) if v_feature_major else _dot(pb, v)).reshape(acc.shape)
    return m_new, l_new, alpha * acc + pv


def _nsa_attn_kernel(qn_ref, qr_ref, kc_ref, vc_ref, ks_ref, vs_ref, kw_ref, vw_ref, gl_ref,
                     cover_ref, expand_ref, o_ref, m_scr, l_scr, acc_scr, *, T):
    tq, tk = NSA_TQ, NSA_TK
    M = NH * tq
    i = pl.program_id(1)
    p0 = i * tq
    qn = qn_ref[0].reshape(M, HEAD_DIM)
    qr = qr_ref[0].reshape(M, HEAD_DIM)
    pos = p0 + lax.broadcasted_iota(jnp.int32, (tq, 1), 0)

    nch = kc_ref.shape[1]
    s_c = _dot_nt(qn, kc_ref[0]).reshape(NH, tq, nch)
    n_idx = lax.broadcasted_iota(jnp.int32, (tq, nch), 1)
    mask_c = (n_idx * NSA_CMP_STRIDE + (NSA_CMP_BLOCK - 1) <= pos)[None]
    s_c = jnp.where(mask_c, s_c, NEG_INF)
    m_c = jnp.max(s_c, axis=-1, keepdims=True)
    p_c = jnp.where(mask_c, jnp.exp(s_c - m_c), 0.0)
    p_c = p_c / jnp.maximum(jnp.sum(p_c, axis=-1, keepdims=True), 1.0)
    p_cb = p_c.astype(BF).reshape(M, nch)
    o_cmp = _dot(p_cb, vc_ref[0]).reshape(NH, tq, HEAD_DIM)
    imp = jnp.sum(_dot(p_cb, cover_ref[...]).reshape(NH, tq, LANES), axis=0)

    j = lax.broadcasted_iota(jnp.int32, (tq, LANES), 1)
    cur = pos // NSA_SEL_BLOCK
    forced = (j == 0) | (j == cur) | (j == cur - 1)
    imp = jnp.where(j > cur, NEG_INF, jnp.where(forced, SEL_FORCE, imp))
    n_sel = T // NSA_SEL_BLOCK
    rank = jnp.zeros((tq, LANES), F32)
    for jp in range(n_sel):
        col = imp[:, jp:jp + 1]
        beats = (col > imp) | ((col == imp) & (j > jp))
        rank = rank + jnp.where(beats, 1.0, 0.0)
    sel = jnp.where((rank < NSA_TOPN) & (j <= cur), 1.0, 0.0).astype(BF)

    _flash_init(m_scr, l_scr, acc_scr)
    kcol = lax.broadcasted_iota(jnp.int32, (tq, tk), 1)
    for kj in range(T // tk):
        @pl.when(kj * tk <= p0 + (tq - 1))
        def _(kj=kj):
            k = ks_ref[0, kj * tk:(kj + 1) * tk, :]
            v = vs_ref[0, kj * tk:(kj + 1) * tk, :]
            e = _dot(sel, expand_ref[kj])
            mask = ((e > 0.5) & (kcol + kj * tk <= pos))[None]
            s = jnp.where(mask, _dot_nt(qr, k).reshape(NH, tq, tk), NEG_INF)
            _flash_step(s, v, m_scr, l_scr, acc_scr)
    o_sel = acc_scr[...] / jnp.maximum(l_scr[...], 1.0)

    band = NSA_WINDOW + tq
    start = pl.multiple_of(jnp.maximum(p0 - NSA_WINDOW, 0), tq)
    kw = kw_ref[0, pl.ds(start, band), :]
    vw = vw_ref[0, pl.ds(start, band), :]
    s_w = _dot_nt(qr, kw).reshape(NH, tq, band)
    kpos = start + lax.broadcasted_iota(jnp.int32, (tq, band), 1)
    mask_w = ((kpos <= pos) & (kpos > pos - NSA_WINDOW))[None]
    s_w = jnp.where(mask_w, s_w, NEG_INF)
    m_w = jnp.max(s_w, axis=-1, keepdims=True)
    p_w = jnp.where(mask_w, jnp.exp(s_w - m_w), 0.0)
    l_w = jnp.sum(p_w, axis=-1, keepdims=True)
    o_win = _dot(p_w.astype(BF).reshape(M, band), vw).reshape(NH, tq, HEAD_DIM) / jnp.maximum(l_w, 1.0)

    sig = jax.nn.sigmoid(gl_ref[...])
    outs = []
    for h in range(NH):
        c = MISC_LANE + 3 * h
        outs.append(sig[:, c:c + 1] * o_cmp[h] + sig[:, c + 1:c + 2] * o_sel[h] + sig[:, c + 2:c + 3] * o_win[h])
    o_ref[...] = jnp.concatenate(outs, axis=1).astype(o_ref.dtype)


def nsa_constants(T):
    nch = T // NSA_CMP_STRIDE
    n_cmp = (T - NSA_CMP_BLOCK) // NSA_CMP_STRIDE + 1
    ci = np.arange(nch)[:, None] * NSA_CMP_STRIDE
    sj = np.arange(LANES)[None, :] * NSA_SEL_BLOCK
    cover = (ci < sj + NSA_SEL_BLOCK) & (ci + NSA_CMP_BLOCK > sj) & (np.arange(nch)[:, None] < n_cmp)
    nkt = T // NSA_TK
    blk = (np.arange(nkt)[:, None, None] * NSA_TK + np.arange(NSA_TK)[None, None, :]) // NSA_SEL_BLOCK
    expand = blk == np.arange(LANES)[None, :, None]
    return jnp.asarray(cover.astype(np.float32), dtype=BF), jnp.asarray(expand.astype(np.float32), dtype=BF)


def nsa_attention_prompt(pp, kc, vc, z, row0, B, T, cover, expand):
    tq = NSA_TQ
    nq = T // tq
    rb0 = row0 // tq
    nch = T // NSA_CMP_STRIDE
    assert T >= NSA_WINDOW + tq and T % NSA_TK == 0
    headmaj = pl.BlockSpec((1, NH, tq, HEAD_DIM), lambda b, i: (b, 0, i, 0))
    full1 = pl.BlockSpec((1, T, HEAD_DIM), lambda b, i: (b, 0, 0))
    cmp1 = pl.BlockSpec((1, nch, HEAD_DIM), lambda b, i: (b, 0, 0))
    return pl.pallas_call(
        functools.partial(_nsa_attn_kernel, T=T),
        grid=(B, nq),
        in_specs=[headmaj, headmaj, cmp1, cmp1, full1, full1, full1, full1,
                  pl.BlockSpec((tq, HW), lambda b, i: (rb0 + b * nq + i, SEC_S2)),
                  pl.BlockSpec((nch, LANES), lambda b, i: (0, 0)),
                  pl.BlockSpec((T // NSA_TK, LANES, NSA_TK), lambda b, i: (0, 0, 0))],
        out_specs=pl.BlockSpec((tq, HW), lambda b, i: (b * nq + i, 0)),
        out_shape=jax.ShapeDtypeStruct((B * T, HW), BF),
        scratch_shapes=[pltpu.VMEM((NH, tq, 1), F32), pltpu.VMEM((NH, tq, 1), F32),
                        pltpu.VMEM((NH, tq, HEAD_DIM), F32)],
        compiler_params=_cparams(("parallel", "parallel")),
        name="nsa_attention",
    )(pp["nqn"], pp["nqr"], kc, vc, pp["nks"], pp["nvs"], pp["nkw"], pp["nvw"], z, cover, expand)


def _flash_init(m_scr, l_scr, acc_scr):
    m_scr[...] = jnp.full_like(m_scr, M_FLOOR)
    l_scr[...] = jnp.zeros_like(l_scr)
    acc_scr[...] = jnp.zeros_like(acc_scr)


def _flash_step(s, v, m_scr, l_scr, acc_scr):
    m_i = m_scr[...]
    m_new = jnp.maximum(m_i, jnp.max(s, axis=-1, keepdims=True))
    p = jnp.exp(s - m_new)
    alpha = jnp.exp(m_i - m_new)
    l_scr[...] = alpha * l_scr[...] + jnp.sum(p, axis=-1, keepdims=True)
    pv = _dot(p.astype(BF).reshape(-1, p.shape[-1]), v)
    acc_scr[...] = alpha * acc_scr[...] + pv.reshape(acc_scr.shape)
    m_scr[...] = m_new


def _moba_attn_kernel(q_ref, k_ref, v_ref, km_ref, o_ref, m_scr, l_scr, acc_scr):
    tq = MOBA_BLOCK
    tks = ATT_SB * MOBA_BLOCK
    n_sb = k_ref.shape[2] // tks
    qi = pl.program_id(1)
    lane = lax.broadcasted_iota(jnp.int32, (tq, LANES), 1)
    qrow = lax.broadcasted_iota(jnp.int32, (tq, tks), 0)
    kcol = lax.broadcasted_iota(jnp.int32, (tq, tks), 1)
    sels = []
    for h in range(NH):
        g = jnp.where(lane < qi, _dot_nt(q_ref[0, h], km_ref[0, h]), NEG_INF)
        sel = jnp.where(lane == qi, 1.0, 0.0)
        for _ in range(MOBA_TOPK):
            m = jnp.max(g, axis=1, keepdims=True)
            idx = jnp.min(jnp.where(g == m, lane, LANES), axis=1, keepdims=True)
            hit = lane == idx
            sel = jnp.where(hit & (lane < qi), 1.0, sel)
            g = jnp.where(hit, NEG_INF, g)
        sels.append(sel)
    _flash_init(m_scr, l_scr, acc_scr)
    for sb in range(n_sb):
        @pl.when(sb * ATT_SB <= qi)
        def _(sb=sb):
            causal = kcol + (sb * tks) <= qrow + qi * tq
            for h in range(NH):
                k = k_ref[0, h, sb * tks:(sb + 1) * tks, :]
                v = v_ref[0, h, sb * tks:(sb + 1) * tks, :]
                picked = jnp.concatenate(
                    [jnp.broadcast_to(sels[h][:, j:j + 1] > 0.5, (tq, MOBA_BLOCK))
                     for j in range(sb * ATT_SB, (sb + 1) * ATT_SB)], axis=1)
                s = jnp.where(picked & causal, _dot_nt(q_ref[0, h], k), NEG_INF)
                _flash_step(s, v, m_scr.at[h], l_scr.at[h], acc_scr.at[h])
    o = acc_scr[...] / jnp.maximum(l_scr[...], 1.0)
    o_ref[...] = jnp.concatenate([o[h] for h in range(NH)], axis=1).astype(o_ref.dtype)


def moba_attention_prompt(pp, B, T):
    tq = MOBA_BLOCK
    nq = T // tq
    km = pp["kmean"].reshape(B, nq, NH, HEAD_DIM).transpose(0, 2, 1, 3)
    km = jnp.pad(km, ((0, 0), (0, 0), (0, LANES - nq), (0, 0))).astype(BF)
    headq = pl.BlockSpec((1, NH, tq, HEAD_DIM), lambda b, i: (b, 0, i, 0))
    headfull = pl.BlockSpec((1, NH, T, HEAD_DIM), lambda b, i: (b, 0, 0, 0))
    return pl.pallas_call(
        _moba_attn_kernel,
        grid=(B, nq),
        in_specs=[headq, headfull, headfull,
                  pl.BlockSpec((1, NH, LANES, HEAD_DIM), lambda b, i: (b, 0, 0, 0))],
        out_specs=pl.BlockSpec((tq, HW), lambda b, i: (b * nq + i, 0)),
        out_shape=jax.ShapeDtypeStruct((B * T, HW), BF),
        scratch_shapes=[pltpu.VMEM((NH, tq, 1), F32), pltpu.VMEM((NH, tq, 1), F32),
                        pltpu.VMEM((NH, tq, HEAD_DIM), F32)],
        compiler_params=_cparams(("parallel", "parallel")),
        name="moba_attention",
    )(pp["mqr"], pp["mkr"], pp["mvv"], km)


def _fox_attn_kernel(q_ref, k_ref, v_ref, cq_ref, ck_ref, o_ref, m_scr, l_scr, acc_scr):
    tq = MOBA_BLOCK
    tks = ATT_SB * MOBA_BLOCK
    n_sb = k_ref.shape[2] // tks
    qi = pl.program_id(1)
    qrow = lax.broadcasted_iota(jnp.int32, (tq, tks), 0)
    kcol = lax.broadcasted_iota(jnp.int32, (tq, tks), 1)
    _flash_init(m_scr, l_scr, acc_scr)
    for sb in range(n_sb):
        last = (sb + 1) * ATT_SB - 1

        def region(causal, sb=sb):
            for h in range(NH):
                k = k_ref[0, h, sb * tks:(sb + 1) * tks, :]
                v = v_ref[0, h, sb * tks:(sb + 1) * tks, :]
                s = _dot_nt(q_ref[0, h], k) + cq_ref[0, h] - ck_ref[0, h, sb]
                if causal:
                    s = jnp.where(kcol + (sb * tks) <= qrow + qi * tq, s, NEG_INF)
                _flash_step(s, v, m_scr.at[h], l_scr.at[h], acc_scr.at[h])

        pl.when(last < qi)(functools.partial(region, False))
        pl.when((sb * ATT_SB <= qi) & (last >= qi))(functools.partial(region, True))
    o = acc_scr[...] / jnp.maximum(l_scr[...], 1.0)
    o_ref[...] = jnp.concatenate([o[h] for h in range(NH)], axis=1).astype(o_ref.dtype)


def fox_attention_prompt(pp, c, B, T):
    tq = MOBA_BLOCK
    nq = T // tq
    ch = jnp.transpose(c, (0, 2, 1))
    cq = ch[..., None]
    tks = ATT_SB * MOBA_BLOCK
    ck = ch.reshape(B, NH, T // tks, 1, tks)
    headq = pl.BlockSpec((1, NH, tq, HEAD_DIM), lambda b, i: (b, 0, i, 0))
    headfull = pl.BlockSpec((1, NH, T, HEAD_DIM), lambda b, i: (b, 0, 0, 0))
    return pl.pallas_call(
        _fox_attn_kernel,
        grid=(B, nq),
        in_specs=[headq, headfull, headfull,
                  pl.BlockSpec((1, NH, tq, 1), lambda b, i: (b, 0, i, 0)),
                  pl.BlockSpec((1, NH, T // tks, 1, tks), lambda b, i: (b, 0, 0, 0, 0))],
        out_specs=pl.BlockSpec((tq, HW), lambda b, i: (b * nq + i, 0)),
        out_shape=jax.ShapeDtypeStruct((B * T, HW), BF),
        scratch_shapes=[pltpu.VMEM((NH, tq, 1), F32), pltpu.VMEM((NH, tq, 1), F32),
                        pltpu.VMEM((NH, tq, HEAD_DIM), F32)],
        compiler_params=_cparams(("parallel", "parallel")),
        name="fox_attention",
    )(pp["fqn"], pp["fkn"], pp["fvv"], cq, ck)


def _hgrn_kernel(q_ref, f_ref, i_ref, g_ref, s0_ref, lb_ref, gain_ref, bd_ref, gmat_ref,
                 o_ref, sout_ref, st_scr, *, C):
    c = pl.program_id(1)

    @pl.when(c == 0)
    def _():
        st_scr[...] = jnp.zeros_like(st_scr)
        for h in range(NH):
            st_scr[h * HEAD_DIM:(h + 1) * HEAD_DIM, h * HEAD_DIM:(h + 1) * HEAD_DIM] = s0_ref[0, h]

    lb = lb_ref[...]
    z = f_ref[...]
    logf = jnp.log(lb + (1.0 - lb) * jax.nn.sigmoid(z))
    kk = (1.0 - lb) * jax.nn.sigmoid(-z)
    q = q_ref[...]
    qf = q * jax.nn.sigmoid(q)
    v = i_ref[...]
    row = lax.broadcasted_iota(jnp.int32, (C, HW), 0)
    b = logf
    sh = 1
    while sh < C:
        b = b + jnp.where(row >= sh, pltpu.roll(b, sh, 0), 0.0)
        sh *= 2
    bd = bd_ref[...]
    st = st_scr[...]
    o_ref[...] = _dot_nt((qf * jnp.exp(b)).astype(BF), st.astype(BF))
    for s in range(C):
        causal = row >= s
        e = jnp.exp(jnp.where(causal, b - b[s:s + 1], 0.0))
        fz = jnp.where(causal, qf * kk[s:s + 1] * e, 0.0)
        o_ref[...] += _dot(fz.astype(BF), bd) * v[s:s + 1]
    o = o_ref[...]
    b_last = b[C - 1:C, :]
    kt = kk * jnp.exp(b_last - b)
    upd = lax.dot_general(v.astype(BF), kt.astype(BF), (((0,), (0,)), ((), ())), preferred_element_type=F32)
    r2 = lax.broadcasted_iota(jnp.int32, (HW, HW), 0) // HEAD_DIM
    c2 = lax.broadcasted_iota(jnp.int32, (HW, HW), 1) // HEAD_DIM
    st_new = st * jnp.exp(b_last) + jnp.where(r2 == c2, upd, 0.0)
    st_scr[...] = st_new
    g = g_ref[...]
    o_ref[...] = _head_rmsnorm(o, gain_ref[...], gmat_ref[...]) * (g * jax.nn.sigmoid(g))

    @pl.when(c == pl.num_programs(1) - 1)
    def _():
        for h in range(NH):
            sout_ref[0, h] = st_new[h * HEAD_DIM:(h + 1) * HEAD_DIM, h * HEAD_DIM:(h + 1) * HEAD_DIM]


def hgrn_mixer(z, row0, B, T, s0, lb, out_gain, gmat):
    C = math.gcd(T, HGRN_CHUNK)
    nc = T // C
    rb0 = row0 // C
    idx = np.arange(HW) // HEAD_DIM
    bd = jnp.asarray((idx[:, None] == idx[None, :]).astype(np.float32), dtype=BF)

    def sec(cidx):
        return pl.BlockSpec((C, HW), lambda b, c, cidx=cidx: (rb0 + b * nc + c, cidx))

    state = pl.BlockSpec((1, NH, HGRN_DV, HGRN_DK), lambda b, c: (b, 0, 0, 0))
    vec = pl.BlockSpec((1, HW), lambda b, c: (0, 0))
    mat = pl.BlockSpec((HW, HW), lambda b, c: (0, 0))
    o, st = pl.pallas_call(
        functools.partial(_hgrn_kernel, C=C),
        grid=(B, nc),
        in_specs=[sec(SEC_HQ), sec(SEC_HF), sec(SEC_HI), sec(SEC_HG), state, vec, vec, mat, mat],
        out_specs=[pl.BlockSpec((C, HW), lambda b, c: (b * nc + c, 0)), state],
        out_shape=[jax.ShapeDtypeStruct((B * T, HW), F32),
                   jax.ShapeDtypeStruct((B, NH, HGRN_DV, HGRN_DK), F32)],
        scratch_shapes=[pltpu.VMEM((HW, HW), F32)],
        compiler_params=_cparams(("parallel", "arbitrary")),
        name="hgrn_recurrence",
    )(z, z, z, z, jnp.swapaxes(s0, 2, 3), lb[None, :], jnp.tile(out_gain, NH)[None, :], bd, gmat)
    return o, jnp.swapaxes(st, 2, 3)


DEC_PG = 8


def _page_specs(tail, PG, base):
    return [pl.BlockSpec((1,) + tail, lambda b, s, pt, j=j: (base + pt[b, s * PG + j], 0, 0)) for j in range(PG)]


def _pool_view(pools, l, width):
    pages = pools.reshape(pools.shape[0] * pools.shape[1], PAGE_SIZE, width)
    return jnp.swapaxes(pages, 1, 2), l * pools.shape[1]


def _rows_nat(q_ref):
    return jnp.concatenate([q_ref[0, h] for h in range(NH)], axis=1)


def _block_diag_q(qnat, T):
    q4 = jnp.concatenate([qnat] * NH, axis=0)
    r = lax.broadcasted_iota(jnp.int32, q4.shape, 0) // T
    c = lax.broadcasted_iota(jnp.int32, q4.shape, 1) // HEAD_DIM
    return jnp.where(r == c, q4, 0.0)


def _diag_heads(x, T):
    return jnp.concatenate([x[h * T:(h + 1) * T, h * HEAD_DIM:(h + 1) * HEAD_DIM] for h in range(NH)], axis=1)


def _new_causal(R, T):
    tq = lax.broadcasted_iota(jnp.int32, (R, T), 0) % T
    tk = lax.broadcasted_iota(jnp.int32, (R, T), 1)
    return tk <= tq


def _fox_decode_kernel(pt_ref, q_ref, new_ref, cq_ref, cn_ref, ck_ref, *rest, PG, T):
    page_refs, o_ref = rest[:PG], rest[PG]
    m_scr, l_scr, acc_scr = rest[PG + 1:]
    s = pl.program_id(1)
    R = NH * T

    @pl.when(s == 0)
    def _():
        m_scr[...] = jnp.full_like(m_scr, NEG_INF)
        l_scr[...] = jnp.zeros_like(l_scr)
        acc_scr[...] = jnp.zeros_like(acc_scr)

    qbd = _block_diag_q(_rows_nat(q_ref), T).astype(BF)
    cq = cq_ref[0]
    m_i, l_i, acc = m_scr[...], l_scr[...], acc_scr[...]
    for j in range(PG):
        page = page_refs[j][0]
        kt = page[:HW].astype(BF)
        vt = page[HW:].astype(BF)
        ck = jnp.broadcast_to(ck_ref[0, j][:, None, :], (NH, T, PAGE_SIZE)).reshape(R, PAGE_SIZE)
        sc = _dot(qbd, kt) + cq - ck
        m_new = jnp.maximum(m_i, jnp.max(sc, axis=-1, keepdims=True))
        p = jnp.exp(sc - m_new)
        alpha = jnp.exp(m_i - m_new)
        l_i = alpha * l_i + jnp.sum(p, axis=-1, keepdims=True)
        acc = alpha * acc + _dot_nt(p.astype(BF), vt)
        m_i = m_new
    m_scr[...] = m_i
    l_scr[...] = l_i
    acc_scr[...] = acc

    @pl.when(s == pl.num_programs(1) - 1)
    def _():
        new = new_ref[...]
        sc = _dot_nt(qbd, new[:, :HW].astype(BF)) + cq - cn_ref[0]
        _, l_f, acc_f = _softmax_update(sc, _new_causal(R, T), m_i, l_i, acc, new[:, HW:].astype(BF), R)
        o_ref[...] = _diag_heads(acc_f / jnp.maximum(l_f, 1.0), T)


def fox_decode(sp, pools, l, page_table, c, row_new, B, T):
    n_pages = page_table.shape[1]
    PG = min(DEC_PG, n_pages)
    past = n_pages * PAGE_SIZE
    R = NH * T
    view, base = _pool_view(pools, l, 2 * HW)
    ch = jnp.transpose(c, (0, 2, 1))
    ck = ch[:, :, :past].reshape(B, NH, n_pages, PAGE_SIZE).transpose(0, 2, 1, 3)
    cnew = ch[:, :, past:]
    cq = cnew.reshape(B, R, 1)
    cn = jnp.broadcast_to(cnew[:, :, None, :], (B, NH, T, T)).reshape(B, R, T)
    return pl.pallas_call(
        functools.partial(_fox_decode_kernel, PG=PG, T=T),
        grid_spec=pltpu.PrefetchScalarGridSpec(
            num_scalar_prefetch=1, grid=(B, n_pages // PG),
            in_specs=[pl.BlockSpec((1, NH, T, HEAD_DIM), lambda b, s, pt: (0, 0, b, 0)),
                      pl.BlockSpec((T, 2 * HW), lambda b, s, pt: (b, 0)),
                      pl.BlockSpec((1, R, 1), lambda b, s, pt: (b, 0, 0)),
                      pl.BlockSpec((1, R, T), lambda b, s, pt: (b, 0, 0)),
                      pl.BlockSpec((1, PG, NH, PAGE_SIZE), lambda b, s, pt: (b, s, 0, 0))]
            + _page_specs((2 * HW, PAGE_SIZE), PG, base),
            out_specs=pl.BlockSpec((T, HW), lambda b, s, pt: (b, 0)),
            scratch_shapes=[pltpu.VMEM((R, 1), F32), pltpu.VMEM((R, 1), F32), pltpu.VMEM((R, HW), F32)]),
        out_shape=jax.ShapeDtypeStruct((B * T, HW), F32),
        compiler_params=_cparams(("parallel", "arbitrary")),
        name="fox_decode",
    )(page_table, sp["fqn"], row_new, cq, cn, ck, *([view] * PG))


def _moba_decode_kernel(pt_ref, q_ref, new_ref, *rest, PG, T, n_pages):
    page_refs, o_ref = rest[:PG], rest[PG]
    m_scr, l_scr, rs_scr, acc_scr = rest[PG + 1:]
    s = pl.program_id(1)
    R = NH * T
    lane = lax.broadcasted_iota(jnp.int32, (R, LANES), 1)

    @pl.when(s == 0)
    def _():
        m_scr[...] = jnp.full_like(m_scr, NEG_INF)
        l_scr[...] = jnp.zeros_like(l_scr)
        rs_scr[...] = jnp.zeros_like(rs_scr)

    qbd = _block_diag_q(_rows_nat(q_ref), T).astype(BF)
    mm, ll, rs = m_scr[...], l_scr[...], rs_scr[...]
    for j in range(PG):
        pid = s * PG + j
        page = page_refs[j][0]
        sc = _dot(qbd, page[:HW].astype(BF))
        m = jnp.max(sc, axis=-1, keepdims=True)
        p = jnp.exp(sc - m)
        acc_scr[pid] = _dot_nt(p.astype(BF), page[HW:].astype(BF))
        hit = lane == pid
        mm = jnp.where(hit, m, mm)
        ll = jnp.where(hit, jnp.sum(p, axis=-1, keepdims=True), ll)
        rs = jnp.where(hit, jnp.sum(sc, axis=-1, keepdims=True), rs)
    m_scr[...] = mm
    l_scr[...] = ll
    rs_scr[...] = rs

    @pl.when(s == pl.num_programs(1) - 1)
    def _():
        block_lane = (lane % 2 == 0) & (lane < n_pages)
        g = jnp.where(block_lane, (rs + pltpu.roll(rs, LANES - 1, 1)) * (1.0 / MOBA_BLOCK), NEG_INF)
        sel = jnp.zeros((R, LANES), F32)
        for _ in range(MOBA_TOPK):
            mx = jnp.max(g, axis=1, keepdims=True)
            idx = jnp.min(jnp.where(g == mx, lane, LANES), axis=1, keepdims=True)
            hit = lane == idx
            sel = jnp.where(hit & block_lane, 1.0, sel)
            g = jnp.where(hit, NEG_INF, g)
        picked = (sel + pltpu.roll(sel, 1, 1)) > 0.5
        new = new_ref[...]
        mask_n = _new_causal(R, T)
        sn = jnp.where(mask_n, _dot_nt(qbd, new[:, :HW].astype(BF)), NEG_INF)
        m_n = jnp.max(sn, axis=-1, keepdims=True)
        p_n = jnp.where(mask_n, jnp.exp(sn - m_n), 0.0)
        m_tot = jnp.maximum(jnp.max(jnp.where(picked, mm, NEG_INF), axis=1, keepdims=True), m_n)
        w = jnp.where(picked, jnp.exp(mm - m_tot), 0.0)
        w_n = jnp.exp(m_n - m_tot)
        l_tot = jnp.sum(w * ll, axis=1, keepdims=True) + w_n * jnp.sum(p_n, axis=-1, keepdims=True)
        acc = w_n * _dot(p_n.astype(BF), new[:, HW:].astype(BF))
        for pg in range(n_pages):
            acc = acc + w[:, pg:pg + 1] * acc_scr[pg]
        o_ref[...] = _diag_heads(acc / jnp.maximum(l_tot, 1.0), T)


def moba_decode(sp, pools, l, page_table, row_new, B, T):
    n_pages = page_table.shape[1]
    assert MOBA_BLOCK == 2 * PAGE_SIZE and n_pages % 2 == 0 and n_pages <= LANES and T <= MOBA_BLOCK
    PG = min(DEC_PG, n_pages)
    R = NH * T
    view, base = _pool_view(pools, l, 2 * HW)
    return pl.pallas_call(
        functools.partial(_moba_decode_kernel, PG=PG, T=T, n_pages=n_pages),
        grid_spec=pltpu.PrefetchScalarGridSpec(
            num_scalar_prefetch=1, grid=(B, n_pages // PG),
            in_specs=[pl.BlockSpec((1, NH, T, HEAD_DIM), lambda b, s, pt: (0, 0, b, 0)),
                      pl.BlockSpec((T, 2 * HW), lambda b, s, pt: (b, 0))]
            + _page_specs((2 * HW, PAGE_SIZE), PG, base),
            out_specs=pl.BlockSpec((T, HW), lambda b, s, pt: (b, 0)),
            scratch_shapes=[pltpu.VMEM((R, LANES), F32), pltpu.VMEM((R, LANES), F32), pltpu.VMEM((R, LANES), F32),
                            pltpu.VMEM((n_pages, R, HW), F32)]),
        out_shape=jax.ShapeDtypeStruct((B * T, HW), F32),
        compiler_params=_cparams(("parallel", "arbitrary")),
        name="moba_decode",
    )(page_table, sp["mqr"], row_new, *([view] * PG))


def _nsa_cmp_phys_kernel(x_ref, pe_ref, w_ref, y_ref):
    x = x_ref[...]
    a = _dot((x + pe_ref[0:1]).astype(BF), w_ref[0])
    b = _dot((x + pe_ref[1:2]).astype(BF), w_ref[1])
    y_ref[...] = jnp.concatenate([a, b], axis=1)


def nsa_compress_pool(pools, l, cmp_pe, cmp_w):
    S = NSA_CMP_STRIDE
    n_phys = pools.shape[1]
    rows = n_phys * (PAGE_SIZE // S)
    cw = S * NSA_ROWS * HEAD_DIM
    pe = jnp.transpose(cmp_pe.reshape(2, 2, S, HEAD_DIM), (1, 2, 0, 3))
    pe = jnp.pad(pe, ((0, 0), (0, 0), (0, 2), (0, 0))).reshape(2, cw)
    w = cmp_w.reshape(2, 2, S, HEAD_DIM, HEAD_DIM)
    wz = jnp.zeros((2, S, NSA_ROWS, HEAD_DIM, 2, HEAD_DIM), F32)
    wz = wz.at[:, :, 0, :, 0, :].set(w[0]).at[:, :, 1, :, 1, :].set(w[1])
    wz = wz.reshape(2, cw, 2 * HEAD_DIM).astype(BF)
    tm = next(t for t in (512, 256, 128, 64, 32, 16, 8) if rows % t == 0)
    return pl.pallas_call(
        _nsa_cmp_phys_kernel,
        grid=(rows // tm,),
        in_specs=[pl.BlockSpec((tm, cw), lambda i: (l * (rows // tm) + i, 0)),
                  pl.BlockSpec((2, cw), lambda i: (0, 0)),
                  pl.BlockSpec((2, cw, 2 * HEAD_DIM), lambda i: (0, 0, 0))],
        out_specs=pl.BlockSpec((tm, HW), lambda i: (i, 0)),
        out_shape=jax.ShapeDtypeStruct((rows, HW), F32),
        compiler_params=_cparams(("parallel",)),
        name="nsa_compress_pool",
    )(pools.reshape(pools.shape[0] * rows, cw), pe, wz)


def _nsa_sel_decode_kernel(y_ref, q_ref, gain_ref, cover_ref, ocmp_ref, sel_ref, *, T, offset, n_cmp, n_sel):
    R = NH * T
    y = y_ref[0]
    nch = y.shape[0]
    c = y[:, :2 * HEAD_DIM] + pltpu.roll(y[:, 2 * HEAD_DIM:], nch - 1, 0)
    lane = lax.broadcasted_iota(jnp.int32, c.shape, 1)
    ms = jnp.sum(jnp.where(lane < HEAD_DIM, c * c, 0.0), axis=1, keepdims=True) / HEAD_DIM
    kn = c * lax.rsqrt(ms + RMS_EPS) * gain_ref[...]
    kc = kn[:, :HEAD_DIM].astype(BF)
    vc = c[:, HEAD_DIM:].astype(BF)
    q = q_ref[0].reshape(R, HEAD_DIM).astype(BF)
    pos = offset + lax.broadcasted_iota(jnp.int32, (R, 1), 0) % T
    n_idx = lax.broadcasted_iota(jnp.int32, (R, nch), 1)
    mask = (n_idx * NSA_CMP_STRIDE + (NSA_CMP_BLOCK - 1) <= pos) & (n_idx < n_cmp)
    s_c = jnp.where(mask, _dot_nt(q, kc), NEG_INF)
    m_c = jnp.max(s_c, axis=-1, keepdims=True)
    p_c = jnp.where(mask, jnp.exp(s_c - m_c), 0.0)
    p_c = (p_c / jnp.maximum(jnp.sum(p_c, axis=-1, keepdims=True), 1.0)).astype(BF)
    ocmp_ref[0] = _dot(p_c, vc)
    imp4 = _dot(p_c, cover_ref[...])
    imp = imp4[0:T]
    for h in range(1, NH):
        imp = imp + imp4[h * T:(h + 1) * T]
    j = lax.broadcasted_iota(jnp.int32, imp.shape, 1)
    cur = (offset + lax.broadcasted_iota(jnp.int32, (T, 1), 0)) // NSA_SEL_BLOCK
    forced = (j == 0) | (j == cur) | (j == cur - 1)
    imp = jnp.where(j > cur, NEG_INF, jnp.where(forced, SEL_FORCE, imp))
    rank = jnp.zeros(imp.shape, F32)
    for jp in range(n_sel):
        col = imp[:, jp:jp + 1]
        rank = rank + jnp.where((col > imp) | ((col == imp) & (j > jp)), 1.0, 0.0)
    sel_ref[0] = jnp.where((rank < NSA_TOPN) & (j <= cur), 1.0, 0.0)


def _nsa_decode_kernel(pt_ref, q_ref, sel_ref, ocmp_ref, newr_ref, win_ref, neww_ref, gl_ref, *rest,
                       PG, T, offset):
    page_refs, o_ref = rest[:PG], rest[PG]
    m_scr, l_scr, acc_scr = rest[PG + 1:]
    s = pl.program_id(1)
    R = NH * T
    D = HEAD_DIM

    @pl.when(s == 0)
    def _():
        m_scr[...] = jnp.full_like(m_scr, NEG_INF)
        l_scr[...] = jnp.zeros_like(l_scr)
        acc_scr[...] = jnp.zeros_like(acc_scr)

    q = q_ref[0].reshape(R, D).astype(BF)
    sel = sel_ref[0]
    jl = lax.broadcasted_iota(jnp.int32, sel.shape, 1)
    half = lax.broadcasted_iota(jnp.int32, (T, PAGE_SIZE), 1) < NSA_SEL_BLOCK
    m_i, l_i, acc = m_scr[...], l_scr[...], acc_scr[...]
    for j in range(PG):
        pid = s * PG + j
        page = page_refs[j][0]
        sc = _dot(q, page[2 * D:3 * D].astype(BF))
        sa = jnp.max(jnp.where(jl == 2 * pid, sel, 0.0), axis=1, keepdims=True)
        sb = jnp.max(jnp.where(jl == 2 * pid + 1, sel, 0.0), axis=1, keepdims=True)
        mt = jnp.where(half, sa, sb) > 0.5
        mask = jnp.broadcast_to(mt[None], (NH, T, PAGE_SIZE)).reshape(R, PAGE_SIZE)
        m_i, l_i, acc = _softmax_update(sc, mask, m_i, l_i, acc, page[3 * D:].astype(BF), R, v_feature_major=True)
    m_scr[...] = m_i
    l_scr[...] = l_i
    acc_scr[...] = acc

    @pl.when(s == pl.num_programs(1) - 1)
    def _():
        causal = _new_causal(R, T)
        newr = newr_ref[...]
        cb = offset // NSA_SEL_BLOCK
        own = jnp.broadcast_to((sel[:, cb:cb + 1] > 0.5)[None], (NH, T, T)).reshape(R, T)
        _, l_s, acc_s = _softmax_update(_dot_nt(q, newr[:, 2 * D:3 * D].astype(BF)), causal & own, m_i, l_i, acc,
                                        newr[:, 3 * D:].astype(BF), R)
        o_sel = acc_s / jnp.maximum(l_s, 1.0)
        win = win_ref[0]
        neww = neww_ref[...]
        wb = win.shape[1]
        tq = lax.broadcasted_iota(jnp.int32, (R, wb), 0) % T
        rk = lax.broadcasted_iota(jnp.int32, (R, wb), 1)
        mask1 = rk > wb + tq - NSA_WINDOW
        s1 = jnp.where(mask1, _dot(q, win[:D].astype(BF)), NEG_INF)
        s2 = jnp.where(causal, _dot_nt(q, neww[:, :D].astype(BF)), NEG_INF)
        m_w = jnp.maximum(jnp.max(s1, axis=-1, keepdims=True), jnp.max(s2, axis=-1, keepdims=True))
        p1 = jnp.where(mask1, jnp.exp(s1 - m_w), 0.0)
        p2 = jnp.where(causal, jnp.exp(s2 - m_w), 0.0)
        l_w = jnp.sum(p1, axis=-1, keepdims=True) + jnp.sum(p2, axis=-1, keepdims=True)
        o_win = (_dot_nt(p1.astype(BF), win[D:].astype(BF)) + _dot(p2.astype(BF), neww[:, D:].astype(BF))) \
            / jnp.maximum(l_w, 1.0)
        o_cmp = ocmp_ref[0]
        sig = jax.nn.sigmoid(gl_ref[...])
        outs = []
        for h in range(NH):
            c = MISC_LANE + 3 * h
            rs = slice(h * T, (h + 1) * T)
            outs.append(sig[:, c:c + 1] * o_cmp[rs] + sig[:, c + 1:c + 2] * o_sel[rs] + sig[:, c + 2:c + 3] * o_win[rs])
        o_ref[...] = jnp.concatenate(outs, axis=1)


def nsa_decode(sp, z, row0, pools, l, page_table, win_state, cmp_pe, cmp_w, gain1, B, T):
    n_pages = page_table.shape[1]
    past = n_pages * PAGE_SIZE
    S = past + T
    R = NH * T
    n_cmp = (S - NSA_CMP_BLOCK) // NSA_CMP_STRIDE + 1
    n_sel = -(-S // NSA_SEL_BLOCK)
    nch = past // NSA_CMP_STRIDE
    assert (n_cmp - 1) * NSA_CMP_STRIDE + NSA_CMP_BLOCK <= past, "compressed blocks must lie inside the cache"
    assert past % NSA_SEL_BLOCK == 0 and T <= NSA_SEL_BLOCK and n_sel <= HW
    PG = min(DEC_PG, n_pages)
    cpp = PAGE_SIZE // NSA_CMP_STRIDE
    y = nsa_compress_pool(pools, l, cmp_pe, cmp_w).reshape(pools.shape[1], cpp, HW)
    yb = y[page_table].reshape(B, nch, HW)
    ci = np.arange(nch)[:, None] * NSA_CMP_STRIDE
    sj = np.arange(HW)[None, :] * NSA_SEL_BLOCK
    cover = (ci < sj + NSA_SEL_BLOCK) & (ci + NSA_CMP_BLOCK > sj) & (np.arange(nch)[:, None] < n_cmp) \
        & (np.arange(HW)[None, :] < n_sel)
    cover = jnp.asarray(cover.astype(np.float32), dtype=BF)
    gain = jnp.concatenate([gain1, jnp.ones((HEAD_DIM,), F32)])[None, :]
    qspec = lambda nargs: pl.BlockSpec((1, NH, T, HEAD_DIM), (lambda b: (0, 0, b, 0)) if nargs == 1
                                       else (lambda b, s, pt: (0, 0, b, 0)))
    ocmp, sel = pl.pallas_call(
        functools.partial(_nsa_sel_decode_kernel, T=T, offset=past, n_cmp=n_cmp, n_sel=n_sel),
        grid=(B,),
        in_specs=[pl.BlockSpec((1, nch, HW), lambda b: (b, 0, 0)), qspec(1),
                  pl.BlockSpec((1, 2 * HEAD_DIM), lambda b: (0, 0)),
                  pl.BlockSpec((nch, HW), lambda b: (0, 0))],
        out_specs=[pl.BlockSpec((1, R, HEAD_DIM), lambda b: (b, 0, 0)), pl.BlockSpec((1, T, HW), lambda b: (b, 0, 0))],
        out_shape=[jax.ShapeDtypeStruct((B, R, HEAD_DIM), F32), jax.ShapeDtypeStruct((B, T, HW), F32)],
        compiler_params=_cparams(("parallel",)),
        name="nsa_select_decode",
    )(yb, sp["nqn"], gain, cover)
    view, base = _pool_view(pools, l, HW)
    wb = win_state.shape[1]
    rb0 = row0 // T
    return pl.pallas_call(
        functools.partial(_nsa_decode_kernel, PG=PG, T=T, offset=past),
        grid_spec=pltpu.PrefetchScalarGridSpec(
            num_scalar_prefetch=1, grid=(B, n_pages // PG),
            in_specs=[qspec(3),
                      pl.BlockSpec((1, T, HW), lambda b, s, pt: (b, 0, 0)),
                      pl.BlockSpec((1, R, HEAD_DIM), lambda b, s, pt: (b, 0, 0)),
                      pl.BlockSpec((T, HW), lambda b, s, pt: (b, 0)),
                      pl.BlockSpec((1, 2 * HEAD_DIM, wb), lambda b, s, pt: (b, 0, 0)),
                      pl.BlockSpec((T, 2 * HEAD_DIM), lambda b, s, pt: (b, 0)),
                      pl.BlockSpec((T, HW), lambda b, s, pt: (rb0 + b, SEC_S2))]
            + _page_specs((HW, PAGE_SIZE), PG, base),
            out_specs=pl.BlockSpec((T, HW), lambda b, s, pt: (b, 0)),
            scratch_shapes=[pltpu.VMEM((R, 1), F32), pltpu.VMEM((R, 1), F32), pltpu.VMEM((R, HEAD_DIM), F32)]),
        out_shape=jax.ShapeDtypeStruct((B * T, HW), F32),
        compiler_params=_cparams(("parallel", "arbitrary")),
        name="nsa_decode",
    )(page_table, sp["nqr"], sel, ocmp, sp["nsa_rows"], jnp.transpose(win_state, (0, 2, 3, 1)).reshape(B, 2 * HEAD_DIM, wb), sp["nsa_win"], z,
      *([view] * PG))


def rmsnorm(x, g):
    xf = x.astype(jnp.float32)
    y = xf * lax.rsqrt(jnp.mean(xf * xf, axis=-1, keepdims=True) + RMS_EPS)
    return (y * g.astype(jnp.float32)).astype(x.dtype)


def rope(x, pos):
    half = HEAD_DIM // 2
    inv = ROPE_THETA ** (-jnp.arange(half, dtype=jnp.float32) / half)
    ang = pos.astype(jnp.float32)[:, None] * inv[None, :]
    cos = jnp.cos(ang)[:, None, :]
    sin = jnp.sin(ang)[:, None, :]
    xf = x.astype(jnp.float32)
    x1, x2 = xf[..., :half], xf[..., half:]
    return jnp.concatenate([x1 * cos - x2 * sin, x2 * cos + x1 * sin], axis=-1).astype(x.dtype)


def masked_softmax(s, mask):
    s = jnp.where(mask, s.astype(jnp.float32), NEG_INF)
    m = jnp.max(s, axis=-1, keepdims=True)
    p = jnp.where(mask, jnp.exp(s - m), 0.0)
    return p / jnp.maximum(jnp.sum(p, axis=-1, keepdims=True), 1.0)


def sweep(fn, blk, *arrays):
    B, T = arrays[0].shape[:2]
    nb = -(-T // blk)
    Tp = nb * blk
    blocks = []
    for a in arrays:
        a = jnp.pad(a, [(0, 0), (0, Tp - T)] + [(0, 0)] * (a.ndim - 2))
        blocks.append(jnp.moveaxis(a.reshape((B, nb, blk) + a.shape[2:]), 1, 0))
    starts = jnp.arange(nb, dtype=jnp.int32) * blk
    out = lax.map(lambda args: fn(args[0], *args[1]), (starts, tuple(blocks)))
    out = jnp.moveaxis(out, 0, 1).reshape((B, Tp) + out.shape[3:])
    return out[:, :T]


def window_attend(q, rows, buf_len):
    B, T, H, Dh = q.shape
    dt = q.dtype
    W = NSA_WINDOW
    qb = min(Q_BLOCK, T)
    nb = -(-T // qb)
    Tp = nb * qb
    band = W + qb
    rp = jnp.pad(rows, ((0, 0), (W, Tp - T), (0, 0), (0, 0)))
    kidx = buf_len + np.arange(nb)[:, None] * qb + np.arange(band)[None, :]
    kb = rp[:, kidx]
    qp = jnp.pad(q, ((0, 0), (0, Tp - T), (0, 0), (0, 0))).reshape(B, nb, qb, H, Dh)
    s = jnp.einsum('bnqhd,bnkd->bhnqk', qp, kb[..., 0, :], preferred_element_type=jnp.float32) * ATTN_SCALE
    qq = buf_len + np.arange(nb)[:, None] * qb + np.arange(qb)[None, :]
    kk = kidx - W
    mask = (kk[:, None, :] >= 0) & (kk[:, None, :] <= qq[:, :, None]) & (kk[:, None, :] > qq[:, :, None] - W)
    p = masked_softmax(s, mask)
    o = jnp.einsum('bhnqk,bnkd->bnqhd', p.astype(dt), kb[..., 1, :]).reshape(B, Tp, H, Dh)
    return o[:, :T]


def nsa_mixer(q, kv_c, kv_s, kv_w, gate_logits, past_rows, win_buf, qk_gain, cmp_pe, cmp_w, offset):
    B, T, H, Dh = q.shape
    dt = q.dtype
    pos = offset + jnp.arange(T, dtype=jnp.int32)
    qn = rmsnorm(q, qk_gain[0])
    qr = rope(qn, pos)
    k_s = rope(rmsnorm(kv_s[:, :, 0:1], qk_gain[2]), pos)[:, :, 0]
    k_w = rope(rmsnorm(kv_w[:, :, 0:1], qk_gain[3]), pos)[:, :, 0]
    new_rows = jnp.stack([kv_c[:, :, 0], kv_c[:, :, 1], k_s, kv_s[:, :, 1]], axis=2)
    rows = jnp.concatenate([past_rows, new_rows], axis=1)
    S = offset + T
    n_cmp = (S - NSA_CMP_BLOCK) // NSA_CMP_STRIDE + 1
    cidx = np.arange(n_cmp)[:, None] * NSA_CMP_STRIDE + np.arange(NSA_CMP_BLOCK)[None, :]

    def compress(r):
        blocks = rows[:, :, r][:, cidx] + cmp_pe[r]
        return blocks.reshape(B, n_cmp, NSA_CMP_BLOCK * Dh) @ cmp_w[r]

    k_cmp = rmsnorm(compress(0), qk_gain[1])
    v_cmp = compress(1)
    s_c = jnp.einsum('bthd,bnd->bhtn', qn, k_cmp, preferred_element_type=jnp.float32) * ATTN_SCALE
    cmp_end = np.arange(n_cmp) * NSA_CMP_STRIDE + NSA_CMP_BLOCK - 1
    p_c = masked_softmax(s_c, cmp_end[None, :] <= pos[:, None])
    o_cmp = jnp.einsum('bhtn,bnd->bthd', p_c.astype(dt), v_cmp)
    n_sel = -(-S // NSA_SEL_BLOCK)
    ci = np.arange(n_cmp)[:, None] * NSA_CMP_STRIDE
    sj = np.arange(n_sel)[None, :] * NSA_SEL_BLOCK
    cover = ((ci < sj + NSA_SEL_BLOCK) & (ci + NSA_CMP_BLOCK > sj)).astype(np.float32)
    imp = jnp.einsum('bhtn,nj->btj', p_c, jnp.asarray(cover))
    cur = (pos // NSA_SEL_BLOCK)[:, None]
    jj = jnp.arange(n_sel)[None, :]
    forced = (jj == 0) | (jj == cur) | (jj == cur - 1)
    imp = jnp.where(jj > cur, NEG_INF, jnp.where(forced, SEL_FORCE, imp))
    _, sel_idx = lax.top_k(imp, min(NSA_TOPN, n_sel))
    kv_sel = jnp.pad(rows[:, :, 2:4], ((0, 0), (0, n_sel * NSA_SEL_BLOCK - S), (0, 0), (0, 0)))
    kv_sel = kv_sel.reshape(B, n_sel, NSA_SEL_BLOCK, 2, Dh)
    bidx = jnp.arange(B)[:, None, None]

    def sel_block(start, q_blk, idx_blk):
        qb = q_blk.shape[1]
        tp = offset + start + jnp.arange(qb)
        g = kv_sel[bidx, idx_blk]
        kpos = idx_blk[..., None] * NSA_SEL_BLOCK + jnp.arange(NSA_SEL_BLOCK)
        mask = (kpos <= tp[None, :, None, None]).reshape(B, 1, qb, -1)
        g = g.reshape(B, qb, -1, 2, Dh)
        s = jnp.einsum('bqhd,bqkd->bhqk', q_blk, g[..., 0, :], preferred_element_type=jnp.float32) * ATTN_SCALE
        p = masked_softmax(s, mask)
        return jnp.einsum('bhqk,bqkd->bqhd', p.astype(dt), g[..., 1, :])

    o_sel = sweep(sel_block, min(GATHER_Q_BLOCK, T), qr, sel_idx)
    win_rows = jnp.concatenate([win_buf, jnp.stack([k_w, kv_w[:, :, 1]], axis=2)], axis=1)
    o_win = window_attend(qr, win_rows, win_buf.shape[1])
    gates = jax.nn.sigmoid(gate_logits.astype(jnp.float32)).astype(dt)
    o = gates[..., 0:1] * o_cmp + gates[..., 1:2] * o_sel + gates[..., 2:3] * o_win
    new_win = win_rows[:, -min(NSA_WINDOW, win_rows.shape[1]):]
    return o, new_rows, new_win


def gated_recurrence(q, k, v, logf, S0):
    B, T, H, DK = q.shape
    DV = v.shape[-1]
    C = math.gcd(T, HGRN_CHUNK)
    nc = T // C

    def chunks(a):
        return jnp.moveaxis(a.reshape((B, nc, C) + a.shape[2:]), 1, 0).swapaxes(2, 3)

    causal = jnp.tril(jnp.ones((C, C), dtype=bool))[:, :, None]

    def step(S, inp):
        qc, kc, vc, gc = inp
        b = jnp.cumsum(gc, axis=2)
        o_inter = jnp.einsum('bhtk,bhkv->bhtv', qc * jnp.exp(b), S)
        diff = b[:, :, :, None, :] - b[:, :, None, :, :]
        decay = jnp.where(causal, jnp.exp(jnp.where(causal, diff, 0.0)), 0.0)
        A = jnp.einsum('bhtk,bhsk,bhtsk->bhts', qc, kc, decay)
        o = o_inter + jnp.einsum('bhts,bhsv->bhtv', A, vc)
        b_last = b[:, :, -1:, :]
        S_new = jnp.exp(b_last[:, :, 0, :])[..., None] * S + jnp.einsum('bhsk,bhsv->bhkv', kc * jnp.exp(b_last - b), vc)
        return S_new, o

    S, o = lax.scan(step, S0, (chunks(q), chunks(k), chunks(v), chunks(logf)))
    o = jnp.moveaxis(o.swapaxes(2, 3), 0, 1).reshape(B, T, H, DV)
    return o, S


def hgrn2_mixer(q, f, i, g, S0, lb, out_gain):
    dt = q.dtype
    H = q.shape[2]
    lb = lb.reshape(H, HGRN_DK)
    z = f.astype(jnp.float32)
    logf = jnp.log(lb + (1.0 - lb) * jax.nn.sigmoid(z))
    k = (1.0 - lb) * jax.nn.sigmoid(-z)
    qf = jax.nn.silu(q.astype(jnp.float32))
    o, S = gated_recurrence(qf, k, i.astype(jnp.float32), logf, S0)
    o = rmsnorm(o, out_gain) * jax.nn.silu(g.astype(jnp.float32))
    return o.astype(dt), S


def moba_mixer(q, k, v, past_rows, qk_gain, offset):
    B, T, H, Dh = q.shape
    dt = q.dtype
    pos = offset + jnp.arange(T, dtype=jnp.int32)
    qr = rope(rmsnorm(q, qk_gain[0]), pos)
    kr = rope(rmsnorm(k, qk_gain[1]), pos)
    new_rows = jnp.stack([kr, v], axis=2)
    rows = jnp.concatenate([past_rows, new_rows], axis=1)
    S = offset + T
    nblk = -(-S // MOBA_BLOCK)
    kvb = jnp.pad(rows, ((0, 0), (0, nblk * MOBA_BLOCK - S), (0, 0), (0, 0), (0, 0)))
    kvb = jnp.transpose(kvb.reshape(B, nblk, MOBA_BLOCK, 2, H, Dh), (0, 4, 1, 2, 3, 5))
    kmean = jnp.mean(kvb[..., 0, :].astype(jnp.float32), axis=3)
    gate = jnp.einsum('bthd,bhnd->bthn', qr.astype(jnp.float32), kmean)
    own = pos // MOBA_BLOCK
    past_ok = jnp.arange(nblk)[None, :] < own[:, None]
    _, top = lax.top_k(jnp.where(past_ok[None, :, None, :], gate, NEG_INF), min(MOBA_TOPK, nblk))
    valid = top < own[None, :, None, None]
    idx = jnp.concatenate([top, jnp.broadcast_to(own[None, :, None, None], (B, T, H, 1)).astype(top.dtype)], axis=-1)
    ok = jnp.concatenate([valid, jnp.ones((B, T, H, 1), dtype=bool)], axis=-1)
    bidx = jnp.arange(B)[:, None, None, None]
    hidx = jnp.arange(H)[None, None, :, None]

    def blk_fn(start, q_blk, idx_blk, ok_blk):
        qb = q_blk.shape[1]
        tp = offset + start + jnp.arange(qb)
        g = kvb[bidx, hidx, idx_blk]
        kpos = idx_blk[..., None] * MOBA_BLOCK + jnp.arange(MOBA_BLOCK)
        mask = (ok_blk[..., None] & (kpos <= tp[None, :, None, None, None])).reshape(B, qb, H, -1)
        g = g.reshape(B, qb, H, -1, 2, Dh)
        s = jnp.einsum('bqhd,bqhkd->bqhk', q_blk, g[..., 0, :], preferred_element_type=jnp.float32) * ATTN_SCALE
        p = masked_softmax(s, mask)
        return jnp.einsum('bqhk,bqhkd->bqhd', p.astype(dt), g[..., 1, :])

    o = sweep(blk_fn, min(GATHER_Q_BLOCK, T), qr, idx, ok)
    return o, new_rows


def fox_mixer(q, k, v, f_logit, past_kv, past_logf, qk_gain, f_bias, offset):
    B, T, H, Dh = q.shape
    dt = q.dtype
    qn = rmsnorm(q, qk_gain[0])
    kn = rmsnorm(k, qk_gain[1])
    logf_new = jax.nn.log_sigmoid(f_logit.astype(jnp.float32) + f_bias.astype(jnp.float32))
    new_rows = jnp.stack([kn, v], axis=2)
    rows = jnp.concatenate([past_kv, new_rows], axis=1)
    c = jnp.cumsum(jnp.concatenate([past_logf.astype(jnp.float32), logf_new], axis=1), axis=1)
    S = offset + T
    K = rows[:, :, 0]
    V = rows[:, :, 1]
    c_k = jnp.moveaxis(c, 1, 2)[:, :, None, :]
    kpos = jnp.arange(S)

    def blk_fn(start, q_blk, cq_blk):
        qb = q_blk.shape[1]
        tp = offset + start + jnp.arange(qb)
        s = jnp.einsum('bqhd,bkhd->bhqk', q_blk, K, preferred_element_type=jnp.float32) * ATTN_SCALE
        s = s + jnp.moveaxis(cq_blk, 1, 2)[..., None] - c_k
        p = masked_softmax(s, kpos[None, :] <= tp[:, None])
        return jnp.einsum('bhqk,bkhd->bqhd', p.astype(dt), V)

    o = sweep(blk_fn, min(Q_BLOCK, T), qn, c[:, offset:])
    return o, new_rows, logf_new.astype(dt)


def _ffn_up_kernel(be_ref, new_ref, x_ref, w1_ref, w3_ref, u_ref, w1_scr, w3_scr):
    i = pl.program_id(1)

    @pl.when(new_ref[i] == 1)
    def _():
        w1_scr[...] = w1_ref[0].astype(BF)
        w3_scr[...] = w3_ref[0].astype(BF)

    x = x_ref[...].astype(BF)
    a = _dot(x, w1_scr[...])
    b = _dot(x, w3_scr[...])
    u_ref[...] = (a * jax.nn.sigmoid(a) * b).astype(u_ref.dtype)


def _ffn_down_kernel(be_ref, new_ref, u_ref, w2_ref, y_ref, w2_scr):
    i = pl.program_id(1)

    @pl.when(new_ref[i] == 1)
    def _():
        w2_scr[...] = w2_ref[0].astype(BF)

    y_ref[...] = _dot(u_ref[...], w2_scr[...])


def grouped_swiglu(x, block_exp, w1, w3, w2, tm, tf, tn):
    R, D = x.shape
    F = w1.shape[2]
    nblk = R // tm
    block_exp = block_exp.astype(jnp.int32)
    new = jnp.concatenate([jnp.ones((1,), jnp.int32), (block_exp[1:] != block_exp[:-1]).astype(jnp.int32)])
    u = pl.pallas_call(
        _ffn_up_kernel,
        grid_spec=pltpu.PrefetchScalarGridSpec(
            num_scalar_prefetch=2, grid=(F // tf, nblk),
            in_specs=[pl.BlockSpec((tm, D), lambda j, i, be, nw: (i, 0)),
                      pl.BlockSpec((1, D, tf), lambda j, i, be, nw: (be[i], 0, j)),
                      pl.BlockSpec((1, D, tf), lambda j, i, be, nw: (be[i], 0, j))],
            out_specs=pl.BlockSpec((tm, tf), lambda j, i, be, nw: (i, j)),
            scratch_shapes=[pltpu.VMEM((D, tf), BF), pltpu.VMEM((D, tf), BF)]),
        out_shape=jax.ShapeDtypeStruct((R, F), BF),
        compiler_params=_cparams(("arbitrary", "arbitrary")),
        name="swiglu_up",
    )(block_exp, new, x, w1, w3)
    return pl.pallas_call(
        _ffn_down_kernel,
        grid_spec=pltpu.PrefetchScalarGridSpec(
            num_scalar_prefetch=2, grid=(D // tn, nblk),
            in_specs=[pl.BlockSpec((tm, F), lambda n, i, be, nw: (i, 0)),
                      pl.BlockSpec((1, F, tn), lambda n, i, be, nw: (be[i], 0, n))],
            out_specs=pl.BlockSpec((tm, tn), lambda n, i, be, nw: (i, n)),
            scratch_shapes=[pltpu.VMEM((F, tn), BF)]),
        out_shape=jax.ShapeDtypeStruct((R, D), F32),
        compiler_params=_cparams(("arbitrary", "arbitrary")),
        name="swiglu_down",
    )(block_exp, new, u, w2)


def swiglu_dense(x, w1, w3, w2):
    tm = 640 if x.shape[0] % 640 == 0 else 256
    be = jnp.zeros((x.shape[0] // tm,), jnp.int32)
    return grouped_swiglu(x, be, w1[None], w3[None], w2[None], tm, DENSE_TF, FFN_TN)


def moe_ffn_grouped(xf, router, w1, w3, w2):
    N, D = xf.shape
    tm = MOE_TM
    rpad = jnp.pad(router, ((0, 0), (0, LANES - N_EXPERTS)))
    logits = matmul(xf, rpad, tm=256, tn=LANES)[:, :N_EXPERTS]
    top_v, top_e = lax.top_k(logits, TOP_K)
    gates = jax.nn.softmax(top_v, axis=-1)
    NK = N * TOP_K
    flat_e = top_e.reshape(NK)
    order = jnp.argsort(flat_e)
    e_sorted = flat_e[order]
    tok_sorted = (order // TOP_K).astype(jnp.int32)
    counts = jnp.sum((flat_e[:, None] == jnp.arange(N_EXPERTS)[None, :]).astype(jnp.int32), axis=0)
    padded = (counts + tm - 1) // tm * tm
    pend = jnp.cumsum(padded)
    pstart = pend - padded
    start = jnp.cumsum(counts) - counts
    dest_sorted = pstart[e_sorted] + (jnp.arange(NK, dtype=jnp.int32) - start[e_sorted])
    n_blocks = -(-NK // tm) + N_EXPERTS
    slot_tok = jnp.full((n_blocks * tm,), N, jnp.int32).at[dest_sorted].set(tok_sorted)
    block_exp = jnp.clip(jnp.searchsorted(pend, jnp.arange(n_blocks) * tm, side='right'), 0, N_EXPERTS - 1)
    xpad = jnp.concatenate([xf, jnp.zeros((1, D), xf.dtype)], axis=0)
    xb = xpad[slot_tok]
    yb = grouped_swiglu(xb, block_exp, w1, w3, w2, tm, MOE_TF, FFN_TN)
    dest = jnp.zeros((NK,), jnp.int32).at[order].set(dest_sorted).reshape(N, TOP_K)
    return yb[dest[:, 0]] * gates[:, 0:1] + yb[dest[:, 1]] * gates[:, 1:2]


def swiglu(h, w1, w3, w2):
    return (jax.nn.silu(h @ w1) * (h @ w3)) @ w2


def moe_ffn(xf, router, w1, w3, w2):
    N, D = xf.shape
    dt = xf.dtype
    logits = (xf @ router).astype(jnp.float32)
    top_v, top_e = lax.top_k(logits, TOP_K)
    gates = jax.nn.softmax(top_v, axis=-1)
    NK = N * TOP_K
    flat_e = top_e.reshape(NK)
    flat_tok = jnp.arange(NK, dtype=jnp.int32) // TOP_K
    order = jnp.argsort(flat_e)
    e_sorted = flat_e[order]
    tok_sorted = flat_tok[order]
    counts = jnp.zeros((N_EXPERTS,), jnp.int32).at[flat_e].add(1)
    padded = (counts + MOE_BLOCK - 1) // MOE_BLOCK * MOE_BLOCK
    pend = jnp.cumsum(padded)
    pstart = pend - padded
    start = jnp.cumsum(counts) - counts
    dest = pstart[e_sorted] + (jnp.arange(NK, dtype=jnp.int32) - start[e_sorted])
    n_blocks = -(-NK // MOE_BLOCK) + N_EXPERTS
    slot_tok = jnp.full((n_blocks * MOE_BLOCK,), N, jnp.int32).at[dest].set(tok_sorted)
    block_exp = jnp.clip(jnp.searchsorted(pend, jnp.arange(n_blocks) * MOE_BLOCK, side='right'), 0, N_EXPERTS - 1)
    xpad = jnp.concatenate([xf, jnp.zeros((1, D), dt)], axis=0)
    xb = xpad[slot_tok].reshape(n_blocks, MOE_BLOCK, D)

    def expert_block(args):
        xblk, e = args
        return swiglu(xblk, w1[e], w3[e], w2[e])

    yb = lax.map(expert_block, (xb, block_exp)).reshape(n_blocks * MOE_BLOCK, D)
    y_assign = yb[dest] * gates.reshape(NK)[order][:, None].astype(dt)
    return jnp.zeros((N, D), dt).at[tok_sorted].add(y_assign)


def z_sections(z):
    s = lambda c, a=0, b=HW: z[..., c * HW + a:c * HW + b]
    d = HEAD_DIM
    return dict(nq=s(SEC_NQ), nkc=s(SEC_S1, 0, 2 * d), nks=s(SEC_S1, 2 * d, 4 * d), nkw=s(SEC_S2, 0, 2 * d),
                ngate=s(SEC_S2, MISC_LANE, MISC_LANE + 12), ff=s(SEC_S2, MISC_LANE + 12, MISC_LANE + 16),
                hq=s(SEC_HQ), hf=s(SEC_HF), hi=s(SEC_HI), hg=s(SEC_HG), mq=s(SEC_MQ), mk=s(SEC_MK), mv=s(SEC_MV),
                fq=s(SEC_FQ), fk=s(SEC_FK), fv=s(SEC_FV))


def kernel(x_prompt, x_sample, cache_nsa, state_nsa_win, state_hgrn, cache_moba, cache_fox_kv, cache_fox_logf,
           page_table, g_mix, g_ffn, w_in, w_out, nsa_qk_gain, nsa_cmp_pe, nsa_cmp_w, hgrn_lb_logits,
           hgrn_out_gain, moba_qk_gain, fox_qk_gain, fox_f_bias, ffn_w1, ffn_w3, ffn_w2, moe_router,
           moe_w1, moe_w3, moe_w2):
    dt = x_prompt.dtype
    Bp, Tp, D = x_prompt.shape
    Bs, Ts, _ = x_sample.shape
    Np, Ns = Bp * Tp, Bs * Ts
    past_len = page_table.shape[1] * PAGE_SIZE
    lb_w = jax.nn.softmax(hgrn_lb_logits.astype(jnp.float32), axis=0)
    lower_bounds = jnp.cumsum(lb_w, axis=0) - lb_w[0:1]

    def gather_pages(pool):
        g = pool[page_table]
        return g.reshape((Bs, past_len) + pool.shape[2:])

    cos_p, sin_p = rope_tables(jnp.arange(Tp, dtype=jnp.int32))
    cos_s, sin_s = rope_tables(past_len + jnp.arange(Ns, dtype=jnp.int32) % Ts)
    assert Ns == PREP_TQ and Np % PREP_TQ == 0
    gmat = group_mean_matrix()
    cover, expand = nsa_constants(Tp)

    x = jnp.concatenate([x_prompt.reshape(Np, D), x_sample.reshape(Ns, D)], axis=0)
    st_p, st_s = [], []
    for l in range(DEPTH):
        i = l // 2
        z = in_projection(x, g_mix[l][None, :], relayout_w_in(w_in[l]))

        gains = head_gains(nsa_qk_gain[l], moba_qk_gain[l], fox_qk_gain[l])
        pp = prep_prompt(z, 0, Bp, Tp, cos_p, sin_p, gains, gmat)
        pe_flat, w_flat, cgain = nsa_compress_weights(nsa_cmp_pe[l], nsa_cmp_w[l], nsa_qk_gain[l][1])
        kc, vc = nsa_compress(pp["nsa_kc"], Bp, Tp, pe_flat, w_flat, cgain)
        o_nsa_p = nsa_attention_prompt(pp, kc, vc, z, 0, Bp, Tp, cover, expand)
        o_mb_p = moba_attention_prompt(pp, Bp, Tp)
        zp = z_sections(z[:Np].reshape(Bp, Tp, N_INP))
        logf_p = jax.nn.log_sigmoid(zp["ff"] + fox_f_bias[l].astype(F32))
        o_fx_p = fox_attention_prompt(pp, jnp.cumsum(logf_p, axis=1), Bp, Tp)
        hd = lambda a, n, d, B, T: a.reshape(B, T, n, d)
        o_hg_p, hg_state_p = hgrn_mixer(z, 0, Bp, Tp, jnp.zeros((Bp, NH, HGRN_DK, HGRN_DV), F32), lower_bounds[l],
                                        hgrn_out_gain[l], gmat)
        o_p = jnp.concatenate([o_nsa_p.astype(F32), o_hg_p, o_mb_p.astype(F32),
                               o_fx_p.astype(F32)], axis=1)
        nsa_win_p = pp["nsa_win"].reshape(Bp, Tp, 2, HEAD_DIM)[:, -min(NSA_WINDOW, Tp):]
        st_p.append((pp["nsa_rows"].reshape(Bp, Tp, NSA_ROWS, HEAD_DIM), nsa_win_p, hg_state_p.astype(dt),
                     pp["moba_rows"].reshape(Bp, Tp, 2, NH, HEAD_DIM), pp["fox_rows"].reshape(Bp, Tp, 2, NH, HEAD_DIM),
                     logf_p.astype(dt)))

        sp = prep_prompt(z, Np, 1, Ns, cos_s, sin_s, gains, gmat, qdt=F32)
        o_nsa_s = nsa_decode(sp, z, Np, cache_nsa, l, page_table, state_nsa_win[l], nsa_cmp_pe[l], nsa_cmp_w[l],
                             nsa_qk_gain[l][1], Bs, Ts)
        o_hg_s, hg_state = hgrn_mixer(z, Np, Bs, Ts, state_hgrn[l].astype(F32), lower_bounds[l], hgrn_out_gain[l], gmat)
        o_mb_s = moba_decode(sp, cache_moba, l, page_table, sp["moba_rows"], Bs, Ts)
        ff_s = z[Np:, SEC_S2 * HW + MISC_LANE + 12:SEC_S2 * HW + MISC_LANE + 16].reshape(Bs, Ts, NH)
        logf_s = jax.nn.log_sigmoid(ff_s + fox_f_bias[l].astype(F32))
        c_s = jnp.cumsum(jnp.concatenate([gather_pages(cache_fox_logf[l]).astype(F32), logf_s], axis=1), axis=1)
        o_fx_s = fox_decode(sp, cache_fox_kv, l, page_table, c_s, sp["fox_rows"], Bs, Ts)
        o_s = jnp.concatenate([o_nsa_s, o_hg_s, o_mb_s, o_fx_s], axis=1)
        win_rows = jnp.concatenate([state_nsa_win[l], sp["nsa_win"].reshape(Bs, Ts, 2, HEAD_DIM)], axis=1)
        st_s.append((sp["nsa_rows"].reshape(Bs, Ts, NSA_ROWS, HEAD_DIM),
                     win_rows[:, -min(NSA_WINDOW, win_rows.shape[1]):], hg_state.astype(dt),
                     sp["moba_rows"].reshape(Bs, Ts, 2, NH, HEAD_DIM), sp["fox_rows"].reshape(Bs, Ts, 2, NH, HEAD_DIM),
                     logf_s.astype(dt)))

        o = jnp.concatenate([o_p, o_s], axis=0)
        x = x + matmul(o, w_out[l])
        hn = rmsnorm(x, g_ffn[l])
        if l % 2 == 0:
            x = x + swiglu_dense(hn, ffn_w1[i], ffn_w3[i], ffn_w2[i])
        else:
            x = x + moe_ffn_grouped(hn, moe_router[i], moe_w1[i], moe_w3[i], moe_w2[i])

    def stk(states, j):
        return jnp.stack([s[j] for s in states], axis=0)

    return (x[:Np].reshape(Bp, Tp, D), x[Np:].reshape(Bs, Ts, D),
            stk(st_p, 0), stk(st_s, 0), stk(st_p, 1), stk(st_s, 1), stk(st_p, 2), stk(st_s, 2),
            stk(st_p, 3), stk(st_s, 3), stk(st_p, 4), stk(st_s, 4), stk(st_p, 5), stk(st_s, 5))
```

```python
import math, functools
import jax, jax.numpy as jnp
from jax import lax
import numpy as np
from jax.experimental import pallas as pl
from jax.experimental.pallas import tpu as pltpu

D_MODEL = 1024
DEPTH = 2
PAGE_SIZE = 128
HEAD_DIM = 64
H_NSA = 4
H_HGRN = 4
H_MOBA = 4
H_FOX = 4
NH = 4
HW = NH * HEAD_DIM
MIX_WIDTH = (H_NSA + H_HGRN + H_MOBA + H_FOX) * HEAD_DIM
HGRN_DK = 64
HGRN_DV = HEAD_DIM
HGRN_CHUNK = 64
NSA_CMP_BLOCK = 32
NSA_CMP_STRIDE = 16
NSA_SEL_BLOCK = 64
NSA_TOPN = 16
NSA_WINDOW = 512
NSA_ROWS = 4
MOBA_BLOCK = 256
MOBA_TOPK = 3
ROPE_THETA = 10000.0
Q_BLOCK = 128
GATHER_Q_BLOCK = 32
N_EXPERTS = 8
TOP_K = 2
MOE_BLOCK = 128
RMS_EPS = 1e-6
NEG_INF = -1e30
SEL_FORCE = 1e6
ATTN_SCALE = HEAD_DIM ** -0.5
IN_SIZES = (H_NSA * HEAD_DIM, 2 * HEAD_DIM, 2 * HEAD_DIM, 2 * HEAD_DIM, 3 * H_NSA,
            H_HGRN * HGRN_DK, H_HGRN * HGRN_DK, H_HGRN * HGRN_DV, H_HGRN * HGRN_DV,
            H_MOBA * HEAD_DIM, H_MOBA * HEAD_DIM, H_MOBA * HEAD_DIM,
            H_FOX * HEAD_DIM, H_FOX * HEAD_DIM, H_FOX * HEAD_DIM, H_FOX)
N_IN = sum(IN_SIZES)
IN_OFFS = tuple(int(v) for v in np.cumsum((0,) + IN_SIZES))

N_SEC = 13
N_INP = N_SEC * HW
SEC_NQ, SEC_S1, SEC_S2, SEC_HQ, SEC_HF, SEC_HI, SEC_HG = 0, 1, 2, 3, 4, 5, 6
SEC_MQ, SEC_MK, SEC_MV, SEC_FQ, SEC_FK, SEC_FV = 7, 8, 9, 10, 11, 12
MISC_LANE = 128

LANES = 128
VMEM_LIMIT = 48 * 1024 * 1024
PREP_TQ = 256
NSA_TQ = 128
NSA_TK = 512
ATT_SB = 2
M_FLOOR = -1e20
MOE_TM = 256
MOE_TF = 1792
DENSE_TF = 1408
FFN_TN = 512
BF = jnp.bfloat16
F32 = jnp.float32


def _round_up(x, m):
    return -(-x // m) * m


def _cparams(sem):
    return pltpu.CompilerParams(dimension_semantics=sem, vmem_limit_bytes=VMEM_LIMIT)


def _dot(a, b):
    return jnp.dot(a, b, preferred_element_type=F32)


def _dot_nt(a, b):
    return lax.dot_general(a, b, (((1,), (1,)), ((), ())), preferred_element_type=F32)


def _mm_kernel(a_ref, b_ref, o_ref):
    k = pl.program_id(2)
    acc = _dot(a_ref[...].astype(BF), b_ref[...].astype(BF))

    @pl.when(k == 0)
    def _():
        o_ref[...] = acc

    @pl.when(k != 0)
    def _():
        o_ref[...] += acc


def matmul(a, b, tm=512, tn=512, tk=1024):
    M, K = a.shape
    _, N = b.shape
    tm = min(tm, _round_up(M, 8))
    Mp, Np = _round_up(M, tm), _round_up(N, tn)
    if K % tk:
        tk = K
    if Mp != M:
        a = jnp.pad(a, ((0, Mp - M), (0, 0)))
    if Np != N:
        b = jnp.pad(b, ((0, 0), (0, Np - N)))
    out = pl.pallas_call(
        _mm_kernel,
        grid=(Mp // tm, Np // tn, K // tk),
        in_specs=[pl.BlockSpec((tm, tk), lambda i, j, k: (i, k)),
                  pl.BlockSpec((tk, tn), lambda i, j, k: (k, j))],
        out_specs=pl.BlockSpec((tm, tn), lambda i, j, k: (i, j)),
        out_shape=jax.ShapeDtypeStruct((Mp, Np), F32),
        compiler_params=_cparams(("parallel", "parallel", "arbitrary")),
        name="dense_matmul",
    )(a, b)
    return out[:M, :N]


def _inproj_kernel(x_ref, g_ref, w_ref, o_ref):
    x = x_ref[...]
    y = x * lax.rsqrt(jnp.mean(x * x, axis=-1, keepdims=True) + RMS_EPS) * g_ref[...]
    o_ref[...] = _dot(y.astype(BF), w_ref[...])


def in_projection(x, g, w_bf, tm=256):
    N, D = x.shape
    return pl.pallas_call(
        _inproj_kernel,
        grid=(N // tm,),
        in_specs=[pl.BlockSpec((tm, D), lambda i: (i, 0)),
                  pl.BlockSpec((1, D), lambda i: (0, 0)),
                  pl.BlockSpec((D, N_INP), lambda i: (0, 0))],
        out_specs=pl.BlockSpec((tm, N_INP), lambda i: (i, 0)),
        out_shape=jax.ShapeDtypeStruct((N, N_INP), F32),
        compiler_params=_cparams(("parallel",)),
        name="rmsnorm_in_projection",
    )(x, g, w_bf)


def relayout_w_in(w):
    def cols(i):
        return w[:, IN_OFFS[i]:IN_OFFS[i + 1]]
    pad = jnp.zeros((w.shape[0], HW - 2 * HEAD_DIM - IN_SIZES[4] - IN_SIZES[15]), w.dtype)
    parts = [cols(0), cols(1), cols(2), cols(3), cols(4), cols(15), pad] + [cols(i) for i in range(5, 15)]
    return jnp.concatenate(parts, axis=1).astype(BF)


def _head_meansq(x, gmat):
    sq = x * x
    hi = sq.astype(BF)
    lo = (sq - hi.astype(F32)).astype(BF)
    return _dot(hi, gmat) + _dot(lo, gmat)


def _head_rmsnorm(x, gain, gmat):
    return x * lax.rsqrt(_head_meansq(x, gmat) + RMS_EPS) * gain


def _rope(x, cos, sin_signed, lo_half):
    w = x.shape[1]
    swapped = jnp.where(lo_half, pltpu.roll(x, w - HEAD_DIM // 2, 1), pltpu.roll(x, HEAD_DIM // 2, 1))
    return x * cos + swapped * sin_signed


def _store_heads(ref, x):
    for h in range(NH):
        ref[0, h] = x[:, h * HEAD_DIM:(h + 1) * HEAD_DIM].astype(ref.dtype)


def _prep_kernel(nq_ref, s1_ref, s2_ref, mq_ref, mk_ref, mv_ref, fq_ref, fk_ref, fv_ref,
                 cos_ref, sin_ref, gains_ref, gmat_ref,
                 nsa_rows_ref, nsa_kc_ref, nsa_win_ref, moba_rows_ref, fox_rows_ref,
                 nqn_ref, nqr_ref, nks_ref, nvs_ref, nkw_ref, nvw_ref,
                 mqr_ref, mkr_ref, mvv_ref, kmean_ref, fqn_ref, fkn_ref, fvv_ref):
    cos = cos_ref[...]
    sin = sin_ref[...]
    gmat = gmat_ref[...]
    t = cos.shape[0]
    lane = lax.broadcasted_iota(jnp.int32, (t, HW), 1)
    lo_half = (lane % HEAD_DIM) < (HEAD_DIM // 2)
    gains = gains_ref[...]

    qn = _head_rmsnorm(nq_ref[...], gains[0:1], gmat)
    qr = _rope(qn, cos, sin, lo_half)
    _store_heads(nqn_ref, qn * ATTN_SCALE)
    _store_heads(nqr_ref, qr * ATTN_SCALE)

    s1 = s1_ref[...]
    s1r = _rope(_head_rmsnorm(s1, gains[1:2], gmat), cos, sin, lo_half)
    third = (lane >= 2 * HEAD_DIM) & (lane < 3 * HEAD_DIM)
    rows = jnp.where(third, s1r, s1)
    nsa_rows_ref[...] = rows
    nsa_kc_ref[...] = rows[:, :2 * HEAD_DIM]
    nks_ref[0] = rows[:, 2 * HEAD_DIM:3 * HEAD_DIM].astype(nks_ref.dtype)
    nvs_ref[0] = rows[:, 3 * HEAD_DIM:].astype(nvs_ref.dtype)

    s2 = s2_ref[...]
    s2r = _rope(_head_rmsnorm(s2, gains[2:3], gmat), cos, sin, lo_half)
    wrows = jnp.where(lane < HEAD_DIM, s2r, s2)
    nsa_win_ref[...] = wrows[:, :2 * HEAD_DIM]
    nkw_ref[0] = wrows[:, :HEAD_DIM].astype(nkw_ref.dtype)
    nvw_ref[0] = wrows[:, HEAD_DIM:2 * HEAD_DIM].astype(nvw_ref.dtype)

    mq = _rope(_head_rmsnorm(mq_ref[...], gains[3:4], gmat), cos, sin, lo_half)
    mk = _rope(_head_rmsnorm(mk_ref[...], gains[4:5], gmat), cos, sin, lo_half)
    mv = mv_ref[...]
    _store_heads(mqr_ref, mq * ATTN_SCALE)
    _store_heads(mkr_ref, mk)
    _store_heads(mvv_ref, mv)
    moba_rows_ref[:, :HW] = mk
    moba_rows_ref[:, HW:] = mv
    kmean_ref[0, 0] = jnp.mean(mk, axis=0, keepdims=True)

    fq = _head_rmsnorm(fq_ref[...], gains[5:6], gmat)
    fk = _head_rmsnorm(fk_ref[...], gains[6:7], gmat)
    fv = fv_ref[...]
    _store_heads(fqn_ref, fq * ATTN_SCALE)
    _store_heads(fkn_ref, fk)
    _store_heads(fvv_ref, fv)
    fox_rows_ref[:, :HW] = fk
    fox_rows_ref[:, HW:] = fv


def rope_tables(pos):
    half = HEAD_DIM // 2
    inv = ROPE_THETA ** (-jnp.arange(half, dtype=F32) / half)
    ang = pos.astype(F32)[:, None] * inv[None, :]
    cos = jnp.cos(ang)
    sin = jnp.sin(ang)
    cos_h = jnp.concatenate([cos, cos], axis=1)
    sin_h = jnp.concatenate([-sin, sin], axis=1)
    return jnp.tile(cos_h, (1, NH)), jnp.tile(sin_h, (1, NH))


def head_gains(nsa_gain, moba_gain, fox_gain):
    one = jnp.ones((HEAD_DIM,), F32)
    t4 = lambda g: jnp.tile(g, NH)
    rows = [t4(nsa_gain[0]),
            jnp.concatenate([one, one, nsa_gain[2], one]),
            jnp.concatenate([nsa_gain[3], one, one, one]),
            t4(moba_gain[0]), t4(moba_gain[1]), t4(fox_gain[0]), t4(fox_gain[1]), t4(one)]
    return jnp.stack(rows, axis=0)


def group_mean_matrix():
    idx = np.arange(HW) // HEAD_DIM
    return jnp.asarray((idx[:, None] == idx[None, :]).astype(np.float32) / HEAD_DIM, dtype=BF)


def prep_prompt(z, row0, B, T, cos, sin, gains, gmat, qdt=None):
    qdt = BF if qdt is None else qdt
    tq = PREP_TQ
    nq = T // tq
    rb0 = row0 // tq

    def sec(c):
        return pl.BlockSpec((tq, HW), lambda b, i, c=c: (rb0 + b * nq + i, c))

    flat = lambda w: pl.BlockSpec((tq, w), lambda b, i: (b * nq + i, 0))
    headmaj = pl.BlockSpec((1, NH, tq, HEAD_DIM), lambda b, i: (b, 0, i, 0))
    single = pl.BlockSpec((1, tq, HEAD_DIM), lambda b, i: (b, i, 0))
    N = B * T
    sd = jax.ShapeDtypeStruct
    hm_shape = sd((B, NH, T, HEAD_DIM), qdt)
    sg_shape = sd((B, T, HEAD_DIM), qdt)
    outs = pl.pallas_call(
        _prep_kernel,
        grid=(B, nq),
        in_specs=[sec(SEC_NQ), sec(SEC_S1), sec(SEC_S2), sec(SEC_MQ), sec(SEC_MK), sec(SEC_MV),
                  sec(SEC_FQ), sec(SEC_FK), sec(SEC_FV),
                  pl.BlockSpec((tq, HW), lambda b, i: (i, 0)),
                  pl.BlockSpec((tq, HW), lambda b, i: (i, 0)),
                  pl.BlockSpec((8, HW), lambda b, i: (0, 0)),
                  pl.BlockSpec((HW, HW), lambda b, i: (0, 0))],
        out_specs=[flat(HW), flat(2 * HEAD_DIM), flat(2 * HEAD_DIM), flat(2 * HW), flat(2 * HW),
                   headmaj, headmaj, single, single, single, single,
                   headmaj, headmaj, headmaj,
                   pl.BlockSpec((1, 1, 1, HW), lambda b, i: (b, i, 0, 0)),
                   headmaj, headmaj, headmaj],
        out_shape=[sd((N, HW), F32), sd((N, 2 * HEAD_DIM), F32), sd((N, 2 * HEAD_DIM), F32),
                   sd((N, 2 * HW), F32), sd((N, 2 * HW), F32),
                   hm_shape, hm_shape, sg_shape, sg_shape, sg_shape, sg_shape,
                   hm_shape, hm_shape, hm_shape,
                   sd((B, nq, 1, HW), F32),
                   hm_shape, hm_shape, hm_shape],
        compiler_params=_cparams(("parallel", "parallel")),
        name="mixer_prep",
    )(z, z, z, z, z, z, z, z, z, cos, sin, gains, gmat)
    keys = ("nsa_rows", "nsa_kc", "nsa_win", "moba_rows", "fox_rows",
            "nqn", "nqr", "nks", "nvs", "nkw", "nvw", "mqr", "mkr", "mvv", "kmean", "fqn", "fkn", "fvv")
    return dict(zip(keys, outs))


def _nsa_compress_kernel(x_ref, pe_ref, w_ref, gain_ref, k_ref, v_ref):
    x = x_ref[0]
    a = _dot((x + pe_ref[0:1]).astype(BF), w_ref[0])
    b = _dot((x + pe_ref[1:2]).astype(BF), w_ref[1])
    nch = x.shape[0]
    y = a + pltpu.roll(b, nch - 1, 0)
    lane = lax.broadcasted_iota(jnp.int32, y.shape, 1)
    ms = jnp.sum(jnp.where(lane < HEAD_DIM, y * y, 0.0), axis=1, keepdims=True) / HEAD_DIM
    kn = y * lax.rsqrt(ms + RMS_EPS) * gain_ref[...]
    k_ref[0] = kn[:, :HEAD_DIM].astype(BF)
    v_ref[0] = y[:, HEAD_DIM:].astype(BF)


def nsa_compress_weights(cmp_pe, cmp_w, gain1):
    S = NSA_CMP_STRIDE
    pe = cmp_pe.reshape(2, 2, S, HEAD_DIM)
    pe_flat = jnp.transpose(pe, (1, 2, 0, 3)).reshape(2, S * 2 * HEAD_DIM)
    w = cmp_w.reshape(2, 2, S, HEAD_DIM, HEAD_DIM)
    wz = jnp.zeros((2, S, 2, HEAD_DIM, 2, HEAD_DIM), F32)
    wz = wz.at[:, :, 0, :, 0, :].set(w[0]).at[:, :, 1, :, 1, :].set(w[1])
    w_flat = wz.reshape(2, S * 2 * HEAD_DIM, 2 * HEAD_DIM).astype(BF)
    gain = jnp.concatenate([gain1, jnp.ones((HEAD_DIM,), F32)])[None, :]
    return pe_flat, w_flat, gain


def nsa_compress(kc, B, T, pe_flat, w_flat, gain):
    nch = T // NSA_CMP_STRIDE
    cw = NSA_CMP_STRIDE * 2 * HEAD_DIM
    x = kc.reshape(B, nch, cw)
    out_spec = pl.BlockSpec((1, nch, HEAD_DIM), lambda b: (b, 0, 0))
    return pl.pallas_call(
        _nsa_compress_kernel,
        grid=(B,),
        in_specs=[pl.BlockSpec((1, nch, cw), lambda b: (b, 0, 0)),
                  pl.BlockSpec((2, cw), lambda b: (0, 0)),
                  pl.BlockSpec((2, cw, 2 * HEAD_DIM), lambda b: (0, 0, 0)),
                  pl.BlockSpec((1, 2 * HEAD_DIM), lambda b: (0, 0))],
        out_specs=[out_spec, out_spec],
        out_shape=[jax.ShapeDtypeStruct((B, nch, HEAD_DIM), BF)] * 2,
        compiler_params=_cparams(("parallel",)),
        name="nsa_compress",
    )(x, pe_flat, w_flat, gain)


def _softmax_update(s, mask, m_i, l_i, acc, v, lead, v_feature_major=False):
    s = jnp.where(mask, s, NEG_INF)
    m_new = jnp.maximum(m_i, jnp.max(s, axis=-1, keepdims=True))
    p = jnp.where(mask, jnp.exp(s - m_new), 0.0)
    alpha = jnp.exp(m_i - m_new)
    l_new = alpha * l_i + jnp.sum(p, axis=-1, keepdims=True)
    pb = p.astype(BF).reshape(lead, p.shape[-1])
    pv = (_dot_nt(pb, v) if v_feature_major else _dot(pb, v)).reshape(acc.shape)
    return m_new, l_new, alpha * acc + pv


def _nsa_attn_kernel(qn_ref, qr_ref, kc_ref, vc_ref, ks_ref, vs_ref, kw_ref, vw_ref, gl_ref,
                     cover_ref, expand_ref, o_ref, m_scr, l_scr, acc_scr, *, T):
    tq, tk = NSA_TQ, NSA_TK
    M = NH * tq
    i = pl.program_id(1)
    p0 = i * tq
    qn = qn_ref[0].reshape(M, HEAD_DIM)
    qr = qr_ref[0].reshape(M, HEAD_DIM)
    pos = p0 + lax.broadcasted_iota(jnp.int32, (tq, 1), 0)

    nch = kc_ref.shape[1]
    s_c = _dot_nt(qn, kc_ref[0]).reshape(NH, tq, nch)
    n_idx = lax.broadcasted_iota(jnp.int32, (tq, nch), 1)
    mask_c = (n_idx * NSA_CMP_STRIDE + (NSA_CMP_BLOCK - 1) <= pos)[None]
    s_c = jnp.where(mask_c, s_c, NEG_INF)
    m_c = jnp.max(s_c, axis=-1, keepdims=True)
    p_c = jnp.where(mask_c, jnp.exp(s_c - m_c), 0.0)
    p_c = p_c / jnp.maximum(jnp.sum(p_c, axis=-1, keepdims=True), 1.0)
    p_cb = p_c.astype(BF).reshape(M, nch)
    o_cmp = _dot(p_cb, vc_ref[0]).reshape(NH, tq, HEAD_DIM)
    imp = jnp.sum(_dot(p_cb, cover_ref[...]).reshape(NH, tq, LANES), axis=0)

    j = lax.broadcasted_iota(jnp.int32, (tq, LANES), 1)
    cur = pos // NSA_SEL_BLOCK
    forced = (j == 0) | (j == cur) | (j == cur - 1)
    imp = jnp.where(j > cur, NEG_INF, jnp.where(forced, SEL_FORCE, imp))
    n_sel = T // NSA_SEL_BLOCK
    rank = jnp.zeros((tq, LANES), F32)
    for jp in range(n_sel):
        col = imp[:, jp:jp + 1]
        beats = (col > imp) | ((col == imp) & (j > jp))
        rank = rank + jnp.where(beats, 1.0, 0.0)
    sel = jnp.where((rank < NSA_TOPN) & (j <= cur), 1.0, 0.0).astype(BF)

    _flash_init(m_scr, l_scr, acc_scr)
    kcol = lax.broadcasted_iota(jnp.int32, (tq, tk), 1)
    for kj in range(T // tk):
        @pl.when(kj * tk <= p0 + (tq - 1))
        def _(kj=kj):
            k = ks_ref[0, kj * tk:(kj + 1) * tk, :]
            v = vs_ref[0, kj * tk:(kj + 1) * tk, :]
            e = _dot(sel, expand_ref[kj])
            mask = ((e > 0.5) & (kcol + kj * tk <= pos))[None]
            s = jnp.where(mask, _dot_nt(qr, k).reshape(NH, tq, tk), NEG_INF)
            _flash_step(s, v, m_scr, l_scr, acc_scr)
    o_sel = acc_scr[...] / jnp.maximum(l_scr[...], 1.0)

    band = NSA_WINDOW + tq
    start = pl.multiple_of(jnp.maximum(p0 - NSA_WINDOW, 0), tq)
    kw = kw_ref[0, pl.ds(start, band), :]
    vw = vw_ref[0, pl.ds(start, band), :]
    s_w = _dot_nt(qr, kw).reshape(NH, tq, band)
    kpos = start + lax.broadcasted_iota(jnp.int32, (tq, band), 1)
    mask_w = ((kpos <= pos) & (kpos > pos - NSA_WINDOW))[None]
    s_w = jnp.where(mask_w, s_w, NEG_INF)
    m_w = jnp.max(s_w, axis=-1, keepdims=True)
    p_w = jnp.where(mask_w, jnp.exp(s_w - m_w), 0.0)
    l_w = jnp.sum(p_w, axis=-1, keepdims=True)
    o_win = _dot(p_w.astype(BF).reshape(M, band), vw).reshape(NH, tq, HEAD_DIM) / jnp.maximum(l_w, 1.0)

    sig = jax.nn.sigmoid(gl_ref[...])
    outs = []
    for h in range(NH):
        c = MISC_LANE + 3 * h
        outs.append(sig[:, c:c + 1] * o_cmp[h] + sig[:, c + 1:c + 2] * o_sel[h] + sig[:, c + 2:c + 3] * o_win[h])
    o_ref[...] = jnp.concatenate(outs, axis=1).astype(o_ref.dtype)


def nsa_constants(T):
    nch = T // NSA_CMP_STRIDE
    n_cmp = (T - NSA_CMP_BLOCK) // NSA_CMP_STRIDE + 1
    ci = np.arange(nch)[:, None] * NSA_CMP_STRIDE
    sj = np.arange(LANES)[None, :] * NSA_SEL_BLOCK
    cover = (ci < sj + NSA_SEL_BLOCK) & (ci + NSA_CMP_BLOCK > sj) & (np.arange(nch)[:, None] < n_cmp)
    nkt = T // NSA_TK
    blk = (np.arange(nkt)[:, None, None] * NSA_TK + np.arange(NSA_TK)[None, None, :]) // NSA_SEL_BLOCK
    expand = blk == np.arange(LANES)[None, :, None]
    return jnp.asarray(cover.astype(np.float32), dtype=BF), jnp.asarray(expand.astype(np.float32), dtype=BF)


def nsa_attention_prompt(pp, kc, vc, z, row0, B, T, cover, expand):
    tq = NSA_TQ
    nq = T // tq
    rb0 = row0 // tq
    nch = T // NSA_CMP_STRIDE
    assert T >= NSA_WINDOW + tq and T % NSA_TK == 0
    headmaj = pl.BlockSpec((1, NH, tq, HEAD_DIM), lambda b, i: (b, 0, i, 0))
    full1 = pl.BlockSpec((1, T, HEAD_DIM), lambda b, i: (b, 0, 0))
    cmp1 = pl.BlockSpec((1, nch, HEAD_DIM), lambda b, i: (b, 0, 0))
    return pl.pallas_call(
        functools.partial(_nsa_attn_kernel, T=T),
        grid=(B, nq),
        in_specs=[headmaj, headmaj, cmp1, cmp1, full1, full1, full1, full1,
                  pl.BlockSpec((tq, HW), lambda b, i: (rb0 + b * nq + i, SEC_S2)),
                  pl.BlockSpec((nch, LANES), lambda b, i: (0, 0)),
                  pl.BlockSpec((T // NSA_TK, LANES, NSA_TK), lambda b, i: (0, 0, 0))],
        out_specs=pl.BlockSpec((tq, HW), lambda b, i: (b * nq + i, 0)),
        out_shape=jax.ShapeDtypeStruct((B * T, HW), BF),
        scratch_shapes=[pltpu.VMEM((NH, tq, 1), F32), pltpu.VMEM((NH, tq, 1), F32),
                        pltpu.VMEM((NH, tq, HEAD_DIM), F32)],
        compiler_params=_cparams(("parallel", "parallel")),
        name="nsa_attention",
    )(pp["nqn"], pp["nqr"], kc, vc, pp["nks"], pp["nvs"], pp["nkw"], pp["nvw"], z, cover, expand)


def _flash_init(m_scr, l_scr, acc_scr):
    m_scr[...] = jnp.full_like(m_scr, M_FLOOR)
    l_scr[...] = jnp.zeros_like(l_scr)
    acc_scr[...] = jnp.zeros_like(acc_scr)


def _flash_step(s, v, m_scr, l_scr, acc_scr):
    m_i = m_scr[...]
    m_new = jnp.maximum(m_i, jnp.max(s, axis=-1, keepdims=True))
    p = jnp.exp(s - m_new)
    alpha = jnp.exp(m_i - m_new)
    l_scr[...] = alpha * l_scr[...] + jnp.sum(p, axis=-1, keepdims=True)
    pv = _dot(p.astype(BF).reshape(-1, p.shape[-1]), v)
    acc_scr[...] = alpha * acc_scr[...] + pv.reshape(acc_scr.shape)
    m_scr[...] = m_new


def _moba_attn_kernel(q_ref, k_ref, v_ref, km_ref, o_ref, m_scr, l_scr, acc_scr):
    tq = MOBA_BLOCK
    tks = ATT_SB * MOBA_BLOCK
    n_sb = k_ref.shape[2] // tks
    qi = pl.program_id(1)
    lane = lax.broadcasted_iota(jnp.int32, (tq, LANES), 1)
    qrow = lax.broadcasted_iota(jnp.int32, (tq, tks), 0)
    kcol = lax.broadcasted_iota(jnp.int32, (tq, tks), 1)
    sels = []
    for h in range(NH):
        g = jnp.where(lane < qi, _dot_nt(q_ref[0, h], km_ref[0, h]), NEG_INF)
        sel = jnp.where(lane == qi, 1.0, 0.0)
        for _ in range(MOBA_TOPK):
            m = jnp.max(g, axis=1, keepdims=True)
            idx = jnp.min(jnp.where(g == m, lane, LANES), axis=1, keepdims=True)
            hit = lane == idx
            sel = jnp.where(hit & (lane < qi), 1.0, sel)
            g = jnp.where(hit, NEG_INF, g)
        sels.append(sel)
    _flash_init(m_scr, l_scr, acc_scr)
    for sb in range(n_sb):
        @pl.when(sb * ATT_SB <= qi)
        def _(sb=sb):
            causal = kcol + (sb * tks) <= qrow + qi * tq
            for h in range(NH):
                k = k_ref[0, h, sb * tks:(sb + 1) * tks, :]
                v = v_ref[0, h, sb * tks:(sb + 1) * tks, :]
                picked = jnp.concatenate(
                    [jnp.broadcast_to(sels[h][:, j:j + 1] > 0.5, (tq, MOBA_BLOCK))
                     for j in range(sb * ATT_SB, (sb + 1) * ATT_SB)], axis=1)
                s = jnp.where(picked & causal, _dot_nt(q_ref[0, h], k), NEG_INF)
                _flash_step(s, v, m_scr.at[h], l_scr.at[h], acc_scr.at[h])
    o = acc_scr[...] / jnp.maximum(l_scr[...], 1.0)
    o_ref[...] = jnp.concatenate([o[h] for h in range(NH)], axis=1).astype(o_ref.dtype)


def moba_attention_prompt(pp, B, T):
    tq = MOBA_BLOCK
    nq = T // tq
    km = pp["kmean"].reshape(B, nq, NH, HEAD_DIM).transpose(0, 2, 1, 3)
    km = jnp.pad(km, ((0, 0), (0, 0), (0, LANES - nq), (0, 0))).astype(BF)
    headq = pl.BlockSpec((1, NH, tq, HEAD_DIM), lambda b, i: (b, 0, i, 0))
    headfull = pl.BlockSpec((1, NH, T, HEAD_DIM), lambda b, i: (b, 0, 0, 0))
    return pl.pallas_call(
        _moba_attn_kernel,
        grid=(B, nq),
        in_specs=[headq, headfull, headfull,
                  pl.BlockSpec((1, NH, LANES, HEAD_DIM), lambda b, i: (b, 0, 0, 0))],
        out_specs=pl.BlockSpec((tq, HW), lambda b, i: (b * nq + i, 0)),
        out_shape=jax.ShapeDtypeStruct((B * T, HW), BF),
        scratch_shapes=[pltpu.VMEM((NH, tq, 1), F32), pltpu.VMEM((NH, tq, 1), F32),
                        pltpu.VMEM((NH, tq, HEAD_DIM), F32)],
        compiler_params=_cparams(("parallel", "parallel")),
        name="moba_attention",
    )(pp["mqr"], pp["mkr"], pp["mvv"], km)


def _fox_attn_kernel(q_ref, k_ref, v_ref, cq_ref, ck_ref, o_ref, m_scr, l_scr, acc_scr):
    tq = MOBA_BLOCK
    tks = ATT_SB * MOBA_BLOCK
    n_sb = k_ref.shape[2] // tks
    qi = pl.program_id(1)
    qrow = lax.broadcasted_iota(jnp.int32, (tq, tks), 0)
    kcol = lax.broadcasted_iota(jnp.int32, (tq, tks), 1)
    _flash_init(m_scr, l_scr, acc_scr)
    for sb in range(n_sb):
        last = (sb + 1) * ATT_SB - 1

        def region(causal, sb=sb):
            for h in range(NH):
                k = k_ref[0, h, sb * tks:(sb + 1) * tks, :]
                v = v_ref[0, h, sb * tks:(sb + 1) * tks, :]
                s = _dot_nt(q_ref[0, h], k) + cq_ref[0, h] - ck_ref[0, h, sb]
                if causal:
                    s = jnp.where(kcol + (sb * tks) <= qrow + qi * tq, s, NEG_INF)
                _flash_step(s, v, m_scr.at[h], l_scr.at[h], acc_scr.at[h])

        pl.when(last < qi)(functools.partial(region, False))
        pl.when((sb * ATT_SB <= qi) & (last >= qi))(functools.partial(region, True))
    o = acc_scr[...] / jnp.maximum(l_scr[...], 1.0)
    o_ref[...] = jnp.concatenate([o[h] for h in range(NH)], axis=1).astype(o_ref.dtype)


def fox_attention_prompt(pp, c, B, T):
    tq = MOBA_BLOCK
    nq = T // tq
    ch = jnp.transpose(c, (0, 2, 1))
    cq = ch[..., None]
    tks = ATT_SB * MOBA_BLOCK
    ck = ch.reshape(B, NH, T // tks, 1, tks)
    headq = pl.BlockSpec((1, NH, tq, HEAD_DIM), lambda b, i: (b, 0, i, 0))
    headfull = pl.BlockSpec((1, NH, T, HEAD_DIM), lambda b, i: (b, 0, 0, 0))
    return pl.pallas_call(
        _fox_attn_kernel,
        grid=(B, nq),
        in_specs=[headq, headfull, headfull,
                  pl.BlockSpec((1, NH, tq, 1), lambda b, i: (b, 0, i, 0)),
                  pl.BlockSpec((1, NH, T // tks, 1, tks), lambda b, i: (b, 0, 0, 0, 0))],
        out_specs=pl.BlockSpec((tq, HW), lambda b, i: (b * nq + i, 0)),
        out_shape=jax.ShapeDtypeStruct((B * T, HW), BF),
        scratch_shapes=[pltpu.VMEM((NH, tq, 1), F32), pltpu.VMEM((NH, tq, 1), F32),
                        pltpu.VMEM((NH, tq, HEAD_DIM), F32)],
        compiler_params=_cparams(("parallel", "parallel")),
        name="fox_attention",
    )(pp["fqn"], pp["fkn"], pp["fvv"], cq, ck)


def _hgrn_kernel(q_ref, f_ref, i_ref, g_ref, s0_ref, lb_ref, gain_ref, bd_ref, gmat_ref,
                 o_ref, sout_ref, st_scr, *, C):
    c = pl.program_id(1)

    @pl.when(c == 0)
    def _():
        st_scr[...] = jnp.zeros_like(st_scr)
        for h in range(NH):
            st_scr[h * HEAD_DIM:(h + 1) * HEAD_DIM, h * HEAD_DIM:(h + 1) * HEAD_DIM] = s0_ref[0, h]

    lb = lb_ref[...]
    z = f_ref[...]
    logf = jnp.log(lb + (1.0 - lb) * jax.nn.sigmoid(z))
    kk = (1.0 - lb) * jax.nn.sigmoid(-z)
    q = q_ref[...]
    qf = q * jax.nn.sigmoid(q)
    v = i_ref[...]
    row = lax.broadcasted_iota(jnp.int32, (C, HW), 0)
    b = logf
    sh = 1
    while sh < C:
        b = b + jnp.where(row >= sh, pltpu.roll(b, sh, 0), 0.0)
        sh *= 2
    bd = bd_ref[...]
    st = st_scr[...]
    o_ref[...] = _dot_nt((qf * jnp.exp(b)).astype(BF), st.astype(BF))
    for s in range(C):
        causal = row >= s
        e = jnp.exp(jnp.where(causal, b - b[s:s + 1], 0.0))
        fz = jnp.where(causal, qf * kk[s:s + 1] * e, 0.0)
        o_ref[...] += _dot(fz.astype(BF), bd) * v[s:s + 1]
    o = o_ref[...]
    b_last = b[C - 1:C, :]
    kt = kk * jnp.exp(b_last - b)
    upd = lax.dot_general(v.astype(BF), kt.astype(BF), (((0,), (0,)), ((), ())), preferred_element_type=F32)
    r2 = lax.broadcasted_iota(jnp.int32, (HW, HW), 0) // HEAD_DIM
    c2 = lax.broadcasted_iota(jnp.int32, (HW, HW), 1) // HEAD_DIM
    st_new = st * jnp.exp(b_last) + jnp.where(r2 == c2, upd, 0.0)
    st_scr[...] = st_new
    g = g_ref[...]
    o_ref[...] = _head_rmsnorm(o, gain_ref[...], gmat_ref[...]) * (g * jax.nn.sigmoid(g))

    @pl.when(c == pl.num_programs(1) - 1)
    def _():
        for h in range(NH):
            sout_ref[0, h] = st_new[h * HEAD_DIM:(h + 1) * HEAD_DIM, h * HEAD_DIM:(h + 1) * HEAD_DIM]


def hgrn_mixer(z, row0, B, T, s0, lb, out_gain, gmat):
    C = math.gcd(T, HGRN_CHUNK)
    nc = T // C
    rb0 = row0 // C
    idx = np.arange(HW) // HEAD_DIM
    bd = jnp.asarray((idx[:, None] == idx[None, :]).astype(np.float32), dtype=BF)

    def sec(cidx):
        return pl.BlockSpec((C, HW), lambda b, c, cidx=cidx: (rb0 + b * nc + c, cidx))

    state = pl.BlockSpec((1, NH, HGRN_DV, HGRN_DK), lambda b, c: (b, 0, 0, 0))
    vec = pl.BlockSpec((1, HW), lambda b, c: (0, 0))
    mat = pl.BlockSpec((HW, HW), lambda b, c: (0, 0))
    o, st = pl.pallas_call(
        functools.partial(_hgrn_kernel, C=C),
        grid=(B, nc),
        in_specs=[sec(SEC_HQ), sec(SEC_HF), sec(SEC_HI), sec(SEC_HG), state, vec, vec, mat, mat],
        out_specs=[pl.BlockSpec((C, HW), lambda b, c: (b * nc + c, 0)), state],
        out_shape=[jax.ShapeDtypeStruct((B * T, HW), F32),
                   jax.ShapeDtypeStruct((B, NH, HGRN_DV, HGRN_DK), F32)],
        scratch_shapes=[pltpu.VMEM((HW, HW), F32)],
        compiler_params=_cparams(("parallel", "arbitrary")),
        name="hgrn_recurrence",
    )(z, z, z, z, jnp.swapaxes(s0, 2, 3), lb[None, :], jnp.tile(out_gain, NH)[None, :], bd, gmat)
    return o, jnp.swapaxes(st, 2, 3)


DEC_PG = 8


def _page_specs(tail, PG, base):
    return [pl.BlockSpec((1,) + tail, lambda b, s, pt, j=j: (base + pt[b, s * PG + j], 0, 0)) for j in range(PG)]


def _pool_view(pools, l, width):
    pages = pools.reshape(pools.shape[0] * pools.shape[1], PAGE_SIZE, width)
    return jnp.swapaxes(pages, 1, 2), l * pools.shape[1]


def _rows_nat(q_ref):
    return jnp.concatenate([q_ref[0, h] for h in range(NH)], axis=1)


def _block_diag_q(qnat, T):
    q4 = jnp.concatenate([qnat] * NH, axis=0)
    r = lax.broadcasted_iota(jnp.int32, q4.shape, 0) // T
    c = lax.broadcasted_iota(jnp.int32, q4.shape, 1) // HEAD_DIM
    return jnp.where(r == c, q4, 0.0)


def _diag_heads(x, T):
    return jnp.concatenate([x[h * T:(h + 1) * T, h * HEAD_DIM:(h + 1) * HEAD_DIM] for h in range(NH)], axis=1)


def _new_causal(R, T):
    tq = lax.broadcasted_iota(jnp.int32, (R, T), 0) % T
    tk = lax.broadcasted_iota(jnp.int32, (R, T), 1)
    return tk <= tq


def _fox_decode_kernel(pt_ref, q_ref, new_ref, cq_ref, cn_ref, ck_ref, *rest, PG, T):
    page_refs, o_ref = rest[:PG], rest[PG]
    m_scr, l_scr, acc_scr = rest[PG + 1:]
    s = pl.program_id(1)
    R = NH * T

    @pl.when(s == 0)
    def _():
        m_scr[...] = jnp.full_like(m_scr, NEG_INF)
        l_scr[...] = jnp.zeros_like(l_scr)
        acc_scr[...] = jnp.zeros_like(acc_scr)

    qbd = _block_diag_q(_rows_nat(q_ref), T).astype(BF)
    cq = cq_ref[0]
    m_i, l_i, acc = m_scr[...], l_scr[...], acc_scr[...]
    for j in range(PG):
        page = page_refs[j][0]
        kt = page[:HW].astype(BF)
        vt = page[HW:].astype(BF)
        ck = jnp.broadcast_to(ck_ref[0, j][:, None, :], (NH, T, PAGE_SIZE)).reshape(R, PAGE_SIZE)
        sc = _dot(qbd, kt) + cq - ck
        m_new = jnp.maximum(m_i, jnp.max(sc, axis=-1, keepdims=True))
        p = jnp.exp(sc - m_new)
        alpha = jnp.exp(m_i - m_new)
        l_i = alpha * l_i + jnp.sum(p, axis=-1, keepdims=True)
        acc = alpha * acc + _dot_nt(p.astype(BF), vt)
        m_i = m_new
    m_scr[...] = m_i
    l_scr[...] = l_i
    acc_scr[...] = acc

    @pl.when(s == pl.num_programs(1) - 1)
    def _():
        new = new_ref[...]
        sc = _dot_nt(qbd, new[:, :HW].astype(BF)) + cq - cn_ref[0]
        _, l_f, acc_f = _softmax_update(sc, _new_causal(R, T), m_i, l_i, acc, new[:, HW:].astype(BF), R)
        o_ref[...] = _diag_heads(acc_f / jnp.maximum(l_f, 1.0), T)


def fox_decode(sp, pools, l, page_table, c, row_new, B, T):
    n_pages = page_table.shape[1]
    PG = min(DEC_PG, n_pages)
    past = n_pages * PAGE_SIZE
    R = NH * T
    view, base = _pool_view(pools, l, 2 * HW)
    ch = jnp.transpose(c, (0, 2, 1))
    ck = ch[:, :, :past].reshape(B, NH, n_pages, PAGE_SIZE).transpose(0, 2, 1, 3)
    cnew = ch[:, :, past:]
    cq = cnew.reshape(B, R, 1)
    cn = jnp.broadcast_to(cnew[:, :, None, :], (B, NH, T, T)).reshape(B, R, T)
    return pl.pallas_call(
        functools.partial(_fox_decode_kernel, PG=PG, T=T),
        grid_spec=pltpu.PrefetchScalarGridSpec(
            num_scalar_prefetch=1, grid=(B, n_pages // PG),
            in_specs=[pl.BlockSpec((1, NH, T, HEAD_DIM), lambda b, s, pt: (0, 0, b, 0)),
                      pl.BlockSpec((T, 2 * HW), lambda b, s, pt: (b, 0)),
                      pl.BlockSpec((1, R, 1), lambda b, s, pt: (b, 0, 0)),
                      pl.BlockSpec((1, R, T), lambda b, s, pt: (b, 0, 0)),
                      pl.BlockSpec((1, PG, NH, PAGE_SIZE), lambda b, s, pt: (b, s, 0, 0))]
            + _page_specs((2 * HW, PAGE_SIZE), PG, base),
            out_specs=pl.BlockSpec((T, HW), lambda b, s, pt: (b, 0)),
            scratch_shapes=[pltpu.VMEM((R, 1), F32), pltpu.VMEM((R, 1), F32), pltpu.VMEM((R, HW), F32)]),
        out_shape=jax.ShapeDtypeStruct((B * T, HW), F32),
        compiler_params=_cparams(("parallel", "arbitrary")),
        name="fox_decode",
    )(page_table, sp["fqn"], row_new, cq, cn, ck, *([view] * PG))


def _moba_decode_kernel(pt_ref, q_ref, new_ref, *rest, PG, T, n_pages):
    page_refs, o_ref = rest[:PG], rest[PG]
    m_scr, l_scr, rs_scr, acc_scr = rest[PG + 1:]
    s = pl.program_id(1)
    R = NH * T
    lane = lax.broadcasted_iota(jnp.int32, (R, LANES), 1)

    @pl.when(s == 0)
    def _():
        m_scr[...] = jnp.full_like(m_scr, NEG_INF)
        l_scr[...] = jnp.zeros_like(l_scr)
        rs_scr[...] = jnp.zeros_like(rs_scr)

    qbd = _block_diag_q(_rows_nat(q_ref), T).astype(BF)
    mm, ll, rs = m_scr[...], l_scr[...], rs_scr[...]
    for j in range(PG):
        pid = s * PG + j
        page = page_refs[j][0]
        sc = _dot(qbd, page[:HW].astype(BF))
        m = jnp.max(sc, axis=-1, keepdims=True)
        p = jnp.exp(sc - m)
        acc_scr[pid] = _dot_nt(p.astype(BF), page[HW:].astype(BF))
        hit = lane == pid
        mm = jnp.where(hit, m, mm)
        ll = jnp.where(hit, jnp.sum(p, axis=-1, keepdims=True), ll)
        rs = jnp.where(hit, jnp.sum(sc, axis=-1, keepdims=True), rs)
    m_scr[...] = mm
    l_scr[...] = ll
    rs_scr[...] = rs

    @pl.when(s == pl.num_programs(1) - 1)
    def _():
        block_lane = (lane % 2 == 0) & (lane < n_pages)
        g = jnp.where(block_lane, (rs + pltpu.roll(rs, LANES - 1, 1)) * (1.0 / MOBA_BLOCK), NEG_INF)
        sel = jnp.zeros((R, LANES), F32)
        for _ in range(MOBA_TOPK):
            mx = jnp.max(g, axis=1, keepdims=True)
            idx = jnp.min(jnp.where(g == mx, lane, LANES), axis=1, keepdims=True)
            hit = lane == idx
            sel = jnp.where(hit & block_lane, 1.0, sel)
            g = jnp.where(hit, NEG_INF, g)
        picked = (sel + pltpu.roll(sel, 1, 1)) > 0.5
        new = new_ref[...]
        mask_n = _new_causal(R, T)
        sn = jnp.where(mask_n, _dot_nt(qbd, new[:, :HW].astype(BF)), NEG_INF)
        m_n = jnp.max(sn, axis=-1, keepdims=True)
        p_n = jnp.where(mask_n, jnp.exp(sn - m_n), 0.0)
        m_tot = jnp.maximum(jnp.max(jnp.where(picked, mm, NEG_INF), axis=1, keepdims=True), m_n)
        w = jnp.where(picked, jnp.exp(mm - m_tot), 0.0)
        w_n = jnp.exp(m_n - m_tot)
        l_tot = jnp.sum(w * ll, axis=1, keepdims=True) + w_n * jnp.sum(p_n, axis=-1, keepdims=True)
        acc = w_n * _dot(p_n.astype(BF), new[:, HW:].astype(BF))
        for pg in range(n_pages):
            acc = acc + w[:, pg:pg + 1] * acc_scr[pg]
        o_ref[...] = _diag_heads(acc / jnp.maximum(l_tot, 1.0), T)


def moba_decode(sp, pools, l, page_table, row_new, B, T):
    n_pages = page_table.shape[1]
    assert MOBA_BLOCK == 2 * PAGE_SIZE and n_pages % 2 == 0 and n_pages <= LANES and T <= MOBA_BLOCK
    PG = min(DEC_PG, n_pages)
    R = NH * T
    view, base = _pool_view(pools, l, 2 * HW)
    return pl.pallas_call(
        functools.partial(_moba_decode_kernel, PG=PG, T=T, n_pages=n_pages),
        grid_spec=pltpu.PrefetchScalarGridSpec(
            num_scalar_prefetch=1, grid=(B, n_pages // PG),
            in_specs=[pl.BlockSpec((1, NH, T, HEAD_DIM), lambda b, s, pt: (0, 0, b, 0)),
                      pl.BlockSpec((T, 2 * HW), lambda b, s, pt: (b, 0))]
            + _page_specs((2 * HW, PAGE_SIZE), PG, base),
            out_specs=pl.BlockSpec((T, HW), lambda b, s, pt: (b, 0)),
            scratch_shapes=[pltpu.VMEM((R, LANES), F32), pltpu.VMEM((R, LANES), F32), pltpu.VMEM((R, LANES), F32),
                            pltpu.VMEM((n_pages, R, HW), F32)]),
        out_shape=jax.ShapeDtypeStruct((B * T, HW), F32),
        compiler_params=_cparams(("parallel", "arbitrary")),
        name="moba_decode",
    )(page_table, sp["mqr"], row_new, *([view] * PG))


def _nsa_cmp_phys_kernel(xt_ref, pe_ref, w_ref, y_ref, xs_scr, *, G):
    S = NSA_CMP_STRIDE
    for g in range(G):
        xs_scr[g * PAGE_SIZE:(g + 1) * PAGE_SIZE, :] = xt_ref[g].T
    m = G * (PAGE_SIZE // S)
    a = jnp.zeros((m, 2 * HEAD_DIM), F32)
    b = jnp.zeros((m, 2 * HEAD_DIM), F32)
    for t in range(S):
        x = xs_scr[pl.ds(t, m, stride=S), :]
        a = a + _dot((x + pe_ref[0, t:t + 1]).astype(BF), w_ref[0, t])
        b = b + _dot((x + pe_ref[1, t:t + 1]).astype(BF), w_ref[1, t])
    y_ref[...] = jnp.concatenate([a, b], axis=1)


def nsa_compress_pool(pools, l, cmp_pe, cmp_w):
    S = NSA_CMP_STRIDE
    n_phys = pools.shape[1]
    cpp = PAGE_SIZE // S
    pe = jnp.transpose(cmp_pe.reshape(2, 2, S, HEAD_DIM), (1, 2, 0, 3)).reshape(2, S, 2 * HEAD_DIM)
    w = cmp_w.reshape(2, 2, S, HEAD_DIM, HEAD_DIM)
    wz = jnp.zeros((2, S, 2, HEAD_DIM, 2, HEAD_DIM), F32)
    wz = wz.at[:, :, 0, :, 0, :].set(w[0]).at[:, :, 1, :, 1, :].set(w[1])
    wz = wz.reshape(2, S, 2 * HEAD_DIM, 2 * HEAD_DIM).astype(BF)
    G = next(g for g in (32, 16, 8, 4, 2, 1) if n_phys % g == 0)
    view, base = _pool_view(pools, l, HW)
    return pl.pallas_call(
        functools.partial(_nsa_cmp_phys_kernel, G=G),
        grid=(n_phys // G,),
        in_specs=[pl.BlockSpec((G, 2 * HEAD_DIM, PAGE_SIZE), lambda i: (base // G + i, 0, 0)),
                  pl.BlockSpec((2, S, 2 * HEAD_DIM), lambda i: (0, 0, 0)),
                  pl.BlockSpec((2, S, 2 * HEAD_DIM, 2 * HEAD_DIM), lambda i: (0, 0, 0, 0))],
        out_specs=pl.BlockSpec((G * cpp, HW), lambda i: (i, 0)),
        out_shape=jax.ShapeDtypeStruct((n_phys * cpp, HW), F32),
        scratch_shapes=[pltpu.VMEM((G * PAGE_SIZE, 2 * HEAD_DIM), F32)],
        compiler_params=_cparams(("parallel",)),
        name="nsa_compress_pool",
    )(view, pe, wz)


def _nsa_sel_decode_kernel(y_ref, q_ref, gain_ref, cover_ref, ocmp_ref, sel_ref, *, T, offset, n_cmp, n_sel):
    R = NH * T
    y = y_ref[0]
    nch = y.shape[0]
    c = y[:, :2 * HEAD_DIM] + pltpu.roll(y[:, 2 * HEAD_DIM:], nch - 1, 0)
    lane = lax.broadcasted_iota(jnp.int32, c.shape, 1)
    ms = jnp.sum(jnp.where(lane < HEAD_DIM, c * c, 0.0), axis=1, keepdims=True) / HEAD_DIM
    kn = c * lax.rsqrt(ms + RMS_EPS) * gain_ref[...]
    kc = kn[:, :HEAD_DIM].astype(BF)
    vc = c[:, HEAD_DIM:].astype(BF)
    q = q_ref[0].reshape(R, HEAD_DIM).astype(BF)
    pos = offset + lax.broadcasted_iota(jnp.int32, (R, 1), 0) % T
    n_idx = lax.broadcasted_iota(jnp.int32, (R, nch), 1)
    mask = (n_idx * NSA_CMP_STRIDE + (NSA_CMP_BLOCK - 1) <= pos) & (n_idx < n_cmp)
    s_c = jnp.where(mask, _dot_nt(q, kc), NEG_INF)
    m_c = jnp.max(s_c, axis=-1, keepdims=True)
    p_c = jnp.where(mask, jnp.exp(s_c - m_c), 0.0)
    p_c = (p_c / jnp.maximum(jnp.sum(p_c, axis=-1, keepdims=True), 1.0)).astype(BF)
    ocmp_ref[0] = _dot(p_c, vc)
    imp4 = _dot(p_c, cover_ref[...])
    imp = imp4[0:T]
    for h in range(1, NH):
        imp = imp + imp4[h * T:(h + 1) * T]
    j = lax.broadcasted_iota(jnp.int32, imp.shape, 1)
    cur = (offset + lax.broadcasted_iota(jnp.int32, (T, 1), 0)) // NSA_SEL_BLOCK
    forced = (j == 0) | (j == cur) | (j == cur - 1)
    imp = jnp.where(j > cur, NEG_INF, jnp.where(forced, SEL_FORCE, imp))
    rank = jnp.zeros(imp.shape, F32)
    for jp in range(n_sel):
        col = imp[:, jp:jp + 1]
        rank = rank + jnp.where((col > imp) | ((col == imp) & (j > jp)), 1.0, 0.0)
    sel_ref[0] = jnp.where((rank < NSA_TOPN) & (j <= cur), 1.0, 0.0)


def _nsa_decode_kernel(pt_ref, q_ref, sel_ref, ocmp_ref, newr_ref, win_ref, neww_ref, gl_ref, *rest,
                       PG, T, offset):
    page_refs, o_ref = rest[:PG], rest[PG]
    m_scr, l_scr, acc_scr = rest[PG + 1:]
    s = pl.program_id(1)
    R = NH * T
    D = HEAD_DIM

    @pl.when(s == 0)
    def _():
        m_scr[...] = jnp.full_like(m_scr, NEG_INF)
        l_scr[...] = jnp.zeros_like(l_scr)
        acc_scr[...] = jnp.zeros_like(acc_scr)

    q = q_ref[0].reshape(R, D).astype(BF)
    sel = sel_ref[0]
    jl = lax.broadcasted_iota(jnp.int32, sel.shape, 1)
    half = lax.broadcasted_iota(jnp.int32, (T, PAGE_SIZE), 1) < NSA_SEL_BLOCK
    m_i, l_i, acc = m_scr[...], l_scr[...], acc_scr[...]
    for j in range(PG):
        pid = s * PG + j
        page = page_refs[j][0]
        sc = _dot(q, page[2 * D:3 * D].astype(BF))
        sa = jnp.max(jnp.where(jl == 2 * pid, sel, 0.0), axis=1, keepdims=True)
        sb = jnp.max(jnp.where(jl == 2 * pid + 1, sel, 0.0), axis=1, keepdims=True)
        mt = jnp.where(half, sa, sb) > 0.5
        mask = jnp.broadcast_to(mt[None], (NH, T, PAGE_SIZE)).reshape(R, PAGE_SIZE)
        m_i, l_i, acc = _softmax_update(sc, mask, m_i, l_i, acc, page[3 * D:].astype(BF), R, v_feature_major=True)
    m_scr[...] = m_i
    l_scr[...] = l_i
    acc_scr[...] = acc

    @pl.when(s == pl.num_programs(1) - 1)
    def _():
        causal = _new_causal(R, T)
        newr = newr_ref[...]
        cb = offset // NSA_SEL_BLOCK
        own = jnp.broadcast_to((sel[:, cb:cb + 1] > 0.5)[None], (NH, T, T)).reshape(R, T)
        _, l_s, acc_s = _softmax_update(_dot_nt(q, newr[:, 2 * D:3 * D].astype(BF)), causal & own, m_i, l_i, acc,
                                        newr[:, 3 * D:].astype(BF), R)
        o_sel = acc_s / jnp.maximum(l_s, 1.0)
        win = win_ref[0]
        neww = neww_ref[...]
        wb = win.shape[1]
        tq = lax.broadcasted_iota(jnp.int32, (R, wb), 0) % T
        rk = lax.broadcasted_iota(jnp.int32, (R, wb), 1)
        mask1 = rk > wb + tq - NSA_WINDOW
        s1 = jnp.where(mask1, _dot(q, win[:D].astype(BF)), NEG_INF)
        s2 = jnp.where(causal, _dot_nt(q, neww[:, :D].astype(BF)), NEG_INF)
        m_w = jnp.maximum(jnp.max(s1, axis=-1, keepdims=True), jnp.max(s2, axis=-1, keepdims=True))
        p1 = jnp.where(mask1, jnp.exp(s1 - m_w), 0.0)
        p2 = jnp.where(causal, jnp.exp(s2 - m_w), 0.0)
        l_w = jnp.sum(p1, axis=-1, keepdims=True) + jnp.sum(p2, axis=-1, keepdims=True)
        o_win = (_dot_nt(p1.astype(BF), win[D:].astype(BF)) + _dot(p2.astype(BF), neww[:, D:].astype(BF))) \
            / jnp.maximum(l_w, 1.0)
        o_cmp = ocmp_ref[0]
        sig = jax.nn.sigmoid(gl_ref[...])
        outs = []
        for h in range(NH):
            c = MISC_LANE + 3 * h
            rs = slice(h * T, (h + 1) * T)
            outs.append(sig[:, c:c + 1] * o_cmp[rs] + sig[:, c + 1:c + 2] * o_sel[rs] + sig[:, c + 2:c + 3] * o_win[rs])
        o_ref[...] = jnp.concatenate(outs, axis=1)


def nsa_decode(sp, z, row0, pools, l, page_table, win_state, cmp_pe, cmp_w, gain1, B, T):
    n_pages = page_table.shape[1]
    past = n_pages * PAGE_SIZE
    S = past + T
    R = NH * T
    n_cmp = (S - NSA_CMP_BLOCK) // NSA_CMP_STRIDE + 1
    n_sel = -(-S // NSA_SEL_BLOCK)
    nch = past // NSA_CMP_STRIDE
    assert (n_cmp - 1) * NSA_CMP_STRIDE + NSA_CMP_BLOCK <= past, "compressed blocks must lie inside the cache"
    assert past % NSA_SEL_BLOCK == 0 and T <= NSA_SEL_BLOCK and n_sel <= HW
    PG = min(DEC_PG, n_pages)
    cpp = PAGE_SIZE // NSA_CMP_STRIDE
    y = nsa_compress_pool(pools, l, cmp_pe, cmp_w).reshape(pools.shape[1], cpp, HW)
    yb = y[page_table].reshape(B, nch, HW)
    ci = np.arange(nch)[:, None] * NSA_CMP_STRIDE
    sj = np.arange(HW)[None, :] * NSA_SEL_BLOCK
    cover = (ci < sj + NSA_SEL_BLOCK) & (ci + NSA_CMP_BLOCK > sj) & (np.arange(nch)[:, None] < n_cmp) \
        & (np.arange(HW)[None, :] < n_sel)
    cover = jnp.asarray(cover.astype(np.float32), dtype=BF)
    gain = jnp.concatenate([gain1, jnp.ones((HEAD_DIM,), F32)])[None, :]
    qspec = lambda nargs: pl.BlockSpec((1, NH, T, HEAD_DIM), (lambda b: (0, 0, b, 0)) if nargs == 1
                                       else (lambda b, s, pt: (0, 0, b, 0)))
    ocmp, sel = pl.pallas_call(
        functools.partial(_nsa_sel_decode_kernel, T=T, offset=past, n_cmp=n_cmp, n_sel=n_sel),
        grid=(B,),
        in_specs=[pl.BlockSpec((1, nch, HW), lambda b: (b, 0, 0)), qspec(1),
                  pl.BlockSpec((1, 2 * HEAD_DIM), lambda b: (0, 0)),
                  pl.BlockSpec((nch, HW), lambda b: (0, 0))],
        out_specs=[pl.BlockSpec((1, R, HEAD_DIM), lambda b: (b, 0, 0)), pl.BlockSpec((1, T, HW), lambda b: (b, 0, 0))],
        out_shape=[jax.ShapeDtypeStruct((B, R, HEAD_DIM), F32), jax.ShapeDtypeStruct((B, T, HW), F32)],
        compiler_params=_cparams(("parallel",)),
        name="nsa_select_decode",
    )(yb, sp["nqn"], gain, cover)
    view, base = _pool_view(pools, l, HW)
    wb = win_state.shape[1]
    rb0 = row0 // T
    return pl.pallas_call(
        functools.partial(_nsa_decode_kernel, PG=PG, T=T, offset=past),
        grid_spec=pltpu.PrefetchScalarGridSpec(
            num_scalar_prefetch=1, grid=(B, n_pages // PG),
            in_specs=[qspec(3),
                      pl.BlockSpec((1, T, HW), lambda b, s, pt: (b, 0, 0)),
                      pl.BlockSpec((1, R, HEAD_DIM), lambda b, s, pt: (b, 0, 0)),
                      pl.BlockSpec((T, HW), lambda b, s, pt: (b, 0)),
                      pl.BlockSpec((1, 2 * HEAD_DIM, wb), lambda b, s, pt: (b, 0, 0)),
                      pl.BlockSpec((T, 2 * HEAD_DIM), lambda b, s, pt: (b, 0)),
                      pl.BlockSpec((T, HW), lambda b, s, pt: (rb0 + b, SEC_S2))]
            + _page_specs((HW, PAGE_SIZE), PG, base),
            out_specs=pl.BlockSpec((T, HW), lambda b, s, pt: (b, 0)),
            scratch_shapes=[pltpu.VMEM((R, 1), F32), pltpu.VMEM((R, 1), F32), pltpu.VMEM((R, HEAD_DIM), F32)]),
        out_shape=jax.ShapeDtypeStruct((B * T, HW), F32),
        compiler_params=_cparams(("parallel", "arbitrary")),
        name="nsa_decode",
    )(page_table, sp["nqr"], sel, ocmp, sp["nsa_rows"], jnp.transpose(win_state, (0, 2, 3, 1)).reshape(B, 2 * HEAD_DIM, wb), sp["nsa_win"], z,
      *([view] * PG))


def rmsnorm(x, g):
    xf = x.astype(jnp.float32)
    y = xf * lax.rsqrt(jnp.mean(xf * xf, axis=-1, keepdims=True) + RMS_EPS)
    return (y * g.astype(jnp.float32)).astype(x.dtype)


def rope(x, pos):
    half = HEAD_DIM // 2
    inv = ROPE_THETA ** (-jnp.arange(half, dtype=jnp.float32) / half)
    ang = pos.astype(jnp.float32)[:, None] * inv[None, :]
    cos = jnp.cos(ang)[:, None, :]
    sin = jnp.sin(ang)[:, None, :]
    xf = x.astype(jnp.float32)
    x1, x2 = xf[..., :half], xf[..., half:]
    return jnp.concatenate([x1 * cos - x2 * sin, x2 * cos + x1 * sin], axis=-1).astype(x.dtype)


def masked_softmax(s, mask):
    s = jnp.where(mask, s.astype(jnp.float32), NEG_INF)
    m = jnp.max(s, axis=-1, keepdims=True)
    p = jnp.where(mask, jnp.exp(s - m), 0.0)
    return p / jnp.maximum(jnp.sum(p, axis=-1, keepdims=True), 1.0)


def sweep(fn, blk, *arrays):
    B, T = arrays[0].shape[:2]
    nb = -(-T // blk)
    Tp = nb * blk
    blocks = []
    for a in arrays:
        a = jnp.pad(a, [(0, 0), (0, Tp - T)] + [(0, 0)] * (a.ndim - 2))
        blocks.append(jnp.moveaxis(a.reshape((B, nb, blk) + a.shape[2:]), 1, 0))
    starts = jnp.arange(nb, dtype=jnp.int32) * blk
    out = lax.map(lambda args: fn(args[0], *args[1]), (starts, tuple(blocks)))
    out = jnp.moveaxis(out, 0, 1).reshape((B, Tp) + out.shape[3:])
    return out[:, :T]


def window_attend(q, rows, buf_len):
    B, T, H, Dh = q.shape
    dt = q.dtype
    W = NSA_WINDOW
    qb = min(Q_BLOCK, T)
    nb = -(-T // qb)
    Tp = nb * qb
    band = W + qb
    rp = jnp.pad(rows, ((0, 0), (W, Tp - T), (0, 0), (0, 0)))
    kidx = buf_len + np.arange(nb)[:, None] * qb + np.arange(band)[None, :]
    kb = rp[:, kidx]
    qp = jnp.pad(q, ((0, 0), (0, Tp - T), (0, 0), (0, 0))).reshape(B, nb, qb, H, Dh)
    s = jnp.einsum('bnqhd,bnkd->bhnqk', qp, kb[..., 0, :], preferred_element_type=jnp.float32) * ATTN_SCALE
    qq = buf_len + np.arange(nb)[:, None] * qb + np.arange(qb)[None, :]
    kk = kidx - W
    mask = (kk[:, None, :] >= 0) & (kk[:, None, :] <= qq[:, :, None]) & (kk[:, None, :] > qq[:, :, None] - W)
    p = masked_softmax(s, mask)
    o = jnp.einsum('bhnqk,bnkd->bnqhd', p.astype(dt), kb[..., 1, :]).reshape(B, Tp, H, Dh)
    return o[:, :T]


def nsa_mixer(q, kv_c, kv_s, kv_w, gate_logits, past_rows, win_buf, qk_gain, cmp_pe, cmp_w, offset):
    B, T, H, Dh = q.shape
    dt = q.dtype
    pos = offset + jnp.arange(T, dtype=jnp.int32)
    qn = rmsnorm(q, qk_gain[0])
    qr = rope(qn, pos)
    k_s = rope(rmsnorm(kv_s[:, :, 0:1], qk_gain[2]), pos)[:, :, 0]
    k_w = rope(rmsnorm(kv_w[:, :, 0:1], qk_gain[3]), pos)[:, :, 0]
    new_rows = jnp.stack([kv_c[:, :, 0], kv_c[:, :, 1], k_s, kv_s[:, :, 1]], axis=2)
    rows = jnp.concatenate([past_rows, new_rows], axis=1)
    S = offset + T
    n_cmp = (S - NSA_CMP_BLOCK) // NSA_CMP_STRIDE + 1
    cidx = np.arange(n_cmp)[:, None] * NSA_CMP_STRIDE + np.arange(NSA_CMP_BLOCK)[None, :]

    def compress(r):
        blocks = rows[:, :, r][:, cidx] + cmp_pe[r]
        return blocks.reshape(B, n_cmp, NSA_CMP_BLOCK * Dh) @ cmp_w[r]

    k_cmp = rmsnorm(compress(0), qk_gain[1])
    v_cmp = compress(1)
    s_c = jnp.einsum('bthd,bnd->bhtn', qn, k_cmp, preferred_element_type=jnp.float32) * ATTN_SCALE
    cmp_end = np.arange(n_cmp) * NSA_CMP_STRIDE + NSA_CMP_BLOCK - 1
    p_c = masked_softmax(s_c, cmp_end[None, :] <= pos[:, None])
    o_cmp = jnp.einsum('bhtn,bnd->bthd', p_c.astype(dt), v_cmp)
    n_sel = -(-S // NSA_SEL_BLOCK)
    ci = np.arange(n_cmp)[:, None] * NSA_CMP_STRIDE
    sj = np.arange(n_sel)[None, :] * NSA_SEL_BLOCK
    cover = ((ci < sj + NSA_SEL_BLOCK) & (ci + NSA_CMP_BLOCK > sj)).astype(np.float32)
    imp = jnp.einsum('bhtn,nj->btj', p_c, jnp.asarray(cover))
    cur = (pos // NSA_SEL_BLOCK)[:, None]
    jj = jnp.arange(n_sel)[None, :]
    forced = (jj == 0) | (jj == cur) | (jj == cur - 1)
    imp = jnp.where(jj > cur, NEG_INF, jnp.where(forced, SEL_FORCE, imp))
    _, sel_idx = lax.top_k(imp, min(NSA_TOPN, n_sel))
    kv_sel = jnp.pad(rows[:, :, 2:4], ((0, 0), (0, n_sel * NSA_SEL_BLOCK - S), (0, 0), (0, 0)))
    kv_sel = kv_sel.reshape(B, n_sel, NSA_SEL_BLOCK, 2, Dh)
    bidx = jnp.arange(B)[:, None, None]

    def sel_block(start, q_blk, idx_blk):
        qb = q_blk.shape[1]
        tp = offset + start + jnp.arange(qb)
        g = kv_sel[bidx, idx_blk]
        kpos = idx_blk[..., None] * NSA_SEL_BLOCK + jnp.arange(NSA_SEL_BLOCK)
        mask = (kpos <= tp[None, :, None, None]).reshape(B, 1, qb, -1)
        g = g.reshape(B, qb, -1, 2, Dh)
        s = jnp.einsum('bqhd,bqkd->bhqk', q_blk, g[..., 0, :], preferred_element_type=jnp.float32) * ATTN_SCALE
        p = masked_softmax(s, mask)
        return jnp.einsum('bhqk,bqkd->bqhd', p.astype(dt), g[..., 1, :])

    o_sel = sweep(sel_block, min(GATHER_Q_BLOCK, T), qr, sel_idx)
    win_rows = jnp.concatenate([win_buf, jnp.stack([k_w, kv_w[:, :, 1]], axis=2)], axis=1)
    o_win = window_attend(qr, win_rows, win_buf.shape[1])
    gates = jax.nn.sigmoid(gate_logits.astype(jnp.float32)).astype(dt)
    o = gates[..., 0:1] * o_cmp + gates[..., 1:2] * o_sel + gates[..., 2:3] * o_win
    new_win = win_rows[:, -min(NSA_WINDOW, win_rows.shape[1]):]
    return o, new_rows, new_win


def gated_recurrence(q, k, v, logf, S0):
    B, T, H, DK = q.shape
    DV = v.shape[-1]
    C = math.gcd(T, HGRN_CHUNK)
    nc = T // C

    def chunks(a):
        return jnp.moveaxis(a.reshape((B, nc, C) + a.shape[2:]), 1, 0).swapaxes(2, 3)

    causal = jnp.tril(jnp.ones((C, C), dtype=bool))[:, :, None]

    def step(S, inp):
        qc, kc, vc, gc = inp
        b = jnp.cumsum(gc, axis=2)
        o_inter = jnp.einsum('bhtk,bhkv->bhtv', qc * jnp.exp(b), S)
        diff = b[:, :, :, None, :] - b[:, :, None, :, :]
        decay = jnp.where(causal, jnp.exp(jnp.where(causal, diff, 0.0)), 0.0)
        A = jnp.einsum('bhtk,bhsk,bhtsk->bhts', qc, kc, decay)
        o = o_inter + jnp.einsum('bhts,bhsv->bhtv', A, vc)
        b_last = b[:, :, -1:, :]
        S_new = jnp.exp(b_last[:, :, 0, :])[..., None] * S + jnp.einsum('bhsk,bhsv->bhkv', kc * jnp.exp(b_last - b), vc)
        return S_new, o

    S, o = lax.scan(step, S0, (chunks(q), chunks(k), chunks(v), chunks(logf)))
    o = jnp.moveaxis(o.swapaxes(2, 3), 0, 1).reshape(B, T, H, DV)
    return o, S


def hgrn2_mixer(q, f, i, g, S0, lb, out_gain):
    dt = q.dtype
    H = q.shape[2]
    lb = lb.reshape(H, HGRN_DK)
    z = f.astype(jnp.float32)
    logf = jnp.log(lb + (1.0 - lb) * jax.nn.sigmoid(z))
    k = (1.0 - lb) * jax.nn.sigmoid(-z)
    qf = jax.nn.silu(q.astype(jnp.float32))
    o, S = gated_recurrence(qf, k, i.astype(jnp.float32), logf, S0)
    o = rmsnorm(o, out_gain) * jax.nn.silu(g.astype(jnp.float32))
    return o.astype(dt), S


def moba_mixer(q, k, v, past_rows, qk_gain, offset):
    B, T, H, Dh = q.shape
    dt = q.dtype
    pos = offset + jnp.arange(T, dtype=jnp.int32)
    qr = rope(rmsnorm(q, qk_gain[0]), pos)
    kr = rope(rmsnorm(k, qk_gain[1]), pos)
    new_rows = jnp.stack([kr, v], axis=2)
    rows = jnp.concatenate([past_rows, new_rows], axis=1)
    S = offset + T
    nblk = -(-S // MOBA_BLOCK)
    kvb = jnp.pad(rows, ((0, 0), (0, nblk * MOBA_BLOCK - S), (0, 0), (0, 0), (0, 0)))
    kvb = jnp.transpose(kvb.reshape(B, nblk, MOBA_BLOCK, 2, H, Dh), (0, 4, 1, 2, 3, 5))
    kmean = jnp.mean(kvb[..., 0, :].astype(jnp.float32), axis=3)
    gate = jnp.einsum('bthd,bhnd->bthn', qr.astype(jnp.float32), kmean)
    own = pos // MOBA_BLOCK
    past_ok = jnp.arange(nblk)[None, :] < own[:, None]
    _, top = lax.top_k(jnp.where(past_ok[None, :, None, :], gate, NEG_INF), min(MOBA_TOPK, nblk))
    valid = top < own[None, :, None, None]
    idx = jnp.concatenate([top, jnp.broadcast_to(own[None, :, None, None], (B, T, H, 1)).astype(top.dtype)], axis=-1)
    ok = jnp.concatenate([valid, jnp.ones((B, T, H, 1), dtype=bool)], axis=-1)
    bidx = jnp.arange(B)[:, None, None, None]
    hidx = jnp.arange(H)[None, None, :, None]

    def blk_fn(start, q_blk, idx_blk, ok_blk):
        qb = q_blk.shape[1]
        tp = offset + start + jnp.arange(qb)
        g = kvb[bidx, hidx, idx_blk]
        kpos = idx_blk[..., None] * MOBA_BLOCK + jnp.arange(MOBA_BLOCK)
        mask = (ok_blk[..., None] & (kpos <= tp[None, :, None, None, None])).reshape(B, qb, H, -1)
        g = g.reshape(B, qb, H, -1, 2, Dh)
        s = jnp.einsum('bqhd,bqhkd->bqhk', q_blk, g[..., 0, :], preferred_element_type=jnp.float32) * ATTN_SCALE
        p = masked_softmax(s, mask)
        return jnp.einsum('bqhk,bqhkd->bqhd', p.astype(dt), g[..., 1, :])

    o = sweep(blk_fn, min(GATHER_Q_BLOCK, T), qr, idx, ok)
    return o, new_rows


def fox_mixer(q, k, v, f_logit, past_kv, past_logf, qk_gain, f_bias, offset):
    B, T, H, Dh = q.shape
    dt = q.dtype
    qn = rmsnorm(q, qk_gain[0])
    kn = rmsnorm(k, qk_gain[1])
    logf_new = jax.nn.log_sigmoid(f_logit.astype(jnp.float32) + f_bias.astype(jnp.float32))
    new_rows = jnp.stack([kn, v], axis=2)
    rows = jnp.concatenate([past_kv, new_rows], axis=1)
    c = jnp.cumsum(jnp.concatenate([past_logf.astype(jnp.float32), logf_new], axis=1), axis=1)
    S = offset + T
    K = rows[:, :, 0]
    V = rows[:, :, 1]
    c_k = jnp.moveaxis(c, 1, 2)[:, :, None, :]
    kpos = jnp.arange(S)

    def blk_fn(start, q_blk, cq_blk):
        qb = q_blk.shape[1]
        tp = offset + start + jnp.arange(qb)
        s = jnp.einsum('bqhd,bkhd->bhqk', q_blk, K, preferred_element_type=jnp.float32) * ATTN_SCALE
        s = s + jnp.moveaxis(cq_blk, 1, 2)[..., None] - c_k
        p = masked_softmax(s, kpos[None, :] <= tp[:, None])
        return jnp.einsum('bhqk,bkhd->bqhd', p.astype(dt), V)

    o = sweep(blk_fn, min(Q_BLOCK, T), qn, c[:, offset:])
    return o, new_rows, logf_new.astype(dt)


def _outproj_kernel(x_ref, pn_ref, ph_ref, pm_ref, pf_ref, sn_ref, sh_ref, sm_ref, sf_ref, w_ref, g_ref,
                    xo_ref, hn_ref, *, n_prompt_blocks):
    i = pl.program_id(0)

    def project(parts):
        acc = x_ref[...]
        for m, part in enumerate(parts):
            acc = acc + _dot(part[...].astype(BF), w_ref[m * HW:(m + 1) * HW, :])
        xo_ref[...] = acc
        hn = acc * lax.rsqrt(jnp.mean(acc * acc, axis=-1, keepdims=True) + RMS_EPS) * g_ref[...]
        hn_ref[...] = hn.astype(hn_ref.dtype)

    pl.when(i < n_prompt_blocks)(lambda: project((pn_ref, ph_ref, pm_ref, pf_ref)))
    pl.when(i >= n_prompt_blocks)(lambda: project((sn_ref, sh_ref, sm_ref, sf_ref)))


def out_projection(x, prompt_parts, sample_parts, w_out, g):
    N, D = x.shape
    tm = PREP_TQ
    npb = prompt_parts[0].shape[0] // tm
    assert sample_parts[0].shape[0] == tm and N == (npb + 1) * tm
    pspec = pl.BlockSpec((tm, HW), lambda i: (jnp.minimum(i, npb - 1), 0))
    sspec = pl.BlockSpec((tm, HW), lambda i: (0, 0))
    row = pl.BlockSpec((tm, D), lambda i: (i, 0))
    return pl.pallas_call(
        functools.partial(_outproj_kernel, n_prompt_blocks=npb),
        grid=(npb + 1,),
        in_specs=[row] + [pspec] * 4 + [sspec] * 4 + [pl.BlockSpec((MIX_WIDTH, D), lambda i: (0, 0)),
                                                     pl.BlockSpec((1, D), lambda i: (0, 0))],
        out_specs=[row, row],
        out_shape=[jax.ShapeDtypeStruct((N, D), F32), jax.ShapeDtypeStruct((N, D), BF)],
        compiler_params=_cparams(("parallel",)),
        name="out_projection_rmsnorm",
    )(x, *prompt_parts, *sample_parts, w_out.astype(BF), g[None, :])


def _ffn_up_kernel(be_ref, new_ref, x_ref, w1_ref, w3_ref, u_ref, w1_scr, w3_scr):
    i = pl.program_id(1)

    @pl.when(new_ref[i] == 1)
    def _():
        w1_scr[...] = w1_ref[0].astype(BF)
        w3_scr[...] = w3_ref[0].astype(BF)

    x = x_ref[...].astype(BF)
    a = _dot(x, w1_scr[...])
    b = _dot(x, w3_scr[...])
    u_ref[...] = (a * jax.nn.sigmoid(a) * b).astype(u_ref.dtype)


def _ffn_down_kernel(be_ref, new_ref, u_ref, w2_ref, *rest, residual):
    res_ref = rest[0] if residual else None
    y_ref, w2_scr = rest[-2], rest[-1]
    i = pl.program_id(1)

    @pl.when(new_ref[i] == 1)
    def _():
        w2_scr[...] = w2_ref[0].astype(BF)

    y = _dot(u_ref[...], w2_scr[...])
    y_ref[...] = res_ref[...] + y if residual else y


def grouped_swiglu(x, block_exp, w1, w3, w2, tm, tf, tn, residual=None):
    R, D = x.shape
    F = w1.shape[2]
    nblk = R // tm
    block_exp = block_exp.astype(jnp.int32)
    new = jnp.concatenate([jnp.ones((1,), jnp.int32), (block_exp[1:] != block_exp[:-1]).astype(jnp.int32)])
    u = pl.pallas_call(
        _ffn_up_kernel,
        grid_spec=pltpu.PrefetchScalarGridSpec(
            num_scalar_prefetch=2, grid=(F // tf, nblk),
            in_specs=[pl.BlockSpec((tm, D), lambda j, i, be, nw: (i, 0)),
                      pl.BlockSpec((1, D, tf), lambda j, i, be, nw: (be[i], 0, j)),
                      pl.BlockSpec((1, D, tf), lambda j, i, be, nw: (be[i], 0, j))],
            out_specs=pl.BlockSpec((tm, tf), lambda j, i, be, nw: (i, j)),
            scratch_shapes=[pltpu.VMEM((D, tf), BF), pltpu.VMEM((D, tf), BF)]),
        out_shape=jax.ShapeDtypeStruct((R, F), BF),
        compiler_params=_cparams(("arbitrary", "arbitrary")),
        name="swiglu_up",
    )(block_exp, new, x, w1, w3)
    out_block = pl.BlockSpec((tm, tn), lambda n, i, be, nw: (i, n))
    extra = () if residual is None else (residual,)
    return pl.pallas_call(
        functools.partial(_ffn_down_kernel, residual=residual is not None),
        grid_spec=pltpu.PrefetchScalarGridSpec(
            num_scalar_prefetch=2, grid=(D // tn, nblk),
            in_specs=[pl.BlockSpec((tm, F), lambda n, i, be, nw: (i, 0)),
                      pl.BlockSpec((1, F, tn), lambda n, i, be, nw: (be[i], 0, n))] + [out_block] * len(extra),
            out_specs=out_block,
            scratch_shapes=[pltpu.VMEM((F, tn), BF)]),
        out_shape=jax.ShapeDtypeStruct((R, D), F32),
        compiler_params=_cparams(("arbitrary", "arbitrary")),
        name="swiglu_down",
    )(block_exp, new, u, w2, *extra)


def swiglu_dense(hn, x, w1, w3, w2):
    tm = 640 if hn.shape[0] % 640 == 0 else 256
    be = jnp.zeros((hn.shape[0] // tm,), jnp.int32)
    return grouped_swiglu(hn, be, w1[None], w3[None], w2[None], tm, DENSE_TF, FFN_TN, residual=x)


def moe_ffn_grouped(xf, router, w1, w3, w2):
    N, D = xf.shape
    tm = MOE_TM
    rpad = jnp.pad(router, ((0, 0), (0, LANES - N_EXPERTS)))
    logits = matmul(xf, rpad, tm=256, tn=LANES)[:, :N_EXPERTS]
    top_v, top_e = lax.top_k(logits, TOP_K)
    gates = jax.nn.softmax(top_v, axis=-1)
    NK = N * TOP_K
    flat_e = top_e.reshape(NK)
    order = jnp.argsort(flat_e)
    e_sorted = flat_e[order]
    tok_sorted = (order // TOP_K).astype(jnp.int32)
    counts = jnp.sum((flat_e[:, None] == jnp.arange(N_EXPERTS)[None, :]).astype(jnp.int32), axis=0)
    padded = (counts + tm - 1) // tm * tm
    pend = jnp.cumsum(padded)
    pstart = pend - padded
    start = jnp.cumsum(counts) - counts
    dest_sorted = pstart[e_sorted] + (jnp.arange(NK, dtype=jnp.int32) - start[e_sorted])
    n_blocks = -(-NK // tm) + N_EXPERTS
    slot_tok = jnp.full((n_blocks * tm,), N, jnp.int32).at[dest_sorted].set(tok_sorted)
    block_exp = jnp.clip(jnp.searchsorted(pend, jnp.arange(n_blocks) * tm, side='right'), 0, N_EXPERTS - 1)
    xpad = jnp.concatenate([xf, jnp.zeros((1, D), xf.dtype)], axis=0)
    xb = xpad[slot_tok]
    yb = grouped_swiglu(xb, block_exp, w1, w3, w2, tm, MOE_TF, FFN_TN)
    dest = jnp.zeros((NK,), jnp.int32).at[order].set(dest_sorted).reshape(N, TOP_K)
    return yb[dest[:, 0]] * gates[:, 0:1] + yb[dest[:, 1]] * gates[:, 1:2]


def swiglu(h, w1, w3, w2):
    return (jax.nn.silu(h @ w1) * (h @ w3)) @ w2


def moe_ffn(xf, router, w1, w3, w2):
    N, D = xf.shape
    dt = xf.dtype
    logits = (xf @ router).astype(jnp.float32)
    top_v, top_e = lax.top_k(logits, TOP_K)
    gates = jax.nn.softmax(top_v, axis=-1)
    NK = N * TOP_K
    flat_e = top_e.reshape(NK)
    flat_tok = jnp.arange(NK, dtype=jnp.int32) // TOP_K
    order = jnp.argsort(flat_e)
    e_sorted = flat_e[order]
    tok_sorted = flat_tok[order]
    counts = jnp.zeros((N_EXPERTS,), jnp.int32).at[flat_e].add(1)
    padded = (counts + MOE_BLOCK - 1) // MOE_BLOCK * MOE_BLOCK
    pend = jnp.cumsum(padded)
    pstart = pend - padded
    start = jnp.cumsum(counts) - counts
    dest = pstart[e_sorted] + (jnp.arange(NK, dtype=jnp.int32) - start[e_sorted])
    n_blocks = -(-NK // MOE_BLOCK) + N_EXPERTS
    slot_tok = jnp.full((n_blocks * MOE_BLOCK,), N, jnp.int32).at[dest].set(tok_sorted)
    block_exp = jnp.clip(jnp.searchsorted(pend, jnp.arange(n_blocks) * MOE_BLOCK, side='right'), 0, N_EXPERTS - 1)
    xpad = jnp.concatenate([xf, jnp.zeros((1, D), dt)], axis=0)
    xb = xpad[slot_tok].reshape(n_blocks, MOE_BLOCK, D)

    def expert_block(args):
        xblk, e = args
        return swiglu(xblk, w1[e], w3[e], w2[e])

    yb = lax.map(expert_block, (xb, block_exp)).reshape(n_blocks * MOE_BLOCK, D)
    y_assign = yb[dest] * gates.reshape(NK)[order][:, None].astype(dt)
    return jnp.zeros((N, D), dt).at[tok_sorted].add(y_assign)


def z_sections(z):
    s = lambda c, a=0, b=HW: z[..., c * HW + a:c * HW + b]
    d = HEAD_DIM
    return dict(nq=s(SEC_NQ), nkc=s(SEC_S1, 0, 2 * d), nks=s(SEC_S1, 2 * d, 4 * d), nkw=s(SEC_S2, 0, 2 * d),
                ngate=s(SEC_S2, MISC_LANE, MISC_LANE + 12), ff=s(SEC_S2, MISC_LANE + 12, MISC_LANE + 16),
                hq=s(SEC_HQ), hf=s(SEC_HF), hi=s(SEC_HI), hg=s(SEC_HG), mq=s(SEC_MQ), mk=s(SEC_MK), mv=s(SEC_MV),
                fq=s(SEC_FQ), fk=s(SEC_FK), fv=s(SEC_FV))


def kernel(x_prompt, x_sample, cache_nsa, state_nsa_win, state_hgrn, cache_moba, cache_fox_kv, cache_fox_logf,
           page_table, g_mix, g_ffn, w_in, w_out, nsa_qk_gain, nsa_cmp_pe, nsa_cmp_w, hgrn_lb_logits,
           hgrn_out_gain, moba_qk_gain, fox_qk_gain, fox_f_bias, ffn_w1, ffn_w3, ffn_w2, moe_router,
           moe_w1, moe_w3, moe_w2):
    dt = x_prompt.dtype
    Bp, Tp, D = x_prompt.shape
    Bs, Ts, _ = x_sample.shape
    Np, Ns = Bp * Tp, Bs * Ts
    past_len = page_table.shape[1] * PAGE_SIZE
    lb_w = jax.nn.softmax(hgrn_lb_logits.astype(jnp.float32), axis=0)
    lower_bounds = jnp.cumsum(lb_w, axis=0) - lb_w[0:1]

    def gather_pages(pool):
        g = pool[page_table]
        return g.reshape((Bs, past_len) + pool.shape[2:])

    cos_p, sin_p = rope_tables(jnp.arange(Tp, dtype=jnp.int32))
    cos_s, sin_s = rope_tables(past_len + jnp.arange(Ns, dtype=jnp.int32) % Ts)
    assert Ns == PREP_TQ and Np % PREP_TQ == 0
    gmat = group_mean_matrix()
    cover, expand = nsa_constants(Tp)

    x = jnp.concatenate([x_prompt.reshape(Np, D), x_sample.reshape(Ns, D)], axis=0)
    st_p, st_s = [], []
    for l in range(DEPTH):
        i = l // 2
        z = in_projection(x, g_mix[l][None, :], relayout_w_in(w_in[l]))

        gains = head_gains(nsa_qk_gain[l], moba_qk_gain[l], fox_qk_gain[l])
        pp = prep_prompt(z, 0, Bp, Tp, cos_p, sin_p, gains, gmat)
        pe_flat, w_flat, cgain = nsa_compress_weights(nsa_cmp_pe[l], nsa_cmp_w[l], nsa_qk_gain[l][1])
        kc, vc = nsa_compress(pp["nsa_kc"], Bp, Tp, pe_flat, w_flat, cgain)
        o_nsa_p = nsa_attention_prompt(pp, kc, vc, z, 0, Bp, Tp, cover, expand)
        o_mb_p = moba_attention_prompt(pp, Bp, Tp)
        ff_lo = SEC_S2 * HW + MISC_LANE + 12
        ff_p = z[:Np, ff_lo:ff_lo + NH].reshape(Bp, Tp, NH)
        logf_p = jax.nn.log_sigmoid(ff_p + fox_f_bias[l].astype(F32))
        o_fx_p = fox_attention_prompt(pp, jnp.cumsum(logf_p, axis=1), Bp, Tp)
        hd = lambda a, n, d, B, T: a.reshape(B, T, n, d)
        o_hg_p, hg_state_p = hgrn_mixer(z, 0, Bp, Tp, jnp.zeros((Bp, NH, HGRN_DK, HGRN_DV), F32), lower_bounds[l],
                                        hgrn_out_gain[l], gmat)
        nsa_win_p = pp["nsa_win"].reshape(Bp, Tp, 2, HEAD_DIM)[:, -min(NSA_WINDOW, Tp):]
        st_p.append((pp["nsa_rows"].reshape(Bp, Tp, NSA_ROWS, HEAD_DIM), nsa_win_p, hg_state_p.astype(dt),
                     pp["moba_rows"].reshape(Bp, Tp, 2, NH, HEAD_DIM), pp["fox_rows"].reshape(Bp, Tp, 2, NH, HEAD_DIM),
                     logf_p.astype(dt)))

        sp = prep_prompt(z, Np, 1, Ns, cos_s, sin_s, gains, gmat, qdt=F32)
        o_nsa_s = nsa_decode(sp, z, Np, cache_nsa, l, page_table, state_nsa_win[l], nsa_cmp_pe[l], nsa_cmp_w[l],
                             nsa_qk_gain[l][1], Bs, Ts)
        o_hg_s, hg_state = hgrn_mixer(z, Np, Bs, Ts, state_hgrn[l].astype(F32), lower_bounds[l], hgrn_out_gain[l], gmat)
        o_mb_s = moba_decode(sp, cache_moba, l, page_table, sp["moba_rows"], Bs, Ts)
        ff_s = z[Np:, SEC_S2 * HW + MISC_LANE + 12:SEC_S2 * HW + MISC_LANE + 16].reshape(Bs, Ts, NH)
        logf_s = jax.nn.log_sigmoid(ff_s + fox_f_bias[l].astype(F32))
        c_s = jnp.cumsum(jnp.concatenate([gather_pages(cache_fox_logf[l]).astype(F32), logf_s], axis=1), axis=1)
        o_fx_s = fox_decode(sp, cache_fox_kv, l, page_table, c_s, sp["fox_rows"], Bs, Ts)
        win_rows = jnp.concatenate([state_nsa_win[l], sp["nsa_win"].reshape(Bs, Ts, 2, HEAD_DIM)], axis=1)
        st_s.append((sp["nsa_rows"].reshape(Bs, Ts, NSA_ROWS, HEAD_DIM),
                     win_rows[:, -min(NSA_WINDOW, win_rows.shape[1]):], hg_state.astype(dt),
                     sp["moba_rows"].reshape(Bs, Ts, 2, NH, HEAD_DIM), sp["fox_rows"].reshape(Bs, Ts, 2, NH, HEAD_DIM),
                     logf_s.astype(dt)))

        x, hn = out_projection(x, (o_nsa_p, o_hg_p, o_mb_p, o_fx_p), (o_nsa_s, o_hg_s, o_mb_s, o_fx_s),
                               w_out[l], g_ffn[l])
        if l % 2 == 0:
            x = swiglu_dense(hn, x, ffn_w1[i], ffn_w3[i], ffn_w2[i])
        else:
            x = x + moe_ffn_grouped(hn, moe_router[i], moe_w1[i], moe_w3[i], moe_w2[i])

    def stk(states, j):
        return jnp.stack([s[j] for s in states], axis=0)

    return (x[:Np].reshape(Bp, Tp, D), x[Np:].reshape(Bs, Ts, D),
            stk(st_p, 0), stk(st_s, 0), stk(st_p, 1), stk(st_s, 1), stk(st_p, 2), stk(st_s, 2),
            stk(st_p, 3), stk(st_s, 3), stk(st_p, 4), stk(st_s, 4), stk(st_p, 5), stk(st_s, 5))
```

```python
import math, functools
import jax, jax.numpy as jnp
from jax import lax
import numpy as np
from jax.experimental import pallas as pl
from jax.experimental.pallas import tpu as pltpu

D_MODEL = 1024
DEPTH = 2
PAGE_SIZE = 128
HEAD_DIM = 64
H_NSA = 4
H_HGRN = 4
H_MOBA = 4
H_FOX = 4
NH = 4
HW = NH * HEAD_DIM
MIX_WIDTH = (H_NSA + H_HGRN + H_MOBA + H_FOX) * HEAD_DIM
HGRN_DK = 64
HGRN_DV = HEAD_DIM
HGRN_CHUNK = 64
NSA_CMP_BLOCK = 32
NSA_CMP_STRIDE = 16
NSA_SEL_BLOCK = 64
NSA_TOPN = 16
NSA_WINDOW = 512
NSA_ROWS = 4
MOBA_BLOCK = 256
MOBA_TOPK = 3
ROPE_THETA = 10000.0
Q_BLOCK = 128
GATHER_Q_BLOCK = 32
N_EXPERTS = 8
TOP_K = 2
MOE_BLOCK = 128
RMS_EPS = 1e-6
NEG_INF = -1e30
SEL_FORCE = 1e6
ATTN_SCALE = HEAD_DIM ** -0.5
IN_SIZES = (H_NSA * HEAD_DIM, 2 * HEAD_DIM, 2 * HEAD_DIM, 2 * HEAD_DIM, 3 * H_NSA,
            H_HGRN * HGRN_DK, H_HGRN * HGRN_DK, H_HGRN * HGRN_DV, H_HGRN * HGRN_DV,
            H_MOBA * HEAD_DIM, H_MOBA * HEAD_DIM, H_MOBA * HEAD_DIM,
            H_FOX * HEAD_DIM, H_FOX * HEAD_DIM, H_FOX * HEAD_DIM, H_FOX)
N_IN = sum(IN_SIZES)
IN_OFFS = tuple(int(v) for v in np.cumsum((0,) + IN_SIZES))

N_SEC = 13
N_INP = N_SEC * HW
SEC_NQ, SEC_S1, SEC_S2, SEC_HQ, SEC_HF, SEC_HI, SEC_HG = 0, 1, 2, 3, 4, 5, 6
SEC_MQ, SEC_MK, SEC_MV, SEC_FQ, SEC_FK, SEC_FV = 7, 8, 9, 10, 11, 12
MISC_LANE = 128

LANES = 128
VMEM_LIMIT = 48 * 1024 * 1024
PREP_TQ = 256
NSA_TQ = 128
NSA_TK = 512
ATT_SB = 2
M_FLOOR = -1e20
MOE_TM = 256
MOE_TF = 1792
DENSE_TF = 1408
FFN_TN = 512
BF = jnp.bfloat16
F32 = jnp.float32


def _round_up(x, m):
    return -(-x // m) * m


def _cparams(sem):
    return pltpu.CompilerParams(dimension_semantics=sem, vmem_limit_bytes=VMEM_LIMIT)


def _dot(a, b):
    return jnp.dot(a, b, preferred_element_type=F32)


def _dot_nt(a, b):
    return lax.dot_general(a, b, (((1,), (1,)), ((), ())), preferred_element_type=F32)


def _mm_kernel(a_ref, b_ref, o_ref):
    k = pl.program_id(2)
    acc = _dot(a_ref[...].astype(BF), b_ref[...].astype(BF))

    @pl.when(k == 0)
    def _():
        o_ref[...] = acc

    @pl.when(k != 0)
    def _():
        o_ref[...] += acc


def matmul(a, b, tm=512, tn=512, tk=1024):
    M, K = a.shape
    _, N = b.shape
    tm = min(tm, _round_up(M, 8))
    Mp, Np = _round_up(M, tm), _round_up(N, tn)
    if K % tk:
        tk = K
    if Mp != M:
        a = jnp.pad(a, ((0, Mp - M), (0, 0)))
    if Np != N:
        b = jnp.pad(b, ((0, 0), (0, Np - N)))
    out = pl.pallas_call(
        _mm_kernel,
        grid=(Mp // tm, Np // tn, K // tk),
        in_specs=[pl.BlockSpec((tm, tk), lambda i, j, k: (i, k)),
                  pl.BlockSpec((tk, tn), lambda i, j, k: (k, j))],
        out_specs=pl.BlockSpec((tm, tn), lambda i, j, k: (i, j)),
        out_shape=jax.ShapeDtypeStruct((Mp, Np), F32),
        compiler_params=_cparams(("parallel", "parallel", "arbitrary")),
        name="dense_matmul",
    )(a, b)
    return out[:M, :N]


def _inproj_kernel(x_ref, g_ref, w_ref, o_ref):
    x = x_ref[...]
    y = x * lax.rsqrt(jnp.mean(x * x, axis=-1, keepdims=True) + RMS_EPS) * g_ref[...]
    o_ref[...] = _dot(y.astype(BF), w_ref[...])


def in_projection(x, g, w_bf, tm=256):
    N, D = x.shape
    return pl.pallas_call(
        _inproj_kernel,
        grid=(N // tm,),
        in_specs=[pl.BlockSpec((tm, D), lambda i: (i, 0)),
                  pl.BlockSpec((1, D), lambda i: (0, 0)),
                  pl.BlockSpec((D, N_INP), lambda i: (0, 0))],
        out_specs=pl.BlockSpec((tm, N_INP), lambda i: (i, 0)),
        out_shape=jax.ShapeDtypeStruct((N, N_INP), F32),
        compiler_params=_cparams(("parallel",)),
        name="rmsnorm_in_projection",
    )(x, g, w_bf)


def relayout_w_in(w):
    def cols(i):
        return w[:, IN_OFFS[i]:IN_OFFS[i + 1]]
    pad = jnp.zeros((w.shape[0], HW - 2 * HEAD_DIM - IN_SIZES[4] - IN_SIZES[15]), w.dtype)
    parts = [cols(0), cols(1), cols(2), cols(3), cols(4), cols(15), pad] + [cols(i) for i in range(5, 15)]
    return jnp.concatenate(parts, axis=1).astype(BF)


def _head_meansq(x, gmat):
    sq = x * x
    hi = sq.astype(BF)
    lo = (sq - hi.astype(F32)).astype(BF)
    return _dot(hi, gmat) + _dot(lo, gmat)


def _head_rmsnorm(x, gain, gmat):
    return x * lax.rsqrt(_head_meansq(x, gmat) + RMS_EPS) * gain


def _rope(x, cos, sin_signed, lo_half):
    w = x.shape[1]
    swapped = jnp.where(lo_half, pltpu.roll(x, w - HEAD_DIM // 2, 1), pltpu.roll(x, HEAD_DIM // 2, 1))
    return x * cos + swapped * sin_signed


def _store_heads(ref, x):
    for h in range(NH):
        ref[0, h] = x[:, h * HEAD_DIM:(h + 1) * HEAD_DIM].astype(ref.dtype)


def _prep_kernel(nq_ref, s1_ref, s2_ref, mq_ref, mk_ref, mv_ref, fq_ref, fk_ref, fv_ref,
                 cos_ref, sin_ref, gains_ref, gmat_ref,
                 nsa_rows_ref, nsa_kc_ref, nsa_win_ref, moba_rows_ref, fox_rows_ref,
                 nqn_ref, nqr_ref, nks_ref, nvs_ref, nkw_ref, nvw_ref,
                 mqr_ref, mkr_ref, mvv_ref, kmean_ref, fqn_ref, fkn_ref, fvv_ref):
    cos = cos_ref[...]
    sin = sin_ref[...]
    gmat = gmat_ref[...]
    t = cos.shape[0]
    lane = lax.broadcasted_iota(jnp.int32, (t, HW), 1)
    lo_half = (lane % HEAD_DIM) < (HEAD_DIM // 2)
    gains = gains_ref[...]

    qn = _head_rmsnorm(nq_ref[...], gains[0:1], gmat)
    qr = _rope(qn, cos, sin, lo_half)
    _store_heads(nqn_ref, qn * ATTN_SCALE)
    _store_heads(nqr_ref, qr * ATTN_SCALE)

    s1 = s1_ref[...]
    s1r = _rope(_head_rmsnorm(s1, gains[1:2], gmat), cos, sin, lo_half)
    third = (lane >= 2 * HEAD_DIM) & (lane < 3 * HEAD_DIM)
    rows = jnp.where(third, s1r, s1)
    nsa_rows_ref[...] = rows
    nsa_kc_ref[...] = rows[:, :2 * HEAD_DIM]
    lane_h = lax.broadcasted_iota(jnp.int32, (t, HEAD_DIM), 1)
    row_h = lax.broadcasted_iota(jnp.int32, (t, HEAD_DIM), 0)
    blk = (pl.program_id(1) * t + row_h) // NSA_SEL_BLOCK
    ones_col = jnp.where(lane_h == 0, 1.0, 0.0)
    nks_ref[0] = jnp.concatenate([rows[:, 2 * HEAD_DIM:3 * HEAD_DIM], jnp.where(lane_h == blk, 1.0, 0.0)],
                                 axis=1).astype(nks_ref.dtype)
    nvs_ref[0] = jnp.concatenate([rows[:, 3 * HEAD_DIM:], ones_col], axis=1).astype(nvs_ref.dtype)

    s2 = s2_ref[...]
    s2r = _rope(_head_rmsnorm(s2, gains[2:3], gmat), cos, sin, lo_half)
    wrows = jnp.where(lane < HEAD_DIM, s2r, s2)
    nsa_win_ref[...] = wrows[:, :2 * HEAD_DIM]
    nkw_ref[0] = wrows[:, :HEAD_DIM].astype(nkw_ref.dtype)
    nvw_ref[0] = wrows[:, HEAD_DIM:2 * HEAD_DIM].astype(nvw_ref.dtype)

    mq = _rope(_head_rmsnorm(mq_ref[...], gains[3:4], gmat), cos, sin, lo_half)
    mk = _rope(_head_rmsnorm(mk_ref[...], gains[4:5], gmat), cos, sin, lo_half)
    mv = mv_ref[...]
    _store_heads(mqr_ref, mq * ATTN_SCALE)
    own_block = jnp.where(lane_h == pl.program_id(1), 1.0, 0.0)
    for h in range(NH):
        hs = slice(h * HEAD_DIM, (h + 1) * HEAD_DIM)
        mkr_ref[0, h] = jnp.concatenate([mk[:, hs], own_block], axis=1).astype(mkr_ref.dtype)
        mvv_ref[0, h] = jnp.concatenate([mv[:, hs], ones_col], axis=1).astype(mvv_ref.dtype)
    moba_rows_ref[:, :HW] = mk
    moba_rows_ref[:, HW:] = mv
    kmean_ref[0, 0] = jnp.mean(mk, axis=0, keepdims=True)

    fq = _head_rmsnorm(fq_ref[...], gains[5:6], gmat)
    fk = _head_rmsnorm(fk_ref[...], gains[6:7], gmat)
    fv = fv_ref[...]
    _store_heads(fqn_ref, fq * ATTN_SCALE)
    _store_heads(fkn_ref, fk)
    _store_heads(fvv_ref, fv)
    fox_rows_ref[:, :HW] = fk
    fox_rows_ref[:, HW:] = fv


def rope_tables(pos):
    half = HEAD_DIM // 2
    inv = ROPE_THETA ** (-jnp.arange(half, dtype=F32) / half)
    ang = pos.astype(F32)[:, None] * inv[None, :]
    cos = jnp.cos(ang)
    sin = jnp.sin(ang)
    cos_h = jnp.concatenate([cos, cos], axis=1)
    sin_h = jnp.concatenate([-sin, sin], axis=1)
    return jnp.tile(cos_h, (1, NH)), jnp.tile(sin_h, (1, NH))


def head_gains(nsa_gain, moba_gain, fox_gain):
    one = jnp.ones((HEAD_DIM,), F32)
    t4 = lambda g: jnp.tile(g, NH)
    rows = [t4(nsa_gain[0]),
            jnp.concatenate([one, one, nsa_gain[2], one]),
            jnp.concatenate([nsa_gain[3], one, one, one]),
            t4(moba_gain[0]), t4(moba_gain[1]), t4(fox_gain[0]), t4(fox_gain[1]), t4(one)]
    return jnp.stack(rows, axis=0)


def group_mean_matrix():
    idx = np.arange(HW) // HEAD_DIM
    return jnp.asarray((idx[:, None] == idx[None, :]).astype(np.float32) / HEAD_DIM, dtype=BF)


def prep_prompt(z, row0, B, T, cos, sin, gains, gmat, qdt=None):
    qdt = BF if qdt is None else qdt
    tq = PREP_TQ
    nq = T // tq
    rb0 = row0 // tq

    def sec(c):
        return pl.BlockSpec((tq, HW), lambda b, i, c=c: (rb0 + b * nq + i, c))

    flat = lambda w: pl.BlockSpec((tq, w), lambda b, i: (b * nq + i, 0))
    headmaj = pl.BlockSpec((1, NH, tq, HEAD_DIM), lambda b, i: (b, 0, i, 0))
    single = pl.BlockSpec((1, tq, HEAD_DIM), lambda b, i: (b, i, 0))
    headmaj_aug = pl.BlockSpec((1, NH, tq, LANES), lambda b, i: (b, 0, i, 0))
    single_aug = pl.BlockSpec((1, tq, LANES), lambda b, i: (b, i, 0))
    N = B * T
    sd = jax.ShapeDtypeStruct
    hm_shape = sd((B, NH, T, HEAD_DIM), qdt)
    sg_shape = sd((B, T, HEAD_DIM), qdt)
    outs = pl.pallas_call(
        _prep_kernel,
        grid=(B, nq),
        in_specs=[sec(SEC_NQ), sec(SEC_S1), sec(SEC_S2), sec(SEC_MQ), sec(SEC_MK), sec(SEC_MV),
                  sec(SEC_FQ), sec(SEC_FK), sec(SEC_FV),
                  pl.BlockSpec((tq, HW), lambda b, i: (i, 0)),
                  pl.BlockSpec((tq, HW), lambda b, i: (i, 0)),
                  pl.BlockSpec((8, HW), lambda b, i: (0, 0)),
                  pl.BlockSpec((HW, HW), lambda b, i: (0, 0))],
        out_specs=[flat(HW), flat(2 * HEAD_DIM), flat(2 * HEAD_DIM), flat(2 * HW), flat(2 * HW),
                   headmaj, headmaj, single_aug, single_aug, single, single,
                   headmaj, headmaj_aug, headmaj_aug,
                   pl.BlockSpec((1, 1, 1, HW), lambda b, i: (b, i, 0, 0)),
                   headmaj, headmaj, headmaj],
        out_shape=[sd((N, HW), F32), sd((N, 2 * HEAD_DIM), F32), sd((N, 2 * HEAD_DIM), F32),
                   sd((N, 2 * HW), F32), sd((N, 2 * HW), F32),
                   hm_shape, hm_shape, sd((B, T, LANES), qdt), sd((B, T, LANES), qdt), sg_shape, sg_shape,
                   hm_shape, sd((B, NH, T, LANES), qdt), sd((B, NH, T, LANES), qdt),
                   sd((B, nq, 1, HW), F32),
                   hm_shape, hm_shape, hm_shape],
        compiler_params=_cparams(("parallel", "parallel")),
        name="mixer_prep",
    )(z, z, z, z, z, z, z, z, z, cos, sin, gains, gmat)
    keys = ("nsa_rows", "nsa_kc", "nsa_win", "moba_rows", "fox_rows",
            "nqn", "nqr", "nks", "nvs", "nkw", "nvw", "mqr", "mkr", "mvv", "kmean", "fqn", "fkn", "fvv")
    return dict(zip(keys, outs))


def _nsa_compress_kernel(x_ref, pe_ref, w_ref, gain_ref, k_ref, v_ref):
    x = x_ref[0]
    a = _dot((x + pe_ref[0:1]).astype(BF), w_ref[0])
    b = _dot((x + pe_ref[1:2]).astype(BF), w_ref[1])
    nch = x.shape[0]
    y = a + pltpu.roll(b, nch - 1, 0)
    lane = lax.broadcasted_iota(jnp.int32, y.shape, 1)
    ms = jnp.sum(jnp.where(lane < HEAD_DIM, y * y, 0.0), axis=1, keepdims=True) / HEAD_DIM
    kn = y * lax.rsqrt(ms + RMS_EPS) * gain_ref[...]
    k_ref[0] = kn[:, :HEAD_DIM].astype(BF)
    v_ref[0] = y[:, HEAD_DIM:].astype(BF)


def nsa_compress_weights(cmp_pe, cmp_w, gain1):
    S = NSA_CMP_STRIDE
    pe = cmp_pe.reshape(2, 2, S, HEAD_DIM)
    pe_flat = jnp.transpose(pe, (1, 2, 0, 3)).reshape(2, S * 2 * HEAD_DIM)
    w = cmp_w.reshape(2, 2, S, HEAD_DIM, HEAD_DIM)
    wz = jnp.zeros((2, S, 2, HEAD_DIM, 2, HEAD_DIM), F32)
    wz = wz.at[:, :, 0, :, 0, :].set(w[0]).at[:, :, 1, :, 1, :].set(w[1])
    w_flat = wz.reshape(2, S * 2 * HEAD_DIM, 2 * HEAD_DIM).astype(BF)
    gain = jnp.concatenate([gain1, jnp.ones((HEAD_DIM,), F32)])[None, :]
    return pe_flat, w_flat, gain


def nsa_compress(kc, B, T, pe_flat, w_flat, gain):
    nch = T // NSA_CMP_STRIDE
    cw = NSA_CMP_STRIDE * 2 * HEAD_DIM
    x = kc.reshape(B, nch, cw)
    out_spec = pl.BlockSpec((1, nch, HEAD_DIM), lambda b: (b, 0, 0))
    return pl.pallas_call(
        _nsa_compress_kernel,
        grid=(B,),
        in_specs=[pl.BlockSpec((1, nch, cw), lambda b: (b, 0, 0)),
                  pl.BlockSpec((2, cw), lambda b: (0, 0)),
                  pl.BlockSpec((2, cw, 2 * HEAD_DIM), lambda b: (0, 0, 0)),
                  pl.BlockSpec((1, 2 * HEAD_DIM), lambda b: (0, 0))],
        out_specs=[out_spec, out_spec],
        out_shape=[jax.ShapeDtypeStruct((B, nch, HEAD_DIM), BF)] * 2,
        compiler_params=_cparams(("parallel",)),
        name="nsa_compress",
    )(x, pe_flat, w_flat, gain)


def _softmax_update(s, mask, m_i, l_i, acc, v, lead, v_feature_major=False):
    s = jnp.where(mask, s, NEG_INF)
    m_new = jnp.maximum(m_i, jnp.max(s, axis=-1, keepdims=True))
    p = jnp.where(mask, jnp.exp(s - m_new), 0.0)
    alpha = jnp.exp(m_i - m_new)
    l_new = alpha * l_i + jnp.sum(p, axis=-1, keepdims=True)
    pb = p.astype(BF).reshape(lead, p.shape[-1])
    pv = (_dot_nt(pb, v) if v_feature_major else _dot(pb, v)).reshape(acc.shape)
    return m_new, l_new, alpha * acc + pv


def _nsa_attn_kernel(qn_ref, qr_ref, kc_ref, vc_ref, ks_ref, vs_ref, kw_ref, vw_ref, gl_ref,
                     cover_ref, o_ref, m_scr, acc_scr, *, T):
    tq, tk = NSA_TQ, NSA_TK
    M = NH * tq
    i = pl.program_id(1)
    p0 = i * tq
    qn = qn_ref[0].reshape(M, HEAD_DIM)
    qr = qr_ref[0].reshape(M, HEAD_DIM)
    pos = p0 + lax.broadcasted_iota(jnp.int32, (tq, 1), 0)

    nch = kc_ref.shape[1]
    s_c = _dot_nt(qn, kc_ref[0]).reshape(NH, tq, nch)
    n_idx = lax.broadcasted_iota(jnp.int32, (tq, nch), 1)
    mask_c = (n_idx * NSA_CMP_STRIDE + (NSA_CMP_BLOCK - 1) <= pos)[None]
    s_c = jnp.where(mask_c, s_c, NEG_INF)
    m_c = jnp.max(s_c, axis=-1, keepdims=True)
    p_c = jnp.where(mask_c, jnp.exp(s_c - m_c), 0.0)
    p_c = p_c / jnp.maximum(jnp.sum(p_c, axis=-1, keepdims=True), 1.0)
    p_cb = p_c.astype(BF).reshape(M, nch)
    o_cmp = _dot(p_cb, vc_ref[0]).reshape(NH, tq, HEAD_DIM)
    imp = jnp.sum(_dot(p_cb, cover_ref[...]).reshape(NH, tq, LANES), axis=0)

    j = lax.broadcasted_iota(jnp.int32, (tq, LANES), 1)
    cur = pos // NSA_SEL_BLOCK
    forced = (j == 0) | (j == cur) | (j == cur - 1)
    imp = jnp.where(j > cur, NEG_INF, jnp.where(forced, SEL_FORCE, imp))
    n_sel = T // NSA_SEL_BLOCK
    rank = jnp.zeros((tq, LANES), F32)
    for jp in range(n_sel):
        col = imp[:, jp:jp + 1]
        beats = (col > imp) | ((col == imp) & (j > jp))
        rank = rank + jnp.where(beats, 1.0, 0.0)
    picked = (rank < NSA_TOPN) & (j <= cur)

    pen = jnp.where(picked, 0.0, NEG_INF)[:, :HEAD_DIM].astype(qr.dtype)
    qa = jnp.concatenate([qr, jnp.concatenate([pen] * NH, axis=0)], axis=1)
    _flash_init_aug(m_scr, acc_scr)
    kcol = lax.broadcasted_iota(jnp.int32, (tq, tk), 1)
    for kj in range(T // tk):
        def region(causal, kj=kj):
            s = _dot_nt(qa, ks_ref[0, kj * tk:(kj + 1) * tk, :]).reshape(NH, tq, tk)
            if causal:
                s = jnp.where((kcol + kj * tk <= pos)[None], s, NEG_INF)
            _flash_step_aug(s, vs_ref[0, kj * tk:(kj + 1) * tk, :], m_scr, acc_scr)

        pl.when((kj + 1) * tk <= p0)(functools.partial(region, False))
        pl.when((kj * tk <= p0) & ((kj + 1) * tk > p0))(functools.partial(region, True))
    acc_s = acc_scr[...]
    o_sel = acc_s[:, :, :HEAD_DIM] / jnp.maximum(acc_s[:, :, HEAD_DIM:HEAD_DIM + 1], 1.0)

    band = NSA_WINDOW + tq
    start = pl.multiple_of(jnp.maximum(p0 - NSA_WINDOW, 0), tq)
    kw = kw_ref[0, pl.ds(start, band), :]
    vw = vw_ref[0, pl.ds(start, band), :]
    s_w = _dot_nt(qr, kw).reshape(NH, tq, band)
    kpos = start + lax.broadcasted_iota(jnp.int32, (tq, band), 1)
    mask_w = ((kpos <= pos) & (kpos > pos - NSA_WINDOW))[None]
    s_w = jnp.where(mask_w, s_w, NEG_INF)
    m_w = jnp.max(s_w, axis=-1, keepdims=True)
    p_w = jnp.where(mask_w, jnp.exp(s_w - m_w), 0.0)
    l_w = jnp.sum(p_w, axis=-1, keepdims=True)
    o_win = _dot(p_w.astype(BF).reshape(M, band), vw).reshape(NH, tq, HEAD_DIM) / jnp.maximum(l_w, 1.0)

    sig = jax.nn.sigmoid(gl_ref[...])
    outs = []
    for h in range(NH):
        c = MISC_LANE + 3 * h
        outs.append(sig[:, c:c + 1] * o_cmp[h] + sig[:, c + 1:c + 2] * o_sel[h] + sig[:, c + 2:c + 3] * o_win[h])
    o_ref[...] = jnp.concatenate(outs, axis=1).astype(o_ref.dtype)


def nsa_constants(T):
    nch = T // NSA_CMP_STRIDE
    n_cmp = (T - NSA_CMP_BLOCK) // NSA_CMP_STRIDE + 1
    ci = np.arange(nch)[:, None] * NSA_CMP_STRIDE
    sj = np.arange(LANES)[None, :] * NSA_SEL_BLOCK
    cover = (ci < sj + NSA_SEL_BLOCK) & (ci + NSA_CMP_BLOCK > sj) & (np.arange(nch)[:, None] < n_cmp)
    return jnp.asarray(cover.astype(np.float32), dtype=BF)


def nsa_attention_prompt(pp, kc, vc, z, row0, B, T, cover):
    tq = NSA_TQ
    nq = T // tq
    rb0 = row0 // tq
    nch = T // NSA_CMP_STRIDE
    assert T >= NSA_WINDOW + tq and T % NSA_TK == 0 and NSA_TK % tq == 0 and T // NSA_SEL_BLOCK <= HEAD_DIM
    headmaj = pl.BlockSpec((1, NH, tq, HEAD_DIM), lambda b, i: (b, 0, i, 0))
    full1 = pl.BlockSpec((1, T, HEAD_DIM), lambda b, i: (b, 0, 0))
    full_aug = pl.BlockSpec((1, T, LANES), lambda b, i: (b, 0, 0))
    cmp1 = pl.BlockSpec((1, nch, HEAD_DIM), lambda b, i: (b, 0, 0))
    return pl.pallas_call(
        functools.partial(_nsa_attn_kernel, T=T),
        grid=(B, nq),
        in_specs=[headmaj, headmaj, cmp1, cmp1, full_aug, full_aug, full1, full1,
                  pl.BlockSpec((tq, HW), lambda b, i: (rb0 + b * nq + i, SEC_S2)),
                  pl.BlockSpec((nch, LANES), lambda b, i: (0, 0))],
        out_specs=pl.BlockSpec((tq, HW), lambda b, i: (b * nq + i, 0)),
        out_shape=jax.ShapeDtypeStruct((B * T, HW), BF),
        scratch_shapes=[pltpu.VMEM((NH, tq, 1), F32), pltpu.VMEM((NH, tq, LANES), F32)],
        compiler_params=_cparams(("parallel", "parallel")),
        name="nsa_attention",
    )(pp["nqn"], pp["nqr"], kc, vc, pp["nks"], pp["nvs"], pp["nkw"], pp["nvw"], z, cover)


def _flash_init(m_scr, l_scr, acc_scr):
    m_scr[...] = jnp.full_like(m_scr, M_FLOOR)
    l_scr[...] = jnp.zeros_like(l_scr)
    acc_scr[...] = jnp.zeros_like(acc_scr)


def _flash_step(s, v, m_scr, l_scr, acc_scr):
    m_i = m_scr[...]
    m_new = jnp.maximum(m_i, jnp.max(s, axis=-1, keepdims=True))
    p = jnp.exp(s - m_new)
    alpha = jnp.exp(m_i - m_new)
    l_scr[...] = alpha * l_scr[...] + jnp.sum(p, axis=-1, keepdims=True)
    pv = _dot(p.astype(BF).reshape(-1, p.shape[-1]), v)
    acc_scr[...] = alpha * acc_scr[...] + pv.reshape(acc_scr.shape)
    m_scr[...] = m_new


def _flash_init_aug(m_scr, acc_scr):
    m_scr[...] = jnp.full_like(m_scr, M_FLOOR)
    acc_scr[...] = jnp.zeros_like(acc_scr)


def _flash_step_aug(s, v_aug, m_scr, acc_scr):
    m_i = m_scr[...]
    m_new = jnp.maximum(m_i, jnp.max(s, axis=-1, keepdims=True))
    p = jnp.exp(s - m_new)
    pv = _dot(p.astype(BF).reshape(-1, p.shape[-1]), v_aug)
    acc_scr[...] = jnp.exp(m_i - m_new) * acc_scr[...] + pv.reshape(acc_scr.shape)
    m_scr[...] = m_new


def _moba_attn_kernel(q_ref, k_ref, v_ref, km_ref, o_ref, m_scr, acc_scr):
    tq = MOBA_BLOCK
    tks = ATT_SB * MOBA_BLOCK
    n_sb = k_ref.shape[2] // tks
    qi = pl.program_id(1)
    lane = lax.broadcasted_iota(jnp.int32, (tq, LANES), 1)
    qrow = lax.broadcasted_iota(jnp.int32, (tq, tks), 0)
    kcol = lax.broadcasted_iota(jnp.int32, (tq, tks), 1)
    sels = []
    for h in range(NH):
        g = jnp.where(lane < qi, _dot_nt(q_ref[0, h], km_ref[0, h]), NEG_INF)
        sel = jnp.where(lane == qi, 1.0, 0.0)
        for _ in range(MOBA_TOPK):
            m = jnp.max(g, axis=1, keepdims=True)
            idx = jnp.min(jnp.where(g == m, lane, LANES), axis=1, keepdims=True)
            hit = lane == idx
            sel = jnp.where(hit & (lane < qi), 1.0, sel)
            g = jnp.where(hit, NEG_INF, g)
        pen = jnp.where(sel > 0.5, 0.0, NEG_INF)[:, :HEAD_DIM]
        sels.append(jnp.concatenate([q_ref[0, h], pen.astype(q_ref.dtype)], axis=1))
    _flash_init_aug(m_scr, acc_scr)
    for sb in range(n_sb):
        last = (sb + 1) * ATT_SB - 1

        def region(causal, sb=sb):
            for h in range(NH):
                s = _dot_nt(sels[h], k_ref[0, h, sb * tks:(sb + 1) * tks, :])
                if causal:
                    s = jnp.where(kcol + (sb * tks) <= qrow + qi * tq, s, NEG_INF)
                _flash_step_aug(s, v_ref[0, h, sb * tks:(sb + 1) * tks, :], m_scr.at[h], acc_scr.at[h])

        pl.when(last < qi)(functools.partial(region, False))
        pl.when((sb * ATT_SB <= qi) & (last >= qi))(functools.partial(region, True))
    acc = acc_scr[...]
    o = acc[:, :, :HEAD_DIM] / jnp.maximum(acc[:, :, HEAD_DIM:HEAD_DIM + 1], 1.0)
    o_ref[...] = jnp.concatenate([o[h] for h in range(NH)], axis=1).astype(o_ref.dtype)


def moba_attention_prompt(pp, B, T):
    tq = MOBA_BLOCK
    nq = T // tq
    km = pp["kmean"].reshape(B, nq, NH, HEAD_DIM).transpose(0, 2, 1, 3)
    km = jnp.pad(km, ((0, 0), (0, 0), (0, LANES - nq), (0, 0))).astype(BF)
    headq = pl.BlockSpec((1, NH, tq, HEAD_DIM), lambda b, i: (b, 0, i, 0))
    headfull = pl.BlockSpec((1, NH, T, LANES), lambda b, i: (b, 0, 0, 0))
    assert nq <= HEAD_DIM
    return pl.pallas_call(
        _moba_attn_kernel,
        grid=(B, nq),
        in_specs=[headq, headfull, headfull,
                  pl.BlockSpec((1, NH, LANES, HEAD_DIM), lambda b, i: (b, 0, 0, 0))],
        out_specs=pl.BlockSpec((tq, HW), lambda b, i: (b * nq + i, 0)),
        out_shape=jax.ShapeDtypeStruct((B * T, HW), BF),
        scratch_shapes=[pltpu.VMEM((NH, tq, 1), F32), pltpu.VMEM((NH, tq, LANES), F32)],
        compiler_params=_cparams(("parallel", "parallel")),
        name="moba_attention",
    )(pp["mqr"], pp["mkr"], pp["mvv"], km)


def _fox_attn_kernel(q_ref, k_ref, v_ref, cq_ref, ck_ref, o_ref, m_scr, l_scr, acc_scr):
    tq = MOBA_BLOCK
    tks = ATT_SB * MOBA_BLOCK
    n_sb = k_ref.shape[2] // tks
    qi = pl.program_id(1)
    qrow = lax.broadcasted_iota(jnp.int32, (tq, tks), 0)
    kcol = lax.broadcasted_iota(jnp.int32, (tq, tks), 1)
    _flash_init(m_scr, l_scr, acc_scr)
    for sb in range(n_sb):
        last = (sb + 1) * ATT_SB - 1

        def region(causal, sb=sb):
            for h in range(NH):
                k = k_ref[0, h, sb * tks:(sb + 1) * tks, :]
                v = v_ref[0, h, sb * tks:(sb + 1) * tks, :]
                s = _dot_nt(q_ref[0, h], k) + cq_ref[0, h] - ck_ref[0, h, sb]
                if causal:
                    s = jnp.where(kcol + (sb * tks) <= qrow + qi * tq, s, NEG_INF)
                _flash_step(s, v, m_scr.at[h], l_scr.at[h], acc_scr.at[h])

        pl.when(last < qi)(functools.partial(region, False))
        pl.when((sb * ATT_SB <= qi) & (last >= qi))(functools.partial(region, True))
    o = acc_scr[...] / jnp.maximum(l_scr[...], 1.0)
    o_ref[...] = jnp.concatenate([o[h] for h in range(NH)], axis=1).astype(o_ref.dtype)


def fox_attention_prompt(pp, c, B, T):
    tq = MOBA_BLOCK
    nq = T // tq
    ch = jnp.transpose(c, (0, 2, 1))
    cq = ch[..., None]
    tks = ATT_SB * MOBA_BLOCK
    ck = ch.reshape(B, NH, T // tks, 1, tks)
    headq = pl.BlockSpec((1, NH, tq, HEAD_DIM), lambda b, i: (b, 0, i, 0))
    headfull = pl.BlockSpec((1, NH, T, HEAD_DIM), lambda b, i: (b, 0, 0, 0))
    return pl.pallas_call(
        _fox_attn_kernel,
        grid=(B, nq),
        in_specs=[headq, headfull, headfull,
                  pl.BlockSpec((1, NH, tq, 1), lambda b, i: (b, 0, i, 0)),
                  pl.BlockSpec((1, NH, T // tks, 1, tks), lambda b, i: (b, 0, 0, 0, 0))],
        out_specs=pl.BlockSpec((tq, HW), lambda b, i: (b * nq + i, 0)),
        out_shape=jax.ShapeDtypeStruct((B * T, HW), BF),
        scratch_shapes=[pltpu.VMEM((NH, tq, 1), F32), pltpu.VMEM((NH, tq, 1), F32),
                        pltpu.VMEM((NH, tq, HEAD_DIM), F32)],
        compiler_params=_cparams(("parallel", "parallel")),
        name="fox_attention",
    )(pp["fqn"], pp["fkn"], pp["fvv"], cq, ck)


def _hgrn_kernel(q_ref, f_ref, i_ref, g_ref, s0_ref, lb_ref, gain_ref, bd_ref, gmat_ref,
                 o_ref, sout_ref, st_scr, *, C):
    c = pl.program_id(1)

    @pl.when(c == 0)
    def _():
        st_scr[...] = jnp.zeros_like(st_scr)
        for h in range(NH):
            st_scr[h * HEAD_DIM:(h + 1) * HEAD_DIM, h * HEAD_DIM:(h + 1) * HEAD_DIM] = s0_ref[0, h]

    lb = lb_ref[...]
    z = f_ref[...]
    logf = jnp.log(lb + (1.0 - lb) * jax.nn.sigmoid(z))
    kk = (1.0 - lb) * jax.nn.sigmoid(-z)
    q = q_ref[...]
    qf = q * jax.nn.sigmoid(q)
    v = i_ref[...]
    row = lax.broadcasted_iota(jnp.int32, (C, HW), 0)
    b = logf
    sh = 1
    while sh < C:
        b = b + jnp.where(row >= sh, pltpu.roll(b, sh, 0), 0.0)
        sh *= 2
    bd = bd_ref[...]
    st = st_scr[...]
    o_ref[...] = _dot_nt((qf * jnp.exp(b)).astype(BF), st.astype(BF))
    for s in range(C):
        causal = row >= s
        e = jnp.exp(jnp.where(causal, b - b[s:s + 1], 0.0))
        fz = jnp.where(causal, qf * kk[s:s + 1] * e, 0.0)
        o_ref[...] += _dot(fz.astype(BF), bd) * v[s:s + 1]
    o = o_ref[...]
    b_last = b[C - 1:C, :]
    kt = kk * jnp.exp(b_last - b)
    upd = lax.dot_general(v.astype(BF), kt.astype(BF), (((0,), (0,)), ((), ())), preferred_element_type=F32)
    r2 = lax.broadcasted_iota(jnp.int32, (HW, HW), 0) // HEAD_DIM
    c2 = lax.broadcasted_iota(jnp.int32, (HW, HW), 1) // HEAD_DIM
    st_new = st * jnp.exp(b_last) + jnp.where(r2 == c2, upd, 0.0)
    st_scr[...] = st_new
    g = g_ref[...]
    o_ref[...] = _head_rmsnorm(o, gain_ref[...], gmat_ref[...]) * (g * jax.nn.sigmoid(g))

    @pl.when(c == pl.num_programs(1) - 1)
    def _():
        for h in range(NH):
            sout_ref[0, h] = st_new[h * HEAD_DIM:(h + 1) * HEAD_DIM, h * HEAD_DIM:(h + 1) * HEAD_DIM]


def hgrn_mixer(z, row0, B, T, s0, lb, out_gain, gmat):
    C = math.gcd(T, HGRN_CHUNK)
    nc = T // C
    rb0 = row0 // C
    idx = np.arange(HW) // HEAD_DIM
    bd = jnp.asarray((idx[:, None] == idx[None, :]).astype(np.float32), dtype=BF)

    def sec(cidx):
        return pl.BlockSpec((C, HW), lambda b, c, cidx=cidx: (rb0 + b * nc + c, cidx))

    state = pl.BlockSpec((1, NH, HGRN_DV, HGRN_DK), lambda b, c: (b, 0, 0, 0))
    vec = pl.BlockSpec((1, HW), lambda b, c: (0, 0))
    mat = pl.BlockSpec((HW, HW), lambda b, c: (0, 0))
    o, st = pl.pallas_call(
        functools.partial(_hgrn_kernel, C=C),
        grid=(B, nc),
        in_specs=[sec(SEC_HQ), sec(SEC_HF), sec(SEC_HI), sec(SEC_HG), state, vec, vec, mat, mat],
        out_specs=[pl.BlockSpec((C, HW), lambda b, c: (b * nc + c, 0)), state],
        out_shape=[jax.ShapeDtypeStruct((B * T, HW), F32),
                   jax.ShapeDtypeStruct((B, NH, HGRN_DV, HGRN_DK), F32)],
        scratch_shapes=[pltpu.VMEM((HW, HW), F32)],
        compiler_params=_cparams(("parallel", "arbitrary")),
        name="hgrn_recurrence",
    )(z, z, z, z, jnp.swapaxes(s0, 2, 3), lb[None, :], jnp.tile(out_gain, NH)[None, :], bd, gmat)
    return o, jnp.swapaxes(st, 2, 3)


DEC_PG = 8


def _page_specs(tail, PG, base):
    return [pl.BlockSpec((1,) + tail, lambda b, s, pt, j=j: (base + pt[b, s * PG + j], 0, 0)) for j in range(PG)]


def _pool_view(pools, l, width):
    pages = pools.reshape(pools.shape[0] * pools.shape[1], PAGE_SIZE, width)
    return jnp.swapaxes(pages, 1, 2), l * pools.shape[1]


def _rows_nat(q_ref):
    return jnp.concatenate([q_ref[0, h] for h in range(NH)], axis=1)


def _block_diag_q(qnat, T):
    q4 = jnp.concatenate([qnat] * NH, axis=0)
    r = lax.broadcasted_iota(jnp.int32, q4.shape, 0) // T
    c = lax.broadcasted_iota(jnp.int32, q4.shape, 1) // HEAD_DIM
    return jnp.where(r == c, q4, 0.0)


def _diag_heads(x, T):
    return jnp.concatenate([x[h * T:(h + 1) * T, h * HEAD_DIM:(h + 1) * HEAD_DIM] for h in range(NH)], axis=1)


def _new_causal(R, T):
    tq = lax.broadcasted_iota(jnp.int32, (R, T), 0) % T
    tk = lax.broadcasted_iota(jnp.int32, (R, T), 1)
    return tk <= tq


def _fox_decode_kernel(pt_ref, q_ref, new_ref, cq_ref, cn_ref, ck_ref, *rest, PG, T):
    page_refs, o_ref = rest[:PG], rest[PG]
    m_scr, l_scr, acc_scr = rest[PG + 1:]
    s = pl.program_id(1)
    R = NH * T

    @pl.when(s == 0)
    def _():
        m_scr[...] = jnp.full_like(m_scr, NEG_INF)
        l_scr[...] = jnp.zeros_like(l_scr)
        acc_scr[...] = jnp.zeros_like(acc_scr)

    qbd = _block_diag_q(_rows_nat(q_ref), T).astype(BF)
    cq = cq_ref[0]
    m_i, l_i, acc = m_scr[...], l_scr[...], acc_scr[...]
    for j in range(PG):
        page = page_refs[j][0]
        kt = page[:HW].astype(BF)
        vt = page[HW:].astype(BF)
        ck = jnp.broadcast_to(ck_ref[0, j][:, None, :], (NH, T, PAGE_SIZE)).reshape(R, PAGE_SIZE)
        sc = _dot(qbd, kt) + cq - ck
        m_new = jnp.maximum(m_i, jnp.max(sc, axis=-1, keepdims=True))
        p = jnp.exp(sc - m_new)
        alpha = jnp.exp(m_i - m_new)
        l_i = alpha * l_i + jnp.sum(p, axis=-1, keepdims=True)
        acc = alpha * acc + _dot_nt(p.astype(BF), vt)
        m_i = m_new
    m_scr[...] = m_i
    l_scr[...] = l_i
    acc_scr[...] = acc

    @pl.when(s == pl.num_programs(1) - 1)
    def _():
        new = new_ref[...]
        sc = _dot_nt(qbd, new[:, :HW].astype(BF)) + cq - cn_ref[0]
        _, l_f, acc_f = _softmax_update(sc, _new_causal(R, T), m_i, l_i, acc, new[:, HW:].astype(BF), R)
        o_ref[...] = _diag_heads(acc_f / jnp.maximum(l_f, 1.0), T)


def fox_decode(sp, pools, l, page_table, c, row_new, B, T):
    n_pages = page_table.shape[1]
    PG = min(DEC_PG, n_pages)
    past = n_pages * PAGE_SIZE
    R = NH * T
    view, base = _pool_view(pools, l, 2 * HW)
    ch = jnp.transpose(c, (0, 2, 1))
    ck = ch[:, :, :past].reshape(B, NH, n_pages, PAGE_SIZE).transpose(0, 2, 1, 3)
    cnew = ch[:, :, past:]
    cq = cnew.reshape(B, R, 1)
    cn = jnp.broadcast_to(cnew[:, :, None, :], (B, NH, T, T)).reshape(B, R, T)
    return pl.pallas_call(
        functools.partial(_fox_decode_kernel, PG=PG, T=T),
        grid_spec=pltpu.PrefetchScalarGridSpec(
            num_scalar_prefetch=1, grid=(B, n_pages // PG),
            in_specs=[pl.BlockSpec((1, NH, T, HEAD_DIM), lambda b, s, pt: (0, 0, b, 0)),
                      pl.BlockSpec((T, 2 * HW), lambda b, s, pt: (b, 0)),
                      pl.BlockSpec((1, R, 1), lambda b, s, pt: (b, 0, 0)),
                      pl.BlockSpec((1, R, T), lambda b, s, pt: (b, 0, 0)),
                      pl.BlockSpec((1, PG, NH, PAGE_SIZE), lambda b, s, pt: (b, s, 0, 0))]
            + _page_specs((2 * HW, PAGE_SIZE), PG, base),
            out_specs=pl.BlockSpec((T, HW), lambda b, s, pt: (b, 0)),
            scratch_shapes=[pltpu.VMEM((R, 1), F32), pltpu.VMEM((R, 1), F32), pltpu.VMEM((R, HW), F32)]),
        out_shape=jax.ShapeDtypeStruct((B * T, HW), F32),
        compiler_params=_cparams(("parallel", "arbitrary")),
        name="fox_decode",
    )(page_table, sp["fqn"], row_new, cq, cn, ck, *([view] * PG))


def _moba_decode_kernel(pt_ref, q_ref, new_ref, *rest, PG, T, n_pages):
    page_refs, o_ref = rest[:PG], rest[PG]
    m_scr, l_scr, rs_scr, acc_scr = rest[PG + 1:]
    s = pl.program_id(1)
    R = NH * T
    lane = lax.broadcasted_iota(jnp.int32, (R, LANES), 1)

    @pl.when(s == 0)
    def _():
        m_scr[...] = jnp.full_like(m_scr, NEG_INF)
        l_scr[...] = jnp.zeros_like(l_scr)
        rs_scr[...] = jnp.zeros_like(rs_scr)

    qbd = _block_diag_q(_rows_nat(q_ref), T).astype(BF)
    mm, ll, rs = m_scr[...], l_scr[...], rs_scr[...]
    for j in range(PG):
        pid = s * PG + j
        page = page_refs[j][0]
        sc = _dot(qbd, page[:HW].astype(BF))
        m = jnp.max(sc, axis=-1, keepdims=True)
        p = jnp.exp(sc - m)
        acc_scr[pid] = _dot_nt(p.astype(BF), page[HW:].astype(BF))
        hit = lane == pid
        mm = jnp.where(hit, m, mm)
        ll = jnp.where(hit, jnp.sum(p, axis=-1, keepdims=True), ll)
        rs = jnp.where(hit, jnp.sum(sc, axis=-1, keepdims=True), rs)
    m_scr[...] = mm
    l_scr[...] = ll
    rs_scr[...] = rs

    @pl.when(s == pl.num_programs(1) - 1)
    def _():
        block_lane = (lane % 2 == 0) & (lane < n_pages)
        g = jnp.where(block_lane, (rs + pltpu.roll(rs, LANES - 1, 1)) * (1.0 / MOBA_BLOCK), NEG_INF)
        sel = jnp.zeros((R, LANES), F32)
        for _ in range(MOBA_TOPK):
            mx = jnp.max(g, axis=1, keepdims=True)
            idx = jnp.min(jnp.where(g == mx, lane, LANES), axis=1, keepdims=True)
            hit = lane == idx
            sel = jnp.where(hit & block_lane, 1.0, sel)
            g = jnp.where(hit, NEG_INF, g)
        picked = (sel + pltpu.roll(sel, 1, 1)) > 0.5
        new = new_ref[...]
        mask_n = _new_causal(R, T)
        sn = jnp.where(mask_n, _dot_nt(qbd, new[:, :HW].astype(BF)), NEG_INF)
        m_n = jnp.max(sn, axis=-1, keepdims=True)
        p_n = jnp.where(mask_n, jnp.exp(sn - m_n), 0.0)
        m_tot = jnp.maximum(jnp.max(jnp.where(picked, mm, NEG_INF), axis=1, keepdims=True), m_n)
        w = jnp.where(picked, jnp.exp(mm - m_tot), 0.0)
        w_n = jnp.exp(m_n - m_tot)
        l_tot = jnp.sum(w * ll, axis=1, keepdims=True) + w_n * jnp.sum(p_n, axis=-1, keepdims=True)
        acc = w_n * _dot(p_n.astype(BF), new[:, HW:].astype(BF))
        for pg in range(n_pages):
            acc = acc + w[:, pg:pg + 1] * acc_scr[pg]
        o_ref[...] = _diag_heads(acc / jnp.maximum(l_tot, 1.0), T)


def moba_decode(sp, pools, l, page_table, row_new, B, T):
    n_pages = page_table.shape[1]
    assert MOBA_BLOCK == 2 * PAGE_SIZE and n_pages % 2 == 0 and n_pages <= LANES and T <= MOBA_BLOCK
    PG = min(DEC_PG, n_pages)
    R = NH * T
    view, base = _pool_view(pools, l, 2 * HW)
    return pl.pallas_call(
        functools.partial(_moba_decode_kernel, PG=PG, T=T, n_pages=n_pages),
        grid_spec=pltpu.PrefetchScalarGridSpec(
            num_scalar_prefetch=1, grid=(B, n_pages // PG),
            in_specs=[pl.BlockSpec((1, NH, T, HEAD_DIM), lambda b, s, pt: (0, 0, b, 0)),
                      pl.BlockSpec((T, 2 * HW), lambda b, s, pt: (b, 0))]
            + _page_specs((2 * HW, PAGE_SIZE), PG, base),
            out_specs=pl.BlockSpec((T, HW), lambda b, s, pt: (b, 0)),
            scratch_shapes=[pltpu.VMEM((R, LANES), F32), pltpu.VMEM((R, LANES), F32), pltpu.VMEM((R, LANES), F32),
                            pltpu.VMEM((n_pages, R, HW), F32)]),
        out_shape=jax.ShapeDtypeStruct((B * T, HW), F32),
        compiler_params=_cparams(("parallel", "arbitrary")),
        name="moba_decode",
    )(page_table, sp["mqr"], row_new, *([view] * PG))


def _nsa_cmp_phys_kernel(xt_ref, pe_ref, w_ref, y_ref, xs_scr, *, G):
    S = NSA_CMP_STRIDE
    for g in range(G):
        xs_scr[g * PAGE_SIZE:(g + 1) * PAGE_SIZE, :] = xt_ref[g].T
    m = G * (PAGE_SIZE // S)
    a = jnp.zeros((m, 2 * HEAD_DIM), F32)
    b = jnp.zeros((m, 2 * HEAD_DIM), F32)
    for t in range(S):
        x = xs_scr[pl.ds(t, m, stride=S), :]
        a = a + _dot((x + pe_ref[0, t:t + 1]).astype(BF), w_ref[0, t])
        b = b + _dot((x + pe_ref[1, t:t + 1]).astype(BF), w_ref[1, t])
    y_ref[...] = jnp.concatenate([a, b], axis=1)


def nsa_compress_pool(pools, l, cmp_pe, cmp_w):
    S = NSA_CMP_STRIDE
    n_phys = pools.shape[1]
    cpp = PAGE_SIZE // S
    pe = jnp.transpose(cmp_pe.reshape(2, 2, S, HEAD_DIM), (1, 2, 0, 3)).reshape(2, S, 2 * HEAD_DIM)
    w = cmp_w.reshape(2, 2, S, HEAD_DIM, HEAD_DIM)
    wz = jnp.zeros((2, S, 2, HEAD_DIM, 2, HEAD_DIM), F32)
    wz = wz.at[:, :, 0, :, 0, :].set(w[0]).at[:, :, 1, :, 1, :].set(w[1])
    wz = wz.reshape(2, S, 2 * HEAD_DIM, 2 * HEAD_DIM).astype(BF)
    G = next(g for g in (32, 16, 8, 4, 2, 1) if n_phys % g == 0)
    view, base = _pool_view(pools, l, HW)
    return pl.pallas_call(
        functools.partial(_nsa_cmp_phys_kernel, G=G),
        grid=(n_phys // G,),
        in_specs=[pl.BlockSpec((G, 2 * HEAD_DIM, PAGE_SIZE), lambda i: (base // G + i, 0, 0)),
                  pl.BlockSpec((2, S, 2 * HEAD_DIM), lambda i: (0, 0, 0)),
                  pl.BlockSpec((2, S, 2 * HEAD_DIM, 2 * HEAD_DIM), lambda i: (0, 0, 0, 0))],
        out_specs=pl.BlockSpec((G * cpp, HW), lambda i: (i, 0)),
        out_shape=jax.ShapeDtypeStruct((n_phys * cpp, HW), F32),
        scratch_shapes=[pltpu.VMEM((G * PAGE_SIZE, 2 * HEAD_DIM), F32)],
        compiler_params=_cparams(("parallel",)),
        name="nsa_compress_pool",
    )(view, pe, wz)


def _nsa_sel_decode_kernel(y_ref, q_ref, gain_ref, cover_ref, ocmp_ref, sel_ref, *, T, offset, n_cmp, n_sel):
    R = NH * T
    y = y_ref[0]
    nch = y.shape[0]
    c = y[:, :2 * HEAD_DIM] + pltpu.roll(y[:, 2 * HEAD_DIM:], nch - 1, 0)
    lane = lax.broadcasted_iota(jnp.int32, c.shape, 1)
    ms = jnp.sum(jnp.where(lane < HEAD_DIM, c * c, 0.0), axis=1, keepdims=True) / HEAD_DIM
    kn = c * lax.rsqrt(ms + RMS_EPS) * gain_ref[...]
    kc = kn[:, :HEAD_DIM].astype(BF)
    vc = c[:, HEAD_DIM:].astype(BF)
    q = q_ref[0].reshape(R, HEAD_DIM).astype(BF)
    pos = offset + lax.broadcasted_iota(jnp.int32, (R, 1), 0) % T
    n_idx = lax.broadcasted_iota(jnp.int32, (R, nch), 1)
    mask = (n_idx * NSA_CMP_STRIDE + (NSA_CMP_BLOCK - 1) <= pos) & (n_idx < n_cmp)
    s_c = jnp.where(mask, _dot_nt(q, kc), NEG_INF)
    m_c = jnp.max(s_c, axis=-1, keepdims=True)
    p_c = jnp.where(mask, jnp.exp(s_c - m_c), 0.0)
    p_c = (p_c / jnp.maximum(jnp.sum(p_c, axis=-1, keepdims=True), 1.0)).astype(BF)
    ocmp_ref[0] = _dot(p_c, vc)
    imp4 = _dot(p_c, cover_ref[...])
    imp = imp4[0:T]
    for h in range(1, NH):
        imp = imp + imp4[h * T:(h + 1) * T]
    j = lax.broadcasted_iota(jnp.int32, imp.shape, 1)
    cur = (offset + lax.broadcasted_iota(jnp.int32, (T, 1), 0)) // NSA_SEL_BLOCK
    forced = (j == 0) | (j == cur) | (j == cur - 1)
    imp = jnp.where(j > cur, NEG_INF, jnp.where(forced, SEL_FORCE, imp))
    rank = jnp.zeros(imp.shape, F32)
    for jp in range(n_sel):
        col = imp[:, jp:jp + 1]
        rank = rank + jnp.where((col > imp) | ((col == imp) & (j > jp)), 1.0, 0.0)
    sel_ref[0] = jnp.where((rank < NSA_TOPN) & (j <= cur), 1.0, 0.0)


def _nsa_decode_kernel(pt_ref, q_ref, sel_ref, ocmp_ref, newr_ref, win_ref, neww_ref, gl_ref, *rest,
                       PG, T, offset):
    page_refs, o_ref = rest[:PG], rest[PG]
    m_scr, l_scr, acc_scr = rest[PG + 1:]
    s = pl.program_id(1)
    R = NH * T
    D = HEAD_DIM

    @pl.when(s == 0)
    def _():
        m_scr[...] = jnp.full_like(m_scr, NEG_INF)
        l_scr[...] = jnp.zeros_like(l_scr)
        acc_scr[...] = jnp.zeros_like(acc_scr)

    q = q_ref[0].reshape(R, D).astype(BF)
    sel = sel_ref[0]
    jl = lax.broadcasted_iota(jnp.int32, sel.shape, 1)
    half = lax.broadcasted_iota(jnp.int32, (T, PAGE_SIZE), 1) < NSA_SEL_BLOCK
    m_i, l_i, acc = m_scr[...], l_scr[...], acc_scr[...]
    for j in range(PG):
        pid = s * PG + j
        page = page_refs[j][0]
        sc = _dot(q, page[2 * D:3 * D].astype(BF))
        sa = jnp.max(jnp.where(jl == 2 * pid, sel, 0.0), axis=1, keepdims=True)
        sb = jnp.max(jnp.where(jl == 2 * pid + 1, sel, 0.0), axis=1, keepdims=True)
        mt = jnp.where(half, sa, sb) > 0.5
        mask = jnp.broadcast_to(mt[None], (NH, T, PAGE_SIZE)).reshape(R, PAGE_SIZE)
        m_i, l_i, acc = _softmax_update(sc, mask, m_i, l_i, acc, page[3 * D:].astype(BF), R, v_feature_major=True)
    m_scr[...] = m_i
    l_scr[...] = l_i
    acc_scr[...] = acc

    @pl.when(s == pl.num_programs(1) - 1)
    def _():
        causal = _new_causal(R, T)
        newr = newr_ref[...]
        cb = offset // NSA_SEL_BLOCK
        own = jnp.broadcast_to((sel[:, cb:cb + 1] > 0.5)[None], (NH, T, T)).reshape(R, T)
        _, l_s, acc_s = _softmax_update(_dot_nt(q, newr[:, 2 * D:3 * D].astype(BF)), causal & own, m_i, l_i, acc,
                                        newr[:, 3 * D:].astype(BF), R)
        o_sel = acc_s / jnp.maximum(l_s, 1.0)
        win = win_ref[0]
        neww = neww_ref[...]
        wb = win.shape[1]
        tq = lax.broadcasted_iota(jnp.int32, (R, wb), 0) % T
        rk = lax.broadcasted_iota(jnp.int32, (R, wb), 1)
        mask1 = rk > wb + tq - NSA_WINDOW
        s1 = jnp.where(mask1, _dot(q, win[:D].astype(BF)), NEG_INF)
        s2 = jnp.where(causal, _dot_nt(q, neww[:, :D].astype(BF)), NEG_INF)
        m_w = jnp.maximum(jnp.max(s1, axis=-1, keepdims=True), jnp.max(s2, axis=-1, keepdims=True))
        p1 = jnp.where(mask1, jnp.exp(s1 - m_w), 0.0)
        p2 = jnp.where(causal, jnp.exp(s2 - m_w), 0.0)
        l_w = jnp.sum(p1, axis=-1, keepdims=True) + jnp.sum(p2, axis=-1, keepdims=True)
        o_win = (_dot_nt(p1.astype(BF), win[D:].astype(BF)) + _dot(p2.astype(BF), neww[:, D:].astype(BF))) \
            / jnp.maximum(l_w, 1.0)
        o_cmp = ocmp_ref[0]
        sig = jax.nn.sigmoid(gl_ref[...])
        outs = []
        for h in range(NH):
            c = MISC_LANE + 3 * h
            rs = slice(h * T, (h + 1) * T)
            outs.append(sig[:, c:c + 1] * o_cmp[rs] + sig[:, c + 1:c + 2] * o_sel[rs] + sig[:, c + 2:c + 3] * o_win[rs])
        o_ref[...] = jnp.concatenate(outs, axis=1)


def nsa_decode(sp, z, row0, pools, l, page_table, win_state, cmp_pe, cmp_w, gain1, B, T):
    n_pages = page_table.shape[1]
    past = n_pages * PAGE_SIZE
    S = past + T
    R = NH * T
    n_cmp = (S - NSA_CMP_BLOCK) // NSA_CMP_STRIDE + 1
    n_sel = -(-S // NSA_SEL_BLOCK)
    nch = past // NSA_CMP_STRIDE
    assert (n_cmp - 1) * NSA_CMP_STRIDE + NSA_CMP_BLOCK <= past, "compressed blocks must lie inside the cache"
    assert past % NSA_SEL_BLOCK == 0 and T <= NSA_SEL_BLOCK and n_sel <= HW
    PG = min(DEC_PG, n_pages)
    cpp = PAGE_SIZE // NSA_CMP_STRIDE
    y = nsa_compress_pool(pools, l, cmp_pe, cmp_w).reshape(pools.shape[1], cpp, HW)
    yb = y[page_table].reshape(B, nch, HW)
    ci = np.arange(nch)[:, None] * NSA_CMP_STRIDE
    sj = np.arange(HW)[None, :] * NSA_SEL_BLOCK
    cover = (ci < sj + NSA_SEL_BLOCK) & (ci + NSA_CMP_BLOCK > sj) & (np.arange(nch)[:, None] < n_cmp) \
        & (np.arange(HW)[None, :] < n_sel)
    cover = jnp.asarray(cover.astype(np.float32), dtype=BF)
    gain = jnp.concatenate([gain1, jnp.ones((HEAD_DIM,), F32)])[None, :]
    qspec = lambda nargs: pl.BlockSpec((1, NH, T, HEAD_DIM), (lambda b: (0, 0, b, 0)) if nargs == 1
                                       else (lambda b, s, pt: (0, 0, b, 0)))
    ocmp, sel = pl.pallas_call(
        functools.partial(_nsa_sel_decode_kernel, T=T, offset=past, n_cmp=n_cmp, n_sel=n_sel),
        grid=(B,),
        in_specs=[pl.BlockSpec((1, nch, HW), lambda b: (b, 0, 0)), qspec(1),
                  pl.BlockSpec((1, 2 * HEAD_DIM), lambda b: (0, 0)),
                  pl.BlockSpec((nch, HW), lambda b: (0, 0))],
        out_specs=[pl.BlockSpec((1, R, HEAD_DIM), lambda b: (b, 0, 0)), pl.BlockSpec((1, T, HW), lambda b: (b, 0, 0))],
        out_shape=[jax.ShapeDtypeStruct((B, R, HEAD_DIM), F32), jax.ShapeDtypeStruct((B, T, HW), F32)],
        compiler_params=_cparams(("parallel",)),
        name="nsa_select_decode",
    )(yb, sp["nqn"], gain, cover)
    view, base = _pool_view(pools, l, HW)
    wb = win_state.shape[1]
    rb0 = row0 // T
    return pl.pallas_call(
        functools.partial(_nsa_decode_kernel, PG=PG, T=T, offset=past),
        grid_spec=pltpu.PrefetchScalarGridSpec(
            num_scalar_prefetch=1, grid=(B, n_pages // PG),
            in_specs=[qspec(3),
                      pl.BlockSpec((1, T, HW), lambda b, s, pt: (b, 0, 0)),
                      pl.BlockSpec((1, R, HEAD_DIM), lambda b, s, pt: (b, 0, 0)),
                      pl.BlockSpec((T, HW), lambda b, s, pt: (b, 0)),
                      pl.BlockSpec((1, 2 * HEAD_DIM, wb), lambda b, s, pt: (b, 0, 0)),
                      pl.BlockSpec((T, 2 * HEAD_DIM), lambda b, s, pt: (b, 0)),
                      pl.BlockSpec((T, HW), lambda b, s, pt: (rb0 + b, SEC_S2))]
            + _page_specs((HW, PAGE_SIZE), PG, base),
            out_specs=pl.BlockSpec((T, HW), lambda b, s, pt: (b, 0)),
            scratch_shapes=[pltpu.VMEM((R, 1), F32), pltpu.VMEM((R, 1), F32), pltpu.VMEM((R, HEAD_DIM), F32)]),
        out_shape=jax.ShapeDtypeStruct((B * T, HW), F32),
        compiler_params=_cparams(("parallel", "arbitrary")),
        name="nsa_decode",
    )(page_table, sp["nqr"], sel, ocmp, sp["nsa_rows"], jnp.transpose(win_state, (0, 2, 3, 1)).reshape(B, 2 * HEAD_DIM, wb), sp["nsa_win"], z,
      *([view] * PG))


def rmsnorm(x, g):
    xf = x.astype(jnp.float32)
    y = xf * lax.rsqrt(jnp.mean(xf * xf, axis=-1, keepdims=True) + RMS_EPS)
    return (y * g.astype(jnp.float32)).astype(x.dtype)


def rope(x, pos):
    half = HEAD_DIM // 2
    inv = ROPE_THETA ** (-jnp.arange(half, dtype=jnp.float32) / half)
    ang = pos.astype(jnp.float32)[:, None] * inv[None, :]
    cos = jnp.cos(ang)[:, None, :]
    sin = jnp.sin(ang)[:, None, :]
    xf = x.astype(jnp.float32)
    x1, x2 = xf[..., :half], xf[..., half:]
    return jnp.concatenate([x1 * cos - x2 * sin, x2 * cos + x1 * sin], axis=-1).astype(x.dtype)


def masked_softmax(s, mask):
    s = jnp.where(mask, s.astype(jnp.float32), NEG_INF)
    m = jnp.max(s, axis=-1, keepdims=True)
    p = jnp.where(mask, jnp.exp(s - m), 0.0)
    return p / jnp.maximum(jnp.sum(p, axis=-1, keepdims=True), 1.0)


def sweep(fn, blk, *arrays):
    B, T = arrays[0].shape[:2]
    nb = -(-T // blk)
    Tp = nb * blk
    blocks = []
    for a in arrays:
        a = jnp.pad(a, [(0, 0), (0, Tp - T)] + [(0, 0)] * (a.ndim - 2))
        blocks.append(jnp.moveaxis(a.reshape((B, nb, blk) + a.shape[2:]), 1, 0))
    starts = jnp.arange(nb, dtype=jnp.int32) * blk
    out = lax.map(lambda args: fn(args[0], *args[1]), (starts, tuple(blocks)))
    out = jnp.moveaxis(out, 0, 1).reshape((B, Tp) + out.shape[3:])
    return out[:, :T]


def window_attend(q, rows, buf_len):
    B, T, H, Dh = q.shape
    dt = q.dtype
    W = NSA_WINDOW
    qb = min(Q_BLOCK, T)
    nb = -(-T // qb)
    Tp = nb * qb
    band = W + qb
    rp = jnp.pad(rows, ((0, 0), (W, Tp - T), (0, 0), (0, 0)))
    kidx = buf_len + np.arange(nb)[:, None] * qb + np.arange(band)[None, :]
    kb = rp[:, kidx]
    qp = jnp.pad(q, ((0, 0), (0, Tp - T), (0, 0), (0, 0))).reshape(B, nb, qb, H, Dh)
    s = jnp.einsum('bnqhd,bnkd->bhnqk', qp, kb[..., 0, :], preferred_element_type=jnp.float32) * ATTN_SCALE
    qq = buf_len + np.arange(nb)[:, None] * qb + np.arange(qb)[None, :]
    kk = kidx - W
    mask = (kk[:, None, :] >= 0) & (kk[:, None, :] <= qq[:, :, None]) & (kk[:, None, :] > qq[:, :, None] - W)
    p = masked_softmax(s, mask)
    o = jnp.einsum('bhnqk,bnkd->bnqhd', p.astype(dt), kb[..., 1, :]).reshape(B, Tp, H, Dh)
    return o[:, :T]


def nsa_mixer(q, kv_c, kv_s, kv_w, gate_logits, past_rows, win_buf, qk_gain, cmp_pe, cmp_w, offset):
    B, T, H, Dh = q.shape
    dt = q.dtype
    pos = offset + jnp.arange(T, dtype=jnp.int32)
    qn = rmsnorm(q, qk_gain[0])
    qr = rope(qn, pos)
    k_s = rope(rmsnorm(kv_s[:, :, 0:1], qk_gain[2]), pos)[:, :, 0]
    k_w = rope(rmsnorm(kv_w[:, :, 0:1], qk_gain[3]), pos)[:, :, 0]
    new_rows = jnp.stack([kv_c[:, :, 0], kv_c[:, :, 1], k_s, kv_s[:, :, 1]], axis=2)
    rows = jnp.concatenate([past_rows, new_rows], axis=1)
    S = offset + T
    n_cmp = (S - NSA_CMP_BLOCK) // NSA_CMP_STRIDE + 1
    cidx = np.arange(n_cmp)[:, None] * NSA_CMP_STRIDE + np.arange(NSA_CMP_BLOCK)[None, :]

    def compress(r):
        blocks = rows[:, :, r][:, cidx] + cmp_pe[r]
        return blocks.reshape(B, n_cmp, NSA_CMP_BLOCK * Dh) @ cmp_w[r]

    k_cmp = rmsnorm(compress(0), qk_gain[1])
    v_cmp = compress(1)
    s_c = jnp.einsum('bthd,bnd->bhtn', qn, k_cmp, preferred_element_type=jnp.float32) * ATTN_SCALE
    cmp_end = np.arange(n_cmp) * NSA_CMP_STRIDE + NSA_CMP_BLOCK - 1
    p_c = masked_softmax(s_c, cmp_end[None, :] <= pos[:, None])
    o_cmp = jnp.einsum('bhtn,bnd->bthd', p_c.astype(dt), v_cmp)
    n_sel = -(-S // NSA_SEL_BLOCK)
    ci = np.arange(n_cmp)[:, None] * NSA_CMP_STRIDE
    sj = np.arange(n_sel)[None, :] * NSA_SEL_BLOCK
    cover = ((ci < sj + NSA_SEL_BLOCK) & (ci + NSA_CMP_BLOCK > sj)).astype(np.float32)
    imp = jnp.einsum('bhtn,nj->btj', p_c, jnp.asarray(cover))
    cur = (pos // NSA_SEL_BLOCK)[:, None]
    jj = jnp.arange(n_sel)[None, :]
    forced = (jj == 0) | (jj == cur) | (jj == cur - 1)
    imp = jnp.where(jj > cur, NEG_INF, jnp.where(forced, SEL_FORCE, imp))
    _, sel_idx = lax.top_k(imp, min(NSA_TOPN, n_sel))
    kv_sel = jnp.pad(rows[:, :, 2:4], ((0, 0), (0, n_sel * NSA_SEL_BLOCK - S), (0, 0), (0, 0)))
    kv_sel = kv_sel.reshape(B, n_sel, NSA_SEL_BLOCK, 2, Dh)
    bidx = jnp.arange(B)[:, None, None]

    def sel_block(start, q_blk, idx_blk):
        qb = q_blk.shape[1]
        tp = offset + start + jnp.arange(qb)
        g = kv_sel[bidx, idx_blk]
        kpos = idx_blk[..., None] * NSA_SEL_BLOCK + jnp.arange(NSA_SEL_BLOCK)
        mask = (kpos <= tp[None, :, None, None]).reshape(B, 1, qb, -1)
        g = g.reshape(B, qb, -1, 2, Dh)
        s = jnp.einsum('bqhd,bqkd->bhqk', q_blk, g[..., 0, :], preferred_element_type=jnp.float32) * ATTN_SCALE
        p = masked_softmax(s, mask)
        return jnp.einsum('bhqk,bqkd->bqhd', p.astype(dt), g[..., 1, :])

    o_sel = sweep(sel_block, min(GATHER_Q_BLOCK, T), qr, sel_idx)
    win_rows = jnp.concatenate([win_buf, jnp.stack([k_w, kv_w[:, :, 1]], axis=2)], axis=1)
    o_win = window_attend(qr, win_rows, win_buf.shape[1])
    gates = jax.nn.sigmoid(gate_logits.astype(jnp.float32)).astype(dt)
    o = gates[..., 0:1] * o_cmp + gates[..., 1:2] * o_sel + gates[..., 2:3] * o_win
    new_win = win_rows[:, -min(NSA_WINDOW, win_rows.shape[1]):]
    return o, new_rows, new_win


def gated_recurrence(q, k, v, logf, S0):
    B, T, H, DK = q.shape
    DV = v.shape[-1]
    C = math.gcd(T, HGRN_CHUNK)
    nc = T // C

    def chunks(a):
        return jnp.moveaxis(a.reshape((B, nc, C) + a.shape[2:]), 1, 0).swapaxes(2, 3)

    causal = jnp.tril(jnp.ones((C, C), dtype=bool))[:, :, None]

    def step(S, inp):
        qc, kc, vc, gc = inp
        b = jnp.cumsum(gc, axis=2)
        o_inter = jnp.einsum('bhtk,bhkv->bhtv', qc * jnp.exp(b), S)
        diff = b[:, :, :, None, :] - b[:, :, None, :, :]
        decay = jnp.where(causal, jnp.exp(jnp.where(causal, diff, 0.0)), 0.0)
        A = jnp.einsum('bhtk,bhsk,bhtsk->bhts', qc, kc, decay)
        o = o_inter + jnp.einsum('bhts,bhsv->bhtv', A, vc)
        b_last = b[:, :, -1:, :]
        S_new = jnp.exp(b_last[:, :, 0, :])[..., None] * S + jnp.einsum('bhsk,bhsv->bhkv', kc * jnp.exp(b_last - b), vc)
        return S_new, o

    S, o = lax.scan(step, S0, (chunks(q), chunks(k), chunks(v), chunks(logf)))
    o = jnp.moveaxis(o.swapaxes(2, 3), 0, 1).reshape(B, T, H, DV)
    return o, S


def hgrn2_mixer(q, f, i, g, S0, lb, out_gain):
    dt = q.dtype
    H = q.shape[2]
    lb = lb.reshape(H, HGRN_DK)
    z = f.astype(jnp.float32)
    logf = jnp.log(lb + (1.0 - lb) * jax.nn.sigmoid(z))
    k = (1.0 - lb) * jax.nn.sigmoid(-z)
    qf = jax.nn.silu(q.astype(jnp.float32))
    o, S = gated_recurrence(qf, k, i.astype(jnp.float32), logf, S0)
    o = rmsnorm(o, out_gain) * jax.nn.silu(g.astype(jnp.float32))
    return o.astype(dt), S


def moba_mixer(q, k, v, past_rows, qk_gain, offset):
    B, T, H, Dh = q.shape
    dt = q.dtype
    pos = offset + jnp.arange(T, dtype=jnp.int32)
    qr = rope(rmsnorm(q, qk_gain[0]), pos)
    kr = rope(rmsnorm(k, qk_gain[1]), pos)
    new_rows = jnp.stack([kr, v], axis=2)
    rows = jnp.concatenate([past_rows, new_rows], axis=1)
    S = offset + T
    nblk = -(-S // MOBA_BLOCK)
    kvb = jnp.pad(rows, ((0, 0), (0, nblk * MOBA_BLOCK - S), (0, 0), (0, 0), (0, 0)))
    kvb = jnp.transpose(kvb.reshape(B, nblk, MOBA_BLOCK, 2, H, Dh), (0, 4, 1, 2, 3, 5))
    kmean = jnp.mean(kvb[..., 0, :].astype(jnp.float32), axis=3)
    gate = jnp.einsum('bthd,bhnd->bthn', qr.astype(jnp.float32), kmean)
    own = pos // MOBA_BLOCK
    past_ok = jnp.arange(nblk)[None, :] < own[:, None]
    _, top = lax.top_k(jnp.where(past_ok[None, :, None, :], gate, NEG_INF), min(MOBA_TOPK, nblk))
    valid = top < own[None, :, None, None]
    idx = jnp.concatenate([top, jnp.broadcast_to(own[None, :, None, None], (B, T, H, 1)).astype(top.dtype)], axis=-1)
    ok = jnp.concatenate([valid, jnp.ones((B, T, H, 1), dtype=bool)], axis=-1)
    bidx = jnp.arange(B)[:, None, None, None]
    hidx = jnp.arange(H)[None, None, :, None]

    def blk_fn(start, q_blk, idx_blk, ok_blk):
        qb = q_blk.shape[1]
        tp = offset + start + jnp.arange(qb)
        g = kvb[bidx, hidx, idx_blk]
        kpos = idx_blk[..., None] * MOBA_BLOCK + jnp.arange(MOBA_BLOCK)
        mask = (ok_blk[..., None] & (kpos <= tp[None, :, None, None, None])).reshape(B, qb, H, -1)
        g = g.reshape(B, qb, H, -1, 2, Dh)
        s = jnp.einsum('bqhd,bqhkd->bqhk', q_blk, g[..., 0, :], preferred_element_type=jnp.float32) * ATTN_SCALE
        p = masked_softmax(s, mask)
        return jnp.einsum('bqhk,bqhkd->bqhd', p.astype(dt), g[..., 1, :])

    o = sweep(blk_fn, min(GATHER_Q_BLOCK, T), qr, idx, ok)
    return o, new_rows


def fox_mixer(q, k, v, f_logit, past_kv, past_logf, qk_gain, f_bias, offset):
    B, T, H, Dh = q.shape
    dt = q.dtype
    qn = rmsnorm(q, qk_gain[0])
    kn = rmsnorm(k, qk_gain[1])
    logf_new = jax.nn.log_sigmoid(f_logit.astype(jnp.float32) + f_bias.astype(jnp.float32))
    new_rows = jnp.stack([kn, v], axis=2)
    rows = jnp.concatenate([past_kv, new_rows], axis=1)
    c = jnp.cumsum(jnp.concatenate([past_logf.astype(jnp.float32), logf_new], axis=1), axis=1)
    S = offset + T
    K = rows[:, :, 0]
    V = rows[:, :, 1]
    c_k = jnp.moveaxis(c, 1, 2)[:, :, None, :]
    kpos = jnp.arange(S)

    def blk_fn(start, q_blk, cq_blk):
        qb = q_blk.shape[1]
        tp = offset + start + jnp.arange(qb)
        s = jnp.einsum('bqhd,bkhd->bhqk', q_blk, K, preferred_element_type=jnp.float32) * ATTN_SCALE
        s = s + jnp.moveaxis(cq_blk, 1, 2)[..., None] - c_k
        p = masked_softmax(s, kpos[None, :] <= tp[:, None])
        return jnp.einsum('bhqk,bkhd->bqhd', p.astype(dt), V)

    o = sweep(blk_fn, min(Q_BLOCK, T), qn, c[:, offset:])
    return o, new_rows, logf_new.astype(dt)


def _outproj_kernel(x_ref, pn_ref, ph_ref, pm_ref, pf_ref, sn_ref, sh_ref, sm_ref, sf_ref, w_ref, g_ref,
                    xo_ref, hn_ref, *, n_prompt_blocks):
    i = pl.program_id(0)

    def project(parts):
        acc = x_ref[...]
        for m, part in enumerate(parts):
            acc = acc + _dot(part[...].astype(BF), w_ref[m * HW:(m + 1) * HW, :])
        xo_ref[...] = acc
        hn = acc * lax.rsqrt(jnp.mean(acc * acc, axis=-1, keepdims=True) + RMS_EPS) * g_ref[...]
        hn_ref[...] = hn.astype(hn_ref.dtype)

    pl.when(i < n_prompt_blocks)(lambda: project((pn_ref, ph_ref, pm_ref, pf_ref)))
    pl.when(i >= n_prompt_blocks)(lambda: project((sn_ref, sh_ref, sm_ref, sf_ref)))


def out_projection(x, prompt_parts, sample_parts, w_out, g):
    N, D = x.shape
    tm = PREP_TQ
    npb = prompt_parts[0].shape[0] // tm
    assert sample_parts[0].shape[0] == tm and N == (npb + 1) * tm
    pspec = pl.BlockSpec((tm, HW), lambda i: (jnp.minimum(i, npb - 1), 0))
    sspec = pl.BlockSpec((tm, HW), lambda i: (0, 0))
    row = pl.BlockSpec((tm, D), lambda i: (i, 0))
    return pl.pallas_call(
        functools.partial(_outproj_kernel, n_prompt_blocks=npb),
        grid=(npb + 1,),
        in_specs=[row] + [pspec] * 4 + [sspec] * 4 + [pl.BlockSpec((MIX_WIDTH, D), lambda i: (0, 0)),
                                                     pl.BlockSpec((1, D), lambda i: (0, 0))],
        out_specs=[row, row],
        out_shape=[jax.ShapeDtypeStruct((N, D), F32), jax.ShapeDtypeStruct((N, D), BF)],
        compiler_params=_cparams(("parallel",)),
        name="out_projection_rmsnorm",
    )(x, *prompt_parts, *sample_parts, w_out.astype(BF), g[None, :])


def _ffn_up_kernel(be_ref, new_ref, x_ref, w1_ref, w3_ref, u_ref, w1_scr, w3_scr):
    i = pl.program_id(1)

    @pl.when(new_ref[i] == 1)
    def _():
        w1_scr[...] = w1_ref[0].astype(BF)
        w3_scr[...] = w3_ref[0].astype(BF)

    x = x_ref[...].astype(BF)
    a = _dot(x, w1_scr[...])
    b = _dot(x, w3_scr[...])
    u_ref[...] = (a * jax.nn.sigmoid(a) * b).astype(u_ref.dtype)


def _ffn_down_kernel(be_ref, new_ref, u_ref, w2_ref, *rest, residual):
    res_ref = rest[0] if residual else None
    y_ref, w2_scr = rest[-2], rest[-1]
    i = pl.program_id(1)

    @pl.when(new_ref[i] == 1)
    def _():
        w2_scr[...] = w2_ref[0].astype(BF)

    y = _dot(u_ref[...], w2_scr[...])
    y_ref[...] = res_ref[...] + y if residual else y


def grouped_swiglu(x, block_exp, w1, w3, w2, tm, tf, tn, residual=None):
    R, D = x.shape
    F = w1.shape[2]
    nblk = R // tm
    block_exp = block_exp.astype(jnp.int32)
    new = jnp.concatenate([jnp.ones((1,), jnp.int32), (block_exp[1:] != block_exp[:-1]).astype(jnp.int32)])
    u = pl.pallas_call(
        _ffn_up_kernel,
        grid_spec=pltpu.PrefetchScalarGridSpec(
            num_scalar_prefetch=2, grid=(F // tf, nblk),
            in_specs=[pl.BlockSpec((tm, D), lambda j, i, be, nw: (i, 0)),
                      pl.BlockSpec((1, D, tf), lambda j, i, be, nw: (be[i], 0, j)),
                      pl.BlockSpec((1, D, tf), lambda j, i, be, nw: (be[i], 0, j))],
            out_specs=pl.BlockSpec((tm, tf), lambda j, i, be, nw: (i, j)),
            scratch_shapes=[pltpu.VMEM((D, tf), BF), pltpu.VMEM((D, tf), BF)]),
        out_shape=jax.ShapeDtypeStruct((R, F), BF),
        compiler_params=_cparams(("arbitrary", "arbitrary")),
        name="swiglu_up",
    )(block_exp, new, x, w1, w3)
    out_block = pl.BlockSpec((tm, tn), lambda n, i, be, nw: (i, n))
    extra = () if residual is None else (residual,)
    return pl.pallas_call(
        functools.partial(_ffn_down_kernel, residual=residual is not None),
        grid_spec=pltpu.PrefetchScalarGridSpec(
            num_scalar_prefetch=2, grid=(D // tn, nblk),
            in_specs=[pl.BlockSpec((tm, F), lambda n, i, be, nw: (i, 0)),
                      pl.BlockSpec((1, F, tn), lambda n, i, be, nw: (be[i], 0, n))] + [out_block] * len(extra),
            out_specs=out_block,
            scratch_shapes=[pltpu.VMEM((F, tn), BF)]),
        out_shape=jax.ShapeDtypeStruct((R, D), F32),
        compiler_params=_cparams(("arbitrary", "arbitrary")),
        name="swiglu_down",
    )(block_exp, new, u, w2, *extra)


def swiglu_dense(hn, x, w1, w3, w2):
    tm = 640 if hn.shape[0] % 640 == 0 else 256
    be = jnp.zeros((hn.shape[0] // tm,), jnp.int32)
    return grouped_swiglu(hn, be, w1[None], w3[None], w2[None], tm, DENSE_TF, FFN_TN, residual=x)


def moe_ffn_grouped(xf, router, w1, w3, w2):
    N, D = xf.shape
    tm = MOE_TM
    rpad = jnp.pad(router, ((0, 0), (0, LANES - N_EXPERTS)))
    logits = matmul(xf, rpad, tm=256, tn=LANES)[:, :N_EXPERTS]
    top_v, top_e = lax.top_k(logits, TOP_K)
    gates = jax.nn.softmax(top_v, axis=-1)
    NK = N * TOP_K
    flat_e = top_e.reshape(NK)
    order = jnp.argsort(flat_e)
    e_sorted = flat_e[order]
    tok_sorted = (order // TOP_K).astype(jnp.int32)
    counts = jnp.sum((flat_e[:, None] == jnp.arange(N_EXPERTS)[None, :]).astype(jnp.int32), axis=0)
    padded = (counts + tm - 1) // tm * tm
    pend = jnp.cumsum(padded)
    pstart = pend - padded
    start = jnp.cumsum(counts) - counts
    dest_sorted = pstart[e_sorted] + (jnp.arange(NK, dtype=jnp.int32) - start[e_sorted])
    n_blocks = -(-NK // tm) + N_EXPERTS
    slot_tok = jnp.full((n_blocks * tm,), N, jnp.int32).at[dest_sorted].set(tok_sorted)
    block_exp = jnp.clip(jnp.searchsorted(pend, jnp.arange(n_blocks) * tm, side='right'), 0, N_EXPERTS - 1)
    xpad = jnp.concatenate([xf, jnp.zeros((1, D), xf.dtype)], axis=0)
    xb = xpad[slot_tok]
    yb = grouped_swiglu(xb, block_exp, w1, w3, w2, tm, MOE_TF, FFN_TN)
    dest = jnp.zeros((NK,), jnp.int32).at[order].set(dest_sorted).reshape(N, TOP_K)
    return yb[dest[:, 0]] * gates[:, 0:1] + yb[dest[:, 1]] * gates[:, 1:2]


def swiglu(h, w1, w3, w2):
    return (jax.nn.silu(h @ w1) * (h @ w3)) @ w2


def moe_ffn(xf, router, w1, w3, w2):
    N, D = xf.shape
    dt = xf.dtype
    logits = (xf @ router).astype(jnp.float32)
    top_v, top_e = lax.top_k(logits, TOP_K)
    gates = jax.nn.softmax(top_v, axis=-1)
    NK = N * TOP_K
    flat_e = top_e.reshape(NK)
    flat_tok = jnp.arange(NK, dtype=jnp.int32) // TOP_K
    order = jnp.argsort(flat_e)
    e_sorted = flat_e[order]
    tok_sorted = flat_tok[order]
    counts = jnp.zeros((N_EXPERTS,), jnp.int32).at[flat_e].add(1)
    padded = (counts + MOE_BLOCK - 1) // MOE_BLOCK * MOE_BLOCK
    pend = jnp.cumsum(padded)
    pstart = pend - padded
    start = jnp.cumsum(counts) - counts
    dest = pstart[e_sorted] + (jnp.arange(NK, dtype=jnp.int32) - start[e_sorted])
    n_blocks = -(-NK // MOE_BLOCK) + N_EXPERTS
    slot_tok = jnp.full((n_blocks * MOE_BLOCK,), N, jnp.int32).at[dest].set(tok_sorted)
    block_exp = jnp.clip(jnp.searchsorted(pend, jnp.arange(n_blocks) * MOE_BLOCK, side='right'), 0, N_EXPERTS - 1)
    xpad = jnp.concatenate([xf, jnp.zeros((1, D), dt)], axis=0)
    xb = xpad[slot_tok].reshape(n_blocks, MOE_BLOCK, D)

    def expert_block(args):
        xblk, e = args
        return swiglu(xblk, w1[e], w3[e], w2[e])

    yb = lax.map(expert_block, (xb, block_exp)).reshape(n_blocks * MOE_BLOCK, D)
    y_assign = yb[dest] * gates.reshape(NK)[order][:, None].astype(dt)
    return jnp.zeros((N, D), dt).at[tok_sorted].add(y_assign)


def z_sections(z):
    s = lambda c, a=0, b=HW: z[..., c * HW + a:c * HW + b]
    d = HEAD_DIM
    return dict(nq=s(SEC_NQ), nkc=s(SEC_S1, 0, 2 * d), nks=s(SEC_S1, 2 * d, 4 * d), nkw=s(SEC_S2, 0, 2 * d),
                ngate=s(SEC_S2, MISC_LANE, MISC_LANE + 12), ff=s(SEC_S2, MISC_LANE + 12, MISC_LANE + 16),
                hq=s(SEC_HQ), hf=s(SEC_HF), hi=s(SEC_HI), hg=s(SEC_HG), mq=s(SEC_MQ), mk=s(SEC_MK), mv=s(SEC_MV),
                fq=s(SEC_FQ), fk=s(SEC_FK), fv=s(SEC_FV))


def kernel(x_prompt, x_sample, cache_nsa, state_nsa_win, state_hgrn, cache_moba, cache_fox_kv, cache_fox_logf,
           page_table, g_mix, g_ffn, w_in, w_out, nsa_qk_gain, nsa_cmp_pe, nsa_cmp_w, hgrn_lb_logits,
           hgrn_out_gain, moba_qk_gain, fox_qk_gain, fox_f_bias, ffn_w1, ffn_w3, ffn_w2, moe_router,
           moe_w1, moe_w3, moe_w2):
    dt = x_prompt.dtype
    Bp, Tp, D = x_prompt.shape
    Bs, Ts, _ = x_sample.shape
    Np, Ns = Bp * Tp, Bs * Ts
    past_len = page_table.shape[1] * PAGE_SIZE
    lb_w = jax.nn.softmax(hgrn_lb_logits.astype(jnp.float32), axis=0)
    lower_bounds = jnp.cumsum(lb_w, axis=0) - lb_w[0:1]

    def gather_pages(pool):
        g = pool[page_table]
        return g.reshape((Bs, past_len) + pool.shape[2:])

    cos_p, sin_p = rope_tables(jnp.arange(Tp, dtype=jnp.int32))
    cos_s, sin_s = rope_tables(past_len + jnp.arange(Ns, dtype=jnp.int32) % Ts)
    assert Ns == PREP_TQ and Np % PREP_TQ == 0
    gmat = group_mean_matrix()
    cover = nsa_constants(Tp)

    x = jnp.concatenate([x_prompt.reshape(Np, D), x_sample.reshape(Ns, D)], axis=0)
    st_p, st_s = [], []
    for l in range(DEPTH):
        i = l // 2
        z = in_projection(x, g_mix[l][None, :], relayout_w_in(w_in[l]))

        gains = head_gains(nsa_qk_gain[l], moba_qk_gain[l], fox_qk_gain[l])
        pp = prep_prompt(z, 0, Bp, Tp, cos_p, sin_p, gains, gmat)
        pe_flat, w_flat, cgain = nsa_compress_weights(nsa_cmp_pe[l], nsa_cmp_w[l], nsa_qk_gain[l][1])
        kc, vc = nsa_compress(pp["nsa_kc"], Bp, Tp, pe_flat, w_flat, cgain)
        o_nsa_p = nsa_attention_prompt(pp, kc, vc, z, 0, Bp, Tp, cover)
        o_mb_p = moba_attention_prompt(pp, Bp, Tp)
        ff_lo = SEC_S2 * HW + MISC_LANE + 12
        ff_p = z[:Np, ff_lo:ff_lo + NH].reshape(Bp, Tp, NH)
        logf_p = jax.nn.log_sigmoid(ff_p + fox_f_bias[l].astype(F32))
        o_fx_p = fox_attention_prompt(pp, jnp.cumsum(logf_p, axis=1), Bp, Tp)
        hd = lambda a, n, d, B, T: a.reshape(B, T, n, d)
        o_hg_p, hg_state_p = hgrn_mixer(z, 0, Bp, Tp, jnp.zeros((Bp, NH, HGRN_DK, HGRN_DV), F32), lower_bounds[l],
                                        hgrn_out_gain[l], gmat)
        nsa_win_p = pp["nsa_win"].reshape(Bp, Tp, 2, HEAD_DIM)[:, -min(NSA_WINDOW, Tp):]
        st_p.append((pp["nsa_rows"].reshape(Bp, Tp, NSA_ROWS, HEAD_DIM), nsa_win_p, hg_state_p.astype(dt),
                     pp["moba_rows"].reshape(Bp, Tp, 2, NH, HEAD_DIM), pp["fox_rows"].reshape(Bp, Tp, 2, NH, HEAD_DIM),
                     logf_p.astype(dt)))

        sp = prep_prompt(z, Np, 1, Ns, cos_s, sin_s, gains, gmat, qdt=F32)
        o_nsa_s = nsa_decode(sp, z, Np, cache_nsa, l, page_table, state_nsa_win[l], nsa_cmp_pe[l], nsa_cmp_w[l],
                             nsa_qk_gain[l][1], Bs, Ts)
        o_hg_s, hg_state = hgrn_mixer(z, Np, Bs, Ts, state_hgrn[l].astype(F32), lower_bounds[l], hgrn_out_gain[l], gmat)
        o_mb_s = moba_decode(sp, cache_moba, l, page_table, sp["moba_rows"], Bs, Ts)
        ff_s = z[Np:, SEC_S2 * HW + MISC_LANE + 12:SEC_S2 * HW + MISC_LANE + 16].reshape(Bs, Ts, NH)
        logf_s = jax.nn.log_sigmoid(ff_s + fox_f_bias[l].astype(F32))
        c_s = jnp.cumsum(jnp.concatenate([gather_pages(cache_fox_logf[l]).astype(F32), logf_s], axis=1), axis=1)
        o_fx_s = fox_decode(sp, cache_fox_kv, l, page_table, c_s, sp["fox_rows"], Bs, Ts)
        win_rows = jnp.concatenate([state_nsa_win[l], sp["nsa_win"].reshape(Bs, Ts, 2, HEAD_DIM)], axis=1)
        st_s.append((sp["nsa_rows"].reshape(Bs, Ts, NSA_ROWS, HEAD_DIM),
                     win_rows[:, -min(NSA_WINDOW, win_rows.shape[1]):], hg_state.astype(dt),
                     sp["moba_rows"].reshape(Bs, Ts, 2, NH, HEAD_DIM), sp["fox_rows"].reshape(Bs, Ts, 2, NH, HEAD_DIM),
                     logf_s.astype(dt)))

        x, hn = out_projection(x, (o_nsa_p, o_hg_p, o_mb_p, o_fx_p), (o_nsa_s, o_hg_s, o_mb_s, o_fx_s),
                               w_out[l], g_ffn[l])
        if l % 2 == 0:
            x = swiglu_dense(hn, x, ffn_w1[i], ffn_w3[i], ffn_w2[i])
        else:
            x = x + moe_ffn_grouped(hn, moe_router[i], moe_w1[i], moe_w3[i], moe_w2[i])

    def stk(states, j):
        return jnp.stack([s[j] for s in states], axis=0)

    return (x[:Np].reshape(Bp, Tp, D), x[Np:].reshape(Bs, Ts, D),
            stk(st_p, 0), stk(st_s, 0), stk(st_p, 1), stk(st_s, 1), stk(st_p, 2), stk(st_s, 2),
            stk(st_p, 3), stk(st_s, 3), stk(st_p, 4), stk(st_s, 4), stk(st_p, 5), stk(st_s, 5))
```

```python
import math, functools
import jax, jax.numpy as jnp
from jax import lax
import numpy as np
from jax.experimental import pallas as pl
from jax.experimental.pallas import tpu as pltpu

D_MODEL = 1024
DEPTH = 2
PAGE_SIZE = 128
HEAD_DIM = 64
H_NSA = 4
H_HGRN = 4
H_MOBA = 4
H_FOX = 4
NH = 4
HW = NH * HEAD_DIM
MIX_WIDTH = (H_NSA + H_HGRN + H_MOBA + H_FOX) * HEAD_DIM
HGRN_DK = 64
HGRN_DV = HEAD_DIM
HGRN_CHUNK = 64
NSA_CMP_BLOCK = 32
NSA_CMP_STRIDE = 16
NSA_SEL_BLOCK = 64
NSA_TOPN = 16
NSA_WINDOW = 512
NSA_ROWS = 4
MOBA_BLOCK = 256
MOBA_TOPK = 3
ROPE_THETA = 10000.0
Q_BLOCK = 128
GATHER_Q_BLOCK = 32
N_EXPERTS = 8
TOP_K = 2
MOE_BLOCK = 128
RMS_EPS = 1e-6
NEG_INF = -1e30
SEL_FORCE = 1e6
ATTN_SCALE = HEAD_DIM ** -0.5
IN_SIZES = (H_NSA * HEAD_DIM, 2 * HEAD_DIM, 2 * HEAD_DIM, 2 * HEAD_DIM, 3 * H_NSA,
            H_HGRN * HGRN_DK, H_HGRN * HGRN_DK, H_HGRN * HGRN_DV, H_HGRN * HGRN_DV,
            H_MOBA * HEAD_DIM, H_MOBA * HEAD_DIM, H_MOBA * HEAD_DIM,
            H_FOX * HEAD_DIM, H_FOX * HEAD_DIM, H_FOX * HEAD_DIM, H_FOX)
N_IN = sum(IN_SIZES)
IN_OFFS = tuple(int(v) for v in np.cumsum((0,) + IN_SIZES))

N_SEC = 13
N_INP = N_SEC * HW
SEC_NQ, SEC_S1, SEC_S2, SEC_HQ, SEC_HF, SEC_HI, SEC_HG = 0, 1, 2, 3, 4, 5, 6
SEC_MQ, SEC_MK, SEC_MV, SEC_FQ, SEC_FK, SEC_FV = 7, 8, 9, 10, 11, 12
MISC_LANE = 128

LANES = 128
VMEM_LIMIT = 48 * 1024 * 1024
PREP_TQ = 256
NSA_TQ = 256
NSA_TK = 1024
ATT_SB = 4
M_FLOOR = -1e20
MOE_TM = 256
MOE_TF = 1792
DENSE_TF = 1408
FFN_TN = 512
MOE_TN = 1024
BF = jnp.bfloat16
F32 = jnp.float32


def _round_up(x, m):
    return -(-x // m) * m


def _cparams(sem):
    return pltpu.CompilerParams(dimension_semantics=sem, vmem_limit_bytes=VMEM_LIMIT)


def _dot(a, b):
    return jnp.dot(a, b, preferred_element_type=F32)


def _dot_nt(a, b):
    return lax.dot_general(a, b, (((1,), (1,)), ((), ())), preferred_element_type=F32)


def _mm_kernel(a_ref, b_ref, o_ref):
    k = pl.program_id(2)
    acc = _dot(a_ref[...].astype(BF), b_ref[...].astype(BF))

    @pl.when(k == 0)
    def _():
        o_ref[...] = acc

    @pl.when(k != 0)
    def _():
        o_ref[...] += acc


def matmul(a, b, tm=512, tn=512, tk=1024):
    M, K = a.shape
    _, N = b.shape
    tm = min(tm, _round_up(M, 8))
    Mp, Np = _round_up(M, tm), _round_up(N, tn)
    if K % tk:
        tk = K
    if Mp != M:
        a = jnp.pad(a, ((0, Mp - M), (0, 0)))
    if Np != N:
        b = jnp.pad(b, ((0, 0), (0, Np - N)))
    out = pl.pallas_call(
        _mm_kernel,
        grid=(Mp // tm, Np // tn, K // tk),
        in_specs=[pl.BlockSpec((tm, tk), lambda i, j, k: (i, k)),
                  pl.BlockSpec((tk, tn), lambda i, j, k: (k, j))],
        out_specs=pl.BlockSpec((tm, tn), lambda i, j, k: (i, j)),
        out_shape=jax.ShapeDtypeStruct((Mp, Np), F32),
        compiler_params=_cparams(("parallel", "parallel", "arbitrary")),
        name="dense_matmul",
    )(a, b)
    return out[:M, :N]


def _inproj_kernel(x_ref, g_ref, w_ref, o_ref):
    x = x_ref[...]
    y = x * lax.rsqrt(jnp.mean(x * x, axis=-1, keepdims=True) + RMS_EPS) * g_ref[...]
    o_ref[...] = _dot(y.astype(BF), w_ref[...])


def in_projection(x, g, w_bf, tm=256):
    N, D = x.shape
    return pl.pallas_call(
        _inproj_kernel,
        grid=(N // tm,),
        in_specs=[pl.BlockSpec((tm, D), lambda i: (i, 0)),
                  pl.BlockSpec((1, D), lambda i: (0, 0)),
                  pl.BlockSpec((D, N_INP), lambda i: (0, 0))],
        out_specs=pl.BlockSpec((tm, N_INP), lambda i: (i, 0)),
        out_shape=jax.ShapeDtypeStruct((N, N_INP), F32),
        compiler_params=_cparams(("parallel",)),
        name="rmsnorm_in_projection",
    )(x, g, w_bf)


def relayout_w_in(w):
    def cols(i):
        return w[:, IN_OFFS[i]:IN_OFFS[i + 1]]
    pad = jnp.zeros((w.shape[0], HW - 2 * HEAD_DIM - IN_SIZES[4] - IN_SIZES[15]), w.dtype)
    parts = [cols(0), cols(1), cols(2), cols(3), cols(4), cols(15), pad] + [cols(i) for i in range(5, 15)]
    return jnp.concatenate(parts, axis=1).astype(BF)


def _head_meansq(x, gmat):
    sq = x * x
    hi = sq.astype(BF)
    lo = (sq - hi.astype(F32)).astype(BF)
    return _dot(hi, gmat) + _dot(lo, gmat)


def _head_rmsnorm(x, gain, gmat):
    return x * lax.rsqrt(_head_meansq(x, gmat) + RMS_EPS) * gain


def _rope(x, cos, sin_signed, lo_half):
    w = x.shape[1]
    swapped = jnp.where(lo_half, pltpu.roll(x, w - HEAD_DIM // 2, 1), pltpu.roll(x, HEAD_DIM // 2, 1))
    return x * cos + swapped * sin_signed


def _store_heads(ref, x):
    for h in range(NH):
        ref[0, h] = x[:, h * HEAD_DIM:(h + 1) * HEAD_DIM].astype(ref.dtype)


def _prep_kernel(nq_ref, s1_ref, s2_ref, mq_ref, mk_ref, mv_ref, fq_ref, fk_ref, fv_ref,
                 cos_ref, sin_ref, gains_ref, gmat_ref,
                 nsa_rows_ref, nsa_kc_ref, nsa_win_ref, moba_rows_ref, fox_rows_ref,
                 nqn_ref, nqr_ref, nks_ref, nvs_ref, nkw_ref, nvw_ref,
                 mqr_ref, mkr_ref, mvv_ref, kmean_ref, fqn_ref, fkn_ref, fvv_ref):
    cos = cos_ref[...]
    sin = sin_ref[...]
    gmat = gmat_ref[...]
    t = cos.shape[0]
    lane = lax.broadcasted_iota(jnp.int32, (t, HW), 1)
    lo_half = (lane % HEAD_DIM) < (HEAD_DIM // 2)
    gains = gains_ref[...]

    qn = _head_rmsnorm(nq_ref[...], gains[0:1], gmat)
    qr = _rope(qn, cos, sin, lo_half)
    _store_heads(nqn_ref, qn * ATTN_SCALE)
    _store_heads(nqr_ref, qr * ATTN_SCALE)

    s1 = s1_ref[...]
    s1r = _rope(_head_rmsnorm(s1, gains[1:2], gmat), cos, sin, lo_half)
    third = (lane >= 2 * HEAD_DIM) & (lane < 3 * HEAD_DIM)
    rows = jnp.where(third, s1r, s1)
    nsa_rows_ref[...] = rows
    nsa_kc_ref[...] = rows[:, :2 * HEAD_DIM]
    lane_h = lax.broadcasted_iota(jnp.int32, (t, HEAD_DIM), 1)
    row_h = lax.broadcasted_iota(jnp.int32, (t, HEAD_DIM), 0)
    blk = (pl.program_id(1) * t + row_h) // NSA_SEL_BLOCK
    ones_col = jnp.where(lane_h == 0, 1.0, 0.0)
    nks_ref[0] = jnp.concatenate([rows[:, 2 * HEAD_DIM:3 * HEAD_DIM], jnp.where(lane_h == blk, 1.0, 0.0)],
                                 axis=1).astype(nks_ref.dtype)
    nvs_ref[0] = jnp.concatenate([rows[:, 3 * HEAD_DIM:], ones_col], axis=1).astype(nvs_ref.dtype)

    s2 = s2_ref[...]
    s2r = _rope(_head_rmsnorm(s2, gains[2:3], gmat), cos, sin, lo_half)
    wrows = jnp.where(lane < HEAD_DIM, s2r, s2)
    nsa_win_ref[...] = wrows[:, :2 * HEAD_DIM]
    nkw_ref[0] = wrows[:, :HEAD_DIM].astype(nkw_ref.dtype)
    nvw_ref[0] = wrows[:, HEAD_DIM:2 * HEAD_DIM].astype(nvw_ref.dtype)

    mq = _rope(_head_rmsnorm(mq_ref[...], gains[3:4], gmat), cos, sin, lo_half)
    mk = _rope(_head_rmsnorm(mk_ref[...], gains[4:5], gmat), cos, sin, lo_half)
    mv = mv_ref[...]
    _store_heads(mqr_ref, mq * ATTN_SCALE)
    _store_heads(mkr_ref, mk)
    _store_heads(mvv_ref, mv)
    moba_rows_ref[:, :HW] = mk
    moba_rows_ref[:, HW:] = mv
    kmean_ref[0, 0] = jnp.mean(mk, axis=0, keepdims=True)

    fq = _head_rmsnorm(fq_ref[...], gains[5:6], gmat)
    fk = _head_rmsnorm(fk_ref[...], gains[6:7], gmat)
    fv = fv_ref[...]
    _store_heads(fqn_ref, fq * ATTN_SCALE)
    _store_heads(fkn_ref, fk)
    _store_heads(fvv_ref, fv)
    fox_rows_ref[:, :HW] = fk
    fox_rows_ref[:, HW:] = fv


def rope_tables(pos):
    half = HEAD_DIM // 2
    inv = ROPE_THETA ** (-jnp.arange(half, dtype=F32) / half)
    ang = pos.astype(F32)[:, None] * inv[None, :]
    cos = jnp.cos(ang)
    sin = jnp.sin(ang)
    cos_h = jnp.concatenate([cos, cos], axis=1)
    sin_h = jnp.concatenate([-sin, sin], axis=1)
    return jnp.tile(cos_h, (1, NH)), jnp.tile(sin_h, (1, NH))


def head_gains(nsa_gain, moba_gain, fox_gain):
    one = jnp.ones((HEAD_DIM,), F32)
    t4 = lambda g: jnp.tile(g, NH)
    rows = [t4(nsa_gain[0]),
            jnp.concatenate([one, one, nsa_gain[2], one]),
            jnp.concatenate([nsa_gain[3], one, one, one]),
            t4(moba_gain[0]), t4(moba_gain[1]), t4(fox_gain[0]), t4(fox_gain[1]), t4(one)]
    return jnp.stack(rows, axis=0)


def group_mean_matrix():
    idx = np.arange(HW) // HEAD_DIM
    return jnp.asarray((idx[:, None] == idx[None, :]).astype(np.float32) / HEAD_DIM, dtype=BF)


def prep_prompt(z, row0, B, T, cos, sin, gains, gmat, qdt=None):
    qdt = BF if qdt is None else qdt
    tq = PREP_TQ
    nq = T // tq
    rb0 = row0 // tq

    def sec(c):
        return pl.BlockSpec((tq, HW), lambda b, i, c=c: (rb0 + b * nq + i, c))

    flat = lambda w: pl.BlockSpec((tq, w), lambda b, i: (b * nq + i, 0))
    headmaj = pl.BlockSpec((1, NH, tq, HEAD_DIM), lambda b, i: (b, 0, i, 0))
    single = pl.BlockSpec((1, tq, HEAD_DIM), lambda b, i: (b, i, 0))
    single_aug = pl.BlockSpec((1, tq, LANES), lambda b, i: (b, i, 0))
    N = B * T
    sd = jax.ShapeDtypeStruct
    hm_shape = sd((B, NH, T, HEAD_DIM), qdt)
    sg_shape = sd((B, T, HEAD_DIM), qdt)
    outs = pl.pallas_call(
        _prep_kernel,
        grid=(B, nq),
        in_specs=[sec(SEC_NQ), sec(SEC_S1), sec(SEC_S2), sec(SEC_MQ), sec(SEC_MK), sec(SEC_MV),
                  sec(SEC_FQ), sec(SEC_FK), sec(SEC_FV),
                  pl.BlockSpec((tq, HW), lambda b, i: (i, 0)),
                  pl.BlockSpec((tq, HW), lambda b, i: (i, 0)),
                  pl.BlockSpec((8, HW), lambda b, i: (0, 0)),
                  pl.BlockSpec((HW, HW), lambda b, i: (0, 0))],
        out_specs=[flat(HW), flat(2 * HEAD_DIM), flat(2 * HEAD_DIM), flat(2 * HW), flat(2 * HW),
                   headmaj, headmaj, single_aug, single_aug, single, single,
                   headmaj, headmaj, headmaj,
                   pl.BlockSpec((1, 1, 1, HW), lambda b, i: (b, i, 0, 0)),
                   headmaj, headmaj, headmaj],
        out_shape=[sd((N, HW), F32), sd((N, 2 * HEAD_DIM), F32), sd((N, 2 * HEAD_DIM), F32),
                   sd((N, 2 * HW), F32), sd((N, 2 * HW), F32),
                   hm_shape, hm_shape, sd((B, T, LANES), qdt), sd((B, T, LANES), qdt), sg_shape, sg_shape,
                   hm_shape, hm_shape, hm_shape,
                   sd((B, nq, 1, HW), F32),
                   hm_shape, hm_shape, hm_shape],
        compiler_params=_cparams(("parallel", "parallel")),
        name="mixer_prep",
    )(z, z, z, z, z, z, z, z, z, cos, sin, gains, gmat)
    keys = ("nsa_rows", "nsa_kc", "nsa_win", "moba_rows", "fox_rows",
            "nqn", "nqr", "nks", "nvs", "nkw", "nvw", "mqr", "mkr", "mvv", "kmean", "fqn", "fkn", "fvv")
    return dict(zip(keys, outs))


def _nsa_compress_kernel(x_ref, pe_ref, w_ref, gain_ref, k_ref, v_ref):
    x = x_ref[0]
    a = _dot((x + pe_ref[0:1]).astype(BF), w_ref[0])
    b = _dot((x + pe_ref[1:2]).astype(BF), w_ref[1])
    nch = x.shape[0]
    y = a + pltpu.roll(b, nch - 1, 0)
    lane = lax.broadcasted_iota(jnp.int32, y.shape, 1)
    ms = jnp.sum(jnp.where(lane < HEAD_DIM, y * y, 0.0), axis=1, keepdims=True) / HEAD_DIM
    kn = y * lax.rsqrt(ms + RMS_EPS) * gain_ref[...]
    k_ref[0] = kn[:, :HEAD_DIM].astype(BF)
    v_ref[0] = y[:, HEAD_DIM:].astype(BF)


def nsa_compress_weights(cmp_pe, cmp_w, gain1):
    S = NSA_CMP_STRIDE
    pe = cmp_pe.reshape(2, 2, S, HEAD_DIM)
    pe_flat = jnp.transpose(pe, (1, 2, 0, 3)).reshape(2, S * 2 * HEAD_DIM)
    w = cmp_w.reshape(2, 2, S, HEAD_DIM, HEAD_DIM)
    wz = jnp.zeros((2, S, 2, HEAD_DIM, 2, HEAD_DIM), F32)
    wz = wz.at[:, :, 0, :, 0, :].set(w[0]).at[:, :, 1, :, 1, :].set(w[1])
    w_flat = wz.reshape(2, S * 2 * HEAD_DIM, 2 * HEAD_DIM).astype(BF)
    gain = jnp.concatenate([gain1, jnp.ones((HEAD_DIM,), F32)])[None, :]
    return pe_flat, w_flat, gain


def nsa_compress(kc, B, T, pe_flat, w_flat, gain):
    nch = T // NSA_CMP_STRIDE
    cw = NSA_CMP_STRIDE * 2 * HEAD_DIM
    x = kc.reshape(B, nch, cw)
    out_spec = pl.BlockSpec((1, nch, HEAD_DIM), lambda b: (b, 0, 0))
    return pl.pallas_call(
        _nsa_compress_kernel,
        grid=(B,),
        in_specs=[pl.BlockSpec((1, nch, cw), lambda b: (b, 0, 0)),
                  pl.BlockSpec((2, cw), lambda b: (0, 0)),
                  pl.BlockSpec((2, cw, 2 * HEAD_DIM), lambda b: (0, 0, 0)),
                  pl.BlockSpec((1, 2 * HEAD_DIM), lambda b: (0, 0))],
        out_specs=[out_spec, out_spec],
        out_shape=[jax.ShapeDtypeStruct((B, nch, HEAD_DIM), BF)] * 2,
        compiler_params=_cparams(("parallel",)),
        name="nsa_compress",
    )(x, pe_flat, w_flat, gain)


def _softmax_update(s, mask, m_i, l_i, acc, v, lead, v_feature_major=False):
    s = jnp.where(mask, s, NEG_INF)
    m_new = jnp.maximum(m_i, jnp.max(s, axis=-1, keepdims=True))
    p = jnp.where(mask, jnp.exp(s - m_new), 0.0)
    alpha = jnp.exp(m_i - m_new)
    l_new = alpha * l_i + jnp.sum(p, axis=-1, keepdims=True)
    pb = p.astype(BF).reshape(lead, p.shape[-1])
    pv = (_dot_nt(pb, v) if v_feature_major else _dot(pb, v)).reshape(acc.shape)
    return m_new, l_new, alpha * acc + pv


def _nsa_attn_kernel(qn_ref, qr_ref, kc_ref, vc_ref, ks_ref, vs_ref, kw_ref, vw_ref, gl_ref,
                     cover_ref, o_ref, m_scr, acc_scr, *, T):
    tq, tk = NSA_TQ, NSA_TK
    M = NH * tq
    i = pl.program_id(1)
    p0 = i * tq
    qn = qn_ref[0].reshape(M, HEAD_DIM)
    qr = qr_ref[0].reshape(M, HEAD_DIM)
    pos = p0 + lax.broadcasted_iota(jnp.int32, (tq, 1), 0)

    nch = kc_ref.shape[1]
    s_c = _dot_nt(qn, kc_ref[0]).reshape(NH, tq, nch)
    n_idx = lax.broadcasted_iota(jnp.int32, (tq, nch), 1)
    mask_c = (n_idx * NSA_CMP_STRIDE + (NSA_CMP_BLOCK - 1) <= pos)[None]
    s_c = jnp.where(mask_c, s_c, NEG_INF)
    m_c = jnp.max(s_c, axis=-1, keepdims=True)
    p_c = jnp.where(mask_c, jnp.exp(s_c - m_c), 0.0)
    p_c = p_c / jnp.maximum(jnp.sum(p_c, axis=-1, keepdims=True), 1.0)
    p_cb = p_c.astype(BF).reshape(M, nch)
    o_cmp = _dot(p_cb, vc_ref[0]).reshape(NH, tq, HEAD_DIM)
    imp = jnp.sum(_dot(p_cb, cover_ref[...]).reshape(NH, tq, LANES), axis=0)

    j = lax.broadcasted_iota(jnp.int32, (tq, LANES), 1)
    cur = pos // NSA_SEL_BLOCK
    forced = (j == 0) | (j == cur) | (j == cur - 1)
    imp = jnp.where(j > cur, NEG_INF, jnp.where(forced, SEL_FORCE, imp))
    n_sel = T // NSA_SEL_BLOCK
    rank = jnp.zeros((tq, LANES), F32)
    for jp in range(n_sel):
        col = imp[:, jp:jp + 1]
        beats = (col > imp) | ((col == imp) & (j > jp))
        rank = rank + jnp.where(beats, 1.0, 0.0)
    picked = (rank < NSA_TOPN) & (j <= cur)

    pen = jnp.where(picked, 0.0, NEG_INF)[:, :HEAD_DIM].astype(qr.dtype)
    qa = jnp.concatenate([qr, jnp.concatenate([pen] * NH, axis=0)], axis=1)
    _flash_init_aug(m_scr, acc_scr)
    kcol = lax.broadcasted_iota(jnp.int32, (tq, tk), 1)
    for kj in range(T // tk):
        def region(causal, kj=kj):
            s = _dot_nt(qa, ks_ref[0, kj * tk:(kj + 1) * tk, :]).reshape(NH, tq, tk)
            if causal:
                s = jnp.where((kcol + kj * tk <= pos)[None], s, NEG_INF)
            _flash_step_aug(s, vs_ref[0, kj * tk:(kj + 1) * tk, :], m_scr, acc_scr)

        pl.when((kj + 1) * tk <= p0)(functools.partial(region, False))
        pl.when((kj * tk <= p0) & ((kj + 1) * tk > p0))(functools.partial(region, True))
    acc_s = acc_scr[...]
    o_sel = acc_s[:, :, :HEAD_DIM] / jnp.maximum(acc_s[:, :, HEAD_DIM:HEAD_DIM + 1], 1.0)

    band = NSA_WINDOW + tq
    start = pl.multiple_of(jnp.maximum(p0 - NSA_WINDOW, 0), tq)
    kw = kw_ref[0, pl.ds(start, band), :]
    vw = vw_ref[0, pl.ds(start, band), :]
    s_w = _dot_nt(qr, kw).reshape(NH, tq, band)
    kpos = start + lax.broadcasted_iota(jnp.int32, (tq, band), 1)
    mask_w = ((kpos <= pos) & (kpos > pos - NSA_WINDOW))[None]
    s_w = jnp.where(mask_w, s_w, NEG_INF)
    m_w = jnp.max(s_w, axis=-1, keepdims=True)
    p_w = jnp.where(mask_w, jnp.exp(s_w - m_w), 0.0)
    l_w = jnp.sum(p_w, axis=-1, keepdims=True)
    o_win = _dot(p_w.astype(BF).reshape(M, band), vw).reshape(NH, tq, HEAD_DIM) / jnp.maximum(l_w, 1.0)

    sig = jax.nn.sigmoid(gl_ref[...])
    outs = []
    for h in range(NH):
        c = MISC_LANE + 3 * h
        outs.append(sig[:, c:c + 1] * o_cmp[h] + sig[:, c + 1:c + 2] * o_sel[h] + sig[:, c + 2:c + 3] * o_win[h])
    o_ref[...] = jnp.concatenate(outs, axis=1).astype(o_ref.dtype)


def nsa_constants(T):
    nch = T // NSA_CMP_STRIDE
    n_cmp = (T - NSA_CMP_BLOCK) // NSA_CMP_STRIDE + 1
    ci = np.arange(nch)[:, None] * NSA_CMP_STRIDE
    sj = np.arange(LANES)[None, :] * NSA_SEL_BLOCK
    cover = (ci < sj + NSA_SEL_BLOCK) & (ci + NSA_CMP_BLOCK > sj) & (np.arange(nch)[:, None] < n_cmp)
    return jnp.asarray(cover.astype(np.float32), dtype=BF)


def nsa_attention_prompt(pp, kc, vc, z, row0, B, T, cover):
    tq = NSA_TQ
    nq = T // tq
    rb0 = row0 // tq
    nch = T // NSA_CMP_STRIDE
    assert T >= NSA_WINDOW + tq and T % NSA_TK == 0 and NSA_TK % tq == 0 and T // NSA_SEL_BLOCK <= HEAD_DIM
    headmaj = pl.BlockSpec((1, NH, tq, HEAD_DIM), lambda b, i: (b, 0, i, 0))
    full1 = pl.BlockSpec((1, T, HEAD_DIM), lambda b, i: (b, 0, 0))
    full_aug = pl.BlockSpec((1, T, LANES), lambda b, i: (b, 0, 0))
    cmp1 = pl.BlockSpec((1, nch, HEAD_DIM), lambda b, i: (b, 0, 0))
    return pl.pallas_call(
        functools.partial(_nsa_attn_kernel, T=T),
        grid=(B, nq),
        in_specs=[headmaj, headmaj, cmp1, cmp1, full_aug, full_aug, full1, full1,
                  pl.BlockSpec((tq, HW), lambda b, i: (rb0 + b * nq + i, SEC_S2)),
                  pl.BlockSpec((nch, LANES), lambda b, i: (0, 0))],
        out_specs=pl.BlockSpec((tq, HW), lambda b, i: (b * nq + i, 0)),
        out_shape=jax.ShapeDtypeStruct((B * T, HW), BF),
        scratch_shapes=[pltpu.VMEM((NH, tq, 1), F32), pltpu.VMEM((NH, tq, LANES), F32)],
        compiler_params=_cparams(("parallel", "parallel")),
        name="nsa_attention",
    )(pp["nqn"], pp["nqr"], kc, vc, pp["nks"], pp["nvs"], pp["nkw"], pp["nvw"], z, cover)


def _flash_init(m_scr, l_scr, acc_scr):
    m_scr[...] = jnp.full_like(m_scr, M_FLOOR)
    l_scr[...] = jnp.zeros_like(l_scr)
    acc_scr[...] = jnp.zeros_like(acc_scr)


def _flash_step(s, v, m_scr, l_scr, acc_scr):
    m_i = m_scr[...]
    m_new = jnp.maximum(m_i, jnp.max(s, axis=-1, keepdims=True))
    p = jnp.exp(s - m_new)
    alpha = jnp.exp(m_i - m_new)
    l_scr[...] = alpha * l_scr[...] + jnp.sum(p, axis=-1, keepdims=True)
    pv = _dot(p.astype(BF).reshape(-1, p.shape[-1]), v)
    acc_scr[...] = alpha * acc_scr[...] + pv.reshape(acc_scr.shape)
    m_scr[...] = m_new


def _flash_init_aug(m_scr, acc_scr):
    m_scr[...] = jnp.full_like(m_scr, M_FLOOR)
    acc_scr[...] = jnp.zeros_like(acc_scr)


def _flash_step_aug(s, v_aug, m_scr, acc_scr):
    m_i = m_scr[...]
    m_new = jnp.maximum(m_i, jnp.max(s, axis=-1, keepdims=True))
    p = jnp.exp(s - m_new)
    pv = _dot(p.astype(BF).reshape(-1, p.shape[-1]), v_aug)
    acc_scr[...] = jnp.exp(m_i - m_new) * acc_scr[...] + pv.reshape(acc_scr.shape)
    m_scr[...] = m_new


def _moba_attn_kernel(q_ref, k_ref, v_ref, km_ref, o_ref, m_scr, l_scr, acc_scr):
    tq = MOBA_BLOCK
    tks = ATT_SB * MOBA_BLOCK
    n_sb = k_ref.shape[2] // tks
    qi = pl.program_id(1)
    lane = lax.broadcasted_iota(jnp.int32, (tq, LANES), 1)
    qrow = lax.broadcasted_iota(jnp.int32, (tq, tks), 0)
    kcol = lax.broadcasted_iota(jnp.int32, (tq, tks), 1)
    sels = []
    for h in range(NH):
        g = jnp.where(lane < qi, _dot_nt(q_ref[0, h], km_ref[0, h]), NEG_INF)
        sel = jnp.where(lane == qi, 1.0, 0.0)
        for _ in range(MOBA_TOPK):
            m = jnp.max(g, axis=1, keepdims=True)
            idx = jnp.min(jnp.where(g == m, lane, LANES), axis=1, keepdims=True)
            hit = lane == idx
            sel = jnp.where(hit & (lane < qi), 1.0, sel)
            g = jnp.where(hit, NEG_INF, g)
        sels.append(sel)
    _flash_init(m_scr, l_scr, acc_scr)
    for sb in range(n_sb):
        @pl.when(sb * ATT_SB <= qi)
        def _(sb=sb):
            causal = kcol + (sb * tks) <= qrow + qi * tq
            for h in range(NH):
                k = k_ref[0, h, sb * tks:(sb + 1) * tks, :]
                v = v_ref[0, h, sb * tks:(sb + 1) * tks, :]
                picked = jnp.concatenate(
                    [jnp.broadcast_to(sels[h][:, j:j + 1] > 0.5, (tq, MOBA_BLOCK))
                     for j in range(sb * ATT_SB, (sb + 1) * ATT_SB)], axis=1)
                s = jnp.where(picked & causal, _dot_nt(q_ref[0, h], k), NEG_INF)
                _flash_step(s, v, m_scr.at[h], l_scr.at[h], acc_scr.at[h])
    o = acc_scr[...] / jnp.maximum(l_scr[...], 1.0)
    o_ref[...] = jnp.concatenate([o[h] for h in range(NH)], axis=1).astype(o_ref.dtype)


def moba_attention_prompt(pp, B, T):
    tq = MOBA_BLOCK
    nq = T // tq
    km = pp["kmean"].reshape(B, nq, NH, HEAD_DIM).transpose(0, 2, 1, 3)
    km = jnp.pad(km, ((0, 0), (0, 0), (0, LANES - nq), (0, 0))).astype(BF)
    headq = pl.BlockSpec((1, NH, tq, HEAD_DIM), lambda b, i: (b, 0, i, 0))
    headfull = pl.BlockSpec((1, NH, T, HEAD_DIM), lambda b, i: (b, 0, 0, 0))
    return pl.pallas_call(
        _moba_attn_kernel,
        grid=(B, nq),
        in_specs=[headq, headfull, headfull,
                  pl.BlockSpec((1, NH, LANES, HEAD_DIM), lambda b, i: (b, 0, 0, 0))],
        out_specs=pl.BlockSpec((tq, HW), lambda b, i: (b * nq + i, 0)),
        out_shape=jax.ShapeDtypeStruct((B * T, HW), BF),
        scratch_shapes=[pltpu.VMEM((NH, tq, 1), F32), pltpu.VMEM((NH, tq, 1), F32),
                        pltpu.VMEM((NH, tq, HEAD_DIM), F32)],
        compiler_params=_cparams(("parallel", "parallel")),
        name="moba_attention",
    )(pp["mqr"], pp["mkr"], pp["mvv"], km)


def _fox_attn_kernel(q_ref, k_ref, v_ref, cq_ref, ck_ref, o_ref, m_scr, l_scr, acc_scr):
    tq = MOBA_BLOCK
    tks = ATT_SB * MOBA_BLOCK
    n_sb = k_ref.shape[2] // tks
    qi = pl.program_id(1)
    qrow = lax.broadcasted_iota(jnp.int32, (tq, tks), 0)
    kcol = lax.broadcasted_iota(jnp.int32, (tq, tks), 1)
    _flash_init(m_scr, l_scr, acc_scr)
    for sb in range(n_sb):
        last = (sb + 1) * ATT_SB - 1

        def region(causal, sb=sb):
            for h in range(NH):
                k = k_ref[0, h, sb * tks:(sb + 1) * tks, :]
                v = v_ref[0, h, sb * tks:(sb + 1) * tks, :]
                s = _dot_nt(q_ref[0, h], k) + cq_ref[0, h] - ck_ref[0, h, sb]
                if causal:
                    s = jnp.where(kcol + (sb * tks) <= qrow + qi * tq, s, NEG_INF)
                _flash_step(s, v, m_scr.at[h], l_scr.at[h], acc_scr.at[h])

        pl.when(last < qi)(functools.partial(region, False))
        pl.when((sb * ATT_SB <= qi) & (last >= qi))(functools.partial(region, True))
    o = acc_scr[...] / jnp.maximum(l_scr[...], 1.0)
    o_ref[...] = jnp.concatenate([o[h] for h in range(NH)], axis=1).astype(o_ref.dtype)


def fox_attention_prompt(pp, ch, B, T):
    tq = MOBA_BLOCK
    nq = T // tq
    cq = ch[..., None]
    tks = ATT_SB * MOBA_BLOCK
    ck = ch.reshape(B, NH, T // tks, 1, tks)
    headq = pl.BlockSpec((1, NH, tq, HEAD_DIM), lambda b, i: (b, 0, i, 0))
    headfull = pl.BlockSpec((1, NH, T, HEAD_DIM), lambda b, i: (b, 0, 0, 0))
    return pl.pallas_call(
        _fox_attn_kernel,
        grid=(B, nq),
        in_specs=[headq, headfull, headfull,
                  pl.BlockSpec((1, NH, tq, 1), lambda b, i: (b, 0, i, 0)),
                  pl.BlockSpec((1, NH, T // tks, 1, tks), lambda b, i: (b, 0, 0, 0, 0))],
        out_specs=pl.BlockSpec((tq, HW), lambda b, i: (b * nq + i, 0)),
        out_shape=jax.ShapeDtypeStruct((B * T, HW), BF),
        scratch_shapes=[pltpu.VMEM((NH, tq, 1), F32), pltpu.VMEM((NH, tq, 1), F32),
                        pltpu.VMEM((NH, tq, HEAD_DIM), F32)],
        compiler_params=_cparams(("parallel", "parallel")),
        name="fox_attention",
    )(pp["fqn"], pp["fkn"], pp["fvv"], cq, ck)


def _hgrn_kernel(q_ref, f_ref, i_ref, g_ref, s0_ref, lb_ref, gain_ref, bd_ref, gmat_ref,
                 o_ref, sout_ref, st_scr, *, C):
    c = pl.program_id(1)

    @pl.when(c == 0)
    def _():
        st_scr[...] = jnp.zeros_like(st_scr)
        for h in range(NH):
            st_scr[h * HEAD_DIM:(h + 1) * HEAD_DIM, h * HEAD_DIM:(h + 1) * HEAD_DIM] = s0_ref[0, h]

    lb = lb_ref[...]
    z = f_ref[...]
    logf = jnp.log(lb + (1.0 - lb) * jax.nn.sigmoid(z))
    kk = (1.0 - lb) * jax.nn.sigmoid(-z)
    q = q_ref[...]
    qf = q * jax.nn.sigmoid(q)
    v = i_ref[...]
    row = lax.broadcasted_iota(jnp.int32, (C, HW), 0)
    b = logf
    sh = 1
    while sh < C:
        b = b + jnp.where(row >= sh, pltpu.roll(b, sh, 0), 0.0)
        sh *= 2
    bd = bd_ref[...]
    st = st_scr[...]
    o_ref[...] = _dot_nt((qf * jnp.exp(b)).astype(BF), st.astype(BF))
    for s in range(C):
        causal = row >= s
        e = jnp.exp(jnp.where(causal, b - b[s:s + 1], 0.0))
        fz = jnp.where(causal, qf * kk[s:s + 1] * e, 0.0)
        o_ref[...] += _dot(fz.astype(BF), bd) * v[s:s + 1]
    o = o_ref[...]
    b_last = b[C - 1:C, :]
    kt = kk * jnp.exp(b_last - b)
    upd = lax.dot_general(v.astype(BF), kt.astype(BF), (((0,), (0,)), ((), ())), preferred_element_type=F32)
    r2 = lax.broadcasted_iota(jnp.int32, (HW, HW), 0) // HEAD_DIM
    c2 = lax.broadcasted_iota(jnp.int32, (HW, HW), 1) // HEAD_DIM
    st_new = st * jnp.exp(b_last) + jnp.where(r2 == c2, upd, 0.0)
    st_scr[...] = st_new
    g = g_ref[...]
    o_ref[...] = _head_rmsnorm(o, gain_ref[...], gmat_ref[...]) * (g * jax.nn.sigmoid(g))

    @pl.when(c == pl.num_programs(1) - 1)
    def _():
        for h in range(NH):
            sout_ref[0, h] = st_new[h * HEAD_DIM:(h + 1) * HEAD_DIM, h * HEAD_DIM:(h + 1) * HEAD_DIM]


def hgrn_mixer(z, row0, B, T, s0, lb, out_gain, gmat):
    C = math.gcd(T, HGRN_CHUNK)
    nc = T // C
    rb0 = row0 // C
    idx = np.arange(HW) // HEAD_DIM
    bd = jnp.asarray((idx[:, None] == idx[None, :]).astype(np.float32), dtype=BF)

    def sec(cidx):
        return pl.BlockSpec((C, HW), lambda b, c, cidx=cidx: (rb0 + b * nc + c, cidx))

    state = pl.BlockSpec((1, NH, HGRN_DV, HGRN_DK), lambda b, c: (b, 0, 0, 0))
    vec = pl.BlockSpec((1, HW), lambda b, c: (0, 0))
    mat = pl.BlockSpec((HW, HW), lambda b, c: (0, 0))
    o, st = pl.pallas_call(
        functools.partial(_hgrn_kernel, C=C),
        grid=(B, nc),
        in_specs=[sec(SEC_HQ), sec(SEC_HF), sec(SEC_HI), sec(SEC_HG), state, vec, vec, mat, mat],
        out_specs=[pl.BlockSpec((C, HW), lambda b, c: (b * nc + c, 0)), state],
        out_shape=[jax.ShapeDtypeStruct((B * T, HW), F32),
                   jax.ShapeDtypeStruct((B, NH, HGRN_DV, HGRN_DK), F32)],
        scratch_shapes=[pltpu.VMEM((HW, HW), F32)],
        compiler_params=_cparams(("parallel", "arbitrary")),
        name="hgrn_recurrence",
    )(z, z, z, z, jnp.swapaxes(s0, 2, 3), lb[None, :], jnp.tile(out_gain, NH)[None, :], bd, gmat)
    return o, jnp.swapaxes(st, 2, 3)


DEC_PG = 32


def _page_specs(tail, PG, base):
    return [pl.BlockSpec((1,) + tail, lambda b, s, pt, j=j: (base + pt[b, s * PG + j], 0, 0)) for j in range(PG)]


def _pool_view(pools, l, width):
    pages = pools.reshape(pools.shape[0] * pools.shape[1], PAGE_SIZE, width)
    return jnp.swapaxes(pages, 1, 2), l * pools.shape[1]


def _rows_nat(q_ref):
    return jnp.concatenate([q_ref[0, h] for h in range(NH)], axis=1)


def _block_diag_q(qnat, T):
    q4 = jnp.concatenate([qnat] * NH, axis=0)
    r = lax.broadcasted_iota(jnp.int32, q4.shape, 0) // T
    c = lax.broadcasted_iota(jnp.int32, q4.shape, 1) // HEAD_DIM
    return jnp.where(r == c, q4, 0.0)


def _diag_heads(x, T):
    return jnp.concatenate([x[h * T:(h + 1) * T, h * HEAD_DIM:(h + 1) * HEAD_DIM] for h in range(NH)], axis=1)


def _new_causal(R, T):
    tq = lax.broadcasted_iota(jnp.int32, (R, T), 0) % T
    tk = lax.broadcasted_iota(jnp.int32, (R, T), 1)
    return tk <= tq


def _fox_decode_kernel(pt_ref, q_ref, new_ref, cq_ref, cn_ref, ck_ref, *rest, PG, T):
    page_refs, o_ref = rest[:PG], rest[PG]
    m_scr, l_scr, acc_scr = rest[PG + 1:]
    s = pl.program_id(1)
    R = NH * T

    @pl.when(s == 0)
    def _():
        m_scr[...] = jnp.full_like(m_scr, NEG_INF)
        l_scr[...] = jnp.zeros_like(l_scr)
        acc_scr[...] = jnp.zeros_like(acc_scr)

    qbd = _block_diag_q(_rows_nat(q_ref), T).astype(BF)
    cq = cq_ref[0]
    m_i, l_i, acc = m_scr[...], l_scr[...], acc_scr[...]
    for j in range(PG):
        page = page_refs[j][0]
        kt = page[:HW].astype(BF)
        vt = page[HW:].astype(BF)
        ck = jnp.broadcast_to(ck_ref[0, j][:, None, :], (NH, T, PAGE_SIZE)).reshape(R, PAGE_SIZE)
        sc = _dot(qbd, kt) + cq - ck
        m_new = jnp.maximum(m_i, jnp.max(sc, axis=-1, keepdims=True))
        p = jnp.exp(sc - m_new)
        alpha = jnp.exp(m_i - m_new)
        l_i = alpha * l_i + jnp.sum(p, axis=-1, keepdims=True)
        acc = alpha * acc + _dot_nt(p.astype(BF), vt)
        m_i = m_new
    m_scr[...] = m_i
    l_scr[...] = l_i
    acc_scr[...] = acc

    @pl.when(s == pl.num_programs(1) - 1)
    def _():
        new = new_ref[...]
        sc = _dot_nt(qbd, new[:, :HW].astype(BF)) + cq - cn_ref[0]
        _, l_f, acc_f = _softmax_update(sc, _new_causal(R, T), m_i, l_i, acc, new[:, HW:].astype(BF), R)
        o_ref[...] = _diag_heads(acc_f / jnp.maximum(l_f, 1.0), T)


def fox_decode(sp, pools, l, page_table, ch, row_new, B, T):
    n_pages = page_table.shape[1]
    PG = min(DEC_PG, n_pages)
    past = n_pages * PAGE_SIZE
    R = NH * T
    view, base = _pool_view(pools, l, 2 * HW)
    ck = ch[:, :, :past].reshape(B, NH, n_pages, PAGE_SIZE).transpose(0, 2, 1, 3)
    cnew = ch[:, :, past:]
    cq = cnew.reshape(B, R, 1)
    cn = jnp.broadcast_to(cnew[:, :, None, :], (B, NH, T, T)).reshape(B, R, T)
    return pl.pallas_call(
        functools.partial(_fox_decode_kernel, PG=PG, T=T),
        grid_spec=pltpu.PrefetchScalarGridSpec(
            num_scalar_prefetch=1, grid=(B, n_pages // PG),
            in_specs=[pl.BlockSpec((1, NH, T, HEAD_DIM), lambda b, s, pt: (0, 0, b, 0)),
                      pl.BlockSpec((T, 2 * HW), lambda b, s, pt: (b, 0)),
                      pl.BlockSpec((1, R, 1), lambda b, s, pt: (b, 0, 0)),
                      pl.BlockSpec((1, R, T), lambda b, s, pt: (b, 0, 0)),
                      pl.BlockSpec((1, PG, NH, PAGE_SIZE), lambda b, s, pt: (b, s, 0, 0))]
            + _page_specs((2 * HW, PAGE_SIZE), PG, base),
            out_specs=pl.BlockSpec((T, HW), lambda b, s, pt: (b, 0)),
            scratch_shapes=[pltpu.VMEM((R, 1), F32), pltpu.VMEM((R, 1), F32), pltpu.VMEM((R, HW), F32)]),
        out_shape=jax.ShapeDtypeStruct((B * T, HW), F32),
        compiler_params=_cparams(("parallel", "arbitrary")),
        name="fox_decode",
    )(page_table, sp["fqn"], row_new, cq, cn, ck, *([view] * PG))


def _moba_decode_kernel(pt_ref, q_ref, new_ref, *rest, PG, T, n_pages):
    page_refs, o_ref = rest[:PG], rest[PG]
    m_scr, l_scr, rs_scr, acc_scr = rest[PG + 1:]
    s = pl.program_id(1)
    R = NH * T
    lane = lax.broadcasted_iota(jnp.int32, (R, LANES), 1)

    @pl.when(s == 0)
    def _():
        m_scr[...] = jnp.full_like(m_scr, NEG_INF)
        l_scr[...] = jnp.zeros_like(l_scr)
        rs_scr[...] = jnp.zeros_like(rs_scr)

    qbd = _block_diag_q(_rows_nat(q_ref), T).astype(BF)
    mm, ll, rs = m_scr[...], l_scr[...], rs_scr[...]
    for j in range(PG):
        pid = s * PG + j
        page = page_refs[j][0]
        sc = _dot(qbd, page[:HW].astype(BF))
        m = jnp.max(sc, axis=-1, keepdims=True)
        p = jnp.exp(sc - m)
        acc_scr[pid] = _dot_nt(p.astype(BF), page[HW:].astype(BF))
        hit = lane == pid
        mm = jnp.where(hit, m, mm)
        ll = jnp.where(hit, jnp.sum(p, axis=-1, keepdims=True), ll)
        rs = jnp.where(hit, jnp.sum(sc, axis=-1, keepdims=True), rs)
    m_scr[...] = mm
    l_scr[...] = ll
    rs_scr[...] = rs

    @pl.when(s == pl.num_programs(1) - 1)
    def _():
        block_lane = (lane % 2 == 0) & (lane < n_pages)
        g = jnp.where(block_lane, (rs + pltpu.roll(rs, LANES - 1, 1)) * (1.0 / MOBA_BLOCK), NEG_INF)
        sel = jnp.zeros((R, LANES), F32)
        for _ in range(MOBA_TOPK):
            mx = jnp.max(g, axis=1, keepdims=True)
            idx = jnp.min(jnp.where(g == mx, lane, LANES), axis=1, keepdims=True)
            hit = lane == idx
            sel = jnp.where(hit & block_lane, 1.0, sel)
            g = jnp.where(hit, NEG_INF, g)
        picked = (sel + pltpu.roll(sel, 1, 1)) > 0.5
        new = new_ref[...]
        mask_n = _new_causal(R, T)
        sn = jnp.where(mask_n, _dot_nt(qbd, new[:, :HW].astype(BF)), NEG_INF)
        m_n = jnp.max(sn, axis=-1, keepdims=True)
        p_n = jnp.where(mask_n, jnp.exp(sn - m_n), 0.0)
        m_tot = jnp.maximum(jnp.max(jnp.where(picked, mm, NEG_INF), axis=1, keepdims=True), m_n)
        w = jnp.where(picked, jnp.exp(mm - m_tot), 0.0)
        w_n = jnp.exp(m_n - m_tot)
        l_tot = jnp.sum(w * ll, axis=1, keepdims=True) + w_n * jnp.sum(p_n, axis=-1, keepdims=True)
        acc = w_n * _dot(p_n.astype(BF), new[:, HW:].astype(BF))
        for pg in range(n_pages):
            acc = acc + w[:, pg:pg + 1] * acc_scr[pg]
        o_ref[...] = _diag_heads(acc / jnp.maximum(l_tot, 1.0), T)


def moba_decode(sp, pools, l, page_table, row_new, B, T):
    n_pages = page_table.shape[1]
    assert MOBA_BLOCK == 2 * PAGE_SIZE and n_pages % 2 == 0 and n_pages <= LANES and T <= MOBA_BLOCK
    PG = min(DEC_PG, n_pages)
    R = NH * T
    view, base = _pool_view(pools, l, 2 * HW)
    return pl.pallas_call(
        functools.partial(_moba_decode_kernel, PG=PG, T=T, n_pages=n_pages),
        grid_spec=pltpu.PrefetchScalarGridSpec(
            num_scalar_prefetch=1, grid=(B, n_pages // PG),
            in_specs=[pl.BlockSpec((1, NH, T, HEAD_DIM), lambda b, s, pt: (0, 0, b, 0)),
                      pl.BlockSpec((T, 2 * HW), lambda b, s, pt: (b, 0))]
            + _page_specs((2 * HW, PAGE_SIZE), PG, base),
            out_specs=pl.BlockSpec((T, HW), lambda b, s, pt: (b, 0)),
            scratch_shapes=[pltpu.VMEM((R, LANES), F32), pltpu.VMEM((R, LANES), F32), pltpu.VMEM((R, LANES), F32),
                            pltpu.VMEM((n_pages, R, HW), F32)]),
        out_shape=jax.ShapeDtypeStruct((B * T, HW), F32),
        compiler_params=_cparams(("parallel", "arbitrary")),
        name="moba_decode",
    )(page_table, sp["mqr"], row_new, *([view] * PG))


def _nsa_cmp_phys_kernel(xt_ref, pe_ref, w_ref, y_ref, xs_scr, *, G):
    S = NSA_CMP_STRIDE
    for g in range(G):
        xs_scr[g * PAGE_SIZE:(g + 1) * PAGE_SIZE, :] = xt_ref[g].T
    m = G * (PAGE_SIZE // S)
    a = jnp.zeros((m, 2 * HEAD_DIM), F32)
    b = jnp.zeros((m, 2 * HEAD_DIM), F32)
    for t in range(S):
        x = xs_scr[pl.ds(t, m, stride=S), :]
        a = a + _dot((x + pe_ref[0, t:t + 1]).astype(BF), w_ref[0, t])
        b = b + _dot((x + pe_ref[1, t:t + 1]).astype(BF), w_ref[1, t])
    y_ref[...] = jnp.concatenate([a, b], axis=1)


def nsa_compress_pool(pools, l, cmp_pe, cmp_w):
    S = NSA_CMP_STRIDE
    n_phys = pools.shape[1]
    cpp = PAGE_SIZE // S
    pe = jnp.transpose(cmp_pe.reshape(2, 2, S, HEAD_DIM), (1, 2, 0, 3)).reshape(2, S, 2 * HEAD_DIM)
    w = cmp_w.reshape(2, 2, S, HEAD_DIM, HEAD_DIM)
    wz = jnp.zeros((2, S, 2, HEAD_DIM, 2, HEAD_DIM), F32)
    wz = wz.at[:, :, 0, :, 0, :].set(w[0]).at[:, :, 1, :, 1, :].set(w[1])
    wz = wz.reshape(2, S, 2 * HEAD_DIM, 2 * HEAD_DIM).astype(BF)
    G = next(g for g in (32, 16, 8, 4, 2, 1) if n_phys % g == 0)
    view, base = _pool_view(pools, l, HW)
    return pl.pallas_call(
        functools.partial(_nsa_cmp_phys_kernel, G=G),
        grid=(n_phys // G,),
        in_specs=[pl.BlockSpec((G, 2 * HEAD_DIM, PAGE_SIZE), lambda i: (base // G + i, 0, 0)),
                  pl.BlockSpec((2, S, 2 * HEAD_DIM), lambda i: (0, 0, 0)),
                  pl.BlockSpec((2, S, 2 * HEAD_DIM, 2 * HEAD_DIM), lambda i: (0, 0, 0, 0))],
        out_specs=pl.BlockSpec((G * cpp, HW), lambda i: (i, 0)),
        out_shape=jax.ShapeDtypeStruct((n_phys * cpp, HW), F32),
        scratch_shapes=[pltpu.VMEM((G * PAGE_SIZE, 2 * HEAD_DIM), F32)],
        compiler_params=_cparams(("parallel",)),
        name="nsa_compress_pool",
    )(view, pe, wz)


def _nsa_sel_decode_kernel(y_ref, q_ref, gain_ref, cover_ref, ocmp_ref, sel_ref, *, T, offset, n_cmp, n_sel):
    R = NH * T
    y = y_ref[0]
    nch = y.shape[0]
    c = y[:, :2 * HEAD_DIM] + pltpu.roll(y[:, 2 * HEAD_DIM:], nch - 1, 0)
    lane = lax.broadcasted_iota(jnp.int32, c.shape, 1)
    ms = jnp.sum(jnp.where(lane < HEAD_DIM, c * c, 0.0), axis=1, keepdims=True) / HEAD_DIM
    kn = c * lax.rsqrt(ms + RMS_EPS) * gain_ref[...]
    kc = kn[:, :HEAD_DIM].astype(BF)
    vc = c[:, HEAD_DIM:].astype(BF)
    q = q_ref[0].reshape(R, HEAD_DIM).astype(BF)
    pos = offset + lax.broadcasted_iota(jnp.int32, (R, 1), 0) % T
    n_idx = lax.broadcasted_iota(jnp.int32, (R, nch), 1)
    mask = (n_idx * NSA_CMP_STRIDE + (NSA_CMP_BLOCK - 1) <= pos) & (n_idx < n_cmp)
    s_c = jnp.where(mask, _dot_nt(q, kc), NEG_INF)
    m_c = jnp.max(s_c, axis=-1, keepdims=True)
    p_c = jnp.where(mask, jnp.exp(s_c - m_c), 0.0)
    p_c = (p_c / jnp.maximum(jnp.sum(p_c, axis=-1, keepdims=True), 1.0)).astype(BF)
    ocmp_ref[0] = _dot(p_c, vc)
    imp4 = _dot(p_c, cover_ref[...])
    imp = imp4[0:T]
    for h in range(1, NH):
        imp = imp + imp4[h * T:(h + 1) * T]
    j = lax.broadcasted_iota(jnp.int32, imp.shape, 1)
    cur = (offset + lax.broadcasted_iota(jnp.int32, (T, 1), 0)) // NSA_SEL_BLOCK
    forced = (j == 0) | (j == cur) | (j == cur - 1)
    imp = jnp.where(j > cur, NEG_INF, jnp.where(forced, SEL_FORCE, imp))
    rank = jnp.zeros(imp.shape, F32)
    for jp in range(n_sel):
        col = imp[:, jp:jp + 1]
        rank = rank + jnp.where((col > imp) | ((col == imp) & (j > jp)), 1.0, 0.0)
    sel_ref[0] = jnp.where((rank < NSA_TOPN) & (j <= cur), 1.0, 0.0)


def _nsa_decode_kernel(pt_ref, q_ref, sel_ref, ocmp_ref, newr_ref, win_ref, neww_ref, gl_ref, *rest,
                       PG, T, offset):
    page_refs, o_ref = rest[:PG], rest[PG]
    m_scr, l_scr, acc_scr = rest[PG + 1:]
    s = pl.program_id(1)
    R = NH * T
    D = HEAD_DIM

    @pl.when(s == 0)
    def _():
        m_scr[...] = jnp.full_like(m_scr, NEG_INF)
        l_scr[...] = jnp.zeros_like(l_scr)
        acc_scr[...] = jnp.zeros_like(acc_scr)

    q = q_ref[0].reshape(R, D).astype(BF)
    sel = sel_ref[0]
    jl = lax.broadcasted_iota(jnp.int32, sel.shape, 1)
    half = lax.broadcasted_iota(jnp.int32, (T, PAGE_SIZE), 1) < NSA_SEL_BLOCK
    m_i, l_i, acc = m_scr[...], l_scr[...], acc_scr[...]
    for j in range(PG):
        pid = s * PG + j
        page = page_refs[j][0]
        sc = _dot(q, page[2 * D:3 * D].astype(BF))
        sa = jnp.max(jnp.where(jl == 2 * pid, sel, 0.0), axis=1, keepdims=True)
        sb = jnp.max(jnp.where(jl == 2 * pid + 1, sel, 0.0), axis=1, keepdims=True)
        mt = jnp.where(half, sa, sb) > 0.5
        mask = jnp.broadcast_to(mt[None], (NH, T, PAGE_SIZE)).reshape(R, PAGE_SIZE)
        m_i, l_i, acc = _softmax_update(sc, mask, m_i, l_i, acc, page[3 * D:].astype(BF), R, v_feature_major=True)
    m_scr[...] = m_i
    l_scr[...] = l_i
    acc_scr[...] = acc

    @pl.when(s == pl.num_programs(1) - 1)
    def _():
        causal = _new_causal(R, T)
        newr = newr_ref[...]
        cb = offset // NSA_SEL_BLOCK
        own = jnp.broadcast_to((sel[:, cb:cb + 1] > 0.5)[None], (NH, T, T)).reshape(R, T)
        _, l_s, acc_s = _softmax_update(_dot_nt(q, newr[:, 2 * D:3 * D].astype(BF)), causal & own, m_i, l_i, acc,
                                        newr[:, 3 * D:].astype(BF), R)
        o_sel = acc_s / jnp.maximum(l_s, 1.0)
        win = win_ref[0]
        neww = neww_ref[...]
        wb = win.shape[1]
        tq = lax.broadcasted_iota(jnp.int32, (R, wb), 0) % T
        rk = lax.broadcasted_iota(jnp.int32, (R, wb), 1)
        mask1 = rk > wb + tq - NSA_WINDOW
        s1 = jnp.where(mask1, _dot(q, win[:D].astype(BF)), NEG_INF)
        s2 = jnp.where(causal, _dot_nt(q, neww[:, :D].astype(BF)), NEG_INF)
        m_w = jnp.maximum(jnp.max(s1, axis=-1, keepdims=True), jnp.max(s2, axis=-1, keepdims=True))
        p1 = jnp.where(mask1, jnp.exp(s1 - m_w), 0.0)
        p2 = jnp.where(causal, jnp.exp(s2 - m_w), 0.0)
        l_w = jnp.sum(p1, axis=-1, keepdims=True) + jnp.sum(p2, axis=-1, keepdims=True)
        o_win = (_dot_nt(p1.astype(BF), win[D:].astype(BF)) + _dot(p2.astype(BF), neww[:, D:].astype(BF))) \
            / jnp.maximum(l_w, 1.0)
        o_cmp = ocmp_ref[0]
        sig = jax.nn.sigmoid(gl_ref[...])
        outs = []
        for h in range(NH):
            c = MISC_LANE + 3 * h
            rs = slice(h * T, (h + 1) * T)
            outs.append(sig[:, c:c + 1] * o_cmp[rs] + sig[:, c + 1:c + 2] * o_sel[rs] + sig[:, c + 2:c + 3] * o_win[rs])
        o_ref[...] = jnp.concatenate(outs, axis=1)


def nsa_decode(sp, z, row0, pools, l, page_table, win_state, cmp_pe, cmp_w, gain1, B, T):
    n_pages = page_table.shape[1]
    past = n_pages * PAGE_SIZE
    S = past + T
    R = NH * T
    n_cmp = (S - NSA_CMP_BLOCK) // NSA_CMP_STRIDE + 1
    n_sel = -(-S // NSA_SEL_BLOCK)
    nch = past // NSA_CMP_STRIDE
    assert (n_cmp - 1) * NSA_CMP_STRIDE + NSA_CMP_BLOCK <= past, "compressed blocks must lie inside the cache"
    assert past % NSA_SEL_BLOCK == 0 and T <= NSA_SEL_BLOCK and n_sel <= HW
    PG = min(DEC_PG, n_pages)
    cpp = PAGE_SIZE // NSA_CMP_STRIDE
    y = nsa_compress_pool(pools, l, cmp_pe, cmp_w).reshape(pools.shape[1], cpp, HW)
    yb = y[page_table].reshape(B, nch, HW)
    ci = np.arange(nch)[:, None] * NSA_CMP_STRIDE
    sj = np.arange(HW)[None, :] * NSA_SEL_BLOCK
    cover = (ci < sj + NSA_SEL_BLOCK) & (ci + NSA_CMP_BLOCK > sj) & (np.arange(nch)[:, None] < n_cmp) \
        & (np.arange(HW)[None, :] < n_sel)
    cover = jnp.asarray(cover.astype(np.float32), dtype=BF)
    gain = jnp.concatenate([gain1, jnp.ones((HEAD_DIM,), F32)])[None, :]
    qspec = lambda nargs: pl.BlockSpec((1, NH, T, HEAD_DIM), (lambda b: (0, 0, b, 0)) if nargs == 1
                                       else (lambda b, s, pt: (0, 0, b, 0)))
    ocmp, sel = pl.pallas_call(
        functools.partial(_nsa_sel_decode_kernel, T=T, offset=past, n_cmp=n_cmp, n_sel=n_sel),
        grid=(B,),
        in_specs=[pl.BlockSpec((1, nch, HW), lambda b: (b, 0, 0)), qspec(1),
                  pl.BlockSpec((1, 2 * HEAD_DIM), lambda b: (0, 0)),
                  pl.BlockSpec((nch, HW), lambda b: (0, 0))],
        out_specs=[pl.BlockSpec((1, R, HEAD_DIM), lambda b: (b, 0, 0)), pl.BlockSpec((1, T, HW), lambda b: (b, 0, 0))],
        out_shape=[jax.ShapeDtypeStruct((B, R, HEAD_DIM), F32), jax.ShapeDtypeStruct((B, T, HW), F32)],
        compiler_params=_cparams(("parallel",)),
        name="nsa_select_decode",
    )(yb, sp["nqn"], gain, cover)
    view, base = _pool_view(pools, l, HW)
    wb = win_state.shape[1]
    rb0 = row0 // T
    return pl.pallas_call(
        functools.partial(_nsa_decode_kernel, PG=PG, T=T, offset=past),
        grid_spec=pltpu.PrefetchScalarGridSpec(
            num_scalar_prefetch=1, grid=(B, n_pages // PG),
            in_specs=[qspec(3),
                      pl.BlockSpec((1, T, HW), lambda b, s, pt: (b, 0, 0)),
                      pl.BlockSpec((1, R, HEAD_DIM), lambda b, s, pt: (b, 0, 0)),
                      pl.BlockSpec((T, HW), lambda b, s, pt: (b, 0)),
                      pl.BlockSpec((1, 2 * HEAD_DIM, wb), lambda b, s, pt: (b, 0, 0)),
                      pl.BlockSpec((T, 2 * HEAD_DIM), lambda b, s, pt: (b, 0)),
                      pl.BlockSpec((T, HW), lambda b, s, pt: (rb0 + b, SEC_S2))]
            + _page_specs((HW, PAGE_SIZE), PG, base),
            out_specs=pl.BlockSpec((T, HW), lambda b, s, pt: (b, 0)),
            scratch_shapes=[pltpu.VMEM((R, 1), F32), pltpu.VMEM((R, 1), F32), pltpu.VMEM((R, HEAD_DIM), F32)]),
        out_shape=jax.ShapeDtypeStruct((B * T, HW), F32),
        compiler_params=_cparams(("parallel", "arbitrary")),
        name="nsa_decode",
    )(page_table, sp["nqr"], sel, ocmp, sp["nsa_rows"], jnp.transpose(win_state, (0, 2, 3, 1)).reshape(B, 2 * HEAD_DIM, wb), sp["nsa_win"], z,
      *([view] * PG))


def rmsnorm(x, g):
    xf = x.astype(jnp.float32)
    y = xf * lax.rsqrt(jnp.mean(xf * xf, axis=-1, keepdims=True) + RMS_EPS)
    return (y * g.astype(jnp.float32)).astype(x.dtype)


def rope(x, pos):
    half = HEAD_DIM // 2
    inv = ROPE_THETA ** (-jnp.arange(half, dtype=jnp.float32) / half)
    ang = pos.astype(jnp.float32)[:, None] * inv[None, :]
    cos = jnp.cos(ang)[:, None, :]
    sin = jnp.sin(ang)[:, None, :]
    xf = x.astype(jnp.float32)
    x1, x2 = xf[..., :half], xf[..., half:]
    return jnp.concatenate([x1 * cos - x2 * sin, x2 * cos + x1 * sin], axis=-1).astype(x.dtype)


def masked_softmax(s, mask):
    s = jnp.where(mask, s.astype(jnp.float32), NEG_INF)
    m = jnp.max(s, axis=-1, keepdims=True)
    p = jnp.where(mask, jnp.exp(s - m), 0.0)
    return p / jnp.maximum(jnp.sum(p, axis=-1, keepdims=True), 1.0)


def sweep(fn, blk, *arrays):
    B, T = arrays[0].shape[:2]
    nb = -(-T // blk)
    Tp = nb * blk
    blocks = []
    for a in arrays:
        a = jnp.pad(a, [(0, 0), (0, Tp - T)] + [(0, 0)] * (a.ndim - 2))
        blocks.append(jnp.moveaxis(a.reshape((B, nb, blk) + a.shape[2:]), 1, 0))
    starts = jnp.arange(nb, dtype=jnp.int32) * blk
    out = lax.map(lambda args: fn(args[0], *args[1]), (starts, tuple(blocks)))
    out = jnp.moveaxis(out, 0, 1).reshape((B, Tp) + out.shape[3:])
    return out[:, :T]


def window_attend(q, rows, buf_len):
    B, T, H, Dh = q.shape
    dt = q.dtype
    W = NSA_WINDOW
    qb = min(Q_BLOCK, T)
    nb = -(-T // qb)
    Tp = nb * qb
    band = W + qb
    rp = jnp.pad(rows, ((0, 0), (W, Tp - T), (0, 0), (0, 0)))
    kidx = buf_len + np.arange(nb)[:, None] * qb + np.arange(band)[None, :]
    kb = rp[:, kidx]
    qp = jnp.pad(q, ((0, 0), (0, Tp - T), (0, 0), (0, 0))).reshape(B, nb, qb, H, Dh)
    s = jnp.einsum('bnqhd,bnkd->bhnqk', qp, kb[..., 0, :], preferred_element_type=jnp.float32) * ATTN_SCALE
    qq = buf_len + np.arange(nb)[:, None] * qb + np.arange(qb)[None, :]
    kk = kidx - W
    mask = (kk[:, None, :] >= 0) & (kk[:, None, :] <= qq[:, :, None]) & (kk[:, None, :] > qq[:, :, None] - W)
    p = masked_softmax(s, mask)
    o = jnp.einsum('bhnqk,bnkd->bnqhd', p.astype(dt), kb[..., 1, :]).reshape(B, Tp, H, Dh)
    return o[:, :T]


def nsa_mixer(q, kv_c, kv_s, kv_w, gate_logits, past_rows, win_buf, qk_gain, cmp_pe, cmp_w, offset):
    B, T, H, Dh = q.shape
    dt = q.dtype
    pos = offset + jnp.arange(T, dtype=jnp.int32)
    qn = rmsnorm(q, qk_gain[0])
    qr = rope(qn, pos)
    k_s = rope(rmsnorm(kv_s[:, :, 0:1], qk_gain[2]), pos)[:, :, 0]
    k_w = rope(rmsnorm(kv_w[:, :, 0:1], qk_gain[3]), pos)[:, :, 0]
    new_rows = jnp.stack([kv_c[:, :, 0], kv_c[:, :, 1], k_s, kv_s[:, :, 1]], axis=2)
    rows = jnp.concatenate([past_rows, new_rows], axis=1)
    S = offset + T
    n_cmp = (S - NSA_CMP_BLOCK) // NSA_CMP_STRIDE + 1
    cidx = np.arange(n_cmp)[:, None] * NSA_CMP_STRIDE + np.arange(NSA_CMP_BLOCK)[None, :]

    def compress(r):
        blocks = rows[:, :, r][:, cidx] + cmp_pe[r]
        return blocks.reshape(B, n_cmp, NSA_CMP_BLOCK * Dh) @ cmp_w[r]

    k_cmp = rmsnorm(compress(0), qk_gain[1])
    v_cmp = compress(1)
    s_c = jnp.einsum('bthd,bnd->bhtn', qn, k_cmp, preferred_element_type=jnp.float32) * ATTN_SCALE
    cmp_end = np.arange(n_cmp) * NSA_CMP_STRIDE + NSA_CMP_BLOCK - 1
    p_c = masked_softmax(s_c, cmp_end[None, :] <= pos[:, None])
    o_cmp = jnp.einsum('bhtn,bnd->bthd', p_c.astype(dt), v_cmp)
    n_sel = -(-S // NSA_SEL_BLOCK)
    ci = np.arange(n_cmp)[:, None] * NSA_CMP_STRIDE
    sj = np.arange(n_sel)[None, :] * NSA_SEL_BLOCK
    cover = ((ci < sj + NSA_SEL_BLOCK) & (ci + NSA_CMP_BLOCK > sj)).astype(np.float32)
    imp = jnp.einsum('bhtn,nj->btj', p_c, jnp.asarray(cover))
    cur = (pos // NSA_SEL_BLOCK)[:, None]
    jj = jnp.arange(n_sel)[None, :]
    forced = (jj == 0) | (jj == cur) | (jj == cur - 1)
    imp = jnp.where(jj > cur, NEG_INF, jnp.where(forced, SEL_FORCE, imp))
    _, sel_idx = lax.top_k(imp, min(NSA_TOPN, n_sel))
    kv_sel = jnp.pad(rows[:, :, 2:4], ((0, 0), (0, n_sel * NSA_SEL_BLOCK - S), (0, 0), (0, 0)))
    kv_sel = kv_sel.reshape(B, n_sel, NSA_SEL_BLOCK, 2, Dh)
    bidx = jnp.arange(B)[:, None, None]

    def sel_block(start, q_blk, idx_blk):
        qb = q_blk.shape[1]
        tp = offset + start + jnp.arange(qb)
        g = kv_sel[bidx, idx_blk]
        kpos = idx_blk[..., None] * NSA_SEL_BLOCK + jnp.arange(NSA_SEL_BLOCK)
        mask = (kpos <= tp[None, :, None, None]).reshape(B, 1, qb, -1)
        g = g.reshape(B, qb, -1, 2, Dh)
        s = jnp.einsum('bqhd,bqkd->bhqk', q_blk, g[..., 0, :], preferred_element_type=jnp.float32) * ATTN_SCALE
        p = masked_softmax(s, mask)
        return jnp.einsum('bhqk,bqkd->bqhd', p.astype(dt), g[..., 1, :])

    o_sel = sweep(sel_block, min(GATHER_Q_BLOCK, T), qr, sel_idx)
    win_rows = jnp.concatenate([win_buf, jnp.stack([k_w, kv_w[:, :, 1]], axis=2)], axis=1)
    o_win = window_attend(qr, win_rows, win_buf.shape[1])
    gates = jax.nn.sigmoid(gate_logits.astype(jnp.float32)).astype(dt)
    o = gates[..., 0:1] * o_cmp + gates[..., 1:2] * o_sel + gates[..., 2:3] * o_win
    new_win = win_rows[:, -min(NSA_WINDOW, win_rows.shape[1]):]
    return o, new_rows, new_win


def gated_recurrence(q, k, v, logf, S0):
    B, T, H, DK = q.shape
    DV = v.shape[-1]
    C = math.gcd(T, HGRN_CHUNK)
    nc = T // C

    def chunks(a):
        return jnp.moveaxis(a.reshape((B, nc, C) + a.shape[2:]), 1, 0).swapaxes(2, 3)

    causal = jnp.tril(jnp.ones((C, C), dtype=bool))[:, :, None]

    def step(S, inp):
        qc, kc, vc, gc = inp
        b = jnp.cumsum(gc, axis=2)
        o_inter = jnp.einsum('bhtk,bhkv->bhtv', qc * jnp.exp(b), S)
        diff = b[:, :, :, None, :] - b[:, :, None, :, :]
        decay = jnp.where(causal, jnp.exp(jnp.where(causal, diff, 0.0)), 0.0)
        A = jnp.einsum('bhtk,bhsk,bhtsk->bhts', qc, kc, decay)
        o = o_inter + jnp.einsum('bhts,bhsv->bhtv', A, vc)
        b_last = b[:, :, -1:, :]
        S_new = jnp.exp(b_last[:, :, 0, :])[..., None] * S + jnp.einsum('bhsk,bhsv->bhkv', kc * jnp.exp(b_last - b), vc)
        return S_new, o

    S, o = lax.scan(step, S0, (chunks(q), chunks(k), chunks(v), chunks(logf)))
    o = jnp.moveaxis(o.swapaxes(2, 3), 0, 1).reshape(B, T, H, DV)
    return o, S


def hgrn2_mixer(q, f, i, g, S0, lb, out_gain):
    dt = q.dtype
    H = q.shape[2]
    lb = lb.reshape(H, HGRN_DK)
    z = f.astype(jnp.float32)
    logf = jnp.log(lb + (1.0 - lb) * jax.nn.sigmoid(z))
    k = (1.0 - lb) * jax.nn.sigmoid(-z)
    qf = jax.nn.silu(q.astype(jnp.float32))
    o, S = gated_recurrence(qf, k, i.astype(jnp.float32), logf, S0)
    o = rmsnorm(o, out_gain) * jax.nn.silu(g.astype(jnp.float32))
    return o.astype(dt), S


def moba_mixer(q, k, v, past_rows, qk_gain, offset):
    B, T, H, Dh = q.shape
    dt = q.dtype
    pos = offset + jnp.arange(T, dtype=jnp.int32)
    qr = rope(rmsnorm(q, qk_gain[0]), pos)
    kr = rope(rmsnorm(k, qk_gain[1]), pos)
    new_rows = jnp.stack([kr, v], axis=2)
    rows = jnp.concatenate([past_rows, new_rows], axis=1)
    S = offset + T
    nblk = -(-S // MOBA_BLOCK)
    kvb = jnp.pad(rows, ((0, 0), (0, nblk * MOBA_BLOCK - S), (0, 0), (0, 0), (0, 0)))
    kvb = jnp.transpose(kvb.reshape(B, nblk, MOBA_BLOCK, 2, H, Dh), (0, 4, 1, 2, 3, 5))
    kmean = jnp.mean(kvb[..., 0, :].astype(jnp.float32), axis=3)
    gate = jnp.einsum('bthd,bhnd->bthn', qr.astype(jnp.float32), kmean)
    own = pos // MOBA_BLOCK
    past_ok = jnp.arange(nblk)[None, :] < own[:, None]
    _, top = lax.top_k(jnp.where(past_ok[None, :, None, :], gate, NEG_INF), min(MOBA_TOPK, nblk))
    valid = top < own[None, :, None, None]
    idx = jnp.concatenate([top, jnp.broadcast_to(own[None, :, None, None], (B, T, H, 1)).astype(top.dtype)], axis=-1)
    ok = jnp.concatenate([valid, jnp.ones((B, T, H, 1), dtype=bool)], axis=-1)
    bidx = jnp.arange(B)[:, None, None, None]
    hidx = jnp.arange(H)[None, None, :, None]

    def blk_fn(start, q_blk, idx_blk, ok_blk):
        qb = q_blk.shape[1]
        tp = offset + start + jnp.arange(qb)
        g = kvb[bidx, hidx, idx_blk]
        kpos = idx_blk[..., None] * MOBA_BLOCK + jnp.arange(MOBA_BLOCK)
        mask = (ok_blk[..., None] & (kpos <= tp[None, :, None, None, None])).reshape(B, qb, H, -1)
        g = g.reshape(B, qb, H, -1, 2, Dh)
        s = jnp.einsum('bqhd,bqhkd->bqhk', q_blk, g[..., 0, :], preferred_element_type=jnp.float32) * ATTN_SCALE
        p = masked_softmax(s, mask)
        return jnp.einsum('bqhk,bqhkd->bqhd', p.astype(dt), g[..., 1, :])

    o = sweep(blk_fn, min(GATHER_Q_BLOCK, T), qr, idx, ok)
    return o, new_rows


def fox_mixer(q, k, v, f_logit, past_kv, past_logf, qk_gain, f_bias, offset):
    B, T, H, Dh = q.shape
    dt = q.dtype
    qn = rmsnorm(q, qk_gain[0])
    kn = rmsnorm(k, qk_gain[1])
    logf_new = jax.nn.log_sigmoid(f_logit.astype(jnp.float32) + f_bias.astype(jnp.float32))
    new_rows = jnp.stack([kn, v], axis=2)
    rows = jnp.concatenate([past_kv, new_rows], axis=1)
    c = jnp.cumsum(jnp.concatenate([past_logf.astype(jnp.float32), logf_new], axis=1), axis=1)
    S = offset + T
    K = rows[:, :, 0]
    V = rows[:, :, 1]
    c_k = jnp.moveaxis(c, 1, 2)[:, :, None, :]
    kpos = jnp.arange(S)

    def blk_fn(start, q_blk, cq_blk):
        qb = q_blk.shape[1]
        tp = offset + start + jnp.arange(qb)
        s = jnp.einsum('bqhd,bkhd->bhqk', q_blk, K, preferred_element_type=jnp.float32) * ATTN_SCALE
        s = s + jnp.moveaxis(cq_blk, 1, 2)[..., None] - c_k
        p = masked_softmax(s, kpos[None, :] <= tp[:, None])
        return jnp.einsum('bhqk,bkhd->bqhd', p.astype(dt), V)

    o = sweep(blk_fn, min(Q_BLOCK, T), qn, c[:, offset:])
    return o, new_rows, logf_new.astype(dt)


def _outproj_kernel(x_ref, pn_ref, ph_ref, pm_ref, pf_ref, sn_ref, sh_ref, sm_ref, sf_ref, w_ref, g_ref,
                    xo_ref, hn_ref, *, n_prompt_blocks):
    i = pl.program_id(0)

    def project(parts):
        acc = x_ref[...]
        for m, part in enumerate(parts):
            acc = acc + _dot(part[...].astype(BF), w_ref[m * HW:(m + 1) * HW, :])
        xo_ref[...] = acc
        hn = acc * lax.rsqrt(jnp.mean(acc * acc, axis=-1, keepdims=True) + RMS_EPS) * g_ref[...]
        hn_ref[...] = hn.astype(hn_ref.dtype)

    pl.when(i < n_prompt_blocks)(lambda: project((pn_ref, ph_ref, pm_ref, pf_ref)))
    pl.when(i >= n_prompt_blocks)(lambda: project((sn_ref, sh_ref, sm_ref, sf_ref)))


def out_projection(x, prompt_parts, sample_parts, w_out, g):
    N, D = x.shape
    tm = PREP_TQ
    npb = prompt_parts[0].shape[0] // tm
    assert sample_parts[0].shape[0] == tm and N == (npb + 1) * tm
    pspec = pl.BlockSpec((tm, HW), lambda i: (jnp.minimum(i, npb - 1), 0))
    sspec = pl.BlockSpec((tm, HW), lambda i: (0, 0))
    row = pl.BlockSpec((tm, D), lambda i: (i, 0))
    return pl.pallas_call(
        functools.partial(_outproj_kernel, n_prompt_blocks=npb),
        grid=(npb + 1,),
        in_specs=[row] + [pspec] * 4 + [sspec] * 4 + [pl.BlockSpec((MIX_WIDTH, D), lambda i: (0, 0)),
                                                     pl.BlockSpec((1, D), lambda i: (0, 0))],
        out_specs=[row, row],
        out_shape=[jax.ShapeDtypeStruct((N, D), F32), jax.ShapeDtypeStruct((N, D), BF)],
        compiler_params=_cparams(("parallel",)),
        name="out_projection_rmsnorm",
    )(x, *prompt_parts, *sample_parts, w_out.astype(BF), g[None, :])


def _ffn_up_kernel(be_ref, new_ref, x_ref, w1_ref, w3_ref, u_ref, w1_scr, w3_scr):
    i = pl.program_id(1)

    @pl.when(new_ref[i] == 1)
    def _():
        w1_scr[...] = w1_ref[0].astype(BF)
        w3_scr[...] = w3_ref[0].astype(BF)

    x = x_ref[...].astype(BF)
    a = _dot(x, w1_scr[...])
    b = _dot(x, w3_scr[...])
    u_ref[...] = (a * jax.nn.sigmoid(a) * b).astype(u_ref.dtype)


def _ffn_down_kernel(be_ref, new_ref, u_ref, w2_ref, *rest, residual):
    res_ref = rest[0] if residual else None
    y_ref, w2_scr = rest[-2], rest[-1]
    i = pl.program_id(1)

    @pl.when(new_ref[i] == 1)
    def _():
        w2_scr[...] = w2_ref[0].astype(BF)

    y = _dot(u_ref[...], w2_scr[...])
    y_ref[...] = res_ref[...] + y if residual else y


def grouped_swiglu(x, block_exp, w1, w3, w2, tm, tf, tn, residual=None):
    R, D = x.shape
    F = w1.shape[2]
    nblk = R // tm
    block_exp = block_exp.astype(jnp.int32)
    new = jnp.concatenate([jnp.ones((1,), jnp.int32), (block_exp[1:] != block_exp[:-1]).astype(jnp.int32)])
    u = pl.pallas_call(
        _ffn_up_kernel,
        grid_spec=pltpu.PrefetchScalarGridSpec(
            num_scalar_prefetch=2, grid=(F // tf, nblk),
            in_specs=[pl.BlockSpec((tm, D), lambda j, i, be, nw: (i, 0)),
                      pl.BlockSpec((1, D, tf), lambda j, i, be, nw: (be[i], 0, j)),
                      pl.BlockSpec((1, D, tf), lambda j, i, be, nw: (be[i], 0, j))],
            out_specs=pl.BlockSpec((tm, tf), lambda j, i, be, nw: (i, j)),
            scratch_shapes=[pltpu.VMEM((D, tf), BF), pltpu.VMEM((D, tf), BF)]),
        out_shape=jax.ShapeDtypeStruct((R, F), BF),
        compiler_params=_cparams(("arbitrary", "arbitrary")),
        name="swiglu_up",
    )(block_exp, new, x, w1, w3)
    out_block = pl.BlockSpec((tm, tn), lambda n, i, be, nw: (i, n))
    extra = () if residual is None else (residual,)
    return pl.pallas_call(
        functools.partial(_ffn_down_kernel, residual=residual is not None),
        grid_spec=pltpu.PrefetchScalarGridSpec(
            num_scalar_prefetch=2, grid=(D // tn, nblk),
            in_specs=[pl.BlockSpec((tm, F), lambda n, i, be, nw: (i, 0)),
                      pl.BlockSpec((1, F, tn), lambda n, i, be, nw: (be[i], 0, n))] + [out_block] * len(extra),
            out_specs=out_block,
            scratch_shapes=[pltpu.VMEM((F, tn), BF)]),
        out_shape=jax.ShapeDtypeStruct((R, D), F32),
        compiler_params=_cparams(("arbitrary", "arbitrary")),
        name="swiglu_down",
    )(block_exp, new, u, w2, *extra)


def swiglu_dense(hn, x, w1, w3, w2):
    tm = 640 if hn.shape[0] % 640 == 0 else 256
    be = jnp.zeros((hn.shape[0] // tm,), jnp.int32)
    return grouped_swiglu(hn, be, w1[None], w3[None], w2[None], tm, DENSE_TF, FFN_TN, residual=x)


def moe_ffn_grouped(xf, router, w1, w3, w2):
    N, D = xf.shape
    tm = MOE_TM
    rpad = jnp.pad(router, ((0, 0), (0, LANES - N_EXPERTS)))
    logits = matmul(xf, rpad, tm=256, tn=LANES)[:, :N_EXPERTS]
    top_v, top_e = lax.top_k(logits, TOP_K)
    gates = jax.nn.softmax(top_v, axis=-1)
    NK = N * TOP_K
    flat_e = top_e.reshape(NK)
    order = jnp.argsort(flat_e)
    e_sorted = flat_e[order]
    tok_sorted = (order // TOP_K).astype(jnp.int32)
    counts = jnp.sum((flat_e[:, None] == jnp.arange(N_EXPERTS)[None, :]).astype(jnp.int32), axis=0)
    padded = (counts + tm - 1) // tm * tm
    pend = jnp.cumsum(padded)
    pstart = pend - padded
    start = jnp.cumsum(counts) - counts
    dest_sorted = pstart[e_sorted] + (jnp.arange(NK, dtype=jnp.int32) - start[e_sorted])
    n_blocks = -(-NK // tm) + N_EXPERTS
    slot_tok = jnp.full((n_blocks * tm,), N, jnp.int32).at[dest_sorted].set(tok_sorted)
    block_exp = jnp.clip(jnp.searchsorted(pend, jnp.arange(n_blocks) * tm, side='right'), 0, N_EXPERTS - 1)
    xpad = jnp.concatenate([xf, jnp.zeros((1, D), xf.dtype)], axis=0)
    xb = xpad[slot_tok]
    yb = grouped_swiglu(xb, block_exp, w1, w3, w2, tm, MOE_TF, MOE_TN)
    dest = jnp.zeros((NK,), jnp.int32).at[order].set(dest_sorted).reshape(N, TOP_K)
    return yb[dest[:, 0]] * gates[:, 0:1] + yb[dest[:, 1]] * gates[:, 1:2]


def swiglu(h, w1, w3, w2):
    return (jax.nn.silu(h @ w1) * (h @ w3)) @ w2


def moe_ffn(xf, router, w1, w3, w2):
    N, D = xf.shape
    dt = xf.dtype
    logits = (xf @ router).astype(jnp.float32)
    top_v, top_e = lax.top_k(logits, TOP_K)
    gates = jax.nn.softmax(top_v, axis=-1)
    NK = N * TOP_K
    flat_e = top_e.reshape(NK)
    flat_tok = jnp.arange(NK, dtype=jnp.int32) // TOP_K
    order = jnp.argsort(flat_e)
    e_sorted = flat_e[order]
    tok_sorted = flat_tok[order]
    counts = jnp.zeros((N_EXPERTS,), jnp.int32).at[flat_e].add(1)
    padded = (counts + MOE_BLOCK - 1) // MOE_BLOCK * MOE_BLOCK
    pend = jnp.cumsum(padded)
    pstart = pend - padded
    start = jnp.cumsum(counts) - counts
    dest = pstart[e_sorted] + (jnp.arange(NK, dtype=jnp.int32) - start[e_sorted])
    n_blocks = -(-NK // MOE_BLOCK) + N_EXPERTS
    slot_tok = jnp.full((n_blocks * MOE_BLOCK,), N, jnp.int32).at[dest].set(tok_sorted)
    block_exp = jnp.clip(jnp.searchsorted(pend, jnp.arange(n_blocks) * MOE_BLOCK, side='right'), 0, N_EXPERTS - 1)
    xpad = jnp.concatenate([xf, jnp.zeros((1, D), dt)], axis=0)
    xb = xpad[slot_tok].reshape(n_blocks, MOE_BLOCK, D)

    def expert_block(args):
        xblk, e = args
        return swiglu(xblk, w1[e], w3[e], w2[e])

    yb = lax.map(expert_block, (xb, block_exp)).reshape(n_blocks * MOE_BLOCK, D)
    y_assign = yb[dest] * gates.reshape(NK)[order][:, None].astype(dt)
    return jnp.zeros((N, D), dt).at[tok_sorted].add(y_assign)


def z_sections(z):
    s = lambda c, a=0, b=HW: z[..., c * HW + a:c * HW + b]
    d = HEAD_DIM
    return dict(nq=s(SEC_NQ), nkc=s(SEC_S1, 0, 2 * d), nks=s(SEC_S1, 2 * d, 4 * d), nkw=s(SEC_S2, 0, 2 * d),
                ngate=s(SEC_S2, MISC_LANE, MISC_LANE + 12), ff=s(SEC_S2, MISC_LANE + 12, MISC_LANE + 16),
                hq=s(SEC_HQ), hf=s(SEC_HF), hi=s(SEC_HI), hg=s(SEC_HG), mq=s(SEC_MQ), mk=s(SEC_MK), mv=s(SEC_MV),
                fq=s(SEC_FQ), fk=s(SEC_FK), fv=s(SEC_FV))


def kernel(x_prompt, x_sample, cache_nsa, state_nsa_win, state_hgrn, cache_moba, cache_fox_kv, cache_fox_logf,
           page_table, g_mix, g_ffn, w_in, w_out, nsa_qk_gain, nsa_cmp_pe, nsa_cmp_w, hgrn_lb_logits,
           hgrn_out_gain, moba_qk_gain, fox_qk_gain, fox_f_bias, ffn_w1, ffn_w3, ffn_w2, moe_router,
           moe_w1, moe_w3, moe_w2):
    dt = x_prompt.dtype
    Bp, Tp, D = x_prompt.shape
    Bs, Ts, _ = x_sample.shape
    Np, Ns = Bp * Tp, Bs * Ts
    past_len = page_table.shape[1] * PAGE_SIZE
    lb_w = jax.nn.softmax(hgrn_lb_logits.astype(jnp.float32), axis=0)
    lower_bounds = jnp.cumsum(lb_w, axis=0) - lb_w[0:1]

    def gather_pages(pool):
        g = pool[page_table]
        return g.reshape((Bs, past_len) + pool.shape[2:])

    cos_p, sin_p = rope_tables(jnp.arange(Tp, dtype=jnp.int32))
    cos_s, sin_s = rope_tables(past_len + jnp.arange(Ns, dtype=jnp.int32) % Ts)
    assert Ns == PREP_TQ and Np % PREP_TQ == 0
    gmat = group_mean_matrix()
    cover = nsa_constants(Tp)

    x = jnp.concatenate([x_prompt.reshape(Np, D), x_sample.reshape(Ns, D)], axis=0)
    st_p, st_s = [], []
    for l in range(DEPTH):
        i = l // 2
        z = in_projection(x, g_mix[l][None, :], relayout_w_in(w_in[l]),
                          tm=640 if x.shape[0] % 640 == 0 else 256)

        gains = head_gains(nsa_qk_gain[l], moba_qk_gain[l], fox_qk_gain[l])
        pp = prep_prompt(z, 0, Bp, Tp, cos_p, sin_p, gains, gmat)
        pe_flat, w_flat, cgain = nsa_compress_weights(nsa_cmp_pe[l], nsa_cmp_w[l], nsa_qk_gain[l][1])
        kc, vc = nsa_compress(pp["nsa_kc"], Bp, Tp, pe_flat, w_flat, cgain)
        o_nsa_p = nsa_attention_prompt(pp, kc, vc, z, 0, Bp, Tp, cover)
        o_mb_p = moba_attention_prompt(pp, Bp, Tp)
        ff_lo = SEC_S2 * HW + MISC_LANE + 12
        ff_p = z[:Np, ff_lo:ff_lo + NH].reshape(Bp, Tp, NH)
        logf_p = jax.nn.log_sigmoid(ff_p + fox_f_bias[l].astype(F32))
        o_fx_p = fox_attention_prompt(pp, jnp.cumsum(jnp.transpose(logf_p, (0, 2, 1)), axis=2), Bp, Tp)
        hd = lambda a, n, d, B, T: a.reshape(B, T, n, d)
        o_hg_p, hg_state_p = hgrn_mixer(z, 0, Bp, Tp, jnp.zeros((Bp, NH, HGRN_DK, HGRN_DV), F32), lower_bounds[l],
                                        hgrn_out_gain[l], gmat)
        nsa_win_p = pp["nsa_win"].reshape(Bp, Tp, 2, HEAD_DIM)[:, -min(NSA_WINDOW, Tp):]
        st_p.append((pp["nsa_rows"].reshape(Bp, Tp, NSA_ROWS, HEAD_DIM), nsa_win_p, hg_state_p.astype(dt),
                     pp["moba_rows"].reshape(Bp, Tp, 2, NH, HEAD_DIM), pp["fox_rows"].reshape(Bp, Tp, 2, NH, HEAD_DIM),
                     logf_p.astype(dt)))

        sp = prep_prompt(z, Np, 1, Ns, cos_s, sin_s, gains, gmat, qdt=F32)
        o_nsa_s = nsa_decode(sp, z, Np, cache_nsa, l, page_table, state_nsa_win[l], nsa_cmp_pe[l], nsa_cmp_w[l],
                             nsa_qk_gain[l][1], Bs, Ts)
        o_hg_s, hg_state = hgrn_mixer(z, Np, Bs, Ts, state_hgrn[l].astype(F32), lower_bounds[l], hgrn_out_gain[l], gmat)
        o_mb_s = moba_decode(sp, cache_moba, l, page_table, sp["moba_rows"], Bs, Ts)
        ff_s = z[Np:, SEC_S2 * HW + MISC_LANE + 12:SEC_S2 * HW + MISC_LANE + 16].reshape(Bs, Ts, NH)
        logf_s = jax.nn.log_sigmoid(ff_s + fox_f_bias[l].astype(F32))
        c_s = jnp.cumsum(jnp.transpose(jnp.concatenate([gather_pages(cache_fox_logf[l]).astype(F32), logf_s], axis=1),
                                       (0, 2, 1)), axis=2)
        o_fx_s = fox_decode(sp, cache_fox_kv, l, page_table, c_s, sp["fox_rows"], Bs, Ts)
        win_rows = jnp.concatenate([state_nsa_win[l], sp["nsa_win"].reshape(Bs, Ts, 2, HEAD_DIM)], axis=1)
        st_s.append((sp["nsa_rows"].reshape(Bs, Ts, NSA_ROWS, HEAD_DIM),
                     win_rows[:, -min(NSA_WINDOW, win_rows.shape[1]):], hg_state.astype(dt),
                     sp["moba_rows"].reshape(Bs, Ts, 2, NH, HEAD_DIM), sp["fox_rows"].reshape(Bs, Ts, 2, NH, HEAD_DIM),
                     logf_s.astype(dt)))

        x, hn = out_projection(x, (o_nsa_p, o_hg_p, o_mb_p, o_fx_p), (o_nsa_s, o_hg_s, o_mb_s, o_fx_s),
                               w_out[l], g_ffn[l])
        if l % 2 == 0:
            x = swiglu_dense(hn, x, ffn_w1[i], ffn_w3[i], ffn_w2[i])
        else:
            x = x + moe_ffn_grouped(hn, moe_router[i], moe_w1[i], moe_w3[i], moe_w2[i])

    def stk(states, j):
        return jnp.stack([s[j] for s in states], axis=0)

    return (x[:Np].reshape(Bp, Tp, D), x[Np:].reshape(Bs, Ts, D),
            stk(st_p, 0), stk(st_s, 0), stk(st_p, 1), stk(st_s, 1), stk(st_p, 2), stk(st_s, 2),
            stk(st_p, 3), stk(st_s, 3), stk(st_p, 4), stk(st_s, 4), stk(st_p, 5), stk(st_s, 5))
```

```python
import math, functools
import jax, jax.numpy as jnp
from jax import lax
import numpy as np
from jax.experimental import pallas as pl
from jax.experimental.pallas import tpu as pltpu

D_MODEL = 1024
DEPTH = 2
PAGE_SIZE = 128
HEAD_DIM = 64
H_NSA = 4
H_HGRN = 4
H_MOBA = 4
H_FOX = 4
NH = 4
HW = NH * HEAD_DIM
MIX_WIDTH = (H_NSA + H_HGRN + H_MOBA + H_FOX) * HEAD_DIM
HGRN_DK = 64
HGRN_DV = HEAD_DIM
HGRN_CHUNK = 64
NSA_CMP_BLOCK = 32
NSA_CMP_STRIDE = 16
NSA_SEL_BLOCK = 64
NSA_TOPN = 16
NSA_WINDOW = 512
NSA_ROWS = 4
MOBA_BLOCK = 256
MOBA_TOPK = 3
ROPE_THETA = 10000.0
Q_BLOCK = 128
GATHER_Q_BLOCK = 32
N_EXPERTS = 8
TOP_K = 2
MOE_BLOCK = 128
RMS_EPS = 1e-6
NEG_INF = -1e30
SEL_FORCE = 1e6
ATTN_SCALE = HEAD_DIM ** -0.5
IN_SIZES = (H_NSA * HEAD_DIM, 2 * HEAD_DIM, 2 * HEAD_DIM, 2 * HEAD_DIM, 3 * H_NSA,
            H_HGRN * HGRN_DK, H_HGRN * HGRN_DK, H_HGRN * HGRN_DV, H_HGRN * HGRN_DV,
            H_MOBA * HEAD_DIM, H_MOBA * HEAD_DIM, H_MOBA * HEAD_DIM,
            H_FOX * HEAD_DIM, H_FOX * HEAD_DIM, H_FOX * HEAD_DIM, H_FOX)
N_IN = sum(IN_SIZES)
IN_OFFS = tuple(int(v) for v in np.cumsum((0,) + IN_SIZES))

N_SEC = 13
N_INP = N_SEC * HW
SEC_NQ, SEC_S1, SEC_S2, SEC_HQ, SEC_HF, SEC_HI, SEC_HG = 0, 1, 2, 3, 4, 5, 6
SEC_MQ, SEC_MK, SEC_MV, SEC_FQ, SEC_FK, SEC_FV = 7, 8, 9, 10, 11, 12
MISC_LANE = 128

LANES = 128
VMEM_LIMIT = 48 * 1024 * 1024
PREP_TQ = 256
NSA_TQ = 256
NSA_TK = 1024
ATT_SB = 4
M_FLOOR = -1e20
MOE_TM = 256
MOE_TF = 1792
DENSE_TF = 1408
FFN_TN = 512
MOE_TN = 1024
BF = jnp.bfloat16
F32 = jnp.float32


def _round_up(x, m):
    return -(-x // m) * m


def _cparams(sem):
    return pltpu.CompilerParams(dimension_semantics=sem, vmem_limit_bytes=VMEM_LIMIT)


def _dot(a, b):
    return jnp.dot(a, b, preferred_element_type=F32)


def _dot_nt(a, b):
    return lax.dot_general(a, b, (((1,), (1,)), ((), ())), preferred_element_type=F32)


def _mm_kernel(a_ref, b_ref, o_ref):
    k = pl.program_id(2)
    acc = _dot(a_ref[...].astype(BF), b_ref[...].astype(BF))

    @pl.when(k == 0)
    def _():
        o_ref[...] = acc

    @pl.when(k != 0)
    def _():
        o_ref[...] += acc


def matmul(a, b, tm=512, tn=512, tk=1024):
    M, K = a.shape
    _, N = b.shape
    tm = min(tm, _round_up(M, 8))
    Mp, Np = _round_up(M, tm), _round_up(N, tn)
    if K % tk:
        tk = K
    if Mp != M:
        a = jnp.pad(a, ((0, Mp - M), (0, 0)))
    if Np != N:
        b = jnp.pad(b, ((0, 0), (0, Np - N)))
    out = pl.pallas_call(
        _mm_kernel,
        grid=(Mp // tm, Np // tn, K // tk),
        in_specs=[pl.BlockSpec((tm, tk), lambda i, j, k: (i, k)),
                  pl.BlockSpec((tk, tn), lambda i, j, k: (k, j))],
        out_specs=pl.BlockSpec((tm, tn), lambda i, j, k: (i, j)),
        out_shape=jax.ShapeDtypeStruct((Mp, Np), F32),
        compiler_params=_cparams(("parallel", "parallel", "arbitrary")),
        name="dense_matmul",
    )(a, b)
    return out[:M, :N]


def _inproj_kernel(x_ref, g_ref, w_ref, o_ref):
    x = x_ref[...]
    y = x * lax.rsqrt(jnp.mean(x * x, axis=-1, keepdims=True) + RMS_EPS) * g_ref[...]
    o_ref[...] = _dot(y.astype(BF), w_ref[...])


def in_projection(x, g, w_bf, tm=256):
    N, D = x.shape
    return pl.pallas_call(
        _inproj_kernel,
        grid=(N // tm,),
        in_specs=[pl.BlockSpec((tm, D), lambda i: (i, 0)),
                  pl.BlockSpec((1, D), lambda i: (0, 0)),
                  pl.BlockSpec((D, N_INP), lambda i: (0, 0))],
        out_specs=pl.BlockSpec((tm, N_INP), lambda i: (i, 0)),
        out_shape=jax.ShapeDtypeStruct((N, N_INP), F32),
        compiler_params=_cparams(("parallel",)),
        name="rmsnorm_in_projection",
    )(x, g, w_bf)


def relayout_w_in(w):
    def cols(i):
        return w[:, IN_OFFS[i]:IN_OFFS[i + 1]]
    pad = jnp.zeros((w.shape[0], HW - 2 * HEAD_DIM - IN_SIZES[4] - IN_SIZES[15]), w.dtype)
    parts = [cols(0), cols(1), cols(2), cols(3), cols(4), cols(15), pad] + [cols(i) for i in range(5, 15)]
    return jnp.concatenate(parts, axis=1).astype(BF)


def _head_meansq(x, gmat):
    sq = x * x
    hi = sq.astype(BF)
    lo = (sq - hi.astype(F32)).astype(BF)
    return _dot(hi, gmat) + _dot(lo, gmat)


def _head_rmsnorm(x, gain, gmat):
    return x * lax.rsqrt(_head_meansq(x, gmat) + RMS_EPS) * gain


def _rope(x, cos, sin_signed, lo_half):
    w = x.shape[1]
    swapped = jnp.where(lo_half, pltpu.roll(x, w - HEAD_DIM // 2, 1), pltpu.roll(x, HEAD_DIM // 2, 1))
    return x * cos + swapped * sin_signed


def _store_heads(ref, x):
    for h in range(NH):
        ref[0, h] = x[:, h * HEAD_DIM:(h + 1) * HEAD_DIM].astype(ref.dtype)


def _prep_kernel(nq_ref, s1_ref, s2_ref, mq_ref, mk_ref, mv_ref, fq_ref, fk_ref, fv_ref,
                 cos_ref, sin_ref, gains_ref, gmat_ref,
                 nsa_rows_ref, nsa_kc_ref, nsa_win_ref, moba_rows_ref, fox_rows_ref,
                 nqn_ref, nqr_ref, nks_ref, nvs_ref, nkw_ref, nvw_ref,
                 mqr_ref, mkr_ref, mvv_ref, kmean_ref, fqn_ref, fkn_ref, fvv_ref):
    cos = cos_ref[...]
    sin = sin_ref[...]
    gmat = gmat_ref[...]
    t = cos.shape[0]
    lane = lax.broadcasted_iota(jnp.int32, (t, HW), 1)
    lo_half = (lane % HEAD_DIM) < (HEAD_DIM // 2)
    gains = gains_ref[...]

    qn = _head_rmsnorm(nq_ref[...], gains[0:1], gmat)
    qr = _rope(qn, cos, sin, lo_half)
    _store_heads(nqn_ref, qn * ATTN_SCALE)
    _store_heads(nqr_ref, qr * ATTN_SCALE)

    s1 = s1_ref[...]
    s1r = _rope(_head_rmsnorm(s1, gains[1:2], gmat), cos, sin, lo_half)
    third = (lane >= 2 * HEAD_DIM) & (lane < 3 * HEAD_DIM)
    rows = jnp.where(third, s1r, s1)
    nsa_rows_ref[...] = rows
    nsa_kc_ref[...] = rows[:, :2 * HEAD_DIM]
    lane_h = lax.broadcasted_iota(jnp.int32, (t, HEAD_DIM), 1)
    row_h = lax.broadcasted_iota(jnp.int32, (t, HEAD_DIM), 0)
    blk = (pl.program_id(1) * t + row_h) // NSA_SEL_BLOCK
    ones_col = jnp.where(lane_h == 0, 1.0, 0.0)
    nks_ref[0] = jnp.concatenate([rows[:, 2 * HEAD_DIM:3 * HEAD_DIM], jnp.where(lane_h == blk, 1.0, 0.0)],
                                 axis=1).astype(nks_ref.dtype)
    nvs_ref[0] = jnp.concatenate([rows[:, 3 * HEAD_DIM:], ones_col], axis=1).astype(nvs_ref.dtype)

    s2 = s2_ref[...]
    s2r = _rope(_head_rmsnorm(s2, gains[2:3], gmat), cos, sin, lo_half)
    wrows = jnp.where(lane < HEAD_DIM, s2r, s2)
    nsa_win_ref[...] = wrows[:, :2 * HEAD_DIM]
    nkw_ref[0] = wrows[:, :HEAD_DIM].astype(nkw_ref.dtype)
    nvw_ref[0] = wrows[:, HEAD_DIM:2 * HEAD_DIM].astype(nvw_ref.dtype)

    mq = _rope(_head_rmsnorm(mq_ref[...], gains[3:4], gmat), cos, sin, lo_half)
    mk = _rope(_head_rmsnorm(mk_ref[...], gains[4:5], gmat), cos, sin, lo_half)
    mv = mv_ref[...]
    _store_heads(mqr_ref, mq * ATTN_SCALE)
    _store_heads(mkr_ref, mk)
    _store_heads(mvv_ref, mv)
    moba_rows_ref[:, :HW] = mk
    moba_rows_ref[:, HW:] = mv
    kmean_ref[0, 0] = jnp.mean(mk, axis=0, keepdims=True)

    fq = _head_rmsnorm(fq_ref[...], gains[5:6], gmat)
    fk = _head_rmsnorm(fk_ref[...], gains[6:7], gmat)
    fv = fv_ref[...]
    _store_heads(fqn_ref, fq * ATTN_SCALE)
    _store_heads(fkn_ref, fk)
    _store_heads(fvv_ref, fv)
    fox_rows_ref[:, :HW] = fk
    fox_rows_ref[:, HW:] = fv


def rope_tables(pos):
    half = HEAD_DIM // 2
    inv = ROPE_THETA ** (-jnp.arange(half, dtype=F32) / half)
    ang = pos.astype(F32)[:, None] * inv[None, :]
    cos = jnp.cos(ang)
    sin = jnp.sin(ang)
    cos_h = jnp.concatenate([cos, cos], axis=1)
    sin_h = jnp.concatenate([-sin, sin], axis=1)
    return jnp.tile(cos_h, (1, NH)), jnp.tile(sin_h, (1, NH))


def head_gains(nsa_gain, moba_gain, fox_gain):
    one = jnp.ones((HEAD_DIM,), F32)
    t4 = lambda g: jnp.tile(g, NH)
    rows = [t4(nsa_gain[0]),
            jnp.concatenate([one, one, nsa_gain[2], one]),
            jnp.concatenate([nsa_gain[3], one, one, one]),
            t4(moba_gain[0]), t4(moba_gain[1]), t4(fox_gain[0]), t4(fox_gain[1]), t4(one)]
    return jnp.stack(rows, axis=0)


def group_mean_matrix():
    idx = np.arange(HW) // HEAD_DIM
    return jnp.asarray((idx[:, None] == idx[None, :]).astype(np.float32) / HEAD_DIM, dtype=BF)


def prep_prompt(z, row0, B, T, cos, sin, gains, gmat, qdt=None):
    qdt = BF if qdt is None else qdt
    tq = PREP_TQ
    nq = T // tq
    rb0 = row0 // tq

    def sec(c):
        return pl.BlockSpec((tq, HW), lambda b, i, c=c: (rb0 + b * nq + i, c))

    flat = lambda w: pl.BlockSpec((tq, w), lambda b, i: (b * nq + i, 0))
    headmaj = pl.BlockSpec((1, NH, tq, HEAD_DIM), lambda b, i: (b, 0, i, 0))
    single = pl.BlockSpec((1, tq, HEAD_DIM), lambda b, i: (b, i, 0))
    single_aug = pl.BlockSpec((1, tq, LANES), lambda b, i: (b, i, 0))
    N = B * T
    sd = jax.ShapeDtypeStruct
    hm_shape = sd((B, NH, T, HEAD_DIM), qdt)
    sg_shape = sd((B, T, HEAD_DIM), qdt)
    outs = pl.pallas_call(
        _prep_kernel,
        grid=(B, nq),
        in_specs=[sec(SEC_NQ), sec(SEC_S1), sec(SEC_S2), sec(SEC_MQ), sec(SEC_MK), sec(SEC_MV),
                  sec(SEC_FQ), sec(SEC_FK), sec(SEC_FV),
                  pl.BlockSpec((tq, HW), lambda b, i: (i, 0)),
                  pl.BlockSpec((tq, HW), lambda b, i: (i, 0)),
                  pl.BlockSpec((8, HW), lambda b, i: (0, 0)),
                  pl.BlockSpec((HW, HW), lambda b, i: (0, 0))],
        out_specs=[flat(HW), flat(2 * HEAD_DIM), flat(2 * HEAD_DIM), flat(2 * HW), flat(2 * HW),
                   headmaj, headmaj, single_aug, single_aug, single, single,
                   headmaj, headmaj, headmaj,
                   pl.BlockSpec((1, 1, 1, HW), lambda b, i: (b, i, 0, 0)),
                   headmaj, headmaj, headmaj],
        out_shape=[sd((N, HW), F32), sd((N, 2 * HEAD_DIM), F32), sd((N, 2 * HEAD_DIM), F32),
                   sd((N, 2 * HW), F32), sd((N, 2 * HW), F32),
                   hm_shape, hm_shape, sd((B, T, LANES), qdt), sd((B, T, LANES), qdt), sg_shape, sg_shape,
                   hm_shape, hm_shape, hm_shape,
                   sd((B, nq, 1, HW), F32),
                   hm_shape, hm_shape, hm_shape],
        compiler_params=_cparams(("parallel", "parallel")),
        name="mixer_prep",
    )(z, z, z, z, z, z, z, z, z, cos, sin, gains, gmat)
    keys = ("nsa_rows", "nsa_kc", "nsa_win", "moba_rows", "fox_rows",
            "nqn", "nqr", "nks", "nvs", "nkw", "nvw", "mqr", "mkr", "mvv", "kmean", "fqn", "fkn", "fvv")
    return dict(zip(keys, outs))


def _nsa_compress_kernel(x_ref, pe_ref, w_ref, gain_ref, k_ref, v_ref):
    x = x_ref[0]
    a = _dot((x + pe_ref[0:1]).astype(BF), w_ref[0])
    b = _dot((x + pe_ref[1:2]).astype(BF), w_ref[1])
    nch = x.shape[0]
    y = a + pltpu.roll(b, nch - 1, 0)
    lane = lax.broadcasted_iota(jnp.int32, y.shape, 1)
    ms = jnp.sum(jnp.where(lane < HEAD_DIM, y * y, 0.0), axis=1, keepdims=True) / HEAD_DIM
    kn = y * lax.rsqrt(ms + RMS_EPS) * gain_ref[...]
    k_ref[0] = kn[:, :HEAD_DIM].astype(BF)
    v_ref[0] = y[:, HEAD_DIM:].astype(BF)


def nsa_compress_weights(cmp_pe, cmp_w, gain1):
    S = NSA_CMP_STRIDE
    pe = cmp_pe.reshape(2, 2, S, HEAD_DIM)
    pe_flat = jnp.transpose(pe, (1, 2, 0, 3)).reshape(2, S * 2 * HEAD_DIM)
    w = cmp_w.reshape(2, 2, S, HEAD_DIM, HEAD_DIM)
    wz = jnp.zeros((2, S, 2, HEAD_DIM, 2, HEAD_DIM), F32)
    wz = wz.at[:, :, 0, :, 0, :].set(w[0]).at[:, :, 1, :, 1, :].set(w[1])
    w_flat = wz.reshape(2, S * 2 * HEAD_DIM, 2 * HEAD_DIM).astype(BF)
    gain = jnp.concatenate([gain1, jnp.ones((HEAD_DIM,), F32)])[None, :]
    return pe_flat, w_flat, gain


def nsa_compress(kc, B, T, pe_flat, w_flat, gain):
    nch = T // NSA_CMP_STRIDE
    cw = NSA_CMP_STRIDE * 2 * HEAD_DIM
    x = kc.reshape(B, nch, cw)
    out_spec = pl.BlockSpec((1, nch, HEAD_DIM), lambda b: (b, 0, 0))
    return pl.pallas_call(
        _nsa_compress_kernel,
        grid=(B,),
        in_specs=[pl.BlockSpec((1, nch, cw), lambda b: (b, 0, 0)),
                  pl.BlockSpec((2, cw), lambda b: (0, 0)),
                  pl.BlockSpec((2, cw, 2 * HEAD_DIM), lambda b: (0, 0, 0)),
                  pl.BlockSpec((1, 2 * HEAD_DIM), lambda b: (0, 0))],
        out_specs=[out_spec, out_spec],
        out_shape=[jax.ShapeDtypeStruct((B, nch, HEAD_DIM), BF)] * 2,
        compiler_params=_cparams(("parallel",)),
        name="nsa_compress",
    )(x, pe_flat, w_flat, gain)


def _softmax_update(s, mask, m_i, l_i, acc, v, lead, v_feature_major=False):
    s = jnp.where(mask, s, NEG_INF)
    m_new = jnp.maximum(m_i, jnp.max(s, axis=-1, keepdims=True))
    p = jnp.where(mask, jnp.exp(s - m_new), 0.0)
    alpha = jnp.exp(m_i - m_new)
    l_new = alpha * l_i + jnp.sum(p, axis=-1, keepdims=True)
    pb = p.astype(BF).reshape(lead, p.shape[-1])
    pv = (_dot_nt(pb, v) if v_feature_major else _dot(pb, v)).reshape(acc.shape)
    return m_new, l_new, alpha * acc + pv


def _nsa_attn_kernel(qn_ref, qr_ref, kc_ref, vc_ref, ks_ref, vs_ref, kw_ref, vw_ref, gl_ref,
                     cover_ref, o_ref, m_scr, acc_scr, *, T):
    tq, tk = NSA_TQ, NSA_TK
    M = NH * tq
    i = pl.program_id(1)
    p0 = i * tq
    qn = qn_ref[0].reshape(M, HEAD_DIM)
    qr = qr_ref[0].reshape(M, HEAD_DIM)
    pos = p0 + lax.broadcasted_iota(jnp.int32, (tq, 1), 0)

    nch = kc_ref.shape[1]
    s_c = _dot_nt(qn, kc_ref[0]).reshape(NH, tq, nch)
    n_idx = lax.broadcasted_iota(jnp.int32, (tq, nch), 1)
    mask_c = (n_idx * NSA_CMP_STRIDE + (NSA_CMP_BLOCK - 1) <= pos)[None]
    s_c = jnp.where(mask_c, s_c, NEG_INF)
    m_c = jnp.max(s_c, axis=-1, keepdims=True)
    p_c = jnp.where(mask_c, jnp.exp(s_c - m_c), 0.0)
    p_c = p_c / jnp.maximum(jnp.sum(p_c, axis=-1, keepdims=True), 1.0)
    p_cb = p_c.astype(BF).reshape(M, nch)
    o_cmp = _dot(p_cb, vc_ref[0]).reshape(NH, tq, HEAD_DIM)
    imp = jnp.sum(_dot(p_cb, cover_ref[...]).reshape(NH, tq, LANES), axis=0)

    j = lax.broadcasted_iota(jnp.int32, (tq, LANES), 1)
    cur = pos // NSA_SEL_BLOCK
    forced = (j == 0) | (j == cur) | (j == cur - 1)
    imp = jnp.where(j > cur, NEG_INF, jnp.where(forced, SEL_FORCE, imp))
    n_sel = T // NSA_SEL_BLOCK
    rank = jnp.zeros((tq, LANES), F32)
    for jp in range(n_sel):
        col = imp[:, jp:jp + 1]
        beats = (col > imp) | ((col == imp) & (j > jp))
        rank = rank + jnp.where(beats, 1.0, 0.0)
    picked = (rank < NSA_TOPN) & (j <= cur)

    pen = jnp.where(picked, 0.0, NEG_INF)[:, :HEAD_DIM].astype(qr.dtype)
    qa = jnp.concatenate([qr, jnp.concatenate([pen] * NH, axis=0)], axis=1)
    _flash_init_aug(m_scr, acc_scr)
    kcol = lax.broadcasted_iota(jnp.int32, (tq, tk), 1)
    for kj in range(T // tk):
        def region(causal, kj=kj):
            s = _dot_nt(qa, ks_ref[0, kj * tk:(kj + 1) * tk, :]).reshape(NH, tq, tk)
            if causal:
                s = jnp.where((kcol + kj * tk <= pos)[None], s, NEG_INF)
            _flash_step_aug(s, vs_ref[0, kj * tk:(kj + 1) * tk, :], m_scr, acc_scr)

        pl.when((kj + 1) * tk <= p0)(functools.partial(region, False))
        pl.when((kj * tk <= p0) & ((kj + 1) * tk > p0))(functools.partial(region, True))
    acc_s = acc_scr[...]
    o_sel = acc_s[:, :, :HEAD_DIM] / jnp.maximum(acc_s[:, :, HEAD_DIM:HEAD_DIM + 1], 1.0)

    band = NSA_WINDOW + tq
    start = pl.multiple_of(jnp.maximum(p0 - NSA_WINDOW, 0), tq)
    kw = kw_ref[0, pl.ds(start, band), :]
    vw = vw_ref[0, pl.ds(start, band), :]
    s_w = _dot_nt(qr, kw).reshape(NH, tq, band)
    kpos = start + lax.broadcasted_iota(jnp.int32, (tq, band), 1)
    mask_w = ((kpos <= pos) & (kpos > pos - NSA_WINDOW))[None]
    s_w = jnp.where(mask_w, s_w, NEG_INF)
    m_w = jnp.max(s_w, axis=-1, keepdims=True)
    p_w = jnp.where(mask_w, jnp.exp(s_w - m_w), 0.0)
    l_w = jnp.sum(p_w, axis=-1, keepdims=True)
    o_win = _dot(p_w.astype(BF).reshape(M, band), vw).reshape(NH, tq, HEAD_DIM) / jnp.maximum(l_w, 1.0)

    sig = jax.nn.sigmoid(gl_ref[...])
    outs = []
    for h in range(NH):
        c = MISC_LANE + 3 * h
        outs.append(sig[:, c:c + 1] * o_cmp[h] + sig[:, c + 1:c + 2] * o_sel[h] + sig[:, c + 2:c + 3] * o_win[h])
    o_ref[...] = jnp.concatenate(outs, axis=1).astype(o_ref.dtype)


def nsa_constants(T):
    nch = T // NSA_CMP_STRIDE
    n_cmp = (T - NSA_CMP_BLOCK) // NSA_CMP_STRIDE + 1
    ci = np.arange(nch)[:, None] * NSA_CMP_STRIDE
    sj = np.arange(LANES)[None, :] * NSA_SEL_BLOCK
    cover = (ci < sj + NSA_SEL_BLOCK) & (ci + NSA_CMP_BLOCK > sj) & (np.arange(nch)[:, None] < n_cmp)
    return jnp.asarray(cover.astype(np.float32), dtype=BF)


def nsa_attention_prompt(pp, kc, vc, z, row0, B, T, cover):
    tq = NSA_TQ
    nq = T // tq
    rb0 = row0 // tq
    nch = T // NSA_CMP_STRIDE
    assert T >= NSA_WINDOW + tq and T % NSA_TK == 0 and NSA_TK % tq == 0 and T // NSA_SEL_BLOCK <= HEAD_DIM
    headmaj = pl.BlockSpec((1, NH, tq, HEAD_DIM), lambda b, i: (b, 0, i, 0))
    full1 = pl.BlockSpec((1, T, HEAD_DIM), lambda b, i: (b, 0, 0))
    full_aug = pl.BlockSpec((1, T, LANES), lambda b, i: (b, 0, 0))
    cmp1 = pl.BlockSpec((1, nch, HEAD_DIM), lambda b, i: (b, 0, 0))
    return pl.pallas_call(
        functools.partial(_nsa_attn_kernel, T=T),
        grid=(B, nq),
        in_specs=[headmaj, headmaj, cmp1, cmp1, full_aug, full_aug, full1, full1,
                  pl.BlockSpec((tq, HW), lambda b, i: (rb0 + b * nq + i, SEC_S2)),
                  pl.BlockSpec((nch, LANES), lambda b, i: (0, 0))],
        out_specs=pl.BlockSpec((tq, HW), lambda b, i: (b * nq + i, 0)),
        out_shape=jax.ShapeDtypeStruct((B * T, HW), BF),
        scratch_shapes=[pltpu.VMEM((NH, tq, 1), F32), pltpu.VMEM((NH, tq, LANES), F32)],
        compiler_params=_cparams(("parallel", "parallel")),
        name="nsa_attention",
    )(pp["nqn"], pp["nqr"], kc, vc, pp["nks"], pp["nvs"], pp["nkw"], pp["nvw"], z, cover)


def _flash_init(m_scr, l_scr, acc_scr):
    m_scr[...] = jnp.full_like(m_scr, M_FLOOR)
    l_scr[...] = jnp.zeros_like(l_scr)
    acc_scr[...] = jnp.zeros_like(acc_scr)


def _flash_step(s, v, m_scr, l_scr, acc_scr):
    m_i = m_scr[...]
    m_new = jnp.maximum(m_i, jnp.max(s, axis=-1, keepdims=True))
    p = jnp.exp(s - m_new)
    alpha = jnp.exp(m_i - m_new)
    l_scr[...] = alpha * l_scr[...] + jnp.sum(p, axis=-1, keepdims=True)
    pv = _dot(p.astype(BF).reshape(-1, p.shape[-1]), v)
    acc_scr[...] = alpha * acc_scr[...] + pv.reshape(acc_scr.shape)
    m_scr[...] = m_new


def _flash_init_aug(m_scr, acc_scr):
    m_scr[...] = jnp.full_like(m_scr, M_FLOOR)
    acc_scr[...] = jnp.zeros_like(acc_scr)


def _flash_step_aug(s, v_aug, m_scr, acc_scr):
    m_i = m_scr[...]
    m_new = jnp.maximum(m_i, jnp.max(s, axis=-1, keepdims=True))
    p = jnp.exp(s - m_new)
    pv = _dot(p.astype(BF).reshape(-1, p.shape[-1]), v_aug)
    acc_scr[...] = jnp.exp(m_i - m_new) * acc_scr[...] + pv.reshape(acc_scr.shape)
    m_scr[...] = m_new


def _moba_attn_kernel(q_ref, k_ref, v_ref, km_ref, o_ref, m_scr, l_scr, acc_scr):
    tq = MOBA_BLOCK
    tks = ATT_SB * MOBA_BLOCK
    n_sb = k_ref.shape[2] // tks
    qi = pl.program_id(1)
    lane = lax.broadcasted_iota(jnp.int32, (tq, LANES), 1)
    qrow = lax.broadcasted_iota(jnp.int32, (tq, tks), 0)
    kcol = lax.broadcasted_iota(jnp.int32, (tq, tks), 1)
    sels = []
    for h in range(NH):
        g = jnp.where(lane < qi, _dot_nt(q_ref[0, h], km_ref[0, h]), NEG_INF)
        sel = jnp.where(lane == qi, 1.0, 0.0)
        for _ in range(MOBA_TOPK):
            m = jnp.max(g, axis=1, keepdims=True)
            idx = jnp.min(jnp.where(g == m, lane, LANES), axis=1, keepdims=True)
            hit = lane == idx
            sel = jnp.where(hit & (lane < qi), 1.0, sel)
            g = jnp.where(hit, NEG_INF, g)
        sels.append(sel)
    _flash_init(m_scr, l_scr, acc_scr)
    for sb in range(n_sb):
        @pl.when(sb * ATT_SB <= qi)
        def _(sb=sb):
            causal = kcol + (sb * tks) <= qrow + qi * tq
            for h in range(NH):
                k = k_ref[0, h, sb * tks:(sb + 1) * tks, :]
                v = v_ref[0, h, sb * tks:(sb + 1) * tks, :]
                picked = jnp.concatenate(
                    [jnp.broadcast_to(sels[h][:, j:j + 1] > 0.5, (tq, MOBA_BLOCK))
                     for j in range(sb * ATT_SB, (sb + 1) * ATT_SB)], axis=1)
                s = jnp.where(picked & causal, _dot_nt(q_ref[0, h], k), NEG_INF)
                _flash_step(s, v, m_scr.at[h], l_scr.at[h], acc_scr.at[h])
    o = acc_scr[...] / jnp.maximum(l_scr[...], 1.0)
    o_ref[...] = jnp.concatenate([o[h] for h in range(NH)], axis=1).astype(o_ref.dtype)


def moba_attention_prompt(pp, B, T):
    tq = MOBA_BLOCK
    nq = T // tq
    km = pp["kmean"].reshape(B, nq, NH, HEAD_DIM).transpose(0, 2, 1, 3)
    km = jnp.pad(km, ((0, 0), (0, 0), (0, LANES - nq), (0, 0))).astype(BF)
    headq = pl.BlockSpec((1, NH, tq, HEAD_DIM), lambda b, i: (b, 0, i, 0))
    headfull = pl.BlockSpec((1, NH, T, HEAD_DIM), lambda b, i: (b, 0, 0, 0))
    return pl.pallas_call(
        _moba_attn_kernel,
        grid=(B, nq),
        in_specs=[headq, headfull, headfull,
                  pl.BlockSpec((1, NH, LANES, HEAD_DIM), lambda b, i: (b, 0, 0, 0))],
        out_specs=pl.BlockSpec((tq, HW), lambda b, i: (b * nq + i, 0)),
        out_shape=jax.ShapeDtypeStruct((B * T, HW), BF),
        scratch_shapes=[pltpu.VMEM((NH, tq, 1), F32), pltpu.VMEM((NH, tq, 1), F32),
                        pltpu.VMEM((NH, tq, HEAD_DIM), F32)],
        compiler_params=_cparams(("parallel", "parallel")),
        name="moba_attention",
    )(pp["mqr"], pp["mkr"], pp["mvv"], km)


def _fox_attn_kernel(q_ref, k_ref, v_ref, cq_ref, ck_ref, o_ref, m_scr, l_scr, acc_scr):
    tq = MOBA_BLOCK
    tks = ATT_SB * MOBA_BLOCK
    n_sb = k_ref.shape[2] // tks
    qi = pl.program_id(1)
    qrow = lax.broadcasted_iota(jnp.int32, (tq, tks), 0)
    kcol = lax.broadcasted_iota(jnp.int32, (tq, tks), 1)
    _flash_init(m_scr, l_scr, acc_scr)
    for sb in range(n_sb):
        last = (sb + 1) * ATT_SB - 1

        def region(causal, sb=sb):
            for h in range(NH):
                k = k_ref[0, h, sb * tks:(sb + 1) * tks, :]
                v = v_ref[0, h, sb * tks:(sb + 1) * tks, :]
                s = _dot_nt(q_ref[0, h], k) + cq_ref[0, h] - ck_ref[0, h, sb]
                if causal:
                    s = jnp.where(kcol + (sb * tks) <= qrow + qi * tq, s, NEG_INF)
                _flash_step(s, v, m_scr.at[h], l_scr.at[h], acc_scr.at[h])

        pl.when(last < qi)(functools.partial(region, False))
        pl.when((sb * ATT_SB <= qi) & (last >= qi))(functools.partial(region, True))
    o = acc_scr[...] / jnp.maximum(l_scr[...], 1.0)
    o_ref[...] = jnp.concatenate([o[h] for h in range(NH)], axis=1).astype(o_ref.dtype)


def fox_attention_prompt(pp, ch, B, T):
    tq = MOBA_BLOCK
    nq = T // tq
    cq = ch[..., None]
    tks = ATT_SB * MOBA_BLOCK
    ck = ch.reshape(B, NH, T // tks, 1, tks)
    headq = pl.BlockSpec((1, NH, tq, HEAD_DIM), lambda b, i: (b, 0, i, 0))
    headfull = pl.BlockSpec((1, NH, T, HEAD_DIM), lambda b, i: (b, 0, 0, 0))
    return pl.pallas_call(
        _fox_attn_kernel,
        grid=(B, nq),
        in_specs=[headq, headfull, headfull,
                  pl.BlockSpec((1, NH, tq, 1), lambda b, i: (b, 0, i, 0)),
                  pl.BlockSpec((1, NH, T // tks, 1, tks), lambda b, i: (b, 0, 0, 0, 0))],
        out_specs=pl.BlockSpec((tq, HW), lambda b, i: (b * nq + i, 0)),
        out_shape=jax.ShapeDtypeStruct((B * T, HW), BF),
        scratch_shapes=[pltpu.VMEM((NH, tq, 1), F32), pltpu.VMEM((NH, tq, 1), F32),
                        pltpu.VMEM((NH, tq, HEAD_DIM), F32)],
        compiler_params=_cparams(("parallel", "parallel")),
        name="fox_attention",
    )(pp["fqn"], pp["fkn"], pp["fvv"], cq, ck)


def _hgrn_kernel(q_ref, f_ref, i_ref, g_ref, s0_ref, lb_ref, gain_ref, bd_ref, gmat_ref,
                 o_ref, sout_ref, st_scr, *, C):
    c = pl.program_id(1)

    @pl.when(c == 0)
    def _():
        st_scr[...] = jnp.zeros_like(st_scr)
        for h in range(NH):
            st_scr[h * HEAD_DIM:(h + 1) * HEAD_DIM, h * HEAD_DIM:(h + 1) * HEAD_DIM] = s0_ref[0, h]

    lb = lb_ref[...]
    z = f_ref[...]
    logf = jnp.log(lb + (1.0 - lb) * jax.nn.sigmoid(z))
    kk = (1.0 - lb) * jax.nn.sigmoid(-z)
    q = q_ref[...]
    qf = q * jax.nn.sigmoid(q)
    v = i_ref[...]
    row = lax.broadcasted_iota(jnp.int32, (C, HW), 0)
    b = logf
    sh = 1
    while sh < C:
        b = b + jnp.where(row >= sh, pltpu.roll(b, sh, 0), 0.0)
        sh *= 2
    bd = bd_ref[...]
    st = st_scr[...]
    o_ref[...] = _dot_nt((qf * jnp.exp(b)).astype(BF), st.astype(BF))
    for s in range(C):
        causal = row >= s
        e = jnp.exp(jnp.where(causal, b - b[s:s + 1], 0.0))
        fz = jnp.where(causal, qf * kk[s:s + 1] * e, 0.0)
        o_ref[...] += _dot(fz.astype(BF), bd) * v[s:s + 1]
    o = o_ref[...]
    b_last = b[C - 1:C, :]
    kt = kk * jnp.exp(b_last - b)
    upd = lax.dot_general(v.astype(BF), kt.astype(BF), (((0,), (0,)), ((), ())), preferred_element_type=F32)
    r2 = lax.broadcasted_iota(jnp.int32, (HW, HW), 0) // HEAD_DIM
    c2 = lax.broadcasted_iota(jnp.int32, (HW, HW), 1) // HEAD_DIM
    st_new = st * jnp.exp(b_last) + jnp.where(r2 == c2, upd, 0.0)
    st_scr[...] = st_new
    g = g_ref[...]
    o_ref[...] = _head_rmsnorm(o, gain_ref[...], gmat_ref[...]) * (g * jax.nn.sigmoid(g))

    @pl.when(c == pl.num_programs(1) - 1)
    def _():
        for h in range(NH):
            sout_ref[0, h] = st_new[h * HEAD_DIM:(h + 1) * HEAD_DIM, h * HEAD_DIM:(h + 1) * HEAD_DIM]


def hgrn_mixer(z, row0, B, T, s0, lb, out_gain, gmat):
    C = math.gcd(T, HGRN_CHUNK)
    nc = T // C
    rb0 = row0 // C
    idx = np.arange(HW) // HEAD_DIM
    bd = jnp.asarray((idx[:, None] == idx[None, :]).astype(np.float32), dtype=BF)

    def sec(cidx):
        return pl.BlockSpec((C, HW), lambda b, c, cidx=cidx: (rb0 + b * nc + c, cidx))

    state = pl.BlockSpec((1, NH, HGRN_DV, HGRN_DK), lambda b, c: (b, 0, 0, 0))
    vec = pl.BlockSpec((1, HW), lambda b, c: (0, 0))
    mat = pl.BlockSpec((HW, HW), lambda b, c: (0, 0))
    o, st = pl.pallas_call(
        functools.partial(_hgrn_kernel, C=C),
        grid=(B, nc),
        in_specs=[sec(SEC_HQ), sec(SEC_HF), sec(SEC_HI), sec(SEC_HG), state, vec, vec, mat, mat],
        out_specs=[pl.BlockSpec((C, HW), lambda b, c: (b * nc + c, 0)), state],
        out_shape=[jax.ShapeDtypeStruct((B * T, HW), F32),
                   jax.ShapeDtypeStruct((B, NH, HGRN_DV, HGRN_DK), F32)],
        scratch_shapes=[pltpu.VMEM((HW, HW), F32)],
        compiler_params=_cparams(("parallel", "arbitrary")),
        name="hgrn_recurrence",
    )(z, z, z, z, jnp.swapaxes(s0, 2, 3), lb[None, :], jnp.tile(out_gain, NH)[None, :], bd, gmat)
    return o, jnp.swapaxes(st, 2, 3)


DEC_PG = 32


def _page_specs(tail, PG, base):
    return [pl.BlockSpec((1,) + tail, lambda b, s, pt, j=j: (base + pt[b, s * PG + j], 0, 0)) for j in range(PG)]


def _pool_view(pools, l, width):
    pages = pools.reshape(pools.shape[0] * pools.shape[1], PAGE_SIZE, width)
    return jnp.swapaxes(pages, 1, 2), l * pools.shape[1]


def _rows_nat(q_ref):
    return jnp.concatenate([q_ref[0, h] for h in range(NH)], axis=1)


def _block_diag_q(qnat, T):
    q4 = jnp.concatenate([qnat] * NH, axis=0)
    r = lax.broadcasted_iota(jnp.int32, q4.shape, 0) // T
    c = lax.broadcasted_iota(jnp.int32, q4.shape, 1) // HEAD_DIM
    return jnp.where(r == c, q4, 0.0)


def _diag_heads(x, T):
    return jnp.concatenate([x[h * T:(h + 1) * T, h * HEAD_DIM:(h + 1) * HEAD_DIM] for h in range(NH)], axis=1)


def _new_causal(R, T):
    tq = lax.broadcasted_iota(jnp.int32, (R, T), 0) % T
    tk = lax.broadcasted_iota(jnp.int32, (R, T), 1)
    return tk <= tq


def _fox_decode_kernel(pt_ref, q_ref, new_ref, cq_ref, cn_ref, ck_ref, *rest, PG, T):
    page_refs, o_ref = rest[:PG], rest[PG]
    m_scr, l_scr, acc_scr = rest[PG + 1:]
    s = pl.program_id(1)
    R = NH * T

    @pl.when(s == 0)
    def _():
        m_scr[...] = jnp.full_like(m_scr, NEG_INF)
        l_scr[...] = jnp.zeros_like(l_scr)
        acc_scr[...] = jnp.zeros_like(acc_scr)

    qbd = _block_diag_q(_rows_nat(q_ref), T).astype(BF)
    cq = cq_ref[0]
    m_i, l_i, acc = m_scr[...], l_scr[...], acc_scr[...]
    for j in range(0, PG, 2):
        pa, pb = page_refs[j][0], page_refs[j + 1][0]
        kt = jnp.concatenate([pa[:HW], pb[:HW]], axis=1).astype(BF)
        vt = jnp.concatenate([pa[HW:], pb[HW:]], axis=1).astype(BF)
        ck2 = jnp.concatenate([ck_ref[0, j], ck_ref[0, j + 1]], axis=1)
        ck = jnp.broadcast_to(ck2[:, None, :], (NH, T, 2 * PAGE_SIZE)).reshape(R, 2 * PAGE_SIZE)
        sc = _dot(qbd, kt) + cq - ck
        m_new = jnp.maximum(m_i, jnp.max(sc, axis=-1, keepdims=True))
        p = jnp.exp(sc - m_new)
        alpha = jnp.exp(m_i - m_new)
        l_i = alpha * l_i + jnp.sum(p, axis=-1, keepdims=True)
        acc = alpha * acc + _dot_nt(p.astype(BF), vt)
        m_i = m_new
    m_scr[...] = m_i
    l_scr[...] = l_i
    acc_scr[...] = acc

    @pl.when(s == pl.num_programs(1) - 1)
    def _():
        new = new_ref[...]
        sc = _dot_nt(qbd, new[:, :HW].astype(BF)) + cq - cn_ref[0]
        _, l_f, acc_f = _softmax_update(sc, _new_causal(R, T), m_i, l_i, acc, new[:, HW:].astype(BF), R)
        o_ref[...] = _diag_heads(acc_f / jnp.maximum(l_f, 1.0), T)


def fox_decode(sp, pools, l, page_table, ch, row_new, B, T):
    n_pages = page_table.shape[1]
    PG = min(DEC_PG, n_pages)
    past = n_pages * PAGE_SIZE
    R = NH * T
    view, base = _pool_view(pools, l, 2 * HW)
    ck = ch[:, :, :past].reshape(B, NH, n_pages, PAGE_SIZE).transpose(0, 2, 1, 3)
    cnew = ch[:, :, past:]
    cq = cnew.reshape(B, R, 1)
    cn = jnp.broadcast_to(cnew[:, :, None, :], (B, NH, T, T)).reshape(B, R, T)
    return pl.pallas_call(
        functools.partial(_fox_decode_kernel, PG=PG, T=T),
        grid_spec=pltpu.PrefetchScalarGridSpec(
            num_scalar_prefetch=1, grid=(B, n_pages // PG),
            in_specs=[pl.BlockSpec((1, NH, T, HEAD_DIM), lambda b, s, pt: (0, 0, b, 0)),
                      pl.BlockSpec((T, 2 * HW), lambda b, s, pt: (b, 0)),
                      pl.BlockSpec((1, R, 1), lambda b, s, pt: (b, 0, 0)),
                      pl.BlockSpec((1, R, T), lambda b, s, pt: (b, 0, 0)),
                      pl.BlockSpec((1, PG, NH, PAGE_SIZE), lambda b, s, pt: (b, s, 0, 0))]
            + _page_specs((2 * HW, PAGE_SIZE), PG, base),
            out_specs=pl.BlockSpec((T, HW), lambda b, s, pt: (b, 0)),
            scratch_shapes=[pltpu.VMEM((R, 1), F32), pltpu.VMEM((R, 1), F32), pltpu.VMEM((R, HW), F32)]),
        out_shape=jax.ShapeDtypeStruct((B * T, HW), F32),
        compiler_params=_cparams(("parallel", "arbitrary")),
        name="fox_decode",
    )(page_table, sp["fqn"], row_new, cq, cn, ck, *([view] * PG))


def _moba_decode_kernel(pt_ref, q_ref, new_ref, *rest, PG, T, n_pages):
    page_refs, o_ref = rest[:PG], rest[PG]
    m_scr, l_scr, rs_scr, acc_scr = rest[PG + 1:]
    s = pl.program_id(1)
    R = NH * T
    lane = lax.broadcasted_iota(jnp.int32, (R, LANES), 1)

    @pl.when(s == 0)
    def _():
        m_scr[...] = jnp.full_like(m_scr, NEG_INF)
        l_scr[...] = jnp.zeros_like(l_scr)
        rs_scr[...] = jnp.zeros_like(rs_scr)

    qbd = _block_diag_q(_rows_nat(q_ref), T).astype(BF)
    mm, ll, rs = m_scr[...], l_scr[...], rs_scr[...]
    for j in range(0, PG, 2):
        bid = (s * PG + j) // 2
        pa, pb = page_refs[j][0], page_refs[j + 1][0]
        sc = _dot(qbd, jnp.concatenate([pa[:HW], pb[:HW]], axis=1).astype(BF))
        m = jnp.max(sc, axis=-1, keepdims=True)
        p = jnp.exp(sc - m)
        acc_scr[bid] = _dot_nt(p.astype(BF), jnp.concatenate([pa[HW:], pb[HW:]], axis=1).astype(BF))
        hit = lane == bid
        mm = jnp.where(hit, m, mm)
        ll = jnp.where(hit, jnp.sum(p, axis=-1, keepdims=True), ll)
        rs = jnp.where(hit, jnp.sum(sc, axis=-1, keepdims=True), rs)
    m_scr[...] = mm
    l_scr[...] = ll
    rs_scr[...] = rs

    @pl.when(s == pl.num_programs(1) - 1)
    def _():
        block_lane = lane < n_pages // 2
        g = jnp.where(block_lane, rs * (1.0 / MOBA_BLOCK), NEG_INF)
        sel = jnp.zeros((R, LANES), F32)
        for _ in range(MOBA_TOPK):
            mx = jnp.max(g, axis=1, keepdims=True)
            idx = jnp.min(jnp.where(g == mx, lane, LANES), axis=1, keepdims=True)
            hit = lane == idx
            sel = jnp.where(hit & block_lane, 1.0, sel)
            g = jnp.where(hit, NEG_INF, g)
        picked = sel > 0.5
        new = new_ref[...]
        mask_n = _new_causal(R, T)
        sn = jnp.where(mask_n, _dot_nt(qbd, new[:, :HW].astype(BF)), NEG_INF)
        m_n = jnp.max(sn, axis=-1, keepdims=True)
        p_n = jnp.where(mask_n, jnp.exp(sn - m_n), 0.0)
        m_tot = jnp.maximum(jnp.max(jnp.where(picked, mm, NEG_INF), axis=1, keepdims=True), m_n)
        w = jnp.where(picked, jnp.exp(mm - m_tot), 0.0)
        w_n = jnp.exp(m_n - m_tot)
        l_tot = jnp.sum(w * ll, axis=1, keepdims=True) + w_n * jnp.sum(p_n, axis=-1, keepdims=True)
        acc = w_n * _dot(p_n.astype(BF), new[:, HW:].astype(BF))
        for blk in range(n_pages // 2):
            acc = acc + w[:, blk:blk + 1] * acc_scr[blk]
        o_ref[...] = _diag_heads(acc / jnp.maximum(l_tot, 1.0), T)


def moba_decode(sp, pools, l, page_table, row_new, B, T):
    n_pages = page_table.shape[1]
    assert MOBA_BLOCK == 2 * PAGE_SIZE and n_pages % 2 == 0 and n_pages <= LANES and T <= MOBA_BLOCK
    PG = min(DEC_PG, n_pages)
    R = NH * T
    view, base = _pool_view(pools, l, 2 * HW)
    return pl.pallas_call(
        functools.partial(_moba_decode_kernel, PG=PG, T=T, n_pages=n_pages),
        grid_spec=pltpu.PrefetchScalarGridSpec(
            num_scalar_prefetch=1, grid=(B, n_pages // PG),
            in_specs=[pl.BlockSpec((1, NH, T, HEAD_DIM), lambda b, s, pt: (0, 0, b, 0)),
                      pl.BlockSpec((T, 2 * HW), lambda b, s, pt: (b, 0))]
            + _page_specs((2 * HW, PAGE_SIZE), PG, base),
            out_specs=pl.BlockSpec((T, HW), lambda b, s, pt: (b, 0)),
            scratch_shapes=[pltpu.VMEM((R, LANES), F32), pltpu.VMEM((R, LANES), F32), pltpu.VMEM((R, LANES), F32),
                            pltpu.VMEM((n_pages // 2, R, HW), F32)]),
        out_shape=jax.ShapeDtypeStruct((B * T, HW), F32),
        compiler_params=_cparams(("parallel", "arbitrary")),
        name="moba_decode",
    )(page_table, sp["mqr"], row_new, *([view] * PG))


def _nsa_cmp_phys_kernel(xt_ref, pe_ref, w_ref, y_ref, xs_scr, *, G):
    S = NSA_CMP_STRIDE
    for g in range(G):
        xs_scr[g * PAGE_SIZE:(g + 1) * PAGE_SIZE, :] = xt_ref[g].T
    m = G * (PAGE_SIZE // S)
    a = jnp.zeros((m, 2 * HEAD_DIM), F32)
    b = jnp.zeros((m, 2 * HEAD_DIM), F32)
    for t in range(S):
        x = xs_scr[pl.ds(t, m, stride=S), :]
        a = a + _dot((x + pe_ref[0, t:t + 1]).astype(BF), w_ref[0, t])
        b = b + _dot((x + pe_ref[1, t:t + 1]).astype(BF), w_ref[1, t])
    y_ref[...] = jnp.concatenate([a, b], axis=1)


def nsa_compress_pool(pools, l, cmp_pe, cmp_w):
    S = NSA_CMP_STRIDE
    n_phys = pools.shape[1]
    cpp = PAGE_SIZE // S
    pe = jnp.transpose(cmp_pe.reshape(2, 2, S, HEAD_DIM), (1, 2, 0, 3)).reshape(2, S, 2 * HEAD_DIM)
    w = cmp_w.reshape(2, 2, S, HEAD_DIM, HEAD_DIM)
    wz = jnp.zeros((2, S, 2, HEAD_DIM, 2, HEAD_DIM), F32)
    wz = wz.at[:, :, 0, :, 0, :].set(w[0]).at[:, :, 1, :, 1, :].set(w[1])
    wz = wz.reshape(2, S, 2 * HEAD_DIM, 2 * HEAD_DIM).astype(BF)
    G = next(g for g in (32, 16, 8, 4, 2, 1) if n_phys % g == 0)
    view, base = _pool_view(pools, l, HW)
    return pl.pallas_call(
        functools.partial(_nsa_cmp_phys_kernel, G=G),
        grid=(n_phys // G,),
        in_specs=[pl.BlockSpec((G, 2 * HEAD_DIM, PAGE_SIZE), lambda i: (base // G + i, 0, 0)),
                  pl.BlockSpec((2, S, 2 * HEAD_DIM), lambda i: (0, 0, 0)),
                  pl.BlockSpec((2, S, 2 * HEAD_DIM, 2 * HEAD_DIM), lambda i: (0, 0, 0, 0))],
        out_specs=pl.BlockSpec((G * cpp, HW), lambda i: (i, 0)),
        out_shape=jax.ShapeDtypeStruct((n_phys * cpp, HW), F32),
        scratch_shapes=[pltpu.VMEM((G * PAGE_SIZE, 2 * HEAD_DIM), F32)],
        compiler_params=_cparams(("parallel",)),
        name="nsa_compress_pool",
    )(view, pe, wz)


def _nsa_sel_decode_kernel(y_ref, q_ref, gain_ref, cover_ref, ocmp_ref, sel_ref, *, T, offset, n_cmp, n_sel):
    R = NH * T
    y = y_ref[0]
    nch = y.shape[0]
    c = y[:, :2 * HEAD_DIM] + pltpu.roll(y[:, 2 * HEAD_DIM:], nch - 1, 0)
    lane = lax.broadcasted_iota(jnp.int32, c.shape, 1)
    ms = jnp.sum(jnp.where(lane < HEAD_DIM, c * c, 0.0), axis=1, keepdims=True) / HEAD_DIM
    kn = c * lax.rsqrt(ms + RMS_EPS) * gain_ref[...]
    kc = kn[:, :HEAD_DIM].astype(BF)
    vc = c[:, HEAD_DIM:].astype(BF)
    q = q_ref[0].reshape(R, HEAD_DIM).astype(BF)
    pos = offset + lax.broadcasted_iota(jnp.int32, (R, 1), 0) % T
    n_idx = lax.broadcasted_iota(jnp.int32, (R, nch), 1)
    mask = (n_idx * NSA_CMP_STRIDE + (NSA_CMP_BLOCK - 1) <= pos) & (n_idx < n_cmp)
    s_c = jnp.where(mask, _dot_nt(q, kc), NEG_INF)
    m_c = jnp.max(s_c, axis=-1, keepdims=True)
    p_c = jnp.where(mask, jnp.exp(s_c - m_c), 0.0)
    p_c = (p_c / jnp.maximum(jnp.sum(p_c, axis=-1, keepdims=True), 1.0)).astype(BF)
    ocmp_ref[0] = _dot(p_c, vc)
    imp4 = _dot(p_c, cover_ref[...])
    imp = imp4[0:T]
    for h in range(1, NH):
        imp = imp + imp4[h * T:(h + 1) * T]
    j = lax.broadcasted_iota(jnp.int32, imp.shape, 1)
    cur = (offset + lax.broadcasted_iota(jnp.int32, (T, 1), 0)) // NSA_SEL_BLOCK
    forced = (j == 0) | (j == cur) | (j == cur - 1)
    imp = jnp.where(j > cur, NEG_INF, jnp.where(forced, SEL_FORCE, imp))
    rank = jnp.zeros(imp.shape, F32)
    for jp in range(n_sel):
        col = imp[:, jp:jp + 1]
        rank = rank + jnp.where((col > imp) | ((col == imp) & (j > jp)), 1.0, 0.0)
    sel_ref[0] = jnp.where((rank < NSA_TOPN) & (j <= cur), 1.0, 0.0)


def _nsa_decode_kernel(pt_ref, q_ref, sel_ref, ocmp_ref, newr_ref, win_ref, neww_ref, gl_ref, *rest,
                       PG, T, offset):
    page_refs, o_ref = rest[:PG], rest[PG]
    m_scr, l_scr, acc_scr = rest[PG + 1:]
    s = pl.program_id(1)
    R = NH * T
    D = HEAD_DIM

    @pl.when(s == 0)
    def _():
        m_scr[...] = jnp.full_like(m_scr, NEG_INF)
        l_scr[...] = jnp.zeros_like(l_scr)
        acc_scr[...] = jnp.zeros_like(acc_scr)

    q = q_ref[0].reshape(R, D).astype(BF)
    sel = sel_ref[0]
    jl = lax.broadcasted_iota(jnp.int32, sel.shape, 1)
    w2 = 2 * PAGE_SIZE
    kblk = lax.broadcasted_iota(jnp.int32, (T, w2), 1) // NSA_SEL_BLOCK
    m_i, l_i, acc = m_scr[...], l_scr[...], acc_scr[...]
    for j in range(0, PG, 2):
        b0 = (s * PG + j) * (PAGE_SIZE // NSA_SEL_BLOCK)
        pa, pb = page_refs[j][0], page_refs[j + 1][0]
        sc = _dot(q, jnp.concatenate([pa[2 * D:3 * D], pb[2 * D:3 * D]], axis=1).astype(BF))
        mt = jnp.zeros((T, w2), F32)
        for c in range(w2 // NSA_SEL_BLOCK):
            flag = jnp.max(jnp.where(jl == b0 + c, sel, 0.0), axis=1, keepdims=True)
            mt = jnp.where(kblk == c, flag, mt)
        mask = jnp.broadcast_to((mt > 0.5)[None], (NH, T, w2)).reshape(R, w2)
        vt = jnp.concatenate([pa[3 * D:], pb[3 * D:]], axis=1).astype(BF)
        m_i, l_i, acc = _softmax_update(sc, mask, m_i, l_i, acc, vt, R, v_feature_major=True)
    m_scr[...] = m_i
    l_scr[...] = l_i
    acc_scr[...] = acc

    @pl.when(s == pl.num_programs(1) - 1)
    def _():
        causal = _new_causal(R, T)
        newr = newr_ref[...]
        cb = offset // NSA_SEL_BLOCK
        own = jnp.broadcast_to((sel[:, cb:cb + 1] > 0.5)[None], (NH, T, T)).reshape(R, T)
        _, l_s, acc_s = _softmax_update(_dot_nt(q, newr[:, 2 * D:3 * D].astype(BF)), causal & own, m_i, l_i, acc,
                                        newr[:, 3 * D:].astype(BF), R)
        o_sel = acc_s / jnp.maximum(l_s, 1.0)
        win = win_ref[0]
        neww = neww_ref[...]
        wb = win.shape[1]
        tq = lax.broadcasted_iota(jnp.int32, (R, wb), 0) % T
        rk = lax.broadcasted_iota(jnp.int32, (R, wb), 1)
        mask1 = rk > wb + tq - NSA_WINDOW
        s1 = jnp.where(mask1, _dot(q, win[:D].astype(BF)), NEG_INF)
        s2 = jnp.where(causal, _dot_nt(q, neww[:, :D].astype(BF)), NEG_INF)
        m_w = jnp.maximum(jnp.max(s1, axis=-1, keepdims=True), jnp.max(s2, axis=-1, keepdims=True))
        p1 = jnp.where(mask1, jnp.exp(s1 - m_w), 0.0)
        p2 = jnp.where(causal, jnp.exp(s2 - m_w), 0.0)
        l_w = jnp.sum(p1, axis=-1, keepdims=True) + jnp.sum(p2, axis=-1, keepdims=True)
        o_win = (_dot_nt(p1.astype(BF), win[D:].astype(BF)) + _dot(p2.astype(BF), neww[:, D:].astype(BF))) \
            / jnp.maximum(l_w, 1.0)
        o_cmp = ocmp_ref[0]
        sig = jax.nn.sigmoid(gl_ref[...])
        outs = []
        for h in range(NH):
            c = MISC_LANE + 3 * h
            rs = slice(h * T, (h + 1) * T)
            outs.append(sig[:, c:c + 1] * o_cmp[rs] + sig[:, c + 1:c + 2] * o_sel[rs] + sig[:, c + 2:c + 3] * o_win[rs])
        o_ref[...] = jnp.concatenate(outs, axis=1)


def nsa_decode(sp, z, row0, pools, l, page_table, win_state, cmp_pe, cmp_w, gain1, B, T):
    n_pages = page_table.shape[1]
    past = n_pages * PAGE_SIZE
    S = past + T
    R = NH * T
    n_cmp = (S - NSA_CMP_BLOCK) // NSA_CMP_STRIDE + 1
    n_sel = -(-S // NSA_SEL_BLOCK)
    nch = past // NSA_CMP_STRIDE
    assert (n_cmp - 1) * NSA_CMP_STRIDE + NSA_CMP_BLOCK <= past, "compressed blocks must lie inside the cache"
    assert past % NSA_SEL_BLOCK == 0 and T <= NSA_SEL_BLOCK and n_sel <= HW
    PG = min(DEC_PG, n_pages)
    cpp = PAGE_SIZE // NSA_CMP_STRIDE
    y = nsa_compress_pool(pools, l, cmp_pe, cmp_w).reshape(pools.shape[1], cpp, HW)
    yb = y[page_table].reshape(B, nch, HW)
    ci = np.arange(nch)[:, None] * NSA_CMP_STRIDE
    sj = np.arange(HW)[None, :] * NSA_SEL_BLOCK
    cover = (ci < sj + NSA_SEL_BLOCK) & (ci + NSA_CMP_BLOCK > sj) & (np.arange(nch)[:, None] < n_cmp) \
        & (np.arange(HW)[None, :] < n_sel)
    cover = jnp.asarray(cover.astype(np.float32), dtype=BF)
    gain = jnp.concatenate([gain1, jnp.ones((HEAD_DIM,), F32)])[None, :]
    qspec = lambda nargs: pl.BlockSpec((1, NH, T, HEAD_DIM), (lambda b: (0, 0, b, 0)) if nargs == 1
                                       else (lambda b, s, pt: (0, 0, b, 0)))
    ocmp, sel = pl.pallas_call(
        functools.partial(_nsa_sel_decode_kernel, T=T, offset=past, n_cmp=n_cmp, n_sel=n_sel),
        grid=(B,),
        in_specs=[pl.BlockSpec((1, nch, HW), lambda b: (b, 0, 0)), qspec(1),
                  pl.BlockSpec((1, 2 * HEAD_DIM), lambda b: (0, 0)),
                  pl.BlockSpec((nch, HW), lambda b: (0, 0))],
        out_specs=[pl.BlockSpec((1, R, HEAD_DIM), lambda b: (b, 0, 0)), pl.BlockSpec((1, T, HW), lambda b: (b, 0, 0))],
        out_shape=[jax.ShapeDtypeStruct((B, R, HEAD_DIM), F32), jax.ShapeDtypeStruct((B, T, HW), F32)],
        compiler_params=_cparams(("parallel",)),
        name="nsa_select_decode",
    )(yb, sp["nqn"], gain, cover)
    view, base = _pool_view(pools, l, HW)
    wb = win_state.shape[1]
    rb0 = row0 // T
    return pl.pallas_call(
        functools.partial(_nsa_decode_kernel, PG=PG, T=T, offset=past),
        grid_spec=pltpu.PrefetchScalarGridSpec(
            num_scalar_prefetch=1, grid=(B, n_pages // PG),
            in_specs=[qspec(3),
                      pl.BlockSpec((1, T, HW), lambda b, s, pt: (b, 0, 0)),
                      pl.BlockSpec((1, R, HEAD_DIM), lambda b, s, pt: (b, 0, 0)),
                      pl.BlockSpec((T, HW), lambda b, s, pt: (b, 0)),
                      pl.BlockSpec((1, 2 * HEAD_DIM, wb), lambda b, s, pt: (b, 0, 0)),
                      pl.BlockSpec((T, 2 * HEAD_DIM), lambda b, s, pt: (b, 0)),
                      pl.BlockSpec((T, HW), lambda b, s, pt: (rb0 + b, SEC_S2))]
            + _page_specs((HW, PAGE_SIZE), PG, base),
            out_specs=pl.BlockSpec((T, HW), lambda b, s, pt: (b, 0)),
            scratch_shapes=[pltpu.VMEM((R, 1), F32), pltpu.VMEM((R, 1), F32), pltpu.VMEM((R, HEAD_DIM), F32)]),
        out_shape=jax.ShapeDtypeStruct((B * T, HW), F32),
        compiler_params=_cparams(("parallel", "arbitrary")),
        name="nsa_decode",
    )(page_table, sp["nqr"], sel, ocmp, sp["nsa_rows"], jnp.transpose(win_state, (0, 2, 3, 1)).reshape(B, 2 * HEAD_DIM, wb), sp["nsa_win"], z,
      *([view] * PG))


def rmsnorm(x, g):
    xf = x.astype(jnp.float32)
    y = xf * lax.rsqrt(jnp.mean(xf * xf, axis=-1, keepdims=True) + RMS_EPS)
    return (y * g.astype(jnp.float32)).astype(x.dtype)


def rope(x, pos):
    half = HEAD_DIM // 2
    inv = ROPE_THETA ** (-jnp.arange(half, dtype=jnp.float32) / half)
    ang = pos.astype(jnp.float32)[:, None] * inv[None, :]
    cos = jnp.cos(ang)[:, None, :]
    sin = jnp.sin(ang)[:, None, :]
    xf = x.astype(jnp.float32)
    x1, x2 = xf[..., :half], xf[..., half:]
    return jnp.concatenate([x1 * cos - x2 * sin, x2 * cos + x1 * sin], axis=-1).astype(x.dtype)


def masked_softmax(s, mask):
    s = jnp.where(mask, s.astype(jnp.float32), NEG_INF)
    m = jnp.max(s, axis=-1, keepdims=True)
    p = jnp.where(mask, jnp.exp(s - m), 0.0)
    return p / jnp.maximum(jnp.sum(p, axis=-1, keepdims=True), 1.0)


def sweep(fn, blk, *arrays):
    B, T = arrays[0].shape[:2]
    nb = -(-T // blk)
    Tp = nb * blk
    blocks = []
    for a in arrays:
        a = jnp.pad(a, [(0, 0), (0, Tp - T)] + [(0, 0)] * (a.ndim - 2))
        blocks.append(jnp.moveaxis(a.reshape((B, nb, blk) + a.shape[2:]), 1, 0))
    starts = jnp.arange(nb, dtype=jnp.int32) * blk
    out = lax.map(lambda args: fn(args[0], *args[1]), (starts, tuple(blocks)))
    out = jnp.moveaxis(out, 0, 1).reshape((B, Tp) + out.shape[3:])
    return out[:, :T]


def window_attend(q, rows, buf_len):
    B, T, H, Dh = q.shape
    dt = q.dtype
    W = NSA_WINDOW
    qb = min(Q_BLOCK, T)
    nb = -(-T // qb)
    Tp = nb * qb
    band = W + qb
    rp = jnp.pad(rows, ((0, 0), (W, Tp - T), (0, 0), (0, 0)))
    kidx = buf_len + np.arange(nb)[:, None] * qb + np.arange(band)[None, :]
    kb = rp[:, kidx]
    qp = jnp.pad(q, ((0, 0), (0, Tp - T), (0, 0), (0, 0))).reshape(B, nb, qb, H, Dh)
    s = jnp.einsum('bnqhd,bnkd->bhnqk', qp, kb[..., 0, :], preferred_element_type=jnp.float32) * ATTN_SCALE
    qq = buf_len + np.arange(nb)[:, None] * qb + np.arange(qb)[None, :]
    kk = kidx - W
    mask = (kk[:, None, :] >= 0) & (kk[:, None, :] <= qq[:, :, None]) & (kk[:, None, :] > qq[:, :, None] - W)
    p = masked_softmax(s, mask)
    o = jnp.einsum('bhnqk,bnkd->bnqhd', p.astype(dt), kb[..., 1, :]).reshape(B, Tp, H, Dh)
    return o[:, :T]


def nsa_mixer(q, kv_c, kv_s, kv_w, gate_logits, past_rows, win_buf, qk_gain, cmp_pe, cmp_w, offset):
    B, T, H, Dh = q.shape
    dt = q.dtype
    pos = offset + jnp.arange(T, dtype=jnp.int32)
    qn = rmsnorm(q, qk_gain[0])
    qr = rope(qn, pos)
    k_s = rope(rmsnorm(kv_s[:, :, 0:1], qk_gain[2]), pos)[:, :, 0]
    k_w = rope(rmsnorm(kv_w[:, :, 0:1], qk_gain[3]), pos)[:, :, 0]
    new_rows = jnp.stack([kv_c[:, :, 0], kv_c[:, :, 1], k_s, kv_s[:, :, 1]], axis=2)
    rows = jnp.concatenate([past_rows, new_rows], axis=1)
    S = offset + T
    n_cmp = (S - NSA_CMP_BLOCK) // NSA_CMP_STRIDE + 1
    cidx = np.arange(n_cmp)[:, None] * NSA_CMP_STRIDE + np.arange(NSA_CMP_BLOCK)[None, :]

    def compress(r):
        blocks = rows[:, :, r][:, cidx] + cmp_pe[r]
        return blocks.reshape(B, n_cmp, NSA_CMP_BLOCK * Dh) @ cmp_w[r]

    k_cmp = rmsnorm(compress(0), qk_gain[1])
    v_cmp = compress(1)
    s_c = jnp.einsum('bthd,bnd->bhtn', qn, k_cmp, preferred_element_type=jnp.float32) * ATTN_SCALE
    cmp_end = np.arange(n_cmp) * NSA_CMP_STRIDE + NSA_CMP_BLOCK - 1
    p_c = masked_softmax(s_c, cmp_end[None, :] <= pos[:, None])
    o_cmp = jnp.einsum('bhtn,bnd->bthd', p_c.astype(dt), v_cmp)
    n_sel = -(-S // NSA_SEL_BLOCK)
    ci = np.arange(n_cmp)[:, None] * NSA_CMP_STRIDE
    sj = np.arange(n_sel)[None, :] * NSA_SEL_BLOCK
    cover = ((ci < sj + NSA_SEL_BLOCK) & (ci + NSA_CMP_BLOCK > sj)).astype(np.float32)
    imp = jnp.einsum('bhtn,nj->btj', p_c, jnp.asarray(cover))
    cur = (pos // NSA_SEL_BLOCK)[:, None]
    jj = jnp.arange(n_sel)[None, :]
    forced = (jj == 0) | (jj == cur) | (jj == cur - 1)
    imp = jnp.where(jj > cur, NEG_INF, jnp.where(forced, SEL_FORCE, imp))
    _, sel_idx = lax.top_k(imp, min(NSA_TOPN, n_sel))
    kv_sel = jnp.pad(rows[:, :, 2:4], ((0, 0), (0, n_sel * NSA_SEL_BLOCK - S), (0, 0), (0, 0)))
    kv_sel = kv_sel.reshape(B, n_sel, NSA_SEL_BLOCK, 2, Dh)
    bidx = jnp.arange(B)[:, None, None]

    def sel_block(start, q_blk, idx_blk):
        qb = q_blk.shape[1]
        tp = offset + start + jnp.arange(qb)
        g = kv_sel[bidx, idx_blk]
        kpos = idx_blk[..., None] * NSA_SEL_BLOCK + jnp.arange(NSA_SEL_BLOCK)
        mask = (kpos <= tp[None, :, None, None]).reshape(B, 1, qb, -1)
        g = g.reshape(B, qb, -1, 2, Dh)
        s = jnp.einsum('bqhd,bqkd->bhqk', q_blk, g[..., 0, :], preferred_element_type=jnp.float32) * ATTN_SCALE
        p = masked_softmax(s, mask)
        return jnp.einsum('bhqk,bqkd->bqhd', p.astype(dt), g[..., 1, :])

    o_sel = sweep(sel_block, min(GATHER_Q_BLOCK, T), qr, sel_idx)
    win_rows = jnp.concatenate([win_buf, jnp.stack([k_w, kv_w[:, :, 1]], axis=2)], axis=1)
    o_win = window_attend(qr, win_rows, win_buf.shape[1])
    gates = jax.nn.sigmoid(gate_logits.astype(jnp.float32)).astype(dt)
    o = gates[..., 0:1] * o_cmp + gates[..., 1:2] * o_sel + gates[..., 2:3] * o_win
    new_win = win_rows[:, -min(NSA_WINDOW, win_rows.shape[1]):]
    return o, new_rows, new_win


def gated_recurrence(q, k, v, logf, S0):
    B, T, H, DK = q.shape
    DV = v.shape[-1]
    C = math.gcd(T, HGRN_CHUNK)
    nc = T // C

    def chunks(a):
        return jnp.moveaxis(a.reshape((B, nc, C) + a.shape[2:]), 1, 0).swapaxes(2, 3)

    causal = jnp.tril(jnp.ones((C, C), dtype=bool))[:, :, None]

    def step(S, inp):
        qc, kc, vc, gc = inp
        b = jnp.cumsum(gc, axis=2)
        o_inter = jnp.einsum('bhtk,bhkv->bhtv', qc * jnp.exp(b), S)
        diff = b[:, :, :, None, :] - b[:, :, None, :, :]
        decay = jnp.where(causal, jnp.exp(jnp.where(causal, diff, 0.0)), 0.0)
        A = jnp.einsum('bhtk,bhsk,bhtsk->bhts', qc, kc, decay)
        o = o_inter + jnp.einsum('bhts,bhsv->bhtv', A, vc)
        b_last = b[:, :, -1:, :]
        S_new = jnp.exp(b_last[:, :, 0, :])[..., None] * S + jnp.einsum('bhsk,bhsv->bhkv', kc * jnp.exp(b_last - b), vc)
        return S_new, o

    S, o = lax.scan(step, S0, (chunks(q), chunks(k), chunks(v), chunks(logf)))
    o = jnp.moveaxis(o.swapaxes(2, 3), 0, 1).reshape(B, T, H, DV)
    return o, S


def hgrn2_mixer(q, f, i, g, S0, lb, out_gain):
    dt = q.dtype
    H = q.shape[2]
    lb = lb.reshape(H, HGRN_DK)
    z = f.astype(jnp.float32)
    logf = jnp.log(lb + (1.0 - lb) * jax.nn.sigmoid(z))
    k = (1.0 - lb) * jax.nn.sigmoid(-z)
    qf = jax.nn.silu(q.astype(jnp.float32))
    o, S = gated_recurrence(qf, k, i.astype(jnp.float32), logf, S0)
    o = rmsnorm(o, out_gain) * jax.nn.silu(g.astype(jnp.float32))
    return o.astype(dt), S


def moba_mixer(q, k, v, past_rows, qk_gain, offset):
    B, T, H, Dh = q.shape
    dt = q.dtype
    pos = offset + jnp.arange(T, dtype=jnp.int32)
    qr = rope(rmsnorm(q, qk_gain[0]), pos)
    kr = rope(rmsnorm(k, qk_gain[1]), pos)
    new_rows = jnp.stack([kr, v], axis=2)
    rows = jnp.concatenate([past_rows, new_rows], axis=1)
    S = offset + T
    nblk = -(-S // MOBA_BLOCK)
    kvb = jnp.pad(rows, ((0, 0), (0, nblk * MOBA_BLOCK - S), (0, 0), (0, 0), (0, 0)))
    kvb = jnp.transpose(kvb.reshape(B, nblk, MOBA_BLOCK, 2, H, Dh), (0, 4, 1, 2, 3, 5))
    kmean = jnp.mean(kvb[..., 0, :].astype(jnp.float32), axis=3)
    gate = jnp.einsum('bthd,bhnd->bthn', qr.astype(jnp.float32), kmean)
    own = pos // MOBA_BLOCK
    past_ok = jnp.arange(nblk)[None, :] < own[:, None]
    _, top = lax.top_k(jnp.where(past_ok[None, :, None, :], gate, NEG_INF), min(MOBA_TOPK, nblk))
    valid = top < own[None, :, None, None]
    idx = jnp.concatenate([top, jnp.broadcast_to(own[None, :, None, None], (B, T, H, 1)).astype(top.dtype)], axis=-1)
    ok = jnp.concatenate([valid, jnp.ones((B, T, H, 1), dtype=bool)], axis=-1)
    bidx = jnp.arange(B)[:, None, None, None]
    hidx = jnp.arange(H)[None, None, :, None]

    def blk_fn(start, q_blk, idx_blk, ok_blk):
        qb = q_blk.shape[1]
        tp = offset + start + jnp.arange(qb)
        g = kvb[bidx, hidx, idx_blk]
        kpos = idx_blk[..., None] * MOBA_BLOCK + jnp.arange(MOBA_BLOCK)
        mask = (ok_blk[..., None] & (kpos <= tp[None, :, None, None, None])).reshape(B, qb, H, -1)
        g = g.reshape(B, qb, H, -1, 2, Dh)
        s = jnp.einsum('bqhd,bqhkd->bqhk', q_blk, g[..., 0, :], preferred_element_type=jnp.float32) * ATTN_SCALE
        p = masked_softmax(s, mask)
        return jnp.einsum('bqhk,bqhkd->bqhd', p.astype(dt), g[..., 1, :])

    o = sweep(blk_fn, min(GATHER_Q_BLOCK, T), qr, idx, ok)
    return o, new_rows


def fox_mixer(q, k, v, f_logit, past_kv, past_logf, qk_gain, f_bias, offset):
    B, T, H, Dh = q.shape
    dt = q.dtype
    qn = rmsnorm(q, qk_gain[0])
    kn = rmsnorm(k, qk_gain[1])
    logf_new = jax.nn.log_sigmoid(f_logit.astype(jnp.float32) + f_bias.astype(jnp.float32))
    new_rows = jnp.stack([kn, v], axis=2)
    rows = jnp.concatenate([past_kv, new_rows], axis=1)
    c = jnp.cumsum(jnp.concatenate([past_logf.astype(jnp.float32), logf_new], axis=1), axis=1)
    S = offset + T
    K = rows[:, :, 0]
    V = rows[:, :, 1]
    c_k = jnp.moveaxis(c, 1, 2)[:, :, None, :]
    kpos = jnp.arange(S)

    def blk_fn(start, q_blk, cq_blk):
        qb = q_blk.shape[1]
        tp = offset + start + jnp.arange(qb)
        s = jnp.einsum('bqhd,bkhd->bhqk', q_blk, K, preferred_element_type=jnp.float32) * ATTN_SCALE
        s = s + jnp.moveaxis(cq_blk, 1, 2)[..., None] - c_k
        p = masked_softmax(s, kpos[None, :] <= tp[:, None])
        return jnp.einsum('bhqk,bkhd->bqhd', p.astype(dt), V)

    o = sweep(blk_fn, min(Q_BLOCK, T), qn, c[:, offset:])
    return o, new_rows, logf_new.astype(dt)


def _outproj_kernel(x_ref, pn_ref, ph_ref, pm_ref, pf_ref, sn_ref, sh_ref, sm_ref, sf_ref, w_ref, g_ref,
                    xo_ref, hn_ref, *, n_prompt_blocks):
    i = pl.program_id(0)

    def project(parts):
        acc = x_ref[...]
        for m, part in enumerate(parts):
            acc = acc + _dot(part[...].astype(BF), w_ref[m * HW:(m + 1) * HW, :])
        xo_ref[...] = acc
        hn = acc * lax.rsqrt(jnp.mean(acc * acc, axis=-1, keepdims=True) + RMS_EPS) * g_ref[...]
        hn_ref[...] = hn.astype(hn_ref.dtype)

    pl.when(i < n_prompt_blocks)(lambda: project((pn_ref, ph_ref, pm_ref, pf_ref)))
    pl.when(i >= n_prompt_blocks)(lambda: project((sn_ref, sh_ref, sm_ref, sf_ref)))


def out_projection(x, prompt_parts, sample_parts, w_out, g):
    N, D = x.shape
    tm = PREP_TQ
    npb = prompt_parts[0].shape[0] // tm
    assert sample_parts[0].shape[0] == tm and N == (npb + 1) * tm
    pspec = pl.BlockSpec((tm, HW), lambda i: (jnp.minimum(i, npb - 1), 0))
    sspec = pl.BlockSpec((tm, HW), lambda i: (0, 0))
    row = pl.BlockSpec((tm, D), lambda i: (i, 0))
    return pl.pallas_call(
        functools.partial(_outproj_kernel, n_prompt_blocks=npb),
        grid=(npb + 1,),
        in_specs=[row] + [pspec] * 4 + [sspec] * 4 + [pl.BlockSpec((MIX_WIDTH, D), lambda i: (0, 0)),
                                                     pl.BlockSpec((1, D), lambda i: (0, 0))],
        out_specs=[row, row],
        out_shape=[jax.ShapeDtypeStruct((N, D), F32), jax.ShapeDtypeStruct((N, D), BF)],
        compiler_params=_cparams(("parallel",)),
        name="out_projection_rmsnorm",
    )(x, *prompt_parts, *sample_parts, w_out.astype(BF), g[None, :])


def _ffn_up_kernel(be_ref, new_ref, x_ref, w1_ref, w3_ref, u_ref, w1_scr, w3_scr):
    i = pl.program_id(1)

    @pl.when(new_ref[i] == 1)
    def _():
        w1_scr[...] = w1_ref[0].astype(BF)
        w3_scr[...] = w3_ref[0].astype(BF)

    x = x_ref[...].astype(BF)
    a = _dot(x, w1_scr[...])
    b = _dot(x, w3_scr[...])
    u_ref[...] = (a * jax.nn.sigmoid(a) * b).astype(u_ref.dtype)


def _ffn_down_kernel(be_ref, new_ref, u_ref, w2_ref, *rest, residual):
    res_ref = rest[0] if residual else None
    y_ref, w2_scr = rest[-2], rest[-1]
    i = pl.program_id(1)

    @pl.when(new_ref[i] == 1)
    def _():
        w2_scr[...] = w2_ref[0].astype(BF)

    y = _dot(u_ref[...], w2_scr[...])
    y_ref[...] = res_ref[...] + y if residual else y


def grouped_swiglu(x, block_exp, w1, w3, w2, tm, tf, tn, residual=None):
    R, D = x.shape
    F = w1.shape[2]
    nblk = R // tm
    block_exp = block_exp.astype(jnp.int32)
    new = jnp.concatenate([jnp.ones((1,), jnp.int32), (block_exp[1:] != block_exp[:-1]).astype(jnp.int32)])
    u = pl.pallas_call(
        _ffn_up_kernel,
        grid_spec=pltpu.PrefetchScalarGridSpec(
            num_scalar_prefetch=2, grid=(F // tf, nblk),
            in_specs=[pl.BlockSpec((tm, D), lambda j, i, be, nw: (i, 0)),
                      pl.BlockSpec((1, D, tf), lambda j, i, be, nw: (be[i], 0, j)),
                      pl.BlockSpec((1, D, tf), lambda j, i, be, nw: (be[i], 0, j))],
            out_specs=pl.BlockSpec((tm, tf), lambda j, i, be, nw: (i, j)),
            scratch_shapes=[pltpu.VMEM((D, tf), BF), pltpu.VMEM((D, tf), BF)]),
        out_shape=jax.ShapeDtypeStruct((R, F), BF),
        compiler_params=_cparams(("arbitrary", "arbitrary")),
        name="swiglu_up",
    )(block_exp, new, x, w1, w3)
    out_block = pl.BlockSpec((tm, tn), lambda n, i, be, nw: (i, n))
    extra = () if residual is None else (residual,)
    return pl.pallas_call(
        functools.partial(_ffn_down_kernel, residual=residual is not None),
        grid_spec=pltpu.PrefetchScalarGridSpec(
            num_scalar_prefetch=2, grid=(D // tn, nblk),
            in_specs=[pl.BlockSpec((tm, F), lambda n, i, be, nw: (i, 0)),
                      pl.BlockSpec((1, F, tn), lambda n, i, be, nw: (be[i], 0, n))] + [out_block] * len(extra),
            out_specs=out_block,
            scratch_shapes=[pltpu.VMEM((F, tn), BF)]),
        out_shape=jax.ShapeDtypeStruct((R, D), F32),
        compiler_params=_cparams(("arbitrary", "arbitrary")),
        name="swiglu_down",
    )(block_exp, new, u, w2, *extra)


def swiglu_dense(hn, x, w1, w3, w2):
    tm = 640 if hn.shape[0] % 640 == 0 else 256
    be = jnp.zeros((hn.shape[0] // tm,), jnp.int32)
    return grouped_swiglu(hn, be, w1[None], w3[None], w2[None], tm, DENSE_TF, FFN_TN, residual=x)


def moe_ffn_grouped(xf, router, w1, w3, w2):
    N, D = xf.shape
    tm = MOE_TM
    rpad = jnp.pad(router, ((0, 0), (0, LANES - N_EXPERTS)))
    logits = matmul(xf, rpad, tm=256, tn=LANES)[:, :N_EXPERTS]
    top_v, top_e = lax.top_k(logits, TOP_K)
    gates = jax.nn.softmax(top_v, axis=-1)
    NK = N * TOP_K
    flat_e = top_e.reshape(NK)
    order = jnp.argsort(flat_e)
    e_sorted = flat_e[order]
    tok_sorted = (order // TOP_K).astype(jnp.int32)
    counts = jnp.sum((flat_e[:, None] == jnp.arange(N_EXPERTS)[None, :]).astype(jnp.int32), axis=0)
    padded = (counts + tm - 1) // tm * tm
    pend = jnp.cumsum(padded)
    pstart = pend - padded
    start = jnp.cumsum(counts) - counts
    dest_sorted = pstart[e_sorted] + (jnp.arange(NK, dtype=jnp.int32) - start[e_sorted])
    n_blocks = -(-NK // tm) + N_EXPERTS
    slot_tok = jnp.full((n_blocks * tm,), N, jnp.int32).at[dest_sorted].set(tok_sorted)
    block_exp = jnp.clip(jnp.searchsorted(pend, jnp.arange(n_blocks) * tm, side='right'), 0, N_EXPERTS - 1)
    xpad = jnp.concatenate([xf, jnp.zeros((1, D), xf.dtype)], axis=0)
    xb = xpad[slot_tok]
    yb = grouped_swiglu(xb, block_exp, w1, w3, w2, tm, MOE_TF, MOE_TN)
    dest = jnp.zeros((NK,), jnp.int32).at[order].set(dest_sorted).reshape(N, TOP_K)
    return yb[dest[:, 0]] * gates[:, 0:1] + yb[dest[:, 1]] * gates[:, 1:2]


def swiglu(h, w1, w3, w2):
    return (jax.nn.silu(h @ w1) * (h @ w3)) @ w2


def moe_ffn(xf, router, w1, w3, w2):
    N, D = xf.shape
    dt = xf.dtype
    logits = (xf @ router).astype(jnp.float32)
    top_v, top_e = lax.top_k(logits, TOP_K)
    gates = jax.nn.softmax(top_v, axis=-1)
    NK = N * TOP_K
    flat_e = top_e.reshape(NK)
    flat_tok = jnp.arange(NK, dtype=jnp.int32) // TOP_K
    order = jnp.argsort(flat_e)
    e_sorted = flat_e[order]
    tok_sorted = flat_tok[order]
    counts = jnp.zeros((N_EXPERTS,), jnp.int32).at[flat_e].add(1)
    padded = (counts + MOE_BLOCK - 1) // MOE_BLOCK * MOE_BLOCK
    pend = jnp.cumsum(padded)
    pstart = pend - padded
    start = jnp.cumsum(counts) - counts
    dest = pstart[e_sorted] + (jnp.arange(NK, dtype=jnp.int32) - start[e_sorted])
    n_blocks = -(-NK // MOE_BLOCK) + N_EXPERTS
    slot_tok = jnp.full((n_blocks * MOE_BLOCK,), N, jnp.int32).at[dest].set(tok_sorted)
    block_exp = jnp.clip(jnp.searchsorted(pend, jnp.arange(n_blocks) * MOE_BLOCK, side='right'), 0, N_EXPERTS - 1)
    xpad = jnp.concatenate([xf, jnp.zeros((1, D), dt)], axis=0)
    xb = xpad[slot_tok].reshape(n_blocks, MOE_BLOCK, D)

    def expert_block(args):
        xblk, e = args
        return swiglu(xblk, w1[e], w3[e], w2[e])

    yb = lax.map(expert_block, (xb, block_exp)).reshape(n_blocks * MOE_BLOCK, D)
    y_assign = yb[dest] * gates.reshape(NK)[order][:, None].astype(dt)
    return jnp.zeros((N, D), dt).at[tok_sorted].add(y_assign)


def z_sections(z):
    s = lambda c, a=0, b=HW: z[..., c * HW + a:c * HW + b]
    d = HEAD_DIM
    return dict(nq=s(SEC_NQ), nkc=s(SEC_S1, 0, 2 * d), nks=s(SEC_S1, 2 * d, 4 * d), nkw=s(SEC_S2, 0, 2 * d),
                ngate=s(SEC_S2, MISC_LANE, MISC_LANE + 12), ff=s(SEC_S2, MISC_LANE + 12, MISC_LANE + 16),
                hq=s(SEC_HQ), hf=s(SEC_HF), hi=s(SEC_HI), hg=s(SEC_HG), mq=s(SEC_MQ), mk=s(SEC_MK), mv=s(SEC_MV),
                fq=s(SEC_FQ), fk=s(SEC_FK), fv=s(SEC_FV))


def kernel(x_prompt, x_sample, cache_nsa, state_nsa_win, state_hgrn, cache_moba, cache_fox_kv, cache_fox_logf,
           page_table, g_mix, g_ffn, w_in, w_out, nsa_qk_gain, nsa_cmp_pe, nsa_cmp_w, hgrn_lb_logits,
           hgrn_out_gain, moba_qk_gain, fox_qk_gain, fox_f_bias, ffn_w1, ffn_w3, ffn_w2, moe_router,
           moe_w1, moe_w3, moe_w2):
    dt = x_prompt.dtype
    Bp, Tp, D = x_prompt.shape
    Bs, Ts, _ = x_sample.shape
    Np, Ns = Bp * Tp, Bs * Ts
    past_len = page_table.shape[1] * PAGE_SIZE
    lb_w = jax.nn.softmax(hgrn_lb_logits.astype(jnp.float32), axis=0)
    lower_bounds = jnp.cumsum(lb_w, axis=0) - lb_w[0:1]

    def gather_pages(pool):
        g = pool[page_table]
        return g.reshape((Bs, past_len) + pool.shape[2:])

    cos_p, sin_p = rope_tables(jnp.arange(Tp, dtype=jnp.int32))
    cos_s, sin_s = rope_tables(past_len + jnp.arange(Ns, dtype=jnp.int32) % Ts)
    assert Ns == PREP_TQ and Np % PREP_TQ == 0
    gmat = group_mean_matrix()
    cover = nsa_constants(Tp)

    x = jnp.concatenate([x_prompt.reshape(Np, D), x_sample.reshape(Ns, D)], axis=0)
    st_p, st_s = [], []
    for l in range(DEPTH):
        i = l // 2
        z = in_projection(x, g_mix[l][None, :], relayout_w_in(w_in[l]),
                          tm=640 if x.shape[0] % 640 == 0 else 256)

        gains = head_gains(nsa_qk_gain[l], moba_qk_gain[l], fox_qk_gain[l])
        pp = prep_prompt(z, 0, Bp, Tp, cos_p, sin_p, gains, gmat)
        pe_flat, w_flat, cgain = nsa_compress_weights(nsa_cmp_pe[l], nsa_cmp_w[l], nsa_qk_gain[l][1])
        kc, vc = nsa_compress(pp["nsa_kc"], Bp, Tp, pe_flat, w_flat, cgain)
        o_nsa_p = nsa_attention_prompt(pp, kc, vc, z, 0, Bp, Tp, cover)
        o_mb_p = moba_attention_prompt(pp, Bp, Tp)
        ff_lo = SEC_S2 * HW + MISC_LANE + 12
        ff_p = z[:Np, ff_lo:ff_lo + NH].reshape(Bp, Tp, NH)
        logf_p = jax.nn.log_sigmoid(ff_p + fox_f_bias[l].astype(F32))
        o_fx_p = fox_attention_prompt(pp, jnp.cumsum(jnp.transpose(logf_p, (0, 2, 1)), axis=2), Bp, Tp)
        hd = lambda a, n, d, B, T: a.reshape(B, T, n, d)
        o_hg_p, hg_state_p = hgrn_mixer(z, 0, Bp, Tp, jnp.zeros((Bp, NH, HGRN_DK, HGRN_DV), F32), lower_bounds[l],
                                        hgrn_out_gain[l], gmat)
        nsa_win_p = pp["nsa_win"].reshape(Bp, Tp, 2, HEAD_DIM)[:, -min(NSA_WINDOW, Tp):]
        st_p.append((pp["nsa_rows"].reshape(Bp, Tp, NSA_ROWS, HEAD_DIM), nsa_win_p, hg_state_p.astype(dt),
                     pp["moba_rows"].reshape(Bp, Tp, 2, NH, HEAD_DIM), pp["fox_rows"].reshape(Bp, Tp, 2, NH, HEAD_DIM),
                     logf_p.astype(dt)))

        sp = prep_prompt(z, Np, 1, Ns, cos_s, sin_s, gains, gmat, qdt=F32)
        o_nsa_s = nsa_decode(sp, z, Np, cache_nsa, l, page_table, state_nsa_win[l], nsa_cmp_pe[l], nsa_cmp_w[l],
                             nsa_qk_gain[l][1], Bs, Ts)
        o_hg_s, hg_state = hgrn_mixer(z, Np, Bs, Ts, state_hgrn[l].astype(F32), lower_bounds[l], hgrn_out_gain[l], gmat)
        o_mb_s = moba_decode(sp, cache_moba, l, page_table, sp["moba_rows"], Bs, Ts)
        ff_s = z[Np:, SEC_S2 * HW + MISC_LANE + 12:SEC_S2 * HW + MISC_LANE + 16].reshape(Bs, Ts, NH)
        logf_s = jax.nn.log_sigmoid(ff_s + fox_f_bias[l].astype(F32))
        c_s = jnp.cumsum(jnp.transpose(jnp.concatenate([gather_pages(cache_fox_logf[l]).astype(F32), logf_s], axis=1),
                                       (0, 2, 1)), axis=2)
        o_fx_s = fox_decode(sp, cache_fox_kv, l, page_table, c_s, sp["fox_rows"], Bs, Ts)
        win_rows = jnp.concatenate([state_nsa_win[l], sp["nsa_win"].reshape(Bs, Ts, 2, HEAD_DIM)], axis=1)
        st_s.append((sp["nsa_rows"].reshape(Bs, Ts, NSA_ROWS, HEAD_DIM),
                     win_rows[:, -min(NSA_WINDOW, win_rows.shape[1]):], hg_state.astype(dt),
                     sp["moba_rows"].reshape(Bs, Ts, 2, NH, HEAD_DIM), sp["fox_rows"].reshape(Bs, Ts, 2, NH, HEAD_DIM),
                     logf_s.astype(dt)))

        x, hn = out_projection(x, (o_nsa_p, o_hg_p, o_mb_p, o_fx_p), (o_nsa_s, o_hg_s, o_mb_s, o_fx_s),
                               w_out[l], g_ffn[l])
        if l % 2 == 0:
            x = swiglu_dense(hn, x, ffn_w1[i], ffn_w3[i], ffn_w2[i])
        else:
            x = x + moe_ffn_grouped(hn, moe_router[i], moe_w1[i], moe_w3[i], moe_w2[i])

    def stk(states, j):
        return jnp.stack([s[j] for s in states], axis=0)

    return (x[:Np].reshape(Bp, Tp, D), x[Np:].reshape(Bs, Ts, D),
            stk(st_p, 0), stk(st_s, 0), stk(st_p, 1), stk(st_s, 1), stk(st_p, 2), stk(st_s, 2),
            stk(st_p, 3), stk(st_s, 3), stk(st_p, 4), stk(st_s, 4), stk(st_p, 5), stk(st_s, 5))
```

```python
import math, functools
import jax, jax.numpy as jnp
from jax import lax
import numpy as np
from jax.experimental import pallas as pl
from jax.experimental.pallas import tpu as pltpu

D_MODEL = 1024
DEPTH = 2
PAGE_SIZE = 128
HEAD_DIM = 64
H_NSA = 4
H_HGRN = 4
H_MOBA = 4
H_FOX = 4
NH = 4
HW = NH * HEAD_DIM
MIX_WIDTH = (H_NSA + H_HGRN + H_MOBA + H_FOX) * HEAD_DIM
HGRN_DK = 64
HGRN_DV = HEAD_DIM
HGRN_CHUNK = 64
NSA_CMP_BLOCK = 32
NSA_CMP_STRIDE = 16
NSA_SEL_BLOCK = 64
NSA_TOPN = 16
NSA_WINDOW = 512
NSA_ROWS = 4
MOBA_BLOCK = 256
MOBA_TOPK = 3
ROPE_THETA = 10000.0
Q_BLOCK = 128
GATHER_Q_BLOCK = 32
N_EXPERTS = 8
TOP_K = 2
MOE_BLOCK = 128
RMS_EPS = 1e-6
NEG_INF = -1e30
SEL_FORCE = 1e6
ATTN_SCALE = HEAD_DIM ** -0.5
IN_SIZES = (H_NSA * HEAD_DIM, 2 * HEAD_DIM, 2 * HEAD_DIM, 2 * HEAD_DIM, 3 * H_NSA,
            H_HGRN * HGRN_DK, H_HGRN * HGRN_DK, H_HGRN * HGRN_DV, H_HGRN * HGRN_DV,
            H_MOBA * HEAD_DIM, H_MOBA * HEAD_DIM, H_MOBA * HEAD_DIM,
            H_FOX * HEAD_DIM, H_FOX * HEAD_DIM, H_FOX * HEAD_DIM, H_FOX)
N_IN = sum(IN_SIZES)
IN_OFFS = tuple(int(v) for v in np.cumsum((0,) + IN_SIZES))

N_SEC = 13
N_INP = N_SEC * HW
SEC_NQ, SEC_S1, SEC_S2, SEC_HQ, SEC_HF, SEC_HI, SEC_HG = 0, 1, 2, 3, 4, 5, 6
SEC_MQ, SEC_MK, SEC_MV, SEC_FQ, SEC_FK, SEC_FV = 7, 8, 9, 10, 11, 12
MISC_LANE = 128

LANES = 128
VMEM_LIMIT = 48 * 1024 * 1024
PREP_TQ = 256
NSA_TQ = 256
NSA_TK = 1024
ATT_SB = 4
M_FLOOR = -1e20
MOE_TM = 256
MOE_TF = 1792
DENSE_TF = 1408
FFN_TN = 512
MOE_TN = 1024
BF = jnp.bfloat16
F32 = jnp.float32


def _round_up(x, m):
    return -(-x // m) * m


def _cparams(sem):
    return pltpu.CompilerParams(dimension_semantics=sem, vmem_limit_bytes=VMEM_LIMIT)


def _dot(a, b):
    return jnp.dot(a, b, preferred_element_type=F32)


def _dot_nt(a, b):
    return lax.dot_general(a, b, (((1,), (1,)), ((), ())), preferred_element_type=F32)


def _mm_kernel(a_ref, b_ref, o_ref):
    k = pl.program_id(2)
    acc = _dot(a_ref[...].astype(BF), b_ref[...].astype(BF))

    @pl.when(k == 0)
    def _():
        o_ref[...] = acc

    @pl.when(k != 0)
    def _():
        o_ref[...] += acc


def matmul(a, b, tm=512, tn=512, tk=1024):
    M, K = a.shape
    _, N = b.shape
    tm = min(tm, _round_up(M, 8))
    Mp, Np = _round_up(M, tm), _round_up(N, tn)
    if K % tk:
        tk = K
    if Mp != M:
        a = jnp.pad(a, ((0, Mp - M), (0, 0)))
    if Np != N:
        b = jnp.pad(b, ((0, 0), (0, Np - N)))
    out = pl.pallas_call(
        _mm_kernel,
        grid=(Mp // tm, Np // tn, K // tk),
        in_specs=[pl.BlockSpec((tm, tk), lambda i, j, k: (i, k)),
                  pl.BlockSpec((tk, tn), lambda i, j, k: (k, j))],
        out_specs=pl.BlockSpec((tm, tn), lambda i, j, k: (i, j)),
        out_shape=jax.ShapeDtypeStruct((Mp, Np), F32),
        compiler_params=_cparams(("parallel", "parallel", "arbitrary")),
        name="dense_matmul",
    )(a, b)
    return out[:M, :N]


def _inproj_kernel(x_ref, g_ref, w_ref, o_ref):
    x = x_ref[...]
    y = x * lax.rsqrt(jnp.mean(x * x, axis=-1, keepdims=True) + RMS_EPS) * g_ref[...]
    o_ref[...] = _dot(y.astype(BF), w_ref[...])


def in_projection(x, g, w_bf, tm=256):
    N, D = x.shape
    return pl.pallas_call(
        _inproj_kernel,
        grid=(N // tm,),
        in_specs=[pl.BlockSpec((tm, D), lambda i: (i, 0)),
                  pl.BlockSpec((1, D), lambda i: (0, 0)),
                  pl.BlockSpec((D, N_INP), lambda i: (0, 0))],
        out_specs=pl.BlockSpec((tm, N_INP), lambda i: (i, 0)),
        out_shape=jax.ShapeDtypeStruct((N, N_INP), F32),
        compiler_params=_cparams(("parallel",)),
        name="rmsnorm_in_projection",
    )(x, g, w_bf)


def relayout_w_in(w):
    def cols(i):
        return w[:, IN_OFFS[i]:IN_OFFS[i + 1]]
    pad = jnp.zeros((w.shape[0], HW - 2 * HEAD_DIM - IN_SIZES[4] - IN_SIZES[15]), w.dtype)
    parts = [cols(0), cols(1), cols(2), cols(3), cols(4), cols(15), pad] + [cols(i) for i in range(5, 15)]
    return jnp.concatenate(parts, axis=1).astype(BF)


def _head_meansq(x, gmat):
    sq = x * x
    hi = sq.astype(BF)
    lo = (sq - hi.astype(F32)).astype(BF)
    return _dot(hi, gmat) + _dot(lo, gmat)


def _head_rmsnorm(x, gain, gmat):
    return x * lax.rsqrt(_head_meansq(x, gmat) + RMS_EPS) * gain


def _rope(x, cos, sin_signed, lo_half):
    w = x.shape[1]
    swapped = jnp.where(lo_half, pltpu.roll(x, w - HEAD_DIM // 2, 1), pltpu.roll(x, HEAD_DIM // 2, 1))
    return x * cos + swapped * sin_signed


def _store_heads(ref, x):
    for h in range(NH):
        ref[0, h] = x[:, h * HEAD_DIM:(h + 1) * HEAD_DIM].astype(ref.dtype)


def _prep_kernel(nq_ref, s1_ref, s2_ref, mq_ref, mk_ref, mv_ref, fq_ref, fk_ref, fv_ref,
                 cos_ref, sin_ref, gains_ref, gmat_ref,
                 nsa_rows_ref, nsa_kc_ref, nsa_win_ref, moba_rows_ref, fox_rows_ref,
                 nqn_ref, nqr_ref, nks_ref, nvs_ref, nkw_ref, nvw_ref,
                 mqr_ref, mkr_ref, mvv_ref, kmean_ref, fqn_ref, fkn_ref, fvv_ref):
    cos = cos_ref[...]
    sin = sin_ref[...]
    gmat = gmat_ref[...]
    t = cos.shape[0]
    lane = lax.broadcasted_iota(jnp.int32, (t, HW), 1)
    lo_half = (lane % HEAD_DIM) < (HEAD_DIM // 2)
    gains = gains_ref[...]

    qn = _head_rmsnorm(nq_ref[...], gains[0:1], gmat)
    qr = _rope(qn, cos, sin, lo_half)
    _store_heads(nqn_ref, qn * ATTN_SCALE)
    _store_heads(nqr_ref, qr * ATTN_SCALE)

    s1 = s1_ref[...]
    s1r = _rope(_head_rmsnorm(s1, gains[1:2], gmat), cos, sin, lo_half)
    third = (lane >= 2 * HEAD_DIM) & (lane < 3 * HEAD_DIM)
    rows = jnp.where(third, s1r, s1)
    nsa_rows_ref[...] = rows
    nsa_kc_ref[...] = rows[:, :2 * HEAD_DIM]
    lane_h = lax.broadcasted_iota(jnp.int32, (t, HEAD_DIM), 1)
    row_h = lax.broadcasted_iota(jnp.int32, (t, HEAD_DIM), 0)
    blk = (pl.program_id(1) * t + row_h) // NSA_SEL_BLOCK
    ones_col = jnp.where(lane_h == 0, 1.0, 0.0)
    nks_ref[0] = jnp.concatenate([rows[:, 2 * HEAD_DIM:3 * HEAD_DIM], jnp.where(lane_h == blk, 1.0, 0.0)],
                                 axis=1).astype(nks_ref.dtype)
    nvs_ref[0] = jnp.concatenate([rows[:, 3 * HEAD_DIM:], ones_col], axis=1).astype(nvs_ref.dtype)

    s2 = s2_ref[...]
    s2r = _rope(_head_rmsnorm(s2, gains[2:3], gmat), cos, sin, lo_half)
    wrows = jnp.where(lane < HEAD_DIM, s2r, s2)
    nsa_win_ref[...] = wrows[:, :2 * HEAD_DIM]
    nkw_ref[0] = wrows[:, :HEAD_DIM].astype(nkw_ref.dtype)
    nvw_ref[0] = wrows[:, HEAD_DIM:2 * HEAD_DIM].astype(nvw_ref.dtype)

    mq = _rope(_head_rmsnorm(mq_ref[...], gains[3:4], gmat), cos, sin, lo_half)
    mk = _rope(_head_rmsnorm(mk_ref[...], gains[4:5], gmat), cos, sin, lo_half)
    mv = mv_ref[...]
    _store_heads(mqr_ref, mq * ATTN_SCALE)
    _store_heads(mkr_ref, mk)
    _store_heads(mvv_ref, mv)
    moba_rows_ref[:, :HW] = mk
    moba_rows_ref[:, HW:] = mv
    kmean_ref[0, 0] = jnp.mean(mk, axis=0, keepdims=True)

    fq = _head_rmsnorm(fq_ref[...], gains[5:6], gmat)
    fk = _head_rmsnorm(fk_ref[...], gains[6:7], gmat)
    fv = fv_ref[...]
    _store_heads(fqn_ref, fq * ATTN_SCALE)
    _store_heads(fkn_ref, fk)
    _store_heads(fvv_ref, fv)
    fox_rows_ref[:, :HW] = fk
    fox_rows_ref[:, HW:] = fv


def rope_tables(pos):
    half = HEAD_DIM // 2
    inv = ROPE_THETA ** (-jnp.arange(half, dtype=F32) / half)
    ang = pos.astype(F32)[:, None] * inv[None, :]
    cos = jnp.cos(ang)
    sin = jnp.sin(ang)
    cos_h = jnp.concatenate([cos, cos], axis=1)
    sin_h = jnp.concatenate([-sin, sin], axis=1)
    return jnp.tile(cos_h, (1, NH)), jnp.tile(sin_h, (1, NH))


def head_gains(nsa_gain, moba_gain, fox_gain):
    one = jnp.ones((HEAD_DIM,), F32)
    t4 = lambda g: jnp.tile(g, NH)
    rows = [t4(nsa_gain[0]),
            jnp.concatenate([one, one, nsa_gain[2], one]),
            jnp.concatenate([nsa_gain[3], one, one, one]),
            t4(moba_gain[0]), t4(moba_gain[1]), t4(fox_gain[0]), t4(fox_gain[1]), t4(one)]
    return jnp.stack(rows, axis=0)


def group_mean_matrix():
    idx = np.arange(HW) // HEAD_DIM
    return jnp.asarray((idx[:, None] == idx[None, :]).astype(np.float32) / HEAD_DIM, dtype=BF)


def prep_prompt(z, row0, B, T, cos, sin, gains, gmat, qdt=None):
    qdt = BF if qdt is None else qdt
    tq = PREP_TQ
    nq = T // tq
    rb0 = row0 // tq

    def sec(c):
        return pl.BlockSpec((tq, HW), lambda b, i, c=c: (rb0 + b * nq + i, c))

    flat = lambda w: pl.BlockSpec((tq, w), lambda b, i: (b * nq + i, 0))
    headmaj = pl.BlockSpec((1, NH, tq, HEAD_DIM), lambda b, i: (b, 0, i, 0))
    single = pl.BlockSpec((1, tq, HEAD_DIM), lambda b, i: (b, i, 0))
    single_aug = pl.BlockSpec((1, tq, LANES), lambda b, i: (b, i, 0))
    N = B * T
    sd = jax.ShapeDtypeStruct
    hm_shape = sd((B, NH, T, HEAD_DIM), qdt)
    sg_shape = sd((B, T, HEAD_DIM), qdt)
    outs = pl.pallas_call(
        _prep_kernel,
        grid=(B, nq),
        in_specs=[sec(SEC_NQ), sec(SEC_S1), sec(SEC_S2), sec(SEC_MQ), sec(SEC_MK), sec(SEC_MV),
                  sec(SEC_FQ), sec(SEC_FK), sec(SEC_FV),
                  pl.BlockSpec((tq, HW), lambda b, i: (i, 0)),
                  pl.BlockSpec((tq, HW), lambda b, i: (i, 0)),
                  pl.BlockSpec((8, HW), lambda b, i: (0, 0)),
                  pl.BlockSpec((HW, HW), lambda b, i: (0, 0))],
        out_specs=[flat(HW), flat(2 * HEAD_DIM), flat(2 * HEAD_DIM), flat(2 * HW), flat(2 * HW),
                   headmaj, headmaj, single_aug, single_aug, single, single,
                   headmaj, headmaj, headmaj,
                   pl.BlockSpec((1, 1, 1, HW), lambda b, i: (b, i, 0, 0)),
                   headmaj, headmaj, headmaj],
        out_shape=[sd((N, HW), F32), sd((N, 2 * HEAD_DIM), F32), sd((N, 2 * HEAD_DIM), F32),
                   sd((N, 2 * HW), F32), sd((N, 2 * HW), F32),
                   hm_shape, hm_shape, sd((B, T, LANES), qdt), sd((B, T, LANES), qdt), sg_shape, sg_shape,
                   hm_shape, hm_shape, hm_shape,
                   sd((B, nq, 1, HW), F32),
                   hm_shape, hm_shape, hm_shape],
        compiler_params=_cparams(("parallel", "parallel")),
        name="mixer_prep",
    )(z, z, z, z, z, z, z, z, z, cos, sin, gains, gmat)
    keys = ("nsa_rows", "nsa_kc", "nsa_win", "moba_rows", "fox_rows",
            "nqn", "nqr", "nks", "nvs", "nkw", "nvw", "mqr", "mkr", "mvv", "kmean", "fqn", "fkn", "fvv")
    return dict(zip(keys, outs))


def _nsa_compress_kernel(x_ref, pe_ref, w_ref, gain_ref, k_ref, v_ref):
    x = x_ref[0]
    a = _dot((x + pe_ref[0:1]).astype(BF), w_ref[0])
    b = _dot((x + pe_ref[1:2]).astype(BF), w_ref[1])
    nch = x.shape[0]
    y = a + pltpu.roll(b, nch - 1, 0)
    lane = lax.broadcasted_iota(jnp.int32, y.shape, 1)
    ms = jnp.sum(jnp.where(lane < HEAD_DIM, y * y, 0.0), axis=1, keepdims=True) / HEAD_DIM
    kn = y * lax.rsqrt(ms + RMS_EPS) * gain_ref[...]
    k_ref[0] = kn[:, :HEAD_DIM].astype(BF)
    v_ref[0] = y[:, HEAD_DIM:].astype(BF)


def nsa_compress_weights(cmp_pe, cmp_w, gain1):
    S = NSA_CMP_STRIDE
    pe = cmp_pe.reshape(2, 2, S, HEAD_DIM)
    pe_flat = jnp.transpose(pe, (1, 2, 0, 3)).reshape(2, S * 2 * HEAD_DIM)
    w = cmp_w.reshape(2, 2, S, HEAD_DIM, HEAD_DIM)
    wz = jnp.zeros((2, S, 2, HEAD_DIM, 2, HEAD_DIM), F32)
    wz = wz.at[:, :, 0, :, 0, :].set(w[0]).at[:, :, 1, :, 1, :].set(w[1])
    w_flat = wz.reshape(2, S * 2 * HEAD_DIM, 2 * HEAD_DIM).astype(BF)
    gain = jnp.concatenate([gain1, jnp.ones((HEAD_DIM,), F32)])[None, :]
    return pe_flat, w_flat, gain


def nsa_compress(kc, B, T, pe_flat, w_flat, gain):
    nch = T // NSA_CMP_STRIDE
    cw = NSA_CMP_STRIDE * 2 * HEAD_DIM
    x = kc.reshape(B, nch, cw)
    out_spec = pl.BlockSpec((1, nch, HEAD_DIM), lambda b: (b, 0, 0))
    return pl.pallas_call(
        _nsa_compress_kernel,
        grid=(B,),
        in_specs=[pl.BlockSpec((1, nch, cw), lambda b: (b, 0, 0)),
                  pl.BlockSpec((2, cw), lambda b: (0, 0)),
                  pl.BlockSpec((2, cw, 2 * HEAD_DIM), lambda b: (0, 0, 0)),
                  pl.BlockSpec((1, 2 * HEAD_DIM), lambda b: (0, 0))],
        out_specs=[out_spec, out_spec],
        out_shape=[jax.ShapeDtypeStruct((B, nch, HEAD_DIM), BF)] * 2,
        compiler_params=_cparams(("parallel",)),
        name="nsa_compress",
    )(x, pe_flat, w_flat, gain)


def _softmax_update(s, mask, m_i, l_i, acc, v, lead, v_feature_major=False):
    s = jnp.where(mask, s, NEG_INF)
    m_new = jnp.maximum(m_i, jnp.max(s, axis=-1, keepdims=True))
    p = jnp.where(mask, jnp.exp(s - m_new), 0.0)
    alpha = jnp.exp(m_i - m_new)
    l_new = alpha * l_i + jnp.sum(p, axis=-1, keepdims=True)
    pb = p.astype(BF).reshape(lead, p.shape[-1])
    pv = (_dot_nt(pb, v) if v_feature_major else _dot(pb, v)).reshape(acc.shape)
    return m_new, l_new, alpha * acc + pv


def _nsa_attn_kernel(qn_ref, qr_ref, kc_ref, vc_ref, ks_ref, vs_ref, kw_ref, vw_ref, gl_ref,
                     cover_ref, o_ref, m_scr, acc_scr, *, T):
    tq, tk = NSA_TQ, NSA_TK
    M = NH * tq
    i = pl.program_id(1)
    p0 = i * tq
    qn = qn_ref[0].reshape(M, HEAD_DIM)
    qr = qr_ref[0].reshape(M, HEAD_DIM)
    pos = p0 + lax.broadcasted_iota(jnp.int32, (tq, 1), 0)

    nch = kc_ref.shape[1]
    s_c = _dot_nt(qn, kc_ref[0]).reshape(NH, tq, nch)
    n_idx = lax.broadcasted_iota(jnp.int32, (tq, nch), 1)
    mask_c = (n_idx * NSA_CMP_STRIDE + (NSA_CMP_BLOCK - 1) <= pos)[None]
    s_c = jnp.where(mask_c, s_c, NEG_INF)
    m_c = jnp.max(s_c, axis=-1, keepdims=True)
    p_c = jnp.where(mask_c, jnp.exp(s_c - m_c), 0.0)
    p_c = p_c / jnp.maximum(jnp.sum(p_c, axis=-1, keepdims=True), 1.0)
    p_cb = p_c.astype(BF).reshape(M, nch)
    o_cmp = _dot(p_cb, vc_ref[0]).reshape(NH, tq, HEAD_DIM)
    imp = jnp.sum(_dot(p_cb, cover_ref[...]).reshape(NH, tq, LANES), axis=0)

    j = lax.broadcasted_iota(jnp.int32, (tq, LANES), 1)
    cur = pos // NSA_SEL_BLOCK
    forced = (j == 0) | (j == cur) | (j == cur - 1)
    imp = jnp.where(j > cur, NEG_INF, jnp.where(forced, SEL_FORCE, imp))
    n_sel = T // NSA_SEL_BLOCK
    rank = jnp.zeros((tq, LANES), F32)
    for jp in range(n_sel):
        col = imp[:, jp:jp + 1]
        beats = (col > imp) | ((col == imp) & (j > jp))
        rank = rank + jnp.where(beats, 1.0, 0.0)
    picked = (rank < NSA_TOPN) & (j <= cur)

    pen = jnp.where(picked, 0.0, NEG_INF)[:, :HEAD_DIM].astype(qr.dtype)
    qa = jnp.concatenate([qr, jnp.concatenate([pen] * NH, axis=0)], axis=1)
    _flash_init_aug(m_scr, acc_scr)
    kcol = lax.broadcasted_iota(jnp.int32, (tq, tk), 1)
    for kj in range(T // tk):
        def region(causal, kj=kj):
            s = _dot_nt(qa, ks_ref[0, kj * tk:(kj + 1) * tk, :]).reshape(NH, tq, tk)
            if causal:
                s = jnp.where((kcol + kj * tk <= pos)[None], s, NEG_INF)
            _flash_step_aug(s, vs_ref[0, kj * tk:(kj + 1) * tk, :], m_scr, acc_scr)

        pl.when((kj + 1) * tk <= p0)(functools.partial(region, False))
        pl.when((kj * tk <= p0) & ((kj + 1) * tk > p0))(functools.partial(region, True))
    acc_s = acc_scr[...]
    o_sel = acc_s[:, :, :HEAD_DIM] / jnp.maximum(acc_s[:, :, HEAD_DIM:HEAD_DIM + 1], 1.0)

    band = NSA_WINDOW + tq
    start = pl.multiple_of(jnp.maximum(p0 - NSA_WINDOW, 0), tq)
    kw = kw_ref[0, pl.ds(start, band), :]
    vw = vw_ref[0, pl.ds(start, band), :]
    s_w = _dot_nt(qr, kw).reshape(NH, tq, band)
    kpos = start + lax.broadcasted_iota(jnp.int32, (tq, band), 1)
    mask_w = ((kpos <= pos) & (kpos > pos - NSA_WINDOW))[None]
    s_w = jnp.where(mask_w, s_w, NEG_INF)
    m_w = jnp.max(s_w, axis=-1, keepdims=True)
    p_w = jnp.where(mask_w, jnp.exp(s_w - m_w), 0.0)
    l_w = jnp.sum(p_w, axis=-1, keepdims=True)
    o_win = _dot(p_w.astype(BF).reshape(M, band), vw).reshape(NH, tq, HEAD_DIM) / jnp.maximum(l_w, 1.0)

    sig = jax.nn.sigmoid(gl_ref[...])
    outs = []
    for h in range(NH):
        c = MISC_LANE + 3 * h
        outs.append(sig[:, c:c + 1] * o_cmp[h] + sig[:, c + 1:c + 2] * o_sel[h] + sig[:, c + 2:c + 3] * o_win[h])
    o_ref[...] = jnp.concatenate(outs, axis=1).astype(o_ref.dtype)


def nsa_constants(T):
    nch = T // NSA_CMP_STRIDE
    n_cmp = (T - NSA_CMP_BLOCK) // NSA_CMP_STRIDE + 1
    ci = np.arange(nch)[:, None] * NSA_CMP_STRIDE
    sj = np.arange(LANES)[None, :] * NSA_SEL_BLOCK
    cover = (ci < sj + NSA_SEL_BLOCK) & (ci + NSA_CMP_BLOCK > sj) & (np.arange(nch)[:, None] < n_cmp)
    return jnp.asarray(cover.astype(np.float32), dtype=BF)


def nsa_attention_prompt(pp, kc, vc, z, row0, B, T, cover):
    tq = NSA_TQ
    nq = T // tq
    rb0 = row0 // tq
    nch = T // NSA_CMP_STRIDE
    assert T >= NSA_WINDOW + tq and T % NSA_TK == 0 and NSA_TK % tq == 0 and T // NSA_SEL_BLOCK <= HEAD_DIM
    headmaj = pl.BlockSpec((1, NH, tq, HEAD_DIM), lambda b, i: (b, 0, i, 0))
    full1 = pl.BlockSpec((1, T, HEAD_DIM), lambda b, i: (b, 0, 0))
    full_aug = pl.BlockSpec((1, T, LANES), lambda b, i: (b, 0, 0))
    cmp1 = pl.BlockSpec((1, nch, HEAD_DIM), lambda b, i: (b, 0, 0))
    return pl.pallas_call(
        functools.partial(_nsa_attn_kernel, T=T),
        grid=(B, nq),
        in_specs=[headmaj, headmaj, cmp1, cmp1, full_aug, full_aug, full1, full1,
                  pl.BlockSpec((tq, HW), lambda b, i: (rb0 + b * nq + i, SEC_S2)),
                  pl.BlockSpec((nch, LANES), lambda b, i: (0, 0))],
        out_specs=pl.BlockSpec((tq, HW), lambda b, i: (b * nq + i, 0)),
        out_shape=jax.ShapeDtypeStruct((B * T, HW), BF),
        scratch_shapes=[pltpu.VMEM((NH, tq, 1), F32), pltpu.VMEM((NH, tq, LANES), F32)],
        compiler_params=_cparams(("parallel", "parallel")),
        name="nsa_attention",
    )(pp["nqn"], pp["nqr"], kc, vc, pp["nks"], pp["nvs"], pp["nkw"], pp["nvw"], z, cover)


def _flash_init(m_scr, l_scr, acc_scr):
    m_scr[...] = jnp.full_like(m_scr, M_FLOOR)
    l_scr[...] = jnp.zeros_like(l_scr)
    acc_scr[...] = jnp.zeros_like(acc_scr)


def _flash_step(s, v, m_scr, l_scr, acc_scr):
    m_i = m_scr[...]
    m_new = jnp.maximum(m_i, jnp.max(s, axis=-1, keepdims=True))
    p = jnp.exp(s - m_new)
    alpha = jnp.exp(m_i - m_new)
    l_scr[...] = alpha * l_scr[...] + jnp.sum(p, axis=-1, keepdims=True)
    pv = _dot(p.astype(BF).reshape(-1, p.shape[-1]), v)
    acc_scr[...] = alpha * acc_scr[...] + pv.reshape(acc_scr.shape)
    m_scr[...] = m_new


def _flash_init_aug(m_scr, acc_scr):
    m_scr[...] = jnp.full_like(m_scr, M_FLOOR)
    acc_scr[...] = jnp.zeros_like(acc_scr)


def _flash_step_aug(s, v_aug, m_scr, acc_scr):
    m_i = m_scr[...]
    m_new = jnp.maximum(m_i, jnp.max(s, axis=-1, keepdims=True))
    p = jnp.exp(s - m_new)
    pv = _dot(p.astype(BF).reshape(-1, p.shape[-1]), v_aug)
    acc_scr[...] = jnp.exp(m_i - m_new) * acc_scr[...] + pv.reshape(acc_scr.shape)
    m_scr[...] = m_new


def _moba_attn_kernel(q_ref, k_ref, v_ref, km_ref, o_ref, m_scr, l_scr, acc_scr):
    tq = MOBA_BLOCK
    tks = ATT_SB * MOBA_BLOCK
    n_sb = k_ref.shape[2] // tks
    qi = pl.program_id(1)
    lane = lax.broadcasted_iota(jnp.int32, (tq, LANES), 1)
    qrow = lax.broadcasted_iota(jnp.int32, (tq, tks), 0)
    kcol = lax.broadcasted_iota(jnp.int32, (tq, tks), 1)
    sels = []
    for h in range(NH):
        g = jnp.where(lane < qi, _dot_nt(q_ref[0, h], km_ref[0, h]), NEG_INF)
        sel = jnp.where(lane == qi, 1.0, 0.0)
        for _ in range(MOBA_TOPK):
            m = jnp.max(g, axis=1, keepdims=True)
            idx = jnp.min(jnp.where(g == m, lane, LANES), axis=1, keepdims=True)
            hit = lane == idx
            sel = jnp.where(hit & (lane < qi), 1.0, sel)
            g = jnp.where(hit, NEG_INF, g)
        sels.append(sel)
    _flash_init(m_scr, l_scr, acc_scr)
    for sb in range(n_sb):
        @pl.when(sb * ATT_SB <= qi)
        def _(sb=sb):
            causal = kcol + (sb * tks) <= qrow + qi * tq
            for h in range(NH):
                k = k_ref[0, h, sb * tks:(sb + 1) * tks, :]
                v = v_ref[0, h, sb * tks:(sb + 1) * tks, :]
                picked = jnp.concatenate(
                    [jnp.broadcast_to(sels[h][:, j:j + 1] > 0.5, (tq, MOBA_BLOCK))
                     for j in range(sb * ATT_SB, (sb + 1) * ATT_SB)], axis=1)
                s = jnp.where(picked & causal, _dot_nt(q_ref[0, h], k), NEG_INF)
                _flash_step(s, v, m_scr.at[h], l_scr.at[h], acc_scr.at[h])
    o = acc_scr[...] / jnp.maximum(l_scr[...], 1.0)
    o_ref[...] = jnp.concatenate([o[h] for h in range(NH)], axis=1).astype(o_ref.dtype)


def moba_attention_prompt(pp, B, T):
    tq = MOBA_BLOCK
    nq = T // tq
    km = pp["kmean"].reshape(B, nq, NH, HEAD_DIM).transpose(0, 2, 1, 3)
    km = jnp.pad(km, ((0, 0), (0, 0), (0, LANES - nq), (0, 0))).astype(BF)
    headq = pl.BlockSpec((1, NH, tq, HEAD_DIM), lambda b, i: (b, 0, i, 0))
    headfull = pl.BlockSpec((1, NH, T, HEAD_DIM), lambda b, i: (b, 0, 0, 0))
    return pl.pallas_call(
        _moba_attn_kernel,
        grid=(B, nq),
        in_specs=[headq, headfull, headfull,
                  pl.BlockSpec((1, NH, LANES, HEAD_DIM), lambda b, i: (b, 0, 0, 0))],
        out_specs=pl.BlockSpec((tq, HW), lambda b, i: (b * nq + i, 0)),
        out_shape=jax.ShapeDtypeStruct((B * T, HW), BF),
        scratch_shapes=[pltpu.VMEM((NH, tq, 1), F32), pltpu.VMEM((NH, tq, 1), F32),
                        pltpu.VMEM((NH, tq, HEAD_DIM), F32)],
        compiler_params=_cparams(("parallel", "parallel")),
        name="moba_attention",
    )(pp["mqr"], pp["mkr"], pp["mvv"], km)


def _fox_attn_kernel(q_ref, k_ref, v_ref, cq_ref, ck_ref, o_ref, m_scr, l_scr, acc_scr):
    tq = MOBA_BLOCK
    tks = ATT_SB * MOBA_BLOCK
    n_sb = k_ref.shape[2] // tks
    qi = pl.program_id(1)
    qrow = lax.broadcasted_iota(jnp.int32, (tq, tks), 0)
    kcol = lax.broadcasted_iota(jnp.int32, (tq, tks), 1)
    _flash_init(m_scr, l_scr, acc_scr)
    for sb in range(n_sb):
        last = (sb + 1) * ATT_SB - 1

        def region(causal, sb=sb):
            for h in range(NH):
                k = k_ref[0, h, sb * tks:(sb + 1) * tks, :]
                v = v_ref[0, h, sb * tks:(sb + 1) * tks, :]
                s = _dot_nt(q_ref[0, h], k) + cq_ref[0, h] - ck_ref[0, h, sb]
                if causal:
                    s = jnp.where(kcol + (sb * tks) <= qrow + qi * tq, s, NEG_INF)
                _flash_step(s, v, m_scr.at[h], l_scr.at[h], acc_scr.at[h])

        pl.when(last < qi)(functools.partial(region, False))
        pl.when((sb * ATT_SB <= qi) & (last >= qi))(functools.partial(region, True))
    o = acc_scr[...] / jnp.maximum(l_scr[...], 1.0)
    o_ref[...] = jnp.concatenate([o[h] for h in range(NH)], axis=1).astype(o_ref.dtype)


def fox_attention_prompt(pp, ch, B, T):
    tq = MOBA_BLOCK
    nq = T // tq
    cq = ch[..., None]
    tks = ATT_SB * MOBA_BLOCK
    ck = ch.reshape(B, NH, T // tks, 1, tks)
    headq = pl.BlockSpec((1, NH, tq, HEAD_DIM), lambda b, i: (b, 0, i, 0))
    headfull = pl.BlockSpec((1, NH, T, HEAD_DIM), lambda b, i: (b, 0, 0, 0))
    return pl.pallas_call(
        _fox_attn_kernel,
        grid=(B, nq),
        in_specs=[headq, headfull, headfull,
                  pl.BlockSpec((1, NH, tq, 1), lambda b, i: (b, 0, i, 0)),
                  pl.BlockSpec((1, NH, T // tks, 1, tks), lambda b, i: (b, 0, 0, 0, 0))],
        out_specs=pl.BlockSpec((tq, HW), lambda b, i: (b * nq + i, 0)),
        out_shape=jax.ShapeDtypeStruct((B * T, HW), BF),
        scratch_shapes=[pltpu.VMEM((NH, tq, 1), F32), pltpu.VMEM((NH, tq, 1), F32),
                        pltpu.VMEM((NH, tq, HEAD_DIM), F32)],
        compiler_params=_cparams(("parallel", "parallel")),
        name="fox_attention",
    )(pp["fqn"], pp["fkn"], pp["fvv"], cq, ck)


def _hgrn_kernel(q_ref, f_ref, i_ref, g_ref, s0_ref, lb_ref, gain_ref, bd_ref, gmat_ref,
                 o_ref, sout_ref, st_scr, *, C):
    c = pl.program_id(1)

    @pl.when(c == 0)
    def _():
        st_scr[...] = jnp.zeros_like(st_scr)
        for h in range(NH):
            st_scr[h * HEAD_DIM:(h + 1) * HEAD_DIM, h * HEAD_DIM:(h + 1) * HEAD_DIM] = s0_ref[0, h]

    lb = lb_ref[...]
    z = f_ref[...]
    logf = jnp.log(lb + (1.0 - lb) * jax.nn.sigmoid(z))
    kk = (1.0 - lb) * jax.nn.sigmoid(-z)
    q = q_ref[...]
    qf = q * jax.nn.sigmoid(q)
    v = i_ref[...]
    row = lax.broadcasted_iota(jnp.int32, (C, HW), 0)
    b = logf
    sh = 1
    while sh < C:
        b = b + jnp.where(row >= sh, pltpu.roll(b, sh, 0), 0.0)
        sh *= 2
    bd = bd_ref[...]
    st = st_scr[...]
    o_ref[...] = _dot_nt((qf * jnp.exp(b)).astype(BF), st.astype(BF))
    for s in range(C):
        causal = row >= s
        e = jnp.exp(jnp.where(causal, b - b[s:s + 1], 0.0))
        fz = jnp.where(causal, qf * kk[s:s + 1] * e, 0.0)
        o_ref[...] += _dot(fz.astype(BF), bd) * v[s:s + 1]
    o = o_ref[...]
    b_last = b[C - 1:C, :]
    kt = kk * jnp.exp(b_last - b)
    upd = lax.dot_general(v.astype(BF), kt.astype(BF), (((0,), (0,)), ((), ())), preferred_element_type=F32)
    r2 = lax.broadcasted_iota(jnp.int32, (HW, HW), 0) // HEAD_DIM
    c2 = lax.broadcasted_iota(jnp.int32, (HW, HW), 1) // HEAD_DIM
    st_new = st * jnp.exp(b_last) + jnp.where(r2 == c2, upd, 0.0)
    st_scr[...] = st_new
    g = g_ref[...]
    o_ref[...] = _head_rmsnorm(o, gain_ref[...], gmat_ref[...]) * (g * jax.nn.sigmoid(g))

    @pl.when(c == pl.num_programs(1) - 1)
    def _():
        for h in range(NH):
            sout_ref[0, h] = st_new[h * HEAD_DIM:(h + 1) * HEAD_DIM, h * HEAD_DIM:(h + 1) * HEAD_DIM]


def hgrn_mixer(z, row0, B, T, s0, lb, out_gain, gmat):
    C = math.gcd(T, HGRN_CHUNK)
    nc = T // C
    rb0 = row0 // C
    idx = np.arange(HW) // HEAD_DIM
    bd = jnp.asarray((idx[:, None] == idx[None, :]).astype(np.float32), dtype=BF)

    def sec(cidx):
        return pl.BlockSpec((C, HW), lambda b, c, cidx=cidx: (rb0 + b * nc + c, cidx))

    state = pl.BlockSpec((1, NH, HGRN_DV, HGRN_DK), lambda b, c: (b, 0, 0, 0))
    vec = pl.BlockSpec((1, HW), lambda b, c: (0, 0))
    mat = pl.BlockSpec((HW, HW), lambda b, c: (0, 0))
    o, st = pl.pallas_call(
        functools.partial(_hgrn_kernel, C=C),
        grid=(B, nc),
        in_specs=[sec(SEC_HQ), sec(SEC_HF), sec(SEC_HI), sec(SEC_HG), state, vec, vec, mat, mat],
        out_specs=[pl.BlockSpec((C, HW), lambda b, c: (b * nc + c, 0)), state],
        out_shape=[jax.ShapeDtypeStruct((B * T, HW), F32),
                   jax.ShapeDtypeStruct((B, NH, HGRN_DV, HGRN_DK), F32)],
        scratch_shapes=[pltpu.VMEM((HW, HW), F32)],
        compiler_params=_cparams(("parallel", "arbitrary")),
        name="hgrn_recurrence",
    )(z, z, z, z, jnp.swapaxes(s0, 2, 3), lb[None, :], jnp.tile(out_gain, NH)[None, :], bd, gmat)
    return o, jnp.swapaxes(st, 2, 3)


DEC_PG = 64


def _page_specs(tail, PG, base):
    return [pl.BlockSpec((1,) + tail, lambda b, s, pt, j=j: (base + pt[b, s * PG + j], 0, 0)) for j in range(PG)]


def _pool_view(pools, l, width):
    pages = pools.reshape(pools.shape[0] * pools.shape[1], PAGE_SIZE, width)
    return jnp.swapaxes(pages, 1, 2), l * pools.shape[1]


def _rows_nat(q_ref):
    return jnp.concatenate([q_ref[0, h] for h in range(NH)], axis=1)


def _block_diag_q(qnat, T):
    q4 = jnp.concatenate([qnat] * NH, axis=0)
    r = lax.broadcasted_iota(jnp.int32, q4.shape, 0) // T
    c = lax.broadcasted_iota(jnp.int32, q4.shape, 1) // HEAD_DIM
    return jnp.where(r == c, q4, 0.0)


def _diag_heads(x, T):
    return jnp.concatenate([x[h * T:(h + 1) * T, h * HEAD_DIM:(h + 1) * HEAD_DIM] for h in range(NH)], axis=1)


def _new_causal(R, T):
    tq = lax.broadcasted_iota(jnp.int32, (R, T), 0) % T
    tk = lax.broadcasted_iota(jnp.int32, (R, T), 1)
    return tk <= tq


def _fox_decode_kernel(pt_ref, q_ref, new_ref, cq_ref, cn_ref, ck_ref, *rest, PG, T):
    page_refs, o_ref = rest[:PG], rest[PG]
    m_scr, l_scr, acc_scr = rest[PG + 1:]
    s = pl.program_id(1)
    R = NH * T

    @pl.when(s == 0)
    def _():
        m_scr[...] = jnp.full_like(m_scr, NEG_INF)
        l_scr[...] = jnp.zeros_like(l_scr)
        acc_scr[...] = jnp.zeros_like(acc_scr)

    qbd = _block_diag_q(_rows_nat(q_ref), T).astype(BF)
    cq = cq_ref[0]
    m_i, l_i, acc = m_scr[...], l_scr[...], acc_scr[...]
    for j in range(0, PG, 2):
        pa, pb = page_refs[j][0], page_refs[j + 1][0]
        kt = jnp.concatenate([pa[:HW], pb[:HW]], axis=1).astype(BF)
        vt = jnp.concatenate([pa[HW:], pb[HW:]], axis=1).astype(BF)
        ck2 = jnp.concatenate([ck_ref[0, j], ck_ref[0, j + 1]], axis=1)
        ck = jnp.broadcast_to(ck2[:, None, :], (NH, T, 2 * PAGE_SIZE)).reshape(R, 2 * PAGE_SIZE)
        sc = _dot(qbd, kt) + cq - ck
        m_new = jnp.maximum(m_i, jnp.max(sc, axis=-1, keepdims=True))
        p = jnp.exp(sc - m_new)
        alpha = jnp.exp(m_i - m_new)
        l_i = alpha * l_i + jnp.sum(p, axis=-1, keepdims=True)
        acc = alpha * acc + _dot_nt(p.astype(BF), vt)
        m_i = m_new
    m_scr[...] = m_i
    l_scr[...] = l_i
    acc_scr[...] = acc

    @pl.when(s == pl.num_programs(1) - 1)
    def _():
        new = new_ref[...]
        sc = _dot_nt(qbd, new[:, :HW].astype(BF)) + cq - cn_ref[0]
        _, l_f, acc_f = _softmax_update(sc, _new_causal(R, T), m_i, l_i, acc, new[:, HW:].astype(BF), R)
        o_ref[...] = _diag_heads(acc_f / jnp.maximum(l_f, 1.0), T)


def fox_decode(sp, pools, l, page_table, ch, row_new, B, T):
    n_pages = page_table.shape[1]
    PG = min(DEC_PG, n_pages)
    past = n_pages * PAGE_SIZE
    R = NH * T
    view, base = _pool_view(pools, l, 2 * HW)
    ck = ch[:, :, :past].reshape(B, NH, n_pages, PAGE_SIZE).transpose(0, 2, 1, 3)
    cnew = ch[:, :, past:]
    cq = cnew.reshape(B, R, 1)
    cn = jnp.broadcast_to(cnew[:, :, None, :], (B, NH, T, T)).reshape(B, R, T)
    return pl.pallas_call(
        functools.partial(_fox_decode_kernel, PG=PG, T=T),
        grid_spec=pltpu.PrefetchScalarGridSpec(
            num_scalar_prefetch=1, grid=(B, n_pages // PG),
            in_specs=[pl.BlockSpec((1, NH, T, HEAD_DIM), lambda b, s, pt: (0, 0, b, 0)),
                      pl.BlockSpec((T, 2 * HW), lambda b, s, pt: (b, 0)),
                      pl.BlockSpec((1, R, 1), lambda b, s, pt: (b, 0, 0)),
                      pl.BlockSpec((1, R, T), lambda b, s, pt: (b, 0, 0)),
                      pl.BlockSpec((1, PG, NH, PAGE_SIZE), lambda b, s, pt: (b, s, 0, 0))]
            + _page_specs((2 * HW, PAGE_SIZE), PG, base),
            out_specs=pl.BlockSpec((T, HW), lambda b, s, pt: (b, 0)),
            scratch_shapes=[pltpu.VMEM((R, 1), F32), pltpu.VMEM((R, 1), F32), pltpu.VMEM((R, HW), F32)]),
        out_shape=jax.ShapeDtypeStruct((B * T, HW), F32),
        compiler_params=_cparams(("parallel", "arbitrary")),
        name="fox_decode",
    )(page_table, sp["fqn"], row_new, cq, cn, ck, *([view] * PG))


def _moba_decode_kernel(pt_ref, q_ref, new_ref, *rest, PG, T, n_pages):
    page_refs, o_ref = rest[:PG], rest[PG]
    m_scr, l_scr, rs_scr, acc_scr = rest[PG + 1:]
    s = pl.program_id(1)
    R = NH * T
    lane = lax.broadcasted_iota(jnp.int32, (R, LANES), 1)

    @pl.when(s == 0)
    def _():
        m_scr[...] = jnp.full_like(m_scr, NEG_INF)
        l_scr[...] = jnp.zeros_like(l_scr)
        rs_scr[...] = jnp.zeros_like(rs_scr)

    qbd = _block_diag_q(_rows_nat(q_ref), T).astype(BF)
    mm, ll, rs = m_scr[...], l_scr[...], rs_scr[...]
    for j in range(0, PG, 2):
        bid = (s * PG + j) // 2
        pa, pb = page_refs[j][0], page_refs[j + 1][0]
        sc = _dot(qbd, jnp.concatenate([pa[:HW], pb[:HW]], axis=1).astype(BF))
        m = jnp.max(sc, axis=-1, keepdims=True)
        p = jnp.exp(sc - m)
        acc_scr[bid] = _dot_nt(p.astype(BF), jnp.concatenate([pa[HW:], pb[HW:]], axis=1).astype(BF))
        hit = lane == bid
        mm = jnp.where(hit, m, mm)
        ll = jnp.where(hit, jnp.sum(p, axis=-1, keepdims=True), ll)
        rs = jnp.where(hit, jnp.sum(sc, axis=-1, keepdims=True), rs)
    m_scr[...] = mm
    l_scr[...] = ll
    rs_scr[...] = rs

    @pl.when(s == pl.num_programs(1) - 1)
    def _():
        block_lane = lane < n_pages // 2
        g = jnp.where(block_lane, rs * (1.0 / MOBA_BLOCK), NEG_INF)
        sel = jnp.zeros((R, LANES), F32)
        for _ in range(MOBA_TOPK):
            mx = jnp.max(g, axis=1, keepdims=True)
            idx = jnp.min(jnp.where(g == mx, lane, LANES), axis=1, keepdims=True)
            hit = lane == idx
            sel = jnp.where(hit & block_lane, 1.0, sel)
            g = jnp.where(hit, NEG_INF, g)
        picked = sel > 0.5
        new = new_ref[...]
        mask_n = _new_causal(R, T)
        sn = jnp.where(mask_n, _dot_nt(qbd, new[:, :HW].astype(BF)), NEG_INF)
        m_n = jnp.max(sn, axis=-1, keepdims=True)
        p_n = jnp.where(mask_n, jnp.exp(sn - m_n), 0.0)
        m_tot = jnp.maximum(jnp.max(jnp.where(picked, mm, NEG_INF), axis=1, keepdims=True), m_n)
        w = jnp.where(picked, jnp.exp(mm - m_tot), 0.0)
        w_n = jnp.exp(m_n - m_tot)
        l_tot = jnp.sum(w * ll, axis=1, keepdims=True) + w_n * jnp.sum(p_n, axis=-1, keepdims=True)
        acc = w_n * _dot(p_n.astype(BF), new[:, HW:].astype(BF))
        for blk in range(n_pages // 2):
            acc = acc + w[:, blk:blk + 1] * acc_scr[blk]
        o_ref[...] = _diag_heads(acc / jnp.maximum(l_tot, 1.0), T)


def moba_decode(sp, pools, l, page_table, row_new, B, T):
    n_pages = page_table.shape[1]
    assert MOBA_BLOCK == 2 * PAGE_SIZE and n_pages % 2 == 0 and n_pages <= LANES and T <= MOBA_BLOCK
    PG = min(DEC_PG, n_pages)
    R = NH * T
    view, base = _pool_view(pools, l, 2 * HW)
    return pl.pallas_call(
        functools.partial(_moba_decode_kernel, PG=PG, T=T, n_pages=n_pages),
        grid_spec=pltpu.PrefetchScalarGridSpec(
            num_scalar_prefetch=1, grid=(B, n_pages // PG),
            in_specs=[pl.BlockSpec((1, NH, T, HEAD_DIM), lambda b, s, pt: (0, 0, b, 0)),
                      pl.BlockSpec((T, 2 * HW), lambda b, s, pt: (b, 0))]
            + _page_specs((2 * HW, PAGE_SIZE), PG, base),
            out_specs=pl.BlockSpec((T, HW), lambda b, s, pt: (b, 0)),
            scratch_shapes=[pltpu.VMEM((R, LANES), F32), pltpu.VMEM((R, LANES), F32), pltpu.VMEM((R, LANES), F32),
                            pltpu.VMEM((n_pages // 2, R, HW), F32)]),
        out_shape=jax.ShapeDtypeStruct((B * T, HW), F32),
        compiler_params=_cparams(("parallel", "arbitrary")),
        name="moba_decode",
    )(page_table, sp["mqr"], row_new, *([view] * PG))


def _nsa_cmp_phys_kernel(xt_ref, pe_ref, w_ref, y_ref, xs_scr, *, G):
    S = NSA_CMP_STRIDE
    for g in range(G):
        xs_scr[g * PAGE_SIZE:(g + 1) * PAGE_SIZE, :] = xt_ref[g].T
    m = G * (PAGE_SIZE // S)
    a = jnp.zeros((m, 2 * HEAD_DIM), F32)
    b = jnp.zeros((m, 2 * HEAD_DIM), F32)
    for t in range(S):
        x = xs_scr[pl.ds(t, m, stride=S), :]
        a = a + _dot((x + pe_ref[0, t:t + 1]).astype(BF), w_ref[0, t])
        b = b + _dot((x + pe_ref[1, t:t + 1]).astype(BF), w_ref[1, t])
    y_ref[...] = jnp.concatenate([a, b], axis=1)


def nsa_compress_pool(pools, l, cmp_pe, cmp_w):
    S = NSA_CMP_STRIDE
    n_phys = pools.shape[1]
    cpp = PAGE_SIZE // S
    pe = jnp.transpose(cmp_pe.reshape(2, 2, S, HEAD_DIM), (1, 2, 0, 3)).reshape(2, S, 2 * HEAD_DIM)
    w = cmp_w.reshape(2, 2, S, HEAD_DIM, HEAD_DIM)
    wz = jnp.zeros((2, S, 2, HEAD_DIM, 2, HEAD_DIM), F32)
    wz = wz.at[:, :, 0, :, 0, :].set(w[0]).at[:, :, 1, :, 1, :].set(w[1])
    wz = wz.reshape(2, S, 2 * HEAD_DIM, 2 * HEAD_DIM).astype(BF)
    G = next(g for g in (32, 16, 8, 4, 2, 1) if n_phys % g == 0)
    view, base = _pool_view(pools, l, HW)
    return pl.pallas_call(
        functools.partial(_nsa_cmp_phys_kernel, G=G),
        grid=(n_phys // G,),
        in_specs=[pl.BlockSpec((G, 2 * HEAD_DIM, PAGE_SIZE), lambda i: (base // G + i, 0, 0)),
                  pl.BlockSpec((2, S, 2 * HEAD_DIM), lambda i: (0, 0, 0)),
                  pl.BlockSpec((2, S, 2 * HEAD_DIM, 2 * HEAD_DIM), lambda i: (0, 0, 0, 0))],
        out_specs=pl.BlockSpec((G * cpp, HW), lambda i: (i, 0)),
        out_shape=jax.ShapeDtypeStruct((n_phys * cpp, HW), F32),
        scratch_shapes=[pltpu.VMEM((G * PAGE_SIZE, 2 * HEAD_DIM), F32)],
        compiler_params=_cparams(("parallel",)),
        name="nsa_compress_pool",
    )(view, pe, wz)


def _nsa_sel_decode_kernel(y_ref, q_ref, gain_ref, cover_ref, ocmp_ref, sel_ref, *, T, offset, n_cmp, n_sel):
    R = NH * T
    y = y_ref[0]
    nch = y.shape[0]
    c = y[:, :2 * HEAD_DIM] + pltpu.roll(y[:, 2 * HEAD_DIM:], nch - 1, 0)
    lane = lax.broadcasted_iota(jnp.int32, c.shape, 1)
    ms = jnp.sum(jnp.where(lane < HEAD_DIM, c * c, 0.0), axis=1, keepdims=True) / HEAD_DIM
    kn = c * lax.rsqrt(ms + RMS_EPS) * gain_ref[...]
    kc = kn[:, :HEAD_DIM].astype(BF)
    vc = c[:, HEAD_DIM:].astype(BF)
    q = q_ref[0].reshape(R, HEAD_DIM).astype(BF)
    pos = offset + lax.broadcasted_iota(jnp.int32, (R, 1), 0) % T
    n_idx = lax.broadcasted_iota(jnp.int32, (R, nch), 1)
    mask = (n_idx * NSA_CMP_STRIDE + (NSA_CMP_BLOCK - 1) <= pos) & (n_idx < n_cmp)
    s_c = jnp.where(mask, _dot_nt(q, kc), NEG_INF)
    m_c = jnp.max(s_c, axis=-1, keepdims=True)
    p_c = jnp.where(mask, jnp.exp(s_c - m_c), 0.0)
    p_c = (p_c / jnp.maximum(jnp.sum(p_c, axis=-1, keepdims=True), 1.0)).astype(BF)
    ocmp_ref[0] = _dot(p_c, vc)
    imp4 = _dot(p_c, cover_ref[...])
    imp = imp4[0:T]
    for h in range(1, NH):
        imp = imp + imp4[h * T:(h + 1) * T]
    j = lax.broadcasted_iota(jnp.int32, imp.shape, 1)
    cur = (offset + lax.broadcasted_iota(jnp.int32, (T, 1), 0)) // NSA_SEL_BLOCK
    forced = (j == 0) | (j == cur) | (j == cur - 1)
    imp = jnp.where(j > cur, NEG_INF, jnp.where(forced, SEL_FORCE, imp))
    rank = jnp.zeros(imp.shape, F32)
    for jp in range(n_sel):
        col = imp[:, jp:jp + 1]
        rank = rank + jnp.where((col > imp) | ((col == imp) & (j > jp)), 1.0, 0.0)
    sel_ref[0] = jnp.where((rank < NSA_TOPN) & (j <= cur), 1.0, 0.0)


def _nsa_decode_kernel(pt_ref, q_ref, sel_ref, ocmp_ref, newr_ref, win_ref, neww_ref, gl_ref, *rest,
                       PG, T, offset):
    page_refs, o_ref = rest[:PG], rest[PG]
    m_scr, l_scr, acc_scr = rest[PG + 1:]
    s = pl.program_id(1)
    R = NH * T
    D = HEAD_DIM

    @pl.when(s == 0)
    def _():
        m_scr[...] = jnp.full_like(m_scr, NEG_INF)
        l_scr[...] = jnp.zeros_like(l_scr)
        acc_scr[...] = jnp.zeros_like(acc_scr)

    q = q_ref[0].reshape(R, D).astype(BF)
    sel = sel_ref[0]
    jl = lax.broadcasted_iota(jnp.int32, sel.shape, 1)
    w2 = 2 * PAGE_SIZE
    kblk = lax.broadcasted_iota(jnp.int32, (T, w2), 1) // NSA_SEL_BLOCK
    m_i, l_i, acc = m_scr[...], l_scr[...], acc_scr[...]
    for j in range(0, PG, 2):
        b0 = (s * PG + j) * (PAGE_SIZE // NSA_SEL_BLOCK)
        pa, pb = page_refs[j][0], page_refs[j + 1][0]
        sc = _dot(q, jnp.concatenate([pa[2 * D:3 * D], pb[2 * D:3 * D]], axis=1).astype(BF))
        mt = jnp.zeros((T, w2), F32)
        for c in range(w2 // NSA_SEL_BLOCK):
            flag = jnp.max(jnp.where(jl == b0 + c, sel, 0.0), axis=1, keepdims=True)
            mt = jnp.where(kblk == c, flag, mt)
        mask = jnp.broadcast_to((mt > 0.5)[None], (NH, T, w2)).reshape(R, w2)
        vt = jnp.concatenate([pa[3 * D:], pb[3 * D:]], axis=1).astype(BF)
        m_i, l_i, acc = _softmax_update(sc, mask, m_i, l_i, acc, vt, R, v_feature_major=True)
    m_scr[...] = m_i
    l_scr[...] = l_i
    acc_scr[...] = acc

    @pl.when(s == pl.num_programs(1) - 1)
    def _():
        causal = _new_causal(R, T)
        newr = newr_ref[...]
        cb = offset // NSA_SEL_BLOCK
        own = jnp.broadcast_to((sel[:, cb:cb + 1] > 0.5)[None], (NH, T, T)).reshape(R, T)
        _, l_s, acc_s = _softmax_update(_dot_nt(q, newr[:, 2 * D:3 * D].astype(BF)), causal & own, m_i, l_i, acc,
                                        newr[:, 3 * D:].astype(BF), R)
        o_sel = acc_s / jnp.maximum(l_s, 1.0)
        win = win_ref[0]
        neww = neww_ref[...]
        wb = win.shape[1]
        tq = lax.broadcasted_iota(jnp.int32, (R, wb), 0) % T
        rk = lax.broadcasted_iota(jnp.int32, (R, wb), 1)
        mask1 = rk > wb + tq - NSA_WINDOW
        s1 = jnp.where(mask1, _dot(q, win[:D].astype(BF)), NEG_INF)
        s2 = jnp.where(causal, _dot_nt(q, neww[:, :D].astype(BF)), NEG_INF)
        m_w = jnp.maximum(jnp.max(s1, axis=-1, keepdims=True), jnp.max(s2, axis=-1, keepdims=True))
        p1 = jnp.where(mask1, jnp.exp(s1 - m_w), 0.0)
        p2 = jnp.where(causal, jnp.exp(s2 - m_w), 0.0)
        l_w = jnp.sum(p1, axis=-1, keepdims=True) + jnp.sum(p2, axis=-1, keepdims=True)
        o_win = (_dot_nt(p1.astype(BF), win[D:].astype(BF)) + _dot(p2.astype(BF), neww[:, D:].astype(BF))) \
            / jnp.maximum(l_w, 1.0)
        o_cmp = ocmp_ref[0]
        sig = jax.nn.sigmoid(gl_ref[...])
        outs = []
        for h in range(NH):
            c = MISC_LANE + 3 * h
            rs = slice(h * T, (h + 1) * T)
            outs.append(sig[:, c:c + 1] * o_cmp[rs] + sig[:, c + 1:c + 2] * o_sel[rs] + sig[:, c + 2:c + 3] * o_win[rs])
        o_ref[...] = jnp.concatenate(outs, axis=1)


def nsa_decode(sp, z, row0, pools, l, page_table, win_state, cmp_pe, cmp_w, gain1, B, T):
    n_pages = page_table.shape[1]
    past = n_pages * PAGE_SIZE
    S = past + T
    R = NH * T
    n_cmp = (S - NSA_CMP_BLOCK) // NSA_CMP_STRIDE + 1
    n_sel = -(-S // NSA_SEL_BLOCK)
    nch = past // NSA_CMP_STRIDE
    assert (n_cmp - 1) * NSA_CMP_STRIDE + NSA_CMP_BLOCK <= past, "compressed blocks must lie inside the cache"
    assert past % NSA_SEL_BLOCK == 0 and T <= NSA_SEL_BLOCK and n_sel <= HW
    PG = min(DEC_PG, n_pages)
    cpp = PAGE_SIZE // NSA_CMP_STRIDE
    y = nsa_compress_pool(pools, l, cmp_pe, cmp_w).reshape(pools.shape[1], cpp, HW)
    yb = y[page_table].reshape(B, nch, HW)
    ci = np.arange(nch)[:, None] * NSA_CMP_STRIDE
    sj = np.arange(HW)[None, :] * NSA_SEL_BLOCK
    cover = (ci < sj + NSA_SEL_BLOCK) & (ci + NSA_CMP_BLOCK > sj) & (np.arange(nch)[:, None] < n_cmp) \
        & (np.arange(HW)[None, :] < n_sel)
    cover = jnp.asarray(cover.astype(np.float32), dtype=BF)
    gain = jnp.concatenate([gain1, jnp.ones((HEAD_DIM,), F32)])[None, :]
    qspec = lambda nargs: pl.BlockSpec((1, NH, T, HEAD_DIM), (lambda b: (0, 0, b, 0)) if nargs == 1
                                       else (lambda b, s, pt: (0, 0, b, 0)))
    ocmp, sel = pl.pallas_call(
        functools.partial(_nsa_sel_decode_kernel, T=T, offset=past, n_cmp=n_cmp, n_sel=n_sel),
        grid=(B,),
        in_specs=[pl.BlockSpec((1, nch, HW), lambda b: (b, 0, 0)), qspec(1),
                  pl.BlockSpec((1, 2 * HEAD_DIM), lambda b: (0, 0)),
                  pl.BlockSpec((nch, HW), lambda b: (0, 0))],
        out_specs=[pl.BlockSpec((1, R, HEAD_DIM), lambda b: (b, 0, 0)), pl.BlockSpec((1, T, HW), lambda b: (b, 0, 0))],
        out_shape=[jax.ShapeDtypeStruct((B, R, HEAD_DIM), F32), jax.ShapeDtypeStruct((B, T, HW), F32)],
        compiler_params=_cparams(("parallel",)),
        name="nsa_select_decode",
    )(yb, sp["nqn"], gain, cover)
    view, base = _pool_view(pools, l, HW)
    wb = win_state.shape[1]
    rb0 = row0 // T
    return pl.pallas_call(
        functools.partial(_nsa_decode_kernel, PG=PG, T=T, offset=past),
        grid_spec=pltpu.PrefetchScalarGridSpec(
            num_scalar_prefetch=1, grid=(B, n_pages // PG),
            in_specs=[qspec(3),
                      pl.BlockSpec((1, T, HW), lambda b, s, pt: (b, 0, 0)),
                      pl.BlockSpec((1, R, HEAD_DIM), lambda b, s, pt: (b, 0, 0)),
                      pl.BlockSpec((T, HW), lambda b, s, pt: (b, 0)),
                      pl.BlockSpec((1, 2 * HEAD_DIM, wb), lambda b, s, pt: (b, 0, 0)),
                      pl.BlockSpec((T, 2 * HEAD_DIM), lambda b, s, pt: (b, 0)),
                      pl.BlockSpec((T, HW), lambda b, s, pt: (rb0 + b, SEC_S2))]
            + _page_specs((HW, PAGE_SIZE), PG, base),
            out_specs=pl.BlockSpec((T, HW), lambda b, s, pt: (b, 0)),
            scratch_shapes=[pltpu.VMEM((R, 1), F32), pltpu.VMEM((R, 1), F32), pltpu.VMEM((R, HEAD_DIM), F32)]),
        out_shape=jax.ShapeDtypeStruct((B * T, HW), F32),
        compiler_params=_cparams(("parallel", "arbitrary")),
        name="nsa_decode",
    )(page_table, sp["nqr"], sel, ocmp, sp["nsa_rows"], jnp.transpose(win_state, (0, 2, 3, 1)).reshape(B, 2 * HEAD_DIM, wb), sp["nsa_win"], z,
      *([view] * PG))


def rmsnorm(x, g):
    xf = x.astype(jnp.float32)
    y = xf * lax.rsqrt(jnp.mean(xf * xf, axis=-1, keepdims=True) + RMS_EPS)
    return (y * g.astype(jnp.float32)).astype(x.dtype)


def rope(x, pos):
    half = HEAD_DIM // 2
    inv = ROPE_THETA ** (-jnp.arange(half, dtype=jnp.float32) / half)
    ang = pos.astype(jnp.float32)[:, None] * inv[None, :]
    cos = jnp.cos(ang)[:, None, :]
    sin = jnp.sin(ang)[:, None, :]
    xf = x.astype(jnp.float32)
    x1, x2 = xf[..., :half], xf[..., half:]
    return jnp.concatenate([x1 * cos - x2 * sin, x2 * cos + x1 * sin], axis=-1).astype(x.dtype)


def masked_softmax(s, mask):
    s = jnp.where(mask, s.astype(jnp.float32), NEG_INF)
    m = jnp.max(s, axis=-1, keepdims=True)
    p = jnp.where(mask, jnp.exp(s - m), 0.0)
    return p / jnp.maximum(jnp.sum(p, axis=-1, keepdims=True), 1.0)


def sweep(fn, blk, *arrays):
    B, T = arrays[0].shape[:2]
    nb = -(-T // blk)
    Tp = nb * blk
    blocks = []
    for a in arrays:
        a = jnp.pad(a, [(0, 0), (0, Tp - T)] + [(0, 0)] * (a.ndim - 2))
        blocks.append(jnp.moveaxis(a.reshape((B, nb, blk) + a.shape[2:]), 1, 0))
    starts = jnp.arange(nb, dtype=jnp.int32) * blk
    out = lax.map(lambda args: fn(args[0], *args[1]), (starts, tuple(blocks)))
    out = jnp.moveaxis(out, 0, 1).reshape((B, Tp) + out.shape[3:])
    return out[:, :T]


def window_attend(q, rows, buf_len):
    B, T, H, Dh = q.shape
    dt = q.dtype
    W = NSA_WINDOW
    qb = min(Q_BLOCK, T)
    nb = -(-T // qb)
    Tp = nb * qb
    band = W + qb
    rp = jnp.pad(rows, ((0, 0), (W, Tp - T), (0, 0), (0, 0)))
    kidx = buf_len + np.arange(nb)[:, None] * qb + np.arange(band)[None, :]
    kb = rp[:, kidx]
    qp = jnp.pad(q, ((0, 0), (0, Tp - T), (0, 0), (0, 0))).reshape(B, nb, qb, H, Dh)
    s = jnp.einsum('bnqhd,bnkd->bhnqk', qp, kb[..., 0, :], preferred_element_type=jnp.float32) * ATTN_SCALE
    qq = buf_len + np.arange(nb)[:, None] * qb + np.arange(qb)[None, :]
    kk = kidx - W
    mask = (kk[:, None, :] >= 0) & (kk[:, None, :] <= qq[:, :, None]) & (kk[:, None, :] > qq[:, :, None] - W)
    p = masked_softmax(s, mask)
    o = jnp.einsum('bhnqk,bnkd->bnqhd', p.astype(dt), kb[..., 1, :]).reshape(B, Tp, H, Dh)
    return o[:, :T]


def nsa_mixer(q, kv_c, kv_s, kv_w, gate_logits, past_rows, win_buf, qk_gain, cmp_pe, cmp_w, offset):
    B, T, H, Dh = q.shape
    dt = q.dtype
    pos = offset + jnp.arange(T, dtype=jnp.int32)
    qn = rmsnorm(q, qk_gain[0])
    qr = rope(qn, pos)
    k_s = rope(rmsnorm(kv_s[:, :, 0:1], qk_gain[2]), pos)[:, :, 0]
    k_w = rope(rmsnorm(kv_w[:, :, 0:1], qk_gain[3]), pos)[:, :, 0]
    new_rows = jnp.stack([kv_c[:, :, 0], kv_c[:, :, 1], k_s, kv_s[:, :, 1]], axis=2)
    rows = jnp.concatenate([past_rows, new_rows], axis=1)
    S = offset + T
    n_cmp = (S - NSA_CMP_BLOCK) // NSA_CMP_STRIDE + 1
    cidx = np.arange(n_cmp)[:, None] * NSA_CMP_STRIDE + np.arange(NSA_CMP_BLOCK)[None, :]

    def compress(r):
        blocks = rows[:, :, r][:, cidx] + cmp_pe[r]
        return blocks.reshape(B, n_cmp, NSA_CMP_BLOCK * Dh) @ cmp_w[r]

    k_cmp = rmsnorm(compress(0), qk_gain[1])
    v_cmp = compress(1)
    s_c = jnp.einsum('bthd,bnd->bhtn', qn, k_cmp, preferred_element_type=jnp.float32) * ATTN_SCALE
    cmp_end = np.arange(n_cmp) * NSA_CMP_STRIDE + NSA_CMP_BLOCK - 1
    p_c = masked_softmax(s_c, cmp_end[None, :] <= pos[:, None])
    o_cmp = jnp.einsum('bhtn,bnd->bthd', p_c.astype(dt), v_cmp)
    n_sel = -(-S // NSA_SEL_BLOCK)
    ci = np.arange(n_cmp)[:, None] * NSA_CMP_STRIDE
    sj = np.arange(n_sel)[None, :] * NSA_SEL_BLOCK
    cover = ((ci < sj + NSA_SEL_BLOCK) & (ci + NSA_CMP_BLOCK > sj)).astype(np.float32)
    imp = jnp.einsum('bhtn,nj->btj', p_c, jnp.asarray(cover))
    cur = (pos // NSA_SEL_BLOCK)[:, None]
    jj = jnp.arange(n_sel)[None, :]
    forced = (jj == 0) | (jj == cur) | (jj == cur - 1)
    imp = jnp.where(jj > cur, NEG_INF, jnp.where(forced, SEL_FORCE, imp))
    _, sel_idx = lax.top_k(imp, min(NSA_TOPN, n_sel))
    kv_sel = jnp.pad(rows[:, :, 2:4], ((0, 0), (0, n_sel * NSA_SEL_BLOCK - S), (0, 0), (0, 0)))
    kv_sel = kv_sel.reshape(B, n_sel, NSA_SEL_BLOCK, 2, Dh)
    bidx = jnp.arange(B)[:, None, None]

    def sel_block(start, q_blk, idx_blk):
        qb = q_blk.shape[1]
        tp = offset + start + jnp.arange(qb)
        g = kv_sel[bidx, idx_blk]
        kpos = idx_blk[..., None] * NSA_SEL_BLOCK + jnp.arange(NSA_SEL_BLOCK)
        mask = (kpos <= tp[None, :, None, None]).reshape(B, 1, qb, -1)
        g = g.reshape(B, qb, -1, 2, Dh)
        s = jnp.einsum('bqhd,bqkd->bhqk', q_blk, g[..., 0, :], preferred_element_type=jnp.float32) * ATTN_SCALE
        p = masked_softmax(s, mask)
        return jnp.einsum('bhqk,bqkd->bqhd', p.astype(dt), g[..., 1, :])

    o_sel = sweep(sel_block, min(GATHER_Q_BLOCK, T), qr, sel_idx)
    win_rows = jnp.concatenate([win_buf, jnp.stack([k_w, kv_w[:, :, 1]], axis=2)], axis=1)
    o_win = window_attend(qr, win_rows, win_buf.shape[1])
    gates = jax.nn.sigmoid(gate_logits.astype(jnp.float32)).astype(dt)
    o = gates[..., 0:1] * o_cmp + gates[..., 1:2] * o_sel + gates[..., 2:3] * o_win
    new_win = win_rows[:, -min(NSA_WINDOW, win_rows.shape[1]):]
    return o, new_rows, new_win


def gated_recurrence(q, k, v, logf, S0):
    B, T, H, DK = q.shape
    DV = v.shape[-1]
    C = math.gcd(T, HGRN_CHUNK)
    nc = T // C

    def chunks(a):
        return jnp.moveaxis(a.reshape((B, nc, C) + a.shape[2:]), 1, 0).swapaxes(2, 3)

    causal = jnp.tril(jnp.ones((C, C), dtype=bool))[:, :, None]

    def step(S, inp):
        qc, kc, vc, gc = inp
        b = jnp.cumsum(gc, axis=2)
        o_inter = jnp.einsum('bhtk,bhkv->bhtv', qc * jnp.exp(b), S)
        diff = b[:, :, :, None, :] - b[:, :, None, :, :]
        decay = jnp.where(causal, jnp.exp(jnp.where(causal, diff, 0.0)), 0.0)
        A = jnp.einsum('bhtk,bhsk,bhtsk->bhts', qc, kc, decay)
        o = o_inter + jnp.einsum('bhts,bhsv->bhtv', A, vc)
        b_last = b[:, :, -1:, :]
        S_new = jnp.exp(b_last[:, :, 0, :])[..., None] * S + jnp.einsum('bhsk,bhsv->bhkv', kc * jnp.exp(b_last - b), vc)
        return S_new, o

    S, o = lax.scan(step, S0, (chunks(q), chunks(k), chunks(v), chunks(logf)))
    o = jnp.moveaxis(o.swapaxes(2, 3), 0, 1).reshape(B, T, H, DV)
    return o, S


def hgrn2_mixer(q, f, i, g, S0, lb, out_gain):
    dt = q.dtype
    H = q.shape[2]
    lb = lb.reshape(H, HGRN_DK)
    z = f.astype(jnp.float32)
    logf = jnp.log(lb + (1.0 - lb) * jax.nn.sigmoid(z))
    k = (1.0 - lb) * jax.nn.sigmoid(-z)
    qf = jax.nn.silu(q.astype(jnp.float32))
    o, S = gated_recurrence(qf, k, i.astype(jnp.float32), logf, S0)
    o = rmsnorm(o, out_gain) * jax.nn.silu(g.astype(jnp.float32))
    return o.astype(dt), S


def moba_mixer(q, k, v, past_rows, qk_gain, offset):
    B, T, H, Dh = q.shape
    dt = q.dtype
    pos = offset + jnp.arange(T, dtype=jnp.int32)
    qr = rope(rmsnorm(q, qk_gain[0]), pos)
    kr = rope(rmsnorm(k, qk_gain[1]), pos)
    new_rows = jnp.stack([kr, v], axis=2)
    rows = jnp.concatenate([past_rows, new_rows], axis=1)
    S = offset + T
    nblk = -(-S // MOBA_BLOCK)
    kvb = jnp.pad(rows, ((0, 0), (0, nblk * MOBA_BLOCK - S), (0, 0), (0, 0), (0, 0)))
    kvb = jnp.transpose(kvb.reshape(B, nblk, MOBA_BLOCK, 2, H, Dh), (0, 4, 1, 2, 3, 5))
    kmean = jnp.mean(kvb[..., 0, :].astype(jnp.float32), axis=3)
    gate = jnp.einsum('bthd,bhnd->bthn', qr.astype(jnp.float32), kmean)
    own = pos // MOBA_BLOCK
    past_ok = jnp.arange(nblk)[None, :] < own[:, None]
    _, top = lax.top_k(jnp.where(past_ok[None, :, None, :], gate, NEG_INF), min(MOBA_TOPK, nblk))
    valid = top < own[None, :, None, None]
    idx = jnp.concatenate([top, jnp.broadcast_to(own[None, :, None, None], (B, T, H, 1)).astype(top.dtype)], axis=-1)
    ok = jnp.concatenate([valid, jnp.ones((B, T, H, 1), dtype=bool)], axis=-1)
    bidx = jnp.arange(B)[:, None, None, None]
    hidx = jnp.arange(H)[None, None, :, None]

    def blk_fn(start, q_blk, idx_blk, ok_blk):
        qb = q_blk.shape[1]
        tp = offset + start + jnp.arange(qb)
        g = kvb[bidx, hidx, idx_blk]
        kpos = idx_blk[..., None] * MOBA_BLOCK + jnp.arange(MOBA_BLOCK)
        mask = (ok_blk[..., None] & (kpos <= tp[None, :, None, None, None])).reshape(B, qb, H, -1)
        g = g.reshape(B, qb, H, -1, 2, Dh)
        s = jnp.einsum('bqhd,bqhkd->bqhk', q_blk, g[..., 0, :], preferred_element_type=jnp.float32) * ATTN_SCALE
        p = masked_softmax(s, mask)
        return jnp.einsum('bqhk,bqhkd->bqhd', p.astype(dt), g[..., 1, :])

    o = sweep(blk_fn, min(GATHER_Q_BLOCK, T), qr, idx, ok)
    return o, new_rows


def fox_mixer(q, k, v, f_logit, past_kv, past_logf, qk_gain, f_bias, offset):
    B, T, H, Dh = q.shape
    dt = q.dtype
    qn = rmsnorm(q, qk_gain[0])
    kn = rmsnorm(k, qk_gain[1])
    logf_new = jax.nn.log_sigmoid(f_logit.astype(jnp.float32) + f_bias.astype(jnp.float32))
    new_rows = jnp.stack([kn, v], axis=2)
    rows = jnp.concatenate([past_kv, new_rows], axis=1)
    c = jnp.cumsum(jnp.concatenate([past_logf.astype(jnp.float32), logf_new], axis=1), axis=1)
    S = offset + T
    K = rows[:, :, 0]
    V = rows[:, :, 1]
    c_k = jnp.moveaxis(c, 1, 2)[:, :, None, :]
    kpos = jnp.arange(S)

    def blk_fn(start, q_blk, cq_blk):
        qb = q_blk.shape[1]
        tp = offset + start + jnp.arange(qb)
        s = jnp.einsum('bqhd,bkhd->bhqk', q_blk, K, preferred_element_type=jnp.float32) * ATTN_SCALE
        s = s + jnp.moveaxis(cq_blk, 1, 2)[..., None] - c_k
        p = masked_softmax(s, kpos[None, :] <= tp[:, None])
        return jnp.einsum('bhqk,bkhd->bqhd', p.astype(dt), V)

    o = sweep(blk_fn, min(Q_BLOCK, T), qn, c[:, offset:])
    return o, new_rows, logf_new.astype(dt)


def _outproj_kernel(x_ref, pn_ref, ph_ref, pm_ref, pf_ref, sn_ref, sh_ref, sm_ref, sf_ref, w_ref, g_ref,
                    xo_ref, hn_ref, *, n_prompt_blocks):
    i = pl.program_id(0)

    def project(parts):
        acc = x_ref[...]
        for m, part in enumerate(parts):
            acc = acc + _dot(part[...].astype(BF), w_ref[m * HW:(m + 1) * HW, :])
        xo_ref[...] = acc
        hn = acc * lax.rsqrt(jnp.mean(acc * acc, axis=-1, keepdims=True) + RMS_EPS) * g_ref[...]
        hn_ref[...] = hn.astype(hn_ref.dtype)

    pl.when(i < n_prompt_blocks)(lambda: project((pn_ref, ph_ref, pm_ref, pf_ref)))
    pl.when(i >= n_prompt_blocks)(lambda: project((sn_ref, sh_ref, sm_ref, sf_ref)))


def out_projection(x, prompt_parts, sample_parts, w_out, g):
    N, D = x.shape
    tm = PREP_TQ
    npb = prompt_parts[0].shape[0] // tm
    assert sample_parts[0].shape[0] == tm and N == (npb + 1) * tm
    pspec = pl.BlockSpec((tm, HW), lambda i: (jnp.minimum(i, npb - 1), 0))
    sspec = pl.BlockSpec((tm, HW), lambda i: (0, 0))
    row = pl.BlockSpec((tm, D), lambda i: (i, 0))
    return pl.pallas_call(
        functools.partial(_outproj_kernel, n_prompt_blocks=npb),
        grid=(npb + 1,),
        in_specs=[row] + [pspec] * 4 + [sspec] * 4 + [pl.BlockSpec((MIX_WIDTH, D), lambda i: (0, 0)),
                                                     pl.BlockSpec((1, D), lambda i: (0, 0))],
        out_specs=[row, row],
        out_shape=[jax.ShapeDtypeStruct((N, D), F32), jax.ShapeDtypeStruct((N, D), BF)],
        compiler_params=_cparams(("parallel",)),
        name="out_projection_rmsnorm",
    )(x, *prompt_parts, *sample_parts, w_out.astype(BF), g[None, :])


def _ffn_up_kernel(be_ref, new_ref, x_ref, w1_ref, w3_ref, u_ref, w1_scr, w3_scr):
    i = pl.program_id(1)

    @pl.when(new_ref[i] == 1)
    def _():
        w1_scr[...] = w1_ref[0].astype(BF)
        w3_scr[...] = w3_ref[0].astype(BF)

    x = x_ref[...].astype(BF)
    a = _dot(x, w1_scr[...])
    b = _dot(x, w3_scr[...])
    u_ref[...] = (a * jax.nn.sigmoid(a) * b).astype(u_ref.dtype)


def _ffn_down_kernel(be_ref, new_ref, u_ref, w2_ref, *rest, residual):
    res_ref = rest[0] if residual else None
    y_ref, w2_scr = rest[-2], rest[-1]
    i = pl.program_id(1)

    @pl.when(new_ref[i] == 1)
    def _():
        w2_scr[...] = w2_ref[0].astype(BF)

    y = _dot(u_ref[...], w2_scr[...])
    y_ref[...] = res_ref[...] + y if residual else y


def grouped_swiglu(x, block_exp, w1, w3, w2, tm, tf, tn, residual=None):
    R, D = x.shape
    F = w1.shape[2]
    nblk = R // tm
    block_exp = block_exp.astype(jnp.int32)
    new = jnp.concatenate([jnp.ones((1,), jnp.int32), (block_exp[1:] != block_exp[:-1]).astype(jnp.int32)])
    u = pl.pallas_call(
        _ffn_up_kernel,
        grid_spec=pltpu.PrefetchScalarGridSpec(
            num_scalar_prefetch=2, grid=(F // tf, nblk),
            in_specs=[pl.BlockSpec((tm, D), lambda j, i, be, nw: (i, 0)),
                      pl.BlockSpec((1, D, tf), lambda j, i, be, nw: (be[i], 0, j)),
                      pl.BlockSpec((1, D, tf), lambda j, i, be, nw: (be[i], 0, j))],
            out_specs=pl.BlockSpec((tm, tf), lambda j, i, be, nw: (i, j)),
            scratch_shapes=[pltpu.VMEM((D, tf), BF), pltpu.VMEM((D, tf), BF)]),
        out_shape=jax.ShapeDtypeStruct((R, F), BF),
        compiler_params=_cparams(("arbitrary", "arbitrary")),
        name="swiglu_up",
    )(block_exp, new, x, w1, w3)
    out_block = pl.BlockSpec((tm, tn), lambda n, i, be, nw: (i, n))
    extra = () if residual is None else (residual,)
    return pl.pallas_call(
        functools.partial(_ffn_down_kernel, residual=residual is not None),
        grid_spec=pltpu.PrefetchScalarGridSpec(
            num_scalar_prefetch=2, grid=(D // tn, nblk),
            in_specs=[pl.BlockSpec((tm, F), lambda n, i, be, nw: (i, 0)),
                      pl.BlockSpec((1, F, tn), lambda n, i, be, nw: (be[i], 0, n))] + [out_block] * len(extra),
            out_specs=out_block,
            scratch_shapes=[pltpu.VMEM((F, tn), BF)]),
        out_shape=jax.ShapeDtypeStruct((R, D), F32),
        compiler_params=_cparams(("arbitrary", "arbitrary")),
        name="swiglu_down",
    )(block_exp, new, u, w2, *extra)


def swiglu_dense(hn, x, w1, w3, w2):
    tm = 640 if hn.shape[0] % 640 == 0 else 256
    be = jnp.zeros((hn.shape[0] // tm,), jnp.int32)
    return grouped_swiglu(hn, be, w1[None], w3[None], w2[None], tm, DENSE_TF, FFN_TN, residual=x)


def moe_ffn_grouped(xf, router, w1, w3, w2):
    N, D = xf.shape
    tm = MOE_TM
    rpad = jnp.pad(router, ((0, 0), (0, LANES - N_EXPERTS)))
    logits = matmul(xf, rpad, tm=256, tn=LANES)[:, :N_EXPERTS]
    top_v, top_e = lax.top_k(logits, TOP_K)
    gates = jax.nn.softmax(top_v, axis=-1)
    NK = N * TOP_K
    flat_e = top_e.reshape(NK)
    order = jnp.argsort(flat_e)
    e_sorted = flat_e[order]
    tok_sorted = (order // TOP_K).astype(jnp.int32)
    counts = jnp.sum((flat_e[:, None] == jnp.arange(N_EXPERTS)[None, :]).astype(jnp.int32), axis=0)
    padded = (counts + tm - 1) // tm * tm
    pend = jnp.cumsum(padded)
    pstart = pend - padded
    start = jnp.cumsum(counts) - counts
    dest_sorted = pstart[e_sorted] + (jnp.arange(NK, dtype=jnp.int32) - start[e_sorted])
    n_blocks = -(-NK // tm) + N_EXPERTS
    slot_tok = jnp.full((n_blocks * tm,), N, jnp.int32).at[dest_sorted].set(tok_sorted)
    block_exp = jnp.clip(jnp.searchsorted(pend, jnp.arange(n_blocks) * tm, side='right'), 0, N_EXPERTS - 1)
    xpad = jnp.concatenate([xf, jnp.zeros((1, D), xf.dtype)], axis=0)
    xb = xpad[slot_tok]
    yb = grouped_swiglu(xb, block_exp, w1, w3, w2, tm, MOE_TF, MOE_TN)
    dest = jnp.zeros((NK,), jnp.int32).at[order].set(dest_sorted).reshape(N, TOP_K)
    return yb[dest[:, 0]] * gates[:, 0:1] + yb[dest[:, 1]] * gates[:, 1:2]


def swiglu(h, w1, w3, w2):
    return (jax.nn.silu(h @ w1) * (h @ w3)) @ w2


def moe_ffn(xf, router, w1, w3, w2):
    N, D = xf.shape
    dt = xf.dtype
    logits = (xf @ router).astype(jnp.float32)
    top_v, top_e = lax.top_k(logits, TOP_K)
    gates = jax.nn.softmax(top_v, axis=-1)
    NK = N * TOP_K
    flat_e = top_e.reshape(NK)
    flat_tok = jnp.arange(NK, dtype=jnp.int32) // TOP_K
    order = jnp.argsort(flat_e)
    e_sorted = flat_e[order]
    tok_sorted = flat_tok[order]
    counts = jnp.zeros((N_EXPERTS,), jnp.int32).at[flat_e].add(1)
    padded = (counts + MOE_BLOCK - 1) // MOE_BLOCK * MOE_BLOCK
    pend = jnp.cumsum(padded)
    pstart = pend - padded
    start = jnp.cumsum(counts) - counts
    dest = pstart[e_sorted] + (jnp.arange(NK, dtype=jnp.int32) - start[e_sorted])
    n_blocks = -(-NK // MOE_BLOCK) + N_EXPERTS
    slot_tok = jnp.full((n_blocks * MOE_BLOCK,), N, jnp.int32).at[dest].set(tok_sorted)
    block_exp = jnp.clip(jnp.searchsorted(pend, jnp.arange(n_blocks) * MOE_BLOCK, side='right'), 0, N_EXPERTS - 1)
    xpad = jnp.concatenate([xf, jnp.zeros((1, D), dt)], axis=0)
    xb = xpad[slot_tok].reshape(n_blocks, MOE_BLOCK, D)

    def expert_block(args):
        xblk, e = args
        return swiglu(xblk, w1[e], w3[e], w2[e])

    yb = lax.map(expert_block, (xb, block_exp)).reshape(n_blocks * MOE_BLOCK, D)
    y_assign = yb[dest] * gates.reshape(NK)[order][:, None].astype(dt)
    return jnp.zeros((N, D), dt).at[tok_sorted].add(y_assign)


def z_sections(z):
    s = lambda c, a=0, b=HW: z[..., c * HW + a:c * HW + b]
    d = HEAD_DIM
    return dict(nq=s(SEC_NQ), nkc=s(SEC_S1, 0, 2 * d), nks=s(SEC_S1, 2 * d, 4 * d), nkw=s(SEC_S2, 0, 2 * d),
                ngate=s(SEC_S2, MISC_LANE, MISC_LANE + 12), ff=s(SEC_S2, MISC_LANE + 12, MISC_LANE + 16),
                hq=s(SEC_HQ), hf=s(SEC_HF), hi=s(SEC_HI), hg=s(SEC_HG), mq=s(SEC_MQ), mk=s(SEC_MK), mv=s(SEC_MV),
                fq=s(SEC_FQ), fk=s(SEC_FK), fv=s(SEC_FV))


def kernel(x_prompt, x_sample, cache_nsa, state_nsa_win, state_hgrn, cache_moba, cache_fox_kv, cache_fox_logf,
           page_table, g_mix, g_ffn, w_in, w_out, nsa_qk_gain, nsa_cmp_pe, nsa_cmp_w, hgrn_lb_logits,
           hgrn_out_gain, moba_qk_gain, fox_qk_gain, fox_f_bias, ffn_w1, ffn_w3, ffn_w2, moe_router,
           moe_w1, moe_w3, moe_w2):
    dt = x_prompt.dtype
    Bp, Tp, D = x_prompt.shape
    Bs, Ts, _ = x_sample.shape
    Np, Ns = Bp * Tp, Bs * Ts
    past_len = page_table.shape[1] * PAGE_SIZE
    lb_w = jax.nn.softmax(hgrn_lb_logits.astype(jnp.float32), axis=0)
    lower_bounds = jnp.cumsum(lb_w, axis=0) - lb_w[0:1]

    def gather_pages(pool):
        g = pool[page_table]
        return g.reshape((Bs, past_len) + pool.shape[2:])

    cos_p, sin_p = rope_tables(jnp.arange(Tp, dtype=jnp.int32))
    cos_s, sin_s = rope_tables(past_len + jnp.arange(Ns, dtype=jnp.int32) % Ts)
    assert Ns == PREP_TQ and Np % PREP_TQ == 0
    gmat = group_mean_matrix()
    cover = nsa_constants(Tp)

    x = jnp.concatenate([x_prompt.reshape(Np, D), x_sample.reshape(Ns, D)], axis=0)
    st_p, st_s = [], []
    for l in range(DEPTH):
        i = l // 2
        z = in_projection(x, g_mix[l][None, :], relayout_w_in(w_in[l]),
                          tm=640 if x.shape[0] % 640 == 0 else 256)

        gains = head_gains(nsa_qk_gain[l], moba_qk_gain[l], fox_qk_gain[l])
        pp = prep_prompt(z, 0, Bp, Tp, cos_p, sin_p, gains, gmat)
        pe_flat, w_flat, cgain = nsa_compress_weights(nsa_cmp_pe[l], nsa_cmp_w[l], nsa_qk_gain[l][1])
        kc, vc = nsa_compress(pp["nsa_kc"], Bp, Tp, pe_flat, w_flat, cgain)
        o_nsa_p = nsa_attention_prompt(pp, kc, vc, z, 0, Bp, Tp, cover)
        o_mb_p = moba_attention_prompt(pp, Bp, Tp)
        ff_lo = SEC_S2 * HW + MISC_LANE + 12
        ff_p = z[:Np, ff_lo:ff_lo + NH].reshape(Bp, Tp, NH)
        logf_p = jax.nn.log_sigmoid(ff_p + fox_f_bias[l].astype(F32))
        o_fx_p = fox_attention_prompt(pp, jnp.cumsum(jnp.transpose(logf_p, (0, 2, 1)), axis=2), Bp, Tp)
        hd = lambda a, n, d, B, T: a.reshape(B, T, n, d)
        o_hg_p, hg_state_p = hgrn_mixer(z, 0, Bp, Tp, jnp.zeros((Bp, NH, HGRN_DK, HGRN_DV), F32), lower_bounds[l],
                                        hgrn_out_gain[l], gmat)
        nsa_win_p = pp["nsa_win"].reshape(Bp, Tp, 2, HEAD_DIM)[:, -min(NSA_WINDOW, Tp):]
        st_p.append((pp["nsa_rows"].reshape(Bp, Tp, NSA_ROWS, HEAD_DIM), nsa_win_p, hg_state_p.astype(dt),
                     pp["moba_rows"].reshape(Bp, Tp, 2, NH, HEAD_DIM), pp["fox_rows"].reshape(Bp, Tp, 2, NH, HEAD_DIM),
                     logf_p.astype(dt)))

        sp = prep_prompt(z, Np, 1, Ns, cos_s, sin_s, gains, gmat, qdt=F32)
        o_nsa_s = nsa_decode(sp, z, Np, cache_nsa, l, page_table, state_nsa_win[l], nsa_cmp_pe[l], nsa_cmp_w[l],
                             nsa_qk_gain[l][1], Bs, Ts)
        o_hg_s, hg_state = hgrn_mixer(z, Np, Bs, Ts, state_hgrn[l].astype(F32), lower_bounds[l], hgrn_out_gain[l], gmat)
        o_mb_s = moba_decode(sp, cache_moba, l, page_table, sp["moba_rows"], Bs, Ts)
        ff_s = z[Np:, SEC_S2 * HW + MISC_LANE + 12:SEC_S2 * HW + MISC_LANE + 16].reshape(Bs, Ts, NH)
        logf_s = jax.nn.log_sigmoid(ff_s + fox_f_bias[l].astype(F32))
        c_s = jnp.cumsum(jnp.transpose(jnp.concatenate([gather_pages(cache_fox_logf[l]).astype(F32), logf_s], axis=1),
                                       (0, 2, 1)), axis=2)
        o_fx_s = fox_decode(sp, cache_fox_kv, l, page_table, c_s, sp["fox_rows"], Bs, Ts)
        win_rows = jnp.concatenate([state_nsa_win[l], sp["nsa_win"].reshape(Bs, Ts, 2, HEAD_DIM)], axis=1)
        st_s.append((sp["nsa_rows"].reshape(Bs, Ts, NSA_ROWS, HEAD_DIM),
                     win_rows[:, -min(NSA_WINDOW, win_rows.shape[1]):], hg_state.astype(dt),
                     sp["moba_rows"].reshape(Bs, Ts, 2, NH, HEAD_DIM), sp["fox_rows"].reshape(Bs, Ts, 2, NH, HEAD_DIM),
                     logf_s.astype(dt)))

        x, hn = out_projection(x, (o_nsa_p, o_hg_p, o_mb_p, o_fx_p), (o_nsa_s, o_hg_s, o_mb_s, o_fx_s),
                               w_out[l], g_ffn[l])
        if l % 2 == 0:
            x = swiglu_dense(hn, x, ffn_w1[i], ffn_w3[i], ffn_w2[i])
        else:
            x = x + moe_ffn_grouped(hn, moe_router[i], moe_w1[i], moe_w3[i], moe_w2[i])

    def stk(states, j):
        return jnp.stack([s[j] for s in states], axis=0)

    return (x[:Np].reshape(Bp, Tp, D), x[Np:].reshape(Bs, Ts, D),
            stk(st_p, 0), stk(st_s, 0), stk(st_p, 1), stk(st_s, 1), stk(st_p, 2), stk(st_s, 2),
            stk(st_p, 3), stk(st_s, 3), stk(st_p, 4), stk(st_s, 4), stk(st_p, 5), stk(st_s, 5))
```
